```python
import math
import jax, jax.numpy as jnp
from jax import lax
import numpy as np

D_MODEL = 1024
BATCH = 4
SEQ = 4096
DEPTH = 2

CHUNK = 64
Q_BLOCK = 128
EPS = 1e-6
MAX_OFFSET = 4096

HG_HEADS = 4
HG_DK = 128
HG_DV = 128
HG_WIDTH = HG_HEADS * HG_DV

MLA_HEADS = 8
MLA_NOPE = 64
MLA_ROPE = 32
MLA_V = 64
MLA_Q_LORA = 384
MLA_KV_LORA = 256
MLA_DQK = MLA_NOPE + MLA_ROPE
MLA_WIDTH = MLA_HEADS * MLA_V
ROPE_BASE = 10000.0

N_EXPERTS = 16
N_GROUPS = 4
EXPERTS_PER_GROUP = N_EXPERTS // N_GROUPS
TOPK_GROUPS = 1
TOP_K = 2
D_EXPERT = 512

IN_SIZES = (HG_HEADS * HG_DK,
            HG_HEADS * HG_DK,
            HG_HEADS * HG_DV,
            HG_WIDTH,
            MLA_Q_LORA,
            MLA_KV_LORA,
            MLA_ROPE,
            2 * D_MODEL)
IN_COLS = int(sum(IN_SIZES))
IN_SPLITS = tuple(int(v) for v in np.cumsum(IN_SIZES)[:-1])

kernel_name = "hybrid_hgrn2_mla_grouped_moe_adaln"


def rmsnorm(x, g):
    xf = x.astype(jnp.float32)
    y = xf * lax.rsqrt(jnp.mean(xf * xf, axis=-1, keepdims=True) + EPS)
    return (y * g.astype(jnp.float32)).astype(x.dtype)


def rotary(x, pos):
    half = x.shape[-1] // 2
    inv = ROPE_BASE ** (-jnp.arange(half, dtype=jnp.float32) / half)
    ang = pos.astype(jnp.float32)[..., None] * inv
    cos = jnp.cos(ang)[:, :, None, :]
    sin = jnp.sin(ang)[:, :, None, :]
    xf = x.astype(jnp.float32)
    x1, x2 = xf[..., :half], xf[..., half:]
    return jnp.concatenate([x1 * cos - x2 * sin, x1 * sin + x2 * cos], -1).astype(x.dtype)


def hgrn2_mix(q, f_logit, v, lb):
    B, S = q.shape[0], q.shape[1]
    nc = S // CHUNK
    lbh = lb.astype(jnp.float32).reshape(HG_HEADS, HG_DK)
    f = lbh + (1.0 - lbh) * jax.nn.sigmoid(f_logit.astype(jnp.float32))
    log_f = jnp.log(f)
    k = 1.0 - f
    qf = jax.nn.silu(q.astype(jnp.float32)) * (HG_DK ** -0.5)
    vf = v.astype(jnp.float32)

    def to_chunks(t):
        return t.reshape(B, nc, CHUNK, HG_HEADS, t.shape[-1]).transpose(1, 0, 3, 2, 4)

    qc, kc, vc, gc = to_chunks(qf), to_chunks(k), to_chunks(vf), to_chunks(log_f)
    causal = jnp.tril(jnp.ones((CHUNK, CHUNK), dtype=bool))[:, :, None]

    def step(state, inp):
        qb, kb, vb, gb = inp
        b = jnp.cumsum(gb, axis=2)
        rel = b[:, :, :, None, :] - b[:, :, None, :, :]
        decay = jnp.exp(jnp.where(causal, rel, -jnp.inf))
        attn = jnp.einsum('bhtd,bhtsd,bhsd->bhts', qb, decay, kb)
        intra = jnp.einsum('bhts,bhsv->bhtv', attn, vb)
        inter = jnp.einsum('bhtd,bhdv->bhtv', qb * jnp.exp(b), state)
        b_last = b[:, :, -1:, :]
        k_dec = kb * jnp.exp(b_last - b)
        new_state = state * jnp.exp(b_last[:, :, 0, :, None]) + jnp.einsum('bhsd,bhsv->bhdv', k_dec, vb)
        return new_state, intra + inter

    state0 = jnp.zeros((B, HG_HEADS, HG_DK, HG_DV), jnp.float32)
    _, out = lax.scan(step, state0, (qc, kc, vc, gc))
    return out.transpose(1, 0, 3, 2, 4).reshape(B, S, HG_HEADS, HG_DV)


def mla_mix(c_q, c_kv, k_rope, pos, q_norm_g, w_q_up, kv_norm_g, w_kv_up):
    B, S = c_q.shape[0], c_q.shape[1]
    q = (rmsnorm(c_q, q_norm_g) @ w_q_up).reshape(B, S, MLA_HEADS, MLA_DQK)
    q = jnp.concatenate([q[..., :MLA_NOPE], rotary(q[..., MLA_NOPE:], pos)], -1)
    kv = (rmsnorm(c_kv, kv_norm_g) @ w_kv_up).reshape(B, S, MLA_HEADS, MLA_NOPE + MLA_V)
    k_nope, v = kv[..., :MLA_NOPE], kv[..., MLA_NOPE:]
    k_pe = rotary(k_rope[:, :, None, :], pos)
    k = jnp.concatenate([k_nope, jnp.broadcast_to(k_pe, (B, S, MLA_HEADS, MLA_ROPE)).astype(k_nope.dtype)], -1)
    scale = MLA_DQK ** -0.5
    nb = S // Q_BLOCK
    qb = q.reshape(B, nb, Q_BLOCK, MLA_HEADS, MLA_DQK).transpose(1, 0, 2, 3, 4)
    key_chunk = jnp.arange(S) // CHUNK

    def block(args):
        qblk, start = args
        q_chunk = (start + jnp.arange(Q_BLOCK)) // CHUNK
        s = jnp.einsum('bqhd,bkhd->bhqk', qblk, k, preferred_element_type=jnp.float32) * scale
        mask = key_chunk[None, :] <= q_chunk[:, None]
        p = jax.nn.softmax(jnp.where(mask, s, -jnp.inf), axis=-1).astype(v.dtype)
        return jnp.einsum('bhqk,bkhd->bqhd', p, v)

    out = lax.map(block, (qb, jnp.arange(nb, dtype=jnp.int32) * Q_BLOCK))
    return out.transpose(1, 0, 2, 3, 4).reshape(B, S, MLA_WIDTH)


def token_mix(h, pos, w_in, lb, hg_norm_g, q_norm_g, w_q_up, kv_norm_g, w_kv_up, w_br_a, w_br_b, w_out):
    B, S, D = h.shape
    proj = h @ w_in
    hq, hf, hi, hgate, cq, ckv, kr, gates = jnp.split(proj, IN_SPLITS, axis=-1)
    o_a = hgrn2_mix(hq.reshape(B, S, HG_HEADS, HG_DK), hf.reshape(B, S, HG_HEADS, HG_DK),
                    hi.reshape(B, S, HG_HEADS, HG_DV), lb)
    o_a = rmsnorm(o_a, hg_norm_g.reshape(HG_HEADS, HG_DV)).reshape(B, S, HG_WIDTH)
    o_a = (o_a * jax.nn.silu(hgate.astype(jnp.float32))).astype(h.dtype)
    y_a = o_a @ w_br_a
    y_b = mla_mix(cq, ckv, kr, pos, q_norm_g, w_q_up, kv_norm_g, w_kv_up) @ w_br_b
    g = jax.nn.sigmoid(gates.astype(jnp.float32))
    g_a, g_b = g[..., :D], g[..., D:]
    merged = (g_a * y_a.astype(jnp.float32) + g_b * y_b.astype(jnp.float32)).astype(h.dtype)
    return merged @ w_out


def grouped_moe(h, w_router, router_bias, w_gate, w_up, w_down):
    B, S, D = h.shape
    t = h.reshape(-1, D)
    scores = jax.nn.sigmoid((t @ w_router).astype(jnp.float32))
    biased = (scores + router_bias.astype(jnp.float32)).reshape(-1, N_GROUPS, EXPERTS_PER_GROUP)
    group_score = lax.top_k(biased, 2)[0].sum(-1)
    _, gidx = lax.top_k(group_score, TOPK_GROUPS)
    gmask = jax.nn.one_hot(gidx, N_GROUPS, dtype=jnp.float32).sum(-2) > 0
    masked = jnp.where(gmask[:, :, None], biased, -jnp.inf).reshape(-1, N_EXPERTS)
    _, eidx = lax.top_k(masked, TOP_K)
    sel = jnp.take_along_axis(scores, eidx, axis=-1)
    w = sel / sel.sum(-1, keepdims=True)
    combine = (jax.nn.one_hot(eidx, N_EXPERTS, dtype=jnp.float32) * w[..., None]).sum(-2)
    y = jnp.zeros(t.shape, jnp.float32)
    for e in range(N_EXPERTS):
        he = jax.nn.silu(t @ w_gate[e]) * (t @ w_up[e])
        y = y + combine[:, e:e + 1] * (he @ w_down[e]).astype(jnp.float32)
    return y.astype(h.dtype).reshape(B, S, D)


def setup_inputs(seed: int = 0) -> dict:
    key = jax.random.key(seed)
    ks = jax.random.split(key, 24)
    f32 = jnp.float32
    D = D_MODEL

    def nrm(k, shape, scale):
        return jax.random.normal(k, shape, f32) * scale

    positions = (jax.random.randint(ks[2], (BATCH, 1), 0, MAX_OFFSET, dtype=jnp.int32)
                 + jnp.arange(SEQ, dtype=jnp.int32)[None, :]).astype(jnp.int32)
    return {
        "x": nrm(ks[0], (BATCH, SEQ, D), 1.0),
        "c": nrm(ks[1], (BATCH, D), 1.0),
        "positions": positions,
        "ada_w": nrm(ks[3], (DEPTH, D, 6 * D), 0.5 * D ** -0.5),
        "ada_b": nrm(ks[4], (DEPTH, 6 * D), 0.02),
        "norm1_g": 1.0 + nrm(ks[5], (DEPTH, D), 0.02),
        "w_in": nrm(ks[6], (DEPTH, D, IN_COLS), D ** -0.5),
        "hg_lb_logits": nrm(ks[7], (DEPTH, HG_HEADS * HG_DK), 1.0),
        "hg_norm_g": 1.0 + nrm(ks[8], (DEPTH, HG_WIDTH), 0.02),
        "q_norm_g": 1.0 + nrm(ks[9], (DEPTH, MLA_Q_LORA), 0.02),
        "w_q_up": nrm(ks[10], (DEPTH, MLA_Q_LORA, MLA_HEADS * MLA_DQK), MLA_Q_LORA ** -0.5),
        "kv_norm_g": 1.0 + nrm(ks[11], (DEPTH, MLA_KV_LORA), 0.02),
        "w_kv_up": nrm(ks[12], (DEPTH, MLA_KV_LORA, MLA_HEADS * (MLA_NOPE + MLA_V)), MLA_KV_LORA ** -0.5),
        "w_br_a": nrm(ks[13], (DEPTH, HG_WIDTH, D), HG_WIDTH ** -0.5),
        "w_br_b": nrm(ks[14], (DEPTH, MLA_WIDTH, D), MLA_WIDTH ** -0.5),
        "w_out": nrm(ks[15], (DEPTH, D, D), D ** -0.5),
        "norm2_g": 1.0 + nrm(ks[16], (DEPTH, D), 0.02),
        "w_router": nrm(ks[17], (D, N_EXPERTS), D ** -0.5),
        "router_bias": nrm(ks[18], (N_EXPERTS,), 0.01),
        "w_gate": nrm(ks[19], (DEPTH, N_EXPERTS, D, D_EXPERT), D ** -0.5),
        "w_up": nrm(ks[20], (DEPTH, N_EXPERTS, D, D_EXPERT), D ** -0.5),
        "w_down": nrm(ks[21], (DEPTH, N_EXPERTS, D_EXPERT, D), D_EXPERT ** -0.5),
        "final_g": 1.0 + nrm(ks[22], (D,), 0.02),
    }


def reference(x, c, positions, ada_w, ada_b, norm1_g, w_in, hg_lb_logits, hg_norm_g,
              q_norm_g, w_q_up, kv_norm_g, w_kv_up, w_br_a, w_br_b, w_out, norm2_g,
              w_router, router_bias, w_gate, w_up, w_down, final_g):
    D = D_MODEL
    lb_soft = jax.nn.softmax(hg_lb_logits.astype(jnp.float32), axis=0)
    lb_all = jnp.cumsum(lb_soft, axis=0) - lb_soft[0:1]
    c_act = jax.nn.silu(c)
    for l in range(DEPTH):
        mod = c_act @ ada_w[l] + ada_b[l]
        sh1, sc1, g1 = mod[:, None, 0:D], mod[:, None, D:2 * D], mod[:, None, 2 * D:3 * D]
        sh2, sc2, g2 = mod[:, None, 3 * D:4 * D], mod[:, None, 4 * D:5 * D], mod[:, None, 5 * D:6 * D]
        h = rmsnorm(x, norm1_g[l]) * (1.0 + sc1) + sh1
        x = x + g1 * token_mix(h, positions, w_in[l], lb_all[l], hg_norm_g[l], q_norm_g[l], w_q_up[l],
                               kv_norm_g[l], w_kv_up[l], w_br_a[l], w_br_b[l], w_out[l])
        h = rmsnorm(x, norm2_g[l]) * (1.0 + sc2) + sh2
        x = x + g2 * grouped_moe(h, w_router, router_bias, w_gate[l], w_up[l], w_down[l])
    return rmsnorm(x, final_g)
```

```python
import functools

import jax
import jax.numpy as jnp
from jax import lax
from jax.experimental import pallas as pl
from jax.experimental.pallas import tpu as pltpu

F32 = jnp.float32
BF16 = jnp.bfloat16

D_MODEL = 1024
CHUNK = 64
EPS = 1e-6

HG_HEADS = 4
HG_DK = 128
HG_DV = 128
HG_WIDTH = HG_HEADS * HG_DV
HG_SUB = 16

MLA_HEADS = 8
MLA_NOPE = 64
MLA_ROPE = 32
MLA_V = 64
MLA_Q_LORA = 384
MLA_KV_LORA = 256
MLA_DQK = MLA_NOPE + MLA_ROPE
MLA_WIDTH = MLA_HEADS * MLA_V
ROPE_BASE = 10000.0
HEAD_PAD = 128

N_EXPERTS = 16
N_GROUPS = 4
EXPERTS_PER_GROUP = N_EXPERTS // N_GROUPS
D_EXPERT = 512

IN_SIZES = (HG_HEADS * HG_DK, HG_HEADS * HG_DK, HG_HEADS * HG_DV, HG_WIDTH,
            MLA_Q_LORA, MLA_KV_LORA, MLA_ROPE, 2 * D_MODEL)

V7X_VMEM_LIMIT_BYTES = 56 * 1024 * 1024


def _params(*sem):
    return pltpu.CompilerParams(dimension_semantics=sem, vmem_limit_bytes=V7X_VMEM_LIMIT_BYTES)


def _resident(shape):
    nd = len(shape)
    return pl.BlockSpec(shape, lambda *_: (0,) * nd, pipeline_mode=pl.Buffered(1))


def _sigmoid(x):
    return 1.0 / (1.0 + jnp.exp(-x))


def _silu(x):
    return x * _sigmoid(x)


def _dot(a, b):
    return jnp.dot(a, b, preferred_element_type=F32)


def _dot_nt(a, b):
    return lax.dot_general(a, b, (((1,), (1,)), ((), ())), preferred_element_type=F32)


def _dot_tn(a, b):
    return lax.dot_general(a, b, (((0,), (0,)), ((), ())), preferred_element_type=F32)


def _split_bf16(x):
    hi = x.astype(BF16)
    lo = (x - hi.astype(F32)).astype(BF16)
    return hi, lo


def _rms(x, g):
    return x * lax.rsqrt(jnp.mean(x * x, axis=-1, keepdims=True) + EPS) * g


def _mod_kernel(c_ref, w_ref, b_ref, o_ref):
    ca = _silu(c_ref[...])
    o_ref[0] = _dot(ca.astype(BF16), w_ref[0].astype(BF16)) + b_ref[0]


def _modulation(c, ada_w, ada_b):
    depth, d, n = ada_w.shape
    b = c.shape[0]
    tn = 1536
    return pl.pallas_call(
        _mod_kernel,
        out_shape=jax.ShapeDtypeStruct((depth, b, n), F32),
        grid=(depth, n // tn),
        in_specs=[pl.BlockSpec((b, d), lambda l, j: (0, 0)),
                  pl.BlockSpec((1, d, tn), lambda l, j: (l, 0, j)),
                  pl.BlockSpec((1, 1, tn), lambda l, j: (l, 0, j))],
        out_specs=pl.BlockSpec((1, b, tn), lambda l, j: (l, 0, j)),
        compiler_params=_params("parallel", "parallel"),
        name="adaln_modulation",
    )(c, ada_w, ada_b.reshape(depth, 1, n))


def _rope_kernel(pos_ref, inv_ref, msk_ref, sgn_ref, cos_ref, sin_ref):
    ang = pos_ref[...].astype(F32) * inv_ref[...]
    cos_ref[...] = jnp.cos(ang) * msk_ref[...]
    sin_ref[...] = jnp.sin(ang) * sgn_ref[...]


def _rope_tables(positions):
    t = positions.size
    half = MLA_ROPE // 2
    inv = ROPE_BASE ** (-jnp.arange(half, dtype=F32) / half)
    z64, z32, one16 = jnp.zeros((MLA_NOPE,), F32), jnp.zeros((32,), F32), jnp.ones((half,), F32)
    inv_row = jnp.concatenate([z64, inv, inv, z32])[None, :]
    msk_row = jnp.concatenate([z64, one16, one16, z32])[None, :]
    sgn_row = jnp.concatenate([z64, -one16, one16, z32])[None, :]
    tr = min(2048, t)
    row = pl.BlockSpec((1, HEAD_PAD), lambda i: (0, 0))
    tab = pl.BlockSpec((tr, HEAD_PAD), lambda i: (i, 0))
    return pl.pallas_call(
        _rope_kernel,
        out_shape=(jax.ShapeDtypeStruct((t, HEAD_PAD), F32),) * 2,
        grid=(t // tr,),
        in_specs=[pl.BlockSpec((tr, 1), lambda i: (i, 0)), row, row, row],
        out_specs=(tab, tab),
        compiler_params=_params("parallel"),
        name="rope_tables",
    )(positions.reshape(t, 1), inv_row, msk_row, sgn_row)


def _proj_kernel(x_ref, mod_ref, n1_ref, cos_ref, sin_ref, wh_ref, wg_ref, wc_ref,
                 qn_ref, wqa_ref, wqs_ref, kn_ref, wk_ref, wv_ref, one_ref,
                 hq_ref, hf_ref, hi_ref, hgate_ref, gsig_ref, q_ref, k_ref, v_ref):
    x = x_ref[...]
    mod = mod_ref[0]
    h = _rms(x, n1_ref[...]) * (1.0 + mod[1:2]) + mod[0:1]
    hb = h.astype(BF16)

    ph = _dot(hb, wh_ref[...])
    w = HG_WIDTH
    hq_ref[...] = ph[:, 0:w].astype(BF16)
    hf_ref[...] = ph[:, w:2 * w]
    hi_ref[...] = ph[:, 2 * w:3 * w].astype(BF16)
    hgate_ref[...] = ph[:, 3 * w:4 * w].astype(BF16)

    gsig_ref[...] = _sigmoid(_dot(hb, wg_ref[...])).astype(BF16)

    pc = _dot(hb, wc_ref[...])
    cq = pc[:, 0:MLA_Q_LORA]
    ckv = pc[:, MLA_Q_LORA:MLA_Q_LORA + MLA_KV_LORA]
    kra = pc[:, MLA_Q_LORA + MLA_KV_LORA:MLA_Q_LORA + MLA_KV_LORA + HEAD_PAD]
    krb = pc[:, MLA_Q_LORA + MLA_KV_LORA + HEAD_PAD:]

    cos_t = cos_ref[...]
    sin_t = sin_ref[...]
    lane = lax.broadcasted_iota(jnp.int32, cos_t.shape, 1)
    scale = MLA_DQK ** -0.5
    cq_tab = jnp.tile(scale * (cos_t + jnp.where(lane < MLA_NOPE, 1.0, 0.0)), (1, MLA_HEADS))
    sq_tab = jnp.tile(scale * sin_t, (1, MLA_HEADS))

    cqn = _rms(cq, qn_ref[...]).astype(BF16)
    q = _dot(cqn, wqa_ref[...]) * cq_tab + _dot(cqn, wqs_ref[...]) * sq_tab
    q_ref[...] = q.astype(BF16)

    ckvn = _rms(ckv, kn_ref[...]).astype(BF16)
    kpe = kra * cos_t + krb * sin_t
    k_ref[...] = (_dot(ckvn, wk_ref[...]) + jnp.tile(kpe, (1, MLA_HEADS))).astype(BF16)
    v_ref[...] = (_dot(ckvn, wv_ref[...]) + one_ref[...]).astype(BF16)


def _pad_heads(w, lo, hi, at):
    k, nh, _ = w.shape
    out = jnp.zeros((k, nh, HEAD_PAD), w.dtype)
    out = out.at[:, :, at:at + (hi - lo)].set(w[:, :, lo:hi])
    return out


def _input_projection(x2, mod_l, n1, cos_t, sin_t, w_in, q_norm_g, w_q_up, kv_norm_g, w_kv_up, seq):
    t, d = x2.shape
    tm = 256
    splits = [0]
    for s in IN_SIZES:
        splits.append(splits[-1] + s)
    w_h = w_in[:, splits[0]:splits[4]].astype(BF16)
    w_g = w_in[:, splits[7]:splits[8]].astype(BF16)
    kr = w_in[:, splits[6]:splits[7]]
    half = MLA_ROPE // 2
    z64 = jnp.zeros((d, MLA_NOPE), F32)
    z32 = jnp.zeros((d, HEAD_PAD - MLA_NOPE - MLA_ROPE), F32)
    kr_a = jnp.concatenate([z64, kr, z32], axis=1)
    kr_b = jnp.concatenate([z64, kr[:, half:], kr[:, :half], z32], axis=1)
    w_c = jnp.concatenate([w_in[:, splits[4]:splits[6]], kr_a, kr_b], axis=1).astype(BF16)

    wq = w_q_up.reshape(MLA_Q_LORA, MLA_HEADS, MLA_DQK)
    wq_all = _pad_heads(wq, 0, MLA_DQK, 0).reshape(MLA_Q_LORA, -1).astype(BF16)
    wq_swap = (_pad_heads(wq, MLA_NOPE + half, MLA_DQK, MLA_NOPE)
               + _pad_heads(wq, MLA_NOPE, MLA_NOPE + half, MLA_NOPE + half))
    wq_swap = wq_swap.reshape(MLA_Q_LORA, -1).astype(BF16)
    wkv = w_kv_up.reshape(MLA_KV_LORA, MLA_HEADS, MLA_NOPE + MLA_V)
    wk_all = _pad_heads(wkv, 0, MLA_NOPE, 0).reshape(MLA_KV_LORA, -1).astype(BF16)
    wv_all = _pad_heads(wkv, MLA_NOPE, MLA_NOPE + MLA_V, 0).reshape(MLA_KV_LORA, -1).astype(BF16)
    ones_row = jnp.tile(jnp.zeros((HEAD_PAD,), F32).at[MLA_V].set(1.0), MLA_HEADS)[None, :]

    hp = MLA_HEADS * HEAD_PAD
    rows = lambda n: pl.BlockSpec((tm, n), lambda i: (i, 0))
    outs = [(HG_WIDTH, BF16), (HG_WIDTH, F32), (HG_WIDTH, BF16), (HG_WIDTH, BF16),
            (2 * d, BF16), (hp, BF16), (hp, BF16), (hp, BF16)]
    return pl.pallas_call(
        _proj_kernel,
        out_shape=tuple(jax.ShapeDtypeStruct((t, n), dt) for n, dt in outs),
        grid=(t // tm,),
        in_specs=[rows(d),
                  pl.BlockSpec((1, 6, d), lambda i: ((i * tm) // seq, 0, 0)),
                  _resident((1, d)),
                  rows(HEAD_PAD), rows(HEAD_PAD),
                  _resident(w_h.shape), _resident(w_g.shape), _resident(w_c.shape),
                  _resident((1, MLA_Q_LORA)), _resident(wq_all.shape), _resident(wq_swap.shape),
                  _resident((1, MLA_KV_LORA)), _resident(wk_all.shape), _resident(wv_all.shape),
                  _resident((1, hp))],
        out_specs=tuple(rows(n) for n, _ in outs),
        compiler_params=_params("parallel"),
        name="norm_input_projection",
    )(x2, mod_l, n1[None, :], cos_t, sin_t, w_h, w_g, w_c,
      q_norm_g[None, :], wq_all, wq_swap, kv_norm_g[None, :], wk_all, wv_all, ones_row)


def _hgrn_kernel(lbl_ref, gn_ref, q_ref, f_ref, v_ref, gate_ref, o_ref, st_ref, *, layer, n_chunks):
    @pl.when(pl.program_id(1) == 0)
    def _():
        st_ref[...] = jnp.zeros_like(st_ref)

    lg = lbl_ref[...]
    ex = jnp.exp(lg - jnp.max(lg, axis=0, keepdims=True))
    soft = ex / jnp.sum(ex, axis=0, keepdims=True)
    lb_all = jnp.zeros_like(soft[0:1])
    for i in range(1, layer + 1):
        lb_all = lb_all + soft[i:i + 1]

    r_i = lax.broadcasted_iota(jnp.int32, (CHUNK, CHUNK), 0)
    c_i = lax.broadcasted_iota(jnp.int32, (CHUNK, CHUNK), 1)
    tri = jnp.where(r_i >= c_i, 1.0, 0.0).astype(BF16)
    sub_row = lax.broadcasted_iota(jnp.int32, (HG_SUB, HG_DK), 0)
    n_sub = CHUNK // HG_SUB

    def chunk_body(ci, carry):
        r0 = pl.multiple_of(ci * CHUNK, CHUNK)
        rows = pl.ds(r0, CHUNK)
        for h in range(HG_HEADS):
            sl = slice(h * HG_DK, (h + 1) * HG_DK)
            lb = lb_all[:, sl]
            q = q_ref[rows, sl].astype(F32)
            vb = v_ref[rows, sl]
            v = vb.astype(F32)
            f = lb + (1.0 - lb) * _sigmoid(f_ref[rows, sl])
            g = jnp.log(f)
            k = 1.0 - f
            qf = _silu(q) * (HG_DK ** -0.5)
            g_hi, g_lo = _split_bf16(g)
            b = _dot(tri, g_hi) + _dot(tri, g_lo)

            blocks = []
            for i in range(n_sub):
                rs = slice(i * HG_SUB, (i + 1) * HG_SUB)
                b_i, q_i, k_i, v_i = b[rs], qf[rs], k[rs], v[rs]
                acc = jnp.zeros((HG_SUB, HG_DV), F32)
                for s in range(HG_SUB):
                    wgt = jnp.where(sub_row >= s, jnp.exp(b_i - b_i[s:s + 1]), 0.0)
                    a = jnp.sum(wgt * (q_i * k_i[s:s + 1]), axis=1, keepdims=True)
                    acc = acc + a * v_i[s:s + 1]
                if i > 0:
                    n_prev = i * HG_SUB
                    ref = b[n_prev - 1:n_prev]
                    qt = (q_i * jnp.exp(b_i - ref)).astype(BF16)
                    kt = (k[:n_prev] * jnp.exp(ref - b[:n_prev])).astype(BF16)
                    att = _dot_nt(qt, kt)
                    acc = acc + _dot(att.astype(BF16), vb[:n_prev])
                blocks.append(acc)
            o = jnp.concatenate(blocks, axis=0)

            st = st_ref[h]
            o = o + _dot_nt((qf * jnp.exp(b)).astype(BF16), st.astype(BF16))
            b_last = b[CHUNK - 1:CHUNK]
            kd = (k * jnp.exp(b_last - b)).astype(BF16)
            st_ref[h] = st * jnp.exp(b_last) + _dot_tn(vb, kd)

            gt = gate_ref[rows, sl].astype(F32)
            o_ref[rows, sl] = (_rms(o, gn_ref[:, sl]) * _silu(gt)).astype(BF16)
        return carry

    lax.fori_loop(0, n_chunks, chunk_body, 0)


def _hgrn(hg_lb_logits, hg_norm_g, hq, hf, hi, hgate, layer, batch, seq):
    t = hq.shape[0]
    lc = min(512, seq)
    nb = seq // lc
    rows = pl.BlockSpec((lc, HG_WIDTH), lambda b, j: (b * nb + j, 0))
    kern = functools.partial(_hgrn_kernel, layer=layer, n_chunks=lc // CHUNK)
    return pl.pallas_call(
        kern,
        out_shape=jax.ShapeDtypeStruct((t, HG_WIDTH), BF16),
        grid=(batch, nb),
        in_specs=[pl.BlockSpec(hg_lb_logits.shape, lambda b, j: (0, 0)),
                  pl.BlockSpec((1, HG_WIDTH), lambda b, j: (0, 0)),
                  rows, rows, rows, rows],
        out_specs=rows,
        scratch_shapes=[pltpu.VMEM((HG_HEADS, HG_DV, HG_DK), F32)],
        compiler_params=_params("parallel", "arbitrary"),
        name="hgrn2_chunkwise",
    )(hg_lb_logits, hg_norm_g[None, :], hq, hf, hi, hgate)


def _attn_kernel(q_ref, k_ref, v_ref, o_ref, m_ref, acc_ref, *, tq):
    i = pl.program_id(1)
    j = pl.program_id(2)

    @pl.when(j == 0)
    def _():
        m_ref[...] = jnp.full_like(m_ref, -jnp.inf)
        acc_ref[...] = jnp.zeros_like(acc_ref)

    def step(diagonal):
        if diagonal:
            qc = lax.broadcasted_iota(jnp.int32, (tq, tq), 0) // CHUNK
            kc = lax.broadcasted_iota(jnp.int32, (tq, tq), 1) // CHUNK
            visible = kc <= qc
        for h in range(MLA_HEADS):
            sl = slice(h * HEAD_PAD, (h + 1) * HEAD_PAD)
            s = _dot_nt(q_ref[:, sl], k_ref[:, sl])
            if diagonal:
                s = jnp.where(visible, s, -jnp.inf)
            m_prev = m_ref[h]
            m_new = jnp.maximum(m_prev, jnp.max(s, axis=1, keepdims=True))
            alpha = jnp.exp(m_prev - m_new)
            p = jnp.exp(s - m_new[:, 0:1])
            acc_ref[h] = acc_ref[h] * alpha + _dot(p.astype(BF16), v_ref[:, sl])
            m_ref[h] = m_new

    @pl.when(j < i)
    def _():
        step(False)

    @pl.when(j == i)
    def _():
        step(True)
        outs = []
        for h in range(MLA_HEADS):
            acc = acc_ref[h]
            outs.append(acc[:, 0:MLA_V] / acc[:, MLA_V:MLA_V + 1])
        o_ref[...] = jnp.concatenate(outs, axis=1).astype(BF16)


def _attention(q, k, v, batch, seq):
    t, hp = q.shape
    tq = min(512, seq)
    nq = seq // tq
    qspec = pl.BlockSpec((tq, hp), lambda b, i, j: (b * nq + i, 0))
    kspec = pl.BlockSpec((tq, hp), lambda b, i, j: (b * nq + jnp.minimum(i, j), 0))
    return pl.pallas_call(
        functools.partial(_attn_kernel, tq=tq),
        out_shape=jax.ShapeDtypeStruct((t, MLA_WIDTH), BF16),
        grid=(batch, nq, nq),
        in_specs=[qspec, kspec, kspec],
        out_specs=pl.BlockSpec((tq, MLA_WIDTH), lambda b, i, j: (b * nq + i, 0)),
        scratch_shapes=[pltpu.VMEM((MLA_HEADS, tq, HEAD_PAD), F32),
                        pltpu.VMEM((MLA_HEADS, tq, HEAD_PAD), F32)],
        compiler_params=_params("parallel", "parallel", "arbitrary"),
        name="mla_flash_attention",
    )(q, k, v)


def _route(scores, bias):
    biased = scores + bias
    col = [biased[:, e:e + 1] for e in range(N_EXPERTS)]
    gscore = []
    for g in range(N_GROUPS):
        a, b, c, d = col[g * EXPERTS_PER_GROUP:(g + 1) * EXPERTS_PER_GROUP]
        gscore.append(jnp.maximum(jnp.maximum(jnp.maximum(a + b, a + c), jnp.maximum(a + d, b + c)),
                                  jnp.maximum(b + d, c + d)))
    sel = []
    for g in range(N_GROUPS):
        ok = None
        for o in range(N_GROUPS):
            if o == g:
                continue
            cond = (gscore[g] > gscore[o]) if o < g else (gscore[g] >= gscore[o])
            ok = cond if ok is None else jnp.logical_and(ok, cond)
        grp = col[g * EXPERTS_PER_GROUP:(g + 1) * EXPERTS_PER_GROUP]
        for e in range(EXPERTS_PER_GROUP):
            beaten = jnp.zeros_like(grp[e])
            for o in range(EXPERTS_PER_GROUP):
                if o == e:
                    continue
                ahead = (grp[o] >= grp[e]) if o < e else (grp[o] > grp[e])
                beaten = beaten + jnp.where(ahead, 1.0, 0.0)
            sel.append(jnp.where(jnp.logical_and(ok, beaten < 1.5), 1.0, 0.0))
    picked = jnp.concatenate(sel, axis=1) * scores
    return picked / jnp.sum(picked, axis=1, keepdims=True)


def _merge_kernel(x_ref, oa_ref, ob_ref, g_ref, mod_ref, wa_ref, wb_ref, wo_ref, n2_ref,
                  wr_ref, rb_ref, xn_ref, h2_ref, comb_ref):
    d = D_MODEL
    mod = mod_ref[0]
    ya = _dot(oa_ref[...], wa_ref[...])
    yb = _dot(ob_ref[...], wb_ref[...])
    g = g_ref[...].astype(F32)
    merged = g[:, 0:d] * ya + g[:, d:2 * d] * yb
    xn = x_ref[...] + mod[2:3] * _dot(merged.astype(BF16), wo_ref[...])
    xn_ref[...] = xn
    h2 = _rms(xn, n2_ref[...]) * (1.0 + mod[4:5]) + mod[3:4]
    h2_ref[...] = h2.astype(BF16)
    h_hi, h_lo = _split_bf16(h2)
    w_hi, w_lo = _split_bf16(wr_ref[...])
    logits = _dot(h_hi, w_hi) + _dot(h_lo, w_hi) + _dot(h_hi, w_lo)
    comb_ref[...] = _route(_sigmoid(logits), rb_ref[...])


def _merge(x2, oa, ob, gsig, mod_l, w_br_a, w_br_b, w_out, n2, w_router, router_bias, seq):
    t, d = x2.shape
    tm = 256
    rows = lambda n: pl.BlockSpec((tm, n), lambda i: (i, 0))
    return pl.pallas_call(
        _merge_kernel,
        out_shape=(jax.ShapeDtypeStruct((t, d), F32), jax.ShapeDtypeStruct((t, d), BF16),
                   jax.ShapeDtypeStruct((t, N_EXPERTS), F32)),
        grid=(t // tm,),
        in_specs=[rows(d), rows(HG_WIDTH), rows(MLA_WIDTH), rows(2 * d),
                  pl.BlockSpec((1, 6, d), lambda i: ((i * tm) // seq, 0, 0)),
                  _resident(w_br_a.shape), _resident(w_br_b.shape), _resident(w_out.shape),
                  _resident((1, d)), _resident(w_router.shape), _resident((1, N_EXPERTS))],
        out_specs=(rows(d), rows(d), rows(N_EXPERTS)),
        compiler_params=_params("parallel"),
        name="merge_outproj_router",
    )(x2, oa, ob, gsig, mod_l, w_br_a.astype(BF16), w_br_b.astype(BF16), w_out.astype(BF16),
      n2[None, :], w_router, router_bias[None, :])


def _moe_kernel(x_ref, h_ref, comb_ref, mod_ref, wg_ref, wu_ref, wd_ref, fg_ref, o_ref, acc_ref, *, final):
    e = pl.program_id(1)

    @pl.when(e == 0)
    def _():
        acc_ref[...] = jnp.zeros_like(acc_ref)

    hb = h_ref[...]
    he = _silu(_dot(hb, wg_ref[0])) * _dot(hb, wu_ref[0])
    comb = comb_ref[...]
    lane = lax.broadcasted_iota(jnp.int32, comb.shape, 1)
    c = jnp.sum(jnp.where(lane == e, comb, 0.0), axis=1, keepdims=True)
    acc_ref[...] += c * _dot(he.astype(BF16), wd_ref[0])

    @pl.when(e == N_EXPERTS - 1)
    def _():
        xo = x_ref[...] + mod_ref[0][5:6] * acc_ref[...]
        if final:
            xo = _rms(xo, fg_ref[...])
        o_ref[...] = xo


def _moe(xn, h2, comb, mod_l, w_gate, w_up, w_down, final_g, final, seq):
    t, d = xn.shape
    tm = min(1024, seq)
    rows = lambda n: pl.BlockSpec((tm, n), lambda i, e: (i, 0))
    return pl.pallas_call(
        functools.partial(_moe_kernel, final=final),
        out_shape=jax.ShapeDtypeStruct((t, d), F32),
        grid=(t // tm, N_EXPERTS),
        in_specs=[rows(d), rows(d), rows(N_EXPERTS),
                  pl.BlockSpec((1, 6, d), lambda i, e: ((i * tm) // seq, 0, 0)),
                  pl.BlockSpec((1, d, D_EXPERT), lambda i, e: (e, 0, 0)),
                  pl.BlockSpec((1, d, D_EXPERT), lambda i, e: (e, 0, 0)),
                  pl.BlockSpec((1, D_EXPERT, d), lambda i, e: (e, 0, 0)),
                  pl.BlockSpec((1, d), lambda i, e: (0, 0))],
        out_specs=rows(d),
        scratch_shapes=[pltpu.VMEM((tm, d), F32)],
        compiler_params=_params("parallel", "arbitrary"),
        name="moe_experts",
    )(xn, h2, comb, mod_l, w_gate.astype(BF16), w_up.astype(BF16), w_down.astype(BF16), final_g[None, :])


def kernel(x, c, positions, ada_w, ada_b, norm1_g, w_in, hg_lb_logits, hg_norm_g, q_norm_g, w_q_up,
           kv_norm_g, w_kv_up, w_br_a, w_br_b, w_out, norm2_g, w_router, router_bias, w_gate, w_up,
           w_down, final_g):
    batch, seq, d = x.shape
    depth = ada_w.shape[0]
    mod = _modulation(c, ada_w, ada_b).reshape(depth, batch, 6, d)
    cos_t, sin_t = _rope_tables(positions)
    x2 = x.reshape(batch * seq, d)
    for l in range(depth):
        hq, hf, hi, hgate, gsig, q, k, v = _input_projection(
            x2, mod[l], norm1_g[l], cos_t, sin_t, w_in[l], q_norm_g[l], w_q_up[l],
            kv_norm_g[l], w_kv_up[l], seq)
        oa = _hgrn(hg_lb_logits, hg_norm_g[l], hq, hf, hi, hgate, l, batch, seq)
        ob = _attention(q, k, v, batch, seq)
        xn, h2, comb = _merge(x2, oa, ob, gsig, mod[l], w_br_a[l], w_br_b[l], w_out[l], norm2_g[l],
                              w_router, router_bias, seq)
        x2 = _moe(xn, h2, comb, mod[l], w_gate[l], w_up[l], w_down[l], final_g, l == depth - 1, seq)
    return x2.reshape(batch, seq, d)
```

```python
import functools
import math

import jax
import jax.numpy as jnp
from jax import lax
from jax.experimental import pallas as pl
from jax.experimental.pallas import tpu as pltpu

F32 = jnp.float32
BF16 = jnp.bfloat16

D_MODEL = 1024
CHUNK = 64
EPS = 1e-6

HG_HEADS = 4
HG_DK = 128
HG_DV = 128
HG_WIDTH = HG_HEADS * HG_DV
HG_SUB = 16

MLA_HEADS = 8
MLA_NOPE = 64
MLA_ROPE = 32
MLA_V = 64
MLA_Q_LORA = 384
MLA_KV_LORA = 256
MLA_DQK = MLA_NOPE + MLA_ROPE
MLA_WIDTH = MLA_HEADS * MLA_V
ROPE_BASE = 10000.0
HEAD_PAD = 128
LOG2_E = math.log2(math.e)

N_EXPERTS = 16
N_GROUPS = 4
EXPERTS_PER_GROUP = N_EXPERTS // N_GROUPS
D_EXPERT = 512

IN_SIZES = (HG_HEADS * HG_DK, HG_HEADS * HG_DK, HG_HEADS * HG_DV, HG_WIDTH,
            MLA_Q_LORA, MLA_KV_LORA, MLA_ROPE, 2 * D_MODEL)

V7X_VMEM_LIMIT_BYTES = 56 * 1024 * 1024


def _params(*sem):
    return pltpu.CompilerParams(dimension_semantics=sem, vmem_limit_bytes=V7X_VMEM_LIMIT_BYTES)


def _resident(shape):
    nd = len(shape)
    return pl.BlockSpec(shape, lambda *_: (0,) * nd, pipeline_mode=pl.Buffered(1))


def _sigmoid(x):
    return 1.0 / (1.0 + jnp.exp(-x))


def _silu(x):
    return x * _sigmoid(x)


def _dot(a, b):
    return jnp.dot(a, b, preferred_element_type=F32)


def _dot_nt(a, b):
    return lax.dot_general(a, b, (((1,), (1,)), ((), ())), preferred_element_type=F32)


def _dot_tn(a, b):
    return lax.dot_general(a, b, (((0,), (0,)), ((), ())), preferred_element_type=F32)


def _split_bf16(x):
    hi = x.astype(BF16)
    lo = (x - hi.astype(F32)).astype(BF16)
    return hi, lo


def _rms(x, g):
    return x * lax.rsqrt(jnp.mean(x * x, axis=-1, keepdims=True) + EPS) * g


def _mod_kernel(c_ref, w_ref, b_ref, o_ref):
    ca = _silu(c_ref[...])
    o_ref[0] = _dot(ca.astype(BF16), w_ref[0].astype(BF16)) + b_ref[0]


def _modulation(c, ada_w, ada_b):
    depth, d, n = ada_w.shape
    b = c.shape[0]
    tn = 1536
    return pl.pallas_call(
        _mod_kernel,
        out_shape=jax.ShapeDtypeStruct((depth, b, n), F32),
        grid=(depth, n // tn),
        in_specs=[pl.BlockSpec((b, d), lambda l, j: (0, 0)),
                  pl.BlockSpec((1, d, tn), lambda l, j: (l, 0, j)),
                  pl.BlockSpec((1, 1, tn), lambda l, j: (l, 0, j))],
        out_specs=pl.BlockSpec((1, b, tn), lambda l, j: (l, 0, j)),
        compiler_params=_params("parallel", "parallel"),
        name="adaln_modulation",
    )(c, ada_w, ada_b.reshape(depth, 1, n))


def _rope_kernel(pos_ref, inv_ref, msk_ref, sgn_ref, cos_ref, sin_ref):
    ang = pos_ref[...].astype(F32) * inv_ref[...]
    cos_ref[...] = jnp.cos(ang) * msk_ref[...]
    sin_ref[...] = jnp.sin(ang) * sgn_ref[...]


def _rope_tables(positions):
    t = positions.size
    half = MLA_ROPE // 2
    inv = ROPE_BASE ** (-jnp.arange(half, dtype=F32) / half)
    z64, z32, one16 = jnp.zeros((MLA_NOPE,), F32), jnp.zeros((32,), F32), jnp.ones((half,), F32)
    inv_row = jnp.concatenate([z64, inv, inv, z32])[None, :]
    msk_row = jnp.concatenate([z64, one16, one16, z32])[None, :]
    sgn_row = jnp.concatenate([z64, -one16, one16, z32])[None, :]
    tr = min(2048, t)
    row = pl.BlockSpec((1, HEAD_PAD), lambda i: (0, 0))
    tab = pl.BlockSpec((tr, HEAD_PAD), lambda i: (i, 0))
    return pl.pallas_call(
        _rope_kernel,
        out_shape=(jax.ShapeDtypeStruct((t, HEAD_PAD), F32),) * 2,
        grid=(t // tr,),
        in_specs=[pl.BlockSpec((tr, 1), lambda i: (i, 0)), row, row, row],
        out_specs=(tab, tab),
        compiler_params=_params("parallel"),
        name="rope_tables",
    )(positions.reshape(t, 1), inv_row, msk_row, sgn_row)


def _proj_kernel(x_ref, mod_ref, n1_ref, cos_ref, sin_ref, wh_ref, wg_ref, wc_ref,
                 qn_ref, wqa_ref, wqs_ref, kn_ref, wk_ref, wv_ref, one_ref,
                 hq_ref, hf_ref, hi_ref, hgate_ref, gsig_ref, q_ref, k_ref, v_ref):
    x = x_ref[...]
    mod = mod_ref[0]
    h = _rms(x, n1_ref[...]) * (1.0 + mod[1:2]) + mod[0:1]
    hb = h.astype(BF16)

    ph = _dot(hb, wh_ref[...])
    w = HG_WIDTH
    hq_ref[...] = ph[:, 0:w].astype(BF16)
    hf_ref[...] = ph[:, w:2 * w]
    hi_ref[...] = ph[:, 2 * w:3 * w].astype(BF16)
    hgate_ref[...] = ph[:, 3 * w:4 * w].astype(BF16)

    gsig_ref[...] = _sigmoid(_dot(hb, wg_ref[...])).astype(BF16)

    pc = _dot(hb, wc_ref[...])
    cq = pc[:, 0:MLA_Q_LORA]
    ckv = pc[:, MLA_Q_LORA:MLA_Q_LORA + MLA_KV_LORA]
    kra = pc[:, MLA_Q_LORA + MLA_KV_LORA:MLA_Q_LORA + MLA_KV_LORA + HEAD_PAD]
    krb = pc[:, MLA_Q_LORA + MLA_KV_LORA + HEAD_PAD:]

    cos_t = cos_ref[...]
    sin_t = sin_ref[...]
    lane = lax.broadcasted_iota(jnp.int32, cos_t.shape, 1)
    scale = MLA_DQK ** -0.5 * LOG2_E
    cq_tab = jnp.tile(scale * (cos_t + jnp.where(lane < MLA_NOPE, 1.0, 0.0)), (1, MLA_HEADS))
    sq_tab = jnp.tile(scale * sin_t, (1, MLA_HEADS))

    cqn = _rms(cq, qn_ref[...]).astype(BF16)
    q = _dot(cqn, wqa_ref[...]) * cq_tab + _dot(cqn, wqs_ref[...]) * sq_tab
    q_ref[...] = q.astype(BF16)

    ckvn = _rms(ckv, kn_ref[...]).astype(BF16)
    kpe = kra * cos_t + krb * sin_t
    k_ref[...] = (_dot(ckvn, wk_ref[...]) + jnp.tile(kpe, (1, MLA_HEADS))).astype(BF16)
    v_ref[...] = (_dot(ckvn, wv_ref[...]) + one_ref[...]).astype(BF16)


def _pad_heads(w, lo, hi, at):
    k, nh, _ = w.shape
    out = jnp.zeros((k, nh, HEAD_PAD), w.dtype)
    out = out.at[:, :, at:at + (hi - lo)].set(w[:, :, lo:hi])
    return out


def _input_projection(x2, mod_l, n1, cos_t, sin_t, w_in, q_norm_g, w_q_up, kv_norm_g, w_kv_up, seq):
    t, d = x2.shape
    tm = 256
    splits = [0]
    for s in IN_SIZES:
        splits.append(splits[-1] + s)
    w_h = w_in[:, splits[0]:splits[4]].astype(BF16)
    w_g = w_in[:, splits[7]:splits[8]].astype(BF16)
    kr = w_in[:, splits[6]:splits[7]]
    half = MLA_ROPE // 2
    z64 = jnp.zeros((d, MLA_NOPE), F32)
    z32 = jnp.zeros((d, HEAD_PAD - MLA_NOPE - MLA_ROPE), F32)
    kr_a = jnp.concatenate([z64, kr, z32], axis=1)
    kr_b = jnp.concatenate([z64, kr[:, half:], kr[:, :half], z32], axis=1)
    w_c = jnp.concatenate([w_in[:, splits[4]:splits[6]], kr_a, kr_b], axis=1).astype(BF16)

    wq = w_q_up.reshape(MLA_Q_LORA, MLA_HEADS, MLA_DQK)
    wq_all = _pad_heads(wq, 0, MLA_DQK, 0).reshape(MLA_Q_LORA, -1).astype(BF16)
    wq_swap = (_pad_heads(wq, MLA_NOPE + half, MLA_DQK, MLA_NOPE)
               + _pad_heads(wq, MLA_NOPE, MLA_NOPE + half, MLA_NOPE + half))
    wq_swap = wq_swap.reshape(MLA_Q_LORA, -1).astype(BF16)
    wkv = w_kv_up.reshape(MLA_KV_LORA, MLA_HEADS, MLA_NOPE + MLA_V)
    wk_all = _pad_heads(wkv, 0, MLA_NOPE, 0).reshape(MLA_KV_LORA, -1).astype(BF16)
    wv_all = _pad_heads(wkv, MLA_NOPE, MLA_NOPE + MLA_V, 0).reshape(MLA_KV_LORA, -1).astype(BF16)
    ones_row = jnp.tile(jnp.zeros((HEAD_PAD,), F32).at[MLA_V].set(1.0), MLA_HEADS)[None, :]

    hp = MLA_HEADS * HEAD_PAD
    rows = lambda n: pl.BlockSpec((tm, n), lambda i: (i, 0))
    outs = [(HG_WIDTH, BF16), (HG_WIDTH, F32), (HG_WIDTH, BF16), (HG_WIDTH, BF16),
            (2 * d, BF16), (hp, BF16), (hp, BF16), (hp, BF16)]
    return pl.pallas_call(
        _proj_kernel,
        out_shape=tuple(jax.ShapeDtypeStruct((t, n), dt) for n, dt in outs),
        grid=(t // tm,),
        in_specs=[rows(d),
                  pl.BlockSpec((1, 6, d), lambda i: ((i * tm) // seq, 0, 0)),
                  _resident((1, d)),
                  rows(HEAD_PAD), rows(HEAD_PAD),
                  _resident(w_h.shape), _resident(w_g.shape), _resident(w_c.shape),
                  _resident((1, MLA_Q_LORA)), _resident(wq_all.shape), _resident(wq_swap.shape),
                  _resident((1, MLA_KV_LORA)), _resident(wk_all.shape), _resident(wv_all.shape),
                  _resident((1, hp))],
        out_specs=tuple(rows(n) for n, _ in outs),
        compiler_params=_params("parallel"),
        name="norm_input_projection",
    )(x2, mod_l, n1[None, :], cos_t, sin_t, w_h, w_g, w_c,
      q_norm_g[None, :], wq_all, wq_swap, kv_norm_g[None, :], wk_all, wv_all, ones_row)


def _hgrn_kernel(lbl_ref, gn_ref, q_ref, f_ref, v_ref, gate_ref, o_ref, st_ref, *, layer, n_chunks):
    @pl.when(pl.program_id(1) == 0)
    def _():
        st_ref[...] = jnp.zeros_like(st_ref)

    lg = lbl_ref[...]
    ex = jnp.exp(lg - jnp.max(lg, axis=0, keepdims=True))
    soft = ex / jnp.sum(ex, axis=0, keepdims=True)
    lb_all = jnp.zeros_like(soft[0:1])
    for i in range(1, layer + 1):
        lb_all = lb_all + soft[i:i + 1]

    r_i = lax.broadcasted_iota(jnp.int32, (CHUNK, CHUNK), 0)
    c_i = lax.broadcasted_iota(jnp.int32, (CHUNK, CHUNK), 1)
    tri = jnp.where(r_i >= c_i, 1.0, 0.0).astype(BF16)
    sub_row = lax.broadcasted_iota(jnp.int32, (HG_SUB, HG_DK), 0)
    n_sub = CHUNK // HG_SUB

    def chunk_body(ci, carry):
        r0 = pl.multiple_of(ci * CHUNK, CHUNK)
        rows = pl.ds(r0, CHUNK)
        for h in range(HG_HEADS):
            sl = slice(h * HG_DK, (h + 1) * HG_DK)
            lb = lb_all[:, sl]
            q = q_ref[rows, sl].astype(F32)
            vb = v_ref[rows, sl]
            v = vb.astype(F32)
            f = lb + (1.0 - lb) * _sigmoid(f_ref[rows, sl])
            g = jnp.log(f)
            k = 1.0 - f
            qf = _silu(q) * (HG_DK ** -0.5)
            g_hi, g_lo = _split_bf16(g)
            b = _dot(tri, g_hi) + _dot(tri, g_lo)

            blocks = []
            for i in range(n_sub):
                rs = slice(i * HG_SUB, (i + 1) * HG_SUB)
                b_i, q_i, k_i, v_i = b[rs], qf[rs], k[rs], v[rs]
                acc = jnp.zeros((HG_SUB, HG_DV), F32)
                for s in range(HG_SUB):
                    wgt = jnp.where(sub_row >= s, jnp.exp(b_i - b_i[s:s + 1]), 0.0)
                    a = jnp.sum(wgt * (q_i * k_i[s:s + 1]), axis=1, keepdims=True)
                    acc = acc + a * v_i[s:s + 1]
                if i > 0:
                    n_prev = i * HG_SUB
                    ref = b[n_prev - 1:n_prev]
                    qt = (q_i * jnp.exp(b_i - ref)).astype(BF16)
                    kt = (k[:n_prev] * jnp.exp(ref - b[:n_prev])).astype(BF16)
                    att = _dot_nt(qt, kt)
                    acc = acc + _dot(att.astype(BF16), vb[:n_prev])
                blocks.append(acc)
            o = jnp.concatenate(blocks, axis=0)

            st = st_ref[h]
            o = o + _dot_nt((qf * jnp.exp(b)).astype(BF16), st.astype(BF16))
            b_last = b[CHUNK - 1:CHUNK]
            kd = (k * jnp.exp(b_last - b)).astype(BF16)
            st_ref[h] = st * jnp.exp(b_last) + _dot_tn(vb, kd)

            gt = gate_ref[rows, sl].astype(F32)
            o_ref[rows, sl] = (_rms(o, gn_ref[:, sl]) * _silu(gt)).astype(BF16)
        return carry

    lax.fori_loop(0, n_chunks, chunk_body, 0)


def _hgrn(hg_lb_logits, hg_norm_g, hq, hf, hi, hgate, layer, batch, seq):
    t = hq.shape[0]
    lc = min(512, seq)
    nb = seq // lc
    rows = pl.BlockSpec((lc, HG_WIDTH), lambda b, j: (b * nb + j, 0))
    kern = functools.partial(_hgrn_kernel, layer=layer, n_chunks=lc // CHUNK)
    return pl.pallas_call(
        kern,
        out_shape=jax.ShapeDtypeStruct((t, HG_WIDTH), BF16),
        grid=(batch, nb),
        in_specs=[pl.BlockSpec(hg_lb_logits.shape, lambda b, j: (0, 0)),
                  pl.BlockSpec((1, HG_WIDTH), lambda b, j: (0, 0)),
                  rows, rows, rows, rows],
        out_specs=rows,
        scratch_shapes=[pltpu.VMEM((HG_HEADS, HG_DV, HG_DK), F32)],
        compiler_params=_params("parallel", "arbitrary"),
        name="hgrn2_chunkwise",
    )(hg_lb_logits, hg_norm_g[None, :], hq, hf, hi, hgate)


def _attn_kernel(qi_ref, kj_ref, q_ref, k_ref, v_ref, o_ref, m_ref, acc_ref, *, tq):
    p_id = pl.program_id(1)
    i = qi_ref[p_id]
    j = kj_ref[p_id]
    n_lane_tiles = tq // HEAD_PAD

    @pl.when(j == 0)
    def _():
        m_ref[...] = jnp.full_like(m_ref, -jnp.inf)
        acc_ref[...] = jnp.zeros_like(acc_ref)

    def step(diagonal):
        if diagonal:
            qc = lax.broadcasted_iota(jnp.int32, (tq, tq), 0) // CHUNK
            kc = lax.broadcasted_iota(jnp.int32, (tq, tq), 1) // CHUNK
            visible = kc <= qc
        for h in range(MLA_HEADS):
            sl = slice(h * HEAD_PAD, (h + 1) * HEAD_PAD)
            s = _dot_nt(q_ref[:, sl], k_ref[:, sl])
            if diagonal:
                s = jnp.where(visible, s, -jnp.inf)
            tiles = [s[:, t * HEAD_PAD:(t + 1) * HEAD_PAD] for t in range(n_lane_tiles)]
            m_tile = tiles[0]
            for t in range(1, n_lane_tiles):
                m_tile = jnp.maximum(m_tile, tiles[t])
            m_prev = m_ref[h]
            m_new = jnp.maximum(m_prev, jnp.max(m_tile, axis=1, keepdims=True))
            alpha = jnp.exp2(m_prev - m_new)
            p = jnp.concatenate([jnp.exp2((tl - m_new).astype(BF16)) for tl in tiles], axis=1)
            acc_ref[h] = acc_ref[h] * alpha + _dot(p, v_ref[:, sl])
            m_ref[h] = m_new

    @pl.when(j < i)
    def _():
        step(False)

    @pl.when(j == i)
    def _():
        step(True)
        outs = []
        for h in range(MLA_HEADS):
            acc = acc_ref[h]
            outs.append(acc[:, 0:MLA_V] / acc[:, MLA_V:MLA_V + 1])
        o_ref[...] = jnp.concatenate(outs, axis=1).astype(BF16)


def _attention(q, k, v, batch, seq):
    t, hp = q.shape
    tq = min(512, seq)
    nq = seq // tq
    pairs = [(i, j) for i in range(nq) for j in range(i + 1)]
    q_of = jnp.asarray([p[0] for p in pairs], jnp.int32)
    k_of = jnp.asarray([p[1] for p in pairs], jnp.int32)
    qspec = pl.BlockSpec((tq, hp), lambda b, p, qi, kj: (b * nq + qi[p], 0))
    kspec = pl.BlockSpec((tq, hp), lambda b, p, qi, kj: (b * nq + kj[p], 0))
    return pl.pallas_call(
        functools.partial(_attn_kernel, tq=tq),
        out_shape=jax.ShapeDtypeStruct((t, MLA_WIDTH), BF16),
        grid_spec=pltpu.PrefetchScalarGridSpec(
            num_scalar_prefetch=2,
            grid=(batch, len(pairs)),
            in_specs=[qspec, kspec, kspec],
            out_specs=pl.BlockSpec((tq, MLA_WIDTH), lambda b, p, qi, kj: (b * nq + qi[p], 0)),
            scratch_shapes=[pltpu.VMEM((MLA_HEADS, tq, HEAD_PAD), F32),
                            pltpu.VMEM((MLA_HEADS, tq, HEAD_PAD), F32)]),
        compiler_params=_params("parallel", "arbitrary"),
        name="mla_flash_attention",
    )(q_of, k_of, q, k, v)


def _route(scores, bias):
    biased = scores + bias
    col = [biased[:, e:e + 1] for e in range(N_EXPERTS)]
    gscore = []
    for g in range(N_GROUPS):
        a, b, c, d = col[g * EXPERTS_PER_GROUP:(g + 1) * EXPERTS_PER_GROUP]
        gscore.append(jnp.maximum(jnp.maximum(jnp.maximum(a + b, a + c), jnp.maximum(a + d, b + c)),
                                  jnp.maximum(b + d, c + d)))
    sel = []
    for g in range(N_GROUPS):
        ok = None
        for o in range(N_GROUPS):
            if o == g:
                continue
            cond = (gscore[g] > gscore[o]) if o < g else (gscore[g] >= gscore[o])
            ok = cond if ok is None else jnp.logical_and(ok, cond)
        grp = col[g * EXPERTS_PER_GROUP:(g + 1) * EXPERTS_PER_GROUP]
        for e in range(EXPERTS_PER_GROUP):
            beaten = jnp.zeros_like(grp[e])
            for o in range(EXPERTS_PER_GROUP):
                if o == e:
                    continue
                ahead = (grp[o] >= grp[e]) if o < e else (grp[o] > grp[e])
                beaten = beaten + jnp.where(ahead, 1.0, 0.0)
            sel.append(jnp.where(jnp.logical_and(ok, beaten < 1.5), 1.0, 0.0))
    picked = jnp.concatenate(sel, axis=1) * scores
    return picked / jnp.sum(picked, axis=1, keepdims=True)


def _merge_kernel(x_ref, oa_ref, ob_ref, g_ref, mod_ref, wa_ref, wb_ref, wo_ref, n2_ref,
                  wr_ref, rb_ref, xn_ref, h2_ref, comb_ref):
    d = D_MODEL
    mod = mod_ref[0]
    ya = _dot(oa_ref[...], wa_ref[...])
    yb = _dot(ob_ref[...], wb_ref[...])
    g = g_ref[...].astype(F32)
    merged = g[:, 0:d] * ya + g[:, d:2 * d] * yb
    xn = x_ref[...] + mod[2:3] * _dot(merged.astype(BF16), wo_ref[...])
    xn_ref[...] = xn
    h2 = _rms(xn, n2_ref[...]) * (1.0 + mod[4:5]) + mod[3:4]
    h2_ref[...] = h2.astype(BF16)
    h_hi, h_lo = _split_bf16(h2)
    w_hi, w_lo = _split_bf16(wr_ref[...])
    logits = _dot(h_hi, w_hi) + _dot(h_lo, w_hi) + _dot(h_hi, w_lo)
    comb_ref[...] = _route(_sigmoid(logits), rb_ref[...])


def _merge(x2, oa, ob, gsig, mod_l, w_br_a, w_br_b, w_out, n2, w_router, router_bias, seq):
    t, d = x2.shape
    tm = 256
    rows = lambda n: pl.BlockSpec((tm, n), lambda i: (i, 0))
    return pl.pallas_call(
        _merge_kernel,
        out_shape=(jax.ShapeDtypeStruct((t, d), F32), jax.ShapeDtypeStruct((t, d), BF16),
                   jax.ShapeDtypeStruct((t, N_EXPERTS), F32)),
        grid=(t // tm,),
        in_specs=[rows(d), rows(HG_WIDTH), rows(MLA_WIDTH), rows(2 * d),
                  pl.BlockSpec((1, 6, d), lambda i: ((i * tm) // seq, 0, 0)),
                  _resident(w_br_a.shape), _resident(w_br_b.shape), _resident(w_out.shape),
                  _resident((1, d)), _resident(w_router.shape), _resident((1, N_EXPERTS))],
        out_specs=(rows(d), rows(d), rows(N_EXPERTS)),
        compiler_params=_params("parallel"),
        name="merge_outproj_router",
    )(x2, oa, ob, gsig, mod_l, w_br_a.astype(BF16), w_br_b.astype(BF16), w_out.astype(BF16),
      n2[None, :], w_router, router_bias[None, :])


def _moe_kernel(x_ref, h_ref, comb_ref, mod_ref, wg_ref, wu_ref, wd_ref, fg_ref, o_ref, acc_ref, *, final):
    e = pl.program_id(1)

    @pl.when(e == 0)
    def _():
        acc_ref[...] = jnp.zeros_like(acc_ref)

    hb = h_ref[...]
    he = _silu(_dot(hb, wg_ref[0])) * _dot(hb, wu_ref[0])
    comb = comb_ref[...]
    lane = lax.broadcasted_iota(jnp.int32, comb.shape, 1)
    c = jnp.sum(jnp.where(lane == e, comb, 0.0), axis=1, keepdims=True)
    acc_ref[...] += c * _dot(he.astype(BF16), wd_ref[0])

    @pl.when(e == N_EXPERTS - 1)
    def _():
        xo = x_ref[...] + mod_ref[0][5:6] * acc_ref[...]
        if final:
            xo = _rms(xo, fg_ref[...])
        o_ref[...] = xo


def _moe(xn, h2, comb, mod_l, w_gate, w_up, w_down, final_g, final, seq):
    t, d = xn.shape
    tm = min(1024, seq)
    rows = lambda n: pl.BlockSpec((tm, n), lambda i, e: (i, 0))
    return pl.pallas_call(
        functools.partial(_moe_kernel, final=final),
        out_shape=jax.ShapeDtypeStruct((t, d), F32),
        grid=(t // tm, N_EXPERTS),
        in_specs=[rows(d), rows(d), rows(N_EXPERTS),
                  pl.BlockSpec((1, 6, d), lambda i, e: ((i * tm) // seq, 0, 0)),
                  pl.BlockSpec((1, d, D_EXPERT), lambda i, e: (e, 0, 0)),
                  pl.BlockSpec((1, d, D_EXPERT), lambda i, e: (e, 0, 0)),
                  pl.BlockSpec((1, D_EXPERT, d), lambda i, e: (e, 0, 0)),
                  pl.BlockSpec((1, d), lambda i, e: (0, 0))],
        out_specs=rows(d),
        scratch_shapes=[pltpu.VMEM((tm, d), F32)],
        compiler_params=_params("parallel", "arbitrary"),
        name="moe_experts",
    )(xn, h2, comb, mod_l, w_gate.astype(BF16), w_up.astype(BF16), w_down.astype(BF16), final_g[None, :])


def kernel(x, c, positions, ada_w, ada_b, norm1_g, w_in, hg_lb_logits, hg_norm_g, q_norm_g, w_q_up,
           kv_norm_g, w_kv_up, w_br_a, w_br_b, w_out, norm2_g, w_router, router_bias, w_gate, w_up,
           w_down, final_g):
    batch, seq, d = x.shape
    depth = ada_w.shape[0]
    mod = _modulation(c, ada_w, ada_b).reshape(depth, batch, 6, d)
    cos_t, sin_t = _rope_tables(positions)
    x2 = x.reshape(batch * seq, d)
    for l in range(depth):
        hq, hf, hi, hgate, gsig, q, k, v = _input_projection(
            x2, mod[l], norm1_g[l], cos_t, sin_t, w_in[l], q_norm_g[l], w_q_up[l],
            kv_norm_g[l], w_kv_up[l], seq)
        oa = _hgrn(hg_lb_logits, hg_norm_g[l], hq, hf, hi, hgate, l, batch, seq)
        ob = _attention(q, k, v, batch, seq)
        xn, h2, comb = _merge(x2, oa, ob, gsig, mod[l], w_br_a[l], w_br_b[l], w_out[l], norm2_g[l],
                              w_router, router_bias, seq)
        x2 = _moe(xn, h2, comb, mod[l], w_gate[l], w_up[l], w_down[l], final_g, l == depth - 1, seq)
    return x2.reshape(batch, seq, d)
```

```python
import functools
import math

import jax
import jax.numpy as jnp
from jax import lax
from jax.experimental import pallas as pl
from jax.experimental.pallas import tpu as pltpu

F32 = jnp.float32
BF16 = jnp.bfloat16

D_MODEL = 1024
CHUNK = 64
EPS = 1e-6

HG_HEADS = 4
HG_DK = 128
HG_DV = 128
HG_WIDTH = HG_HEADS * HG_DV
HG_SUB = 16

MLA_HEADS = 8
MLA_NOPE = 64
MLA_ROPE = 32
MLA_V = 64
MLA_Q_LORA = 384
MLA_KV_LORA = 256
MLA_DQK = MLA_NOPE + MLA_ROPE
MLA_WIDTH = MLA_HEADS * MLA_V
ROPE_BASE = 10000.0
HEAD_PAD = 128
LOG2_E = math.log2(math.e)

N_EXPERTS = 16
N_GROUPS = 4
EXPERTS_PER_GROUP = N_EXPERTS // N_GROUPS
D_EXPERT = 512

IN_SIZES = (HG_HEADS * HG_DK, HG_HEADS * HG_DK, HG_HEADS * HG_DV, HG_WIDTH,
            MLA_Q_LORA, MLA_KV_LORA, MLA_ROPE, 2 * D_MODEL)

V7X_VMEM_LIMIT_BYTES = 56 * 1024 * 1024


def _params(*sem):
    return pltpu.CompilerParams(dimension_semantics=sem, vmem_limit_bytes=V7X_VMEM_LIMIT_BYTES)


def _resident(shape):
    nd = len(shape)
    return pl.BlockSpec(shape, lambda *_: (0,) * nd, pipeline_mode=pl.Buffered(1))


def _sigmoid(x):
    return 1.0 / (1.0 + jnp.exp(-x))


def _silu(x):
    return x * _sigmoid(x)


def _dot(a, b):
    return jnp.dot(a, b, preferred_element_type=F32)


def _dot_nt(a, b):
    return lax.dot_general(a, b, (((1,), (1,)), ((), ())), preferred_element_type=F32)


def _dot_tn(a, b):
    return lax.dot_general(a, b, (((0,), (0,)), ((), ())), preferred_element_type=F32)


def _split_bf16(x):
    hi = x.astype(BF16)
    lo = (x - hi.astype(F32)).astype(BF16)
    return hi, lo


def _rms(x, g):
    return x * lax.rsqrt(jnp.mean(x * x, axis=-1, keepdims=True) + EPS) * g


def _mod_kernel(c_ref, w_ref, b_ref, o_ref):
    ca = _silu(c_ref[...])
    o_ref[0] = _dot(ca.astype(BF16), w_ref[0].astype(BF16)) + b_ref[0]


def _modulation(c, ada_w, ada_b):
    depth, d, n = ada_w.shape
    b = c.shape[0]
    tn = 1536
    return pl.pallas_call(
        _mod_kernel,
        out_shape=jax.ShapeDtypeStruct((depth, b, n), F32),
        grid=(depth, n // tn),
        in_specs=[pl.BlockSpec((b, d), lambda l, j: (0, 0)),
                  pl.BlockSpec((1, d, tn), lambda l, j: (l, 0, j)),
                  pl.BlockSpec((1, 1, tn), lambda l, j: (l, 0, j))],
        out_specs=pl.BlockSpec((1, b, tn), lambda l, j: (l, 0, j)),
        compiler_params=_params("parallel", "parallel"),
        name="adaln_modulation",
    )(c, ada_w, ada_b.reshape(depth, 1, n))


def _rope_kernel(pos_ref, inv_ref, msk_ref, sgn_ref, cos_ref, sin_ref):
    ang = pos_ref[...].astype(F32) * inv_ref[...]
    cos_ref[...] = jnp.cos(ang) * msk_ref[...]
    sin_ref[...] = jnp.sin(ang) * sgn_ref[...]


def _rope_tables(positions):
    t = positions.size
    half = MLA_ROPE // 2
    inv = ROPE_BASE ** (-jnp.arange(half, dtype=F32) / half)
    z64, z32, one16 = jnp.zeros((MLA_NOPE,), F32), jnp.zeros((32,), F32), jnp.ones((half,), F32)
    inv_row = jnp.concatenate([z64, inv, inv, z32])[None, :]
    msk_row = jnp.concatenate([z64, one16, one16, z32])[None, :]
    sgn_row = jnp.concatenate([z64, -one16, one16, z32])[None, :]
    tr = min(2048, t)
    row = pl.BlockSpec((1, HEAD_PAD), lambda i: (0, 0))
    tab = pl.BlockSpec((tr, HEAD_PAD), lambda i: (i, 0))
    return pl.pallas_call(
        _rope_kernel,
        out_shape=(jax.ShapeDtypeStruct((t, HEAD_PAD), F32),) * 2,
        grid=(t // tr,),
        in_specs=[pl.BlockSpec((tr, 1), lambda i: (i, 0)), row, row, row],
        out_specs=(tab, tab),
        compiler_params=_params("parallel"),
        name="rope_tables",
    )(positions.reshape(t, 1), inv_row, msk_row, sgn_row)


def _proj_kernel(x_ref, mod_ref, n1_ref, cos_ref, sin_ref, wh_ref, wg_ref, wc_ref,
                 qn_ref, wqa_ref, wqs_ref, kn_ref, wk_ref, wv_ref, one_ref,
                 hq_ref, hf_ref, hi_ref, hgate_ref, gsig_ref, q_ref, k_ref, v_ref):
    x = x_ref[...]
    mod = mod_ref[0]
    h = _rms(x, n1_ref[...]) * (1.0 + mod[1:2]) + mod[0:1]
    hb = h.astype(BF16)

    ph = _dot(hb, wh_ref[...])
    w = HG_WIDTH
    hq_ref[...] = ph[:, 0:w].astype(BF16)
    hf_ref[...] = ph[:, w:2 * w]
    hi_ref[...] = ph[:, 2 * w:3 * w].astype(BF16)
    hgate_ref[...] = ph[:, 3 * w:4 * w].astype(BF16)

    gsig_ref[...] = _sigmoid(_dot(hb, wg_ref[...])).astype(BF16)

    pc = _dot(hb, wc_ref[...])
    cq = pc[:, 0:MLA_Q_LORA]
    ckv = pc[:, MLA_Q_LORA:MLA_Q_LORA + MLA_KV_LORA]
    kra = pc[:, MLA_Q_LORA + MLA_KV_LORA:MLA_Q_LORA + MLA_KV_LORA + HEAD_PAD]
    krb = pc[:, MLA_Q_LORA + MLA_KV_LORA + HEAD_PAD:]

    cos_t = cos_ref[...]
    sin_t = sin_ref[...]
    lane = lax.broadcasted_iota(jnp.int32, cos_t.shape, 1)
    scale = MLA_DQK ** -0.5 * LOG2_E
    cq_tab = jnp.tile(scale * (cos_t + jnp.where(lane < MLA_NOPE, 1.0, 0.0)), (1, MLA_HEADS))
    sq_tab = jnp.tile(scale * sin_t, (1, MLA_HEADS))

    cqn = _rms(cq, qn_ref[...]).astype(BF16)
    q = _dot(cqn, wqa_ref[...]) * cq_tab + _dot(cqn, wqs_ref[...]) * sq_tab
    q_ref[...] = q.astype(BF16)

    ckvn = _rms(ckv, kn_ref[...]).astype(BF16)
    kpe = kra * cos_t + krb * sin_t
    k_ref[...] = (_dot(ckvn, wk_ref[...]) + jnp.tile(kpe, (1, MLA_HEADS))).astype(BF16)
    v_ref[...] = (_dot(ckvn, wv_ref[...]) + one_ref[...]).astype(BF16)


def _pad_heads(w, lo, hi, at):
    k, nh, _ = w.shape
    out = jnp.zeros((k, nh, HEAD_PAD), w.dtype)
    out = out.at[:, :, at:at + (hi - lo)].set(w[:, :, lo:hi])
    return out


def _input_projection(x2, mod_l, n1, cos_t, sin_t, w_in, q_norm_g, w_q_up, kv_norm_g, w_kv_up, seq):
    t, d = x2.shape
    tm = 256
    splits = [0]
    for s in IN_SIZES:
        splits.append(splits[-1] + s)
    w_h = w_in[:, splits[0]:splits[4]].astype(BF16)
    w_g = w_in[:, splits[7]:splits[8]].astype(BF16)
    kr = w_in[:, splits[6]:splits[7]]
    half = MLA_ROPE // 2
    z64 = jnp.zeros((d, MLA_NOPE), F32)
    z32 = jnp.zeros((d, HEAD_PAD - MLA_NOPE - MLA_ROPE), F32)
    kr_a = jnp.concatenate([z64, kr, z32], axis=1)
    kr_b = jnp.concatenate([z64, kr[:, half:], kr[:, :half], z32], axis=1)
    w_c = jnp.concatenate([w_in[:, splits[4]:splits[6]], kr_a, kr_b], axis=1).astype(BF16)

    wq = w_q_up.reshape(MLA_Q_LORA, MLA_HEADS, MLA_DQK)
    wq_all = _pad_heads(wq, 0, MLA_DQK, 0).reshape(MLA_Q_LORA, -1).astype(BF16)
    wq_swap = (_pad_heads(wq, MLA_NOPE + half, MLA_DQK, MLA_NOPE)
               + _pad_heads(wq, MLA_NOPE, MLA_NOPE + half, MLA_NOPE + half))
    wq_swap = wq_swap.reshape(MLA_Q_LORA, -1).astype(BF16)
    wkv = w_kv_up.reshape(MLA_KV_LORA, MLA_HEADS, MLA_NOPE + MLA_V)
    wk_all = _pad_heads(wkv, 0, MLA_NOPE, 0).reshape(MLA_KV_LORA, -1).astype(BF16)
    wv_all = _pad_heads(wkv, MLA_NOPE, MLA_NOPE + MLA_V, 0).reshape(MLA_KV_LORA, -1).astype(BF16)
    ones_row = jnp.tile(jnp.zeros((HEAD_PAD,), F32).at[MLA_V].set(1.0), MLA_HEADS)[None, :]

    hp = MLA_HEADS * HEAD_PAD
    rows = lambda n: pl.BlockSpec((tm, n), lambda i: (i, 0))
    outs = [(HG_WIDTH, BF16), (HG_WIDTH, F32), (HG_WIDTH, BF16), (HG_WIDTH, BF16),
            (2 * d, BF16), (hp, BF16), (hp, BF16), (hp, BF16)]
    return pl.pallas_call(
        _proj_kernel,
        out_shape=tuple(jax.ShapeDtypeStruct((t, n), dt) for n, dt in outs),
        grid=(t // tm,),
        in_specs=[rows(d),
                  pl.BlockSpec((1, 6, d), lambda i: ((i * tm) // seq, 0, 0)),
                  _resident((1, d)),
                  rows(HEAD_PAD), rows(HEAD_PAD),
                  _resident(w_h.shape), _resident(w_g.shape), _resident(w_c.shape),
                  _resident((1, MLA_Q_LORA)), _resident(wq_all.shape), _resident(wq_swap.shape),
                  _resident((1, MLA_KV_LORA)), _resident(wk_all.shape), _resident(wv_all.shape),
                  _resident((1, hp))],
        out_specs=tuple(rows(n) for n, _ in outs),
        compiler_params=_params("parallel"),
        name="norm_input_projection",
    )(x2, mod_l, n1[None, :], cos_t, sin_t, w_h, w_g, w_c,
      q_norm_g[None, :], wq_all, wq_swap, kv_norm_g[None, :], wk_all, wv_all, ones_row)


def _hgrn_kernel(lbl_ref, gn_ref, q_ref, f_ref, v_ref, gate_ref, o_ref, st_ref, *, layer, n_chunks):
    @pl.when(pl.program_id(1) == 0)
    def _():
        st_ref[...] = jnp.zeros_like(st_ref)

    lg = lbl_ref[...]
    ex = jnp.exp(lg - jnp.max(lg, axis=0, keepdims=True))
    soft = ex / jnp.sum(ex, axis=0, keepdims=True)
    lb_all = jnp.zeros_like(soft[0:1])
    for i in range(1, layer + 1):
        lb_all = lb_all + soft[i:i + 1]

    r_i = lax.broadcasted_iota(jnp.int32, (CHUNK, CHUNK), 0)
    c_i = lax.broadcasted_iota(jnp.int32, (CHUNK, CHUNK), 1)
    tri = jnp.where(r_i >= c_i, 1.0, 0.0).astype(BF16)
    sub_row = lax.broadcasted_iota(jnp.int32, (HG_SUB, HG_DK), 0)
    n_sub = CHUNK // HG_SUB

    def chunk_body(ci, carry):
        r0 = pl.multiple_of(ci * CHUNK, CHUNK)
        rows = pl.ds(r0, CHUNK)
        for h in range(HG_HEADS):
            sl = slice(h * HG_DK, (h + 1) * HG_DK)
            lb = lb_all[:, sl]
            q = q_ref[rows, sl].astype(F32)
            vb = v_ref[rows, sl]
            v = vb.astype(F32)
            f = lb + (1.0 - lb) * _sigmoid(f_ref[rows, sl])
            g = jnp.log(f)
            k = 1.0 - f
            qf = _silu(q) * (HG_DK ** -0.5)
            g_hi, g_lo = _split_bf16(g)
            b = _dot(tri, g_hi) + _dot(tri, g_lo)

            blocks = []
            for i in range(n_sub):
                rs = slice(i * HG_SUB, (i + 1) * HG_SUB)
                b_i, q_i, k_i, v_i = b[rs], qf[rs], k[rs], v[rs]
                acc = jnp.zeros((HG_SUB, HG_DV), F32)
                for s in range(HG_SUB):
                    wgt = jnp.where(sub_row >= s, jnp.exp(b_i - b_i[s:s + 1]), 0.0)
                    a = jnp.sum(wgt * (q_i * k_i[s:s + 1]), axis=1, keepdims=True)
                    acc = acc + a * v_i[s:s + 1]
                if i > 0:
                    n_prev = i * HG_SUB
                    ref = b[n_prev - 1:n_prev]
                    qt = (q_i * jnp.exp(b_i - ref)).astype(BF16)
                    kt = (k[:n_prev] * jnp.exp(ref - b[:n_prev])).astype(BF16)
                    att = _dot_nt(qt, kt)
                    acc = acc + _dot(att.astype(BF16), vb[:n_prev])
                blocks.append(acc)
            o = jnp.concatenate(blocks, axis=0)

            st = st_ref[h]
            o = o + _dot_nt((qf * jnp.exp(b)).astype(BF16), st.astype(BF16))
            b_last = b[CHUNK - 1:CHUNK]
            kd = (k * jnp.exp(b_last - b)).astype(BF16)
            st_ref[h] = st * jnp.exp(b_last) + _dot_tn(vb, kd)

            gt = gate_ref[rows, sl].astype(F32)
            o_ref[rows, sl] = (_rms(o, gn_ref[:, sl]) * _silu(gt)).astype(BF16)
        return carry

    lax.fori_loop(0, n_chunks, chunk_body, 0)


def _hgrn(hg_lb_logits, hg_norm_g, hq, hf, hi, hgate, layer, batch, seq):
    t = hq.shape[0]
    lc = min(512, seq)
    nb = seq // lc
    rows = pl.BlockSpec((lc, HG_WIDTH), lambda b, j: (b * nb + j, 0))
    kern = functools.partial(_hgrn_kernel, layer=layer, n_chunks=lc // CHUNK)
    return pl.pallas_call(
        kern,
        out_shape=jax.ShapeDtypeStruct((t, HG_WIDTH), BF16),
        grid=(batch, nb),
        in_specs=[pl.BlockSpec(hg_lb_logits.shape, lambda b, j: (0, 0)),
                  pl.BlockSpec((1, HG_WIDTH), lambda b, j: (0, 0)),
                  rows, rows, rows, rows],
        out_specs=rows,
        scratch_shapes=[pltpu.VMEM((HG_HEADS, HG_DV, HG_DK), F32)],
        compiler_params=_params("parallel", "arbitrary"),
        name="hgrn2_chunkwise",
    )(hg_lb_logits, hg_norm_g[None, :], hq, hf, hi, hgate)


def _attn_kernel(qi_ref, kj_ref, q_ref, k_ref, v_ref, o_ref, m_ref, acc_ref, *, tq):
    p_id = pl.program_id(1)
    i = qi_ref[p_id]
    j = kj_ref[p_id]
    n_lane_tiles = tq // HEAD_PAD

    @pl.when(j == 0)
    def _():
        m_ref[...] = jnp.full_like(m_ref, -jnp.inf)
        acc_ref[...] = jnp.zeros_like(acc_ref)

    def step(diagonal):
        if diagonal:
            qc = lax.broadcasted_iota(jnp.int32, (tq, tq), 0) // CHUNK
            kc = lax.broadcasted_iota(jnp.int32, (tq, tq), 1) // CHUNK
            visible = kc <= qc
        for h in range(MLA_HEADS):
            sl = slice(h * HEAD_PAD, (h + 1) * HEAD_PAD)
            s = _dot_nt(q_ref[:, sl], k_ref[:, sl])
            if diagonal:
                s = jnp.where(visible, s, -jnp.inf)
            tiles = [s[:, t * HEAD_PAD:(t + 1) * HEAD_PAD] for t in range(n_lane_tiles)]
            m_tile = tiles[0]
            for t in range(1, n_lane_tiles):
                m_tile = jnp.maximum(m_tile, tiles[t])
            m_prev = m_ref[h]
            m_new = jnp.maximum(m_prev, jnp.max(m_tile, axis=1, keepdims=True))
            alpha = jnp.exp2(m_prev - m_new)
            p = jnp.concatenate([jnp.exp2((tl - m_new).astype(BF16)) for tl in tiles], axis=1)
            acc_ref[h] = acc_ref[h] * alpha + _dot(p, v_ref[:, sl])
            m_ref[h] = m_new

    @pl.when(j < i)
    def _():
        step(False)

    @pl.when(j == i)
    def _():
        step(True)
        outs = []
        for h in range(MLA_HEADS):
            acc = acc_ref[h]
            outs.append(acc[:, 0:MLA_V] / acc[:, MLA_V:MLA_V + 1])
        o_ref[...] = jnp.concatenate(outs, axis=1).astype(BF16)


def _attention(q, k, v, batch, seq):
    t, hp = q.shape
    tq = min(512, seq)
    nq = seq // tq
    pairs = [(i, j) for i in range(nq) for j in range(i + 1)]
    q_of = jnp.asarray([p[0] for p in pairs], jnp.int32)
    k_of = jnp.asarray([p[1] for p in pairs], jnp.int32)
    qspec = pl.BlockSpec((tq, hp), lambda b, p, qi, kj: (b * nq + qi[p], 0))
    kspec = pl.BlockSpec((tq, hp), lambda b, p, qi, kj: (b * nq + kj[p], 0))
    return pl.pallas_call(
        functools.partial(_attn_kernel, tq=tq),
        out_shape=jax.ShapeDtypeStruct((t, MLA_WIDTH), BF16),
        grid_spec=pltpu.PrefetchScalarGridSpec(
            num_scalar_prefetch=2,
            grid=(batch, len(pairs)),
            in_specs=[qspec, kspec, kspec],
            out_specs=pl.BlockSpec((tq, MLA_WIDTH), lambda b, p, qi, kj: (b * nq + qi[p], 0)),
            scratch_shapes=[pltpu.VMEM((MLA_HEADS, tq, HEAD_PAD), F32),
                            pltpu.VMEM((MLA_HEADS, tq, HEAD_PAD), F32)]),
        compiler_params=_params("parallel", "arbitrary"),
        name="mla_flash_attention",
    )(q_of, k_of, q, k, v)


PAIRS_PER_GROUP = EXPERTS_PER_GROUP * (EXPERTS_PER_GROUP - 1) // 2
N_CLASSES = N_GROUPS * PAIRS_PER_GROUP
ROUTE_LANES = 128
ROW_WIDTH = D_MODEL + ROUTE_LANES
TOKEN_TILE = 256
EXPERT_TILE = 256
DMA_WINDOW = 256


def _route(scores, bias):
    biased = scores + bias
    col = [biased[:, e:e + 1] for e in range(N_EXPERTS)]
    gscore = []
    for g in range(N_GROUPS):
        a, b, c, d = col[g * EXPERTS_PER_GROUP:(g + 1) * EXPERTS_PER_GROUP]
        gscore.append(jnp.maximum(jnp.maximum(jnp.maximum(a + b, a + c), jnp.maximum(a + d, b + c)),
                                  jnp.maximum(b + d, c + d)))
    sel = []
    for g in range(N_GROUPS):
        ok = None
        for o in range(N_GROUPS):
            if o == g:
                continue
            cond = (gscore[g] > gscore[o]) if o < g else (gscore[g] >= gscore[o])
            ok = cond if ok is None else jnp.logical_and(ok, cond)
        grp = col[g * EXPERTS_PER_GROUP:(g + 1) * EXPERTS_PER_GROUP]
        for e in range(EXPERTS_PER_GROUP):
            beaten = jnp.zeros_like(grp[e])
            for o in range(EXPERTS_PER_GROUP):
                if o == e:
                    continue
                ahead = (grp[o] >= grp[e]) if o < e else (grp[o] > grp[e])
                beaten = beaten + jnp.where(ahead, 1.0, 0.0)
            sel.append(jnp.logical_and(ok, beaten < 1.5))
    lo = jnp.full_like(col[0], float(N_EXPERTS))
    hi = jnp.full_like(col[0], -1.0)
    for e in range(N_EXPERTS):
        lo = jnp.where(sel[e], jnp.minimum(lo, float(e)), lo)
        hi = jnp.where(sel[e], jnp.maximum(hi, float(e)), hi)
    s_lo = jnp.zeros_like(col[0])
    s_hi = jnp.zeros_like(col[0])
    for e in range(N_EXPERTS):
        s_e = scores[:, e:e + 1]
        s_lo = s_lo + jnp.where(lo == float(e), s_e, 0.0)
        s_hi = s_hi + jnp.where(hi == float(e), s_e, 0.0)
    total = s_lo + s_hi
    return lo, hi, s_lo / total, s_hi / total


def _merge_kernel(x_ref, oa_ref, ob_ref, g_ref, mod_ref, wa_ref, wb_ref, wo_ref, n2_ref,
                  wr_ref, rb_ref, xn_ref, row_ref, meta_ref, cnt_ref):
    d = D_MODEL
    tm = x_ref.shape[0]
    mod = mod_ref[0]
    ya = _dot(oa_ref[...], wa_ref[...])
    yb = _dot(ob_ref[...], wb_ref[...])
    g = g_ref[...].astype(F32)
    merged = g[:, 0:d] * ya + g[:, d:2 * d] * yb
    xn = x_ref[...] + mod[2:3] * _dot(merged.astype(BF16), wo_ref[...])
    xn_ref[...] = xn
    h2 = _rms(xn, n2_ref[...]) * (1.0 + mod[4:5]) + mod[3:4]
    h_hi, h_lo = _split_bf16(h2)
    w_hi, w_lo = _split_bf16(wr_ref[...])
    logits = _dot(h_hi, w_hi) + _dot(h_lo, w_hi) + _dot(h_hi, w_lo)
    lo, hi, w_lo_e, w_hi_e = _route(_sigmoid(logits), rb_ref[...])

    lane = lax.broadcasted_iota(jnp.int32, (tm, ROUTE_LANES), 1)
    row_ref[:, 0:d] = h2
    row_ref[:, d:d + ROUTE_LANES] = jnp.where(lane == 0, w_lo_e, jnp.where(lane == 1, w_hi_e, 0.0))

    grp = jnp.floor(lo * (1.0 / EXPERTS_PER_GROUP))
    a = lo - grp * EXPERTS_PER_GROUP
    b = hi - grp * EXPERTS_PER_GROUP
    cls = grp * PAIRS_PER_GROUP + a * (2 * EXPERTS_PER_GROUP - 1 - a) * 0.5 + (b - a - 1.0)
    onehot = jnp.where(lane.astype(F32) == cls, 1.0, 0.0)
    r_i = lax.broadcasted_iota(jnp.int32, (tm, tm), 0)
    c_i = lax.broadcasted_iota(jnp.int32, (tm, tm), 1)
    before = jnp.where(r_i > c_i, 1.0, 0.0).astype(BF16)
    rank = jnp.sum(onehot * _dot(before, onehot.astype(BF16)), axis=1, keepdims=True)
    lane_m = lax.broadcasted_iota(jnp.int32, (tm, N_EXPERTS), 1)
    meta_ref[...] = jnp.where(lane_m == 0, cls, jnp.where(lane_m == 1, rank, 0.0))
    cnt_ref[...] = jnp.broadcast_to(jnp.sum(onehot, axis=0, keepdims=True), cnt_ref.shape)


def _merge(x2, oa, ob, gsig, mod_l, w_br_a, w_br_b, w_out, n2, w_router, router_bias, seq):
    t, d = x2.shape
    tm = TOKEN_TILE
    rows = lambda n: pl.BlockSpec((tm, n), lambda i: (i, 0))
    return pl.pallas_call(
        _merge_kernel,
        out_shape=(jax.ShapeDtypeStruct((t, d), F32), jax.ShapeDtypeStruct((t, ROW_WIDTH), F32),
                   jax.ShapeDtypeStruct((t, N_EXPERTS), F32),
                   jax.ShapeDtypeStruct((t // tm * 8, ROUTE_LANES), F32)),
        grid=(t // tm,),
        in_specs=[rows(d), rows(HG_WIDTH), rows(MLA_WIDTH), rows(2 * d),
                  pl.BlockSpec((1, 6, d), lambda i: ((i * tm) // seq, 0, 0)),
                  _resident(w_br_a.shape), _resident(w_br_b.shape), _resident(w_out.shape),
                  _resident((1, d)), _resident(w_router.shape), _resident((1, N_EXPERTS))],
        out_specs=(rows(d), rows(ROW_WIDTH), rows(N_EXPERTS),
                   pl.BlockSpec((8, ROUTE_LANES), lambda i: (i, 0))),
        compiler_params=_params("parallel"),
        name="merge_outproj_router",
    )(x2, oa, ob, gsig, mod_l, w_br_a.astype(BF16), w_br_b.astype(BF16), w_out.astype(BF16),
      n2[None, :], w_router, router_bias[None, :])


def _dispatch_plan(meta, cnt, n_tiles_e):
    t = meta.shape[0]
    n_tok_tiles = t // TOKEN_TILE
    cls = meta[:, 0].astype(jnp.int32)
    rank = meta[:, 1].astype(jnp.int32)
    counts = cnt.reshape(n_tok_tiles, 8, ROUTE_LANES)[:, 0, :N_CLASSES].astype(jnp.int32)
    total = jnp.sum(counts, axis=0)
    total_pad = (total + EXPERT_TILE - 1) // EXPERT_TILE * EXPERT_TILE
    ends = jnp.cumsum(total_pad)
    base = (ends - total_pad)[None, :] + jnp.cumsum(counts, axis=0) - counts
    tok_tile = jnp.arange(t, dtype=jnp.int32) // TOKEN_TILE
    pos = jnp.take(base.reshape(-1), tok_tile * N_CLASSES + cls) + rank

    tile_start = jnp.arange(n_tiles_e, dtype=jnp.int32) * EXPERT_TILE
    n_valid = ends[-1] // EXPERT_TILE
    tile_cls = jnp.sum((tile_start[:, None] >= ends[None, :]).astype(jnp.int32), axis=1)
    last_cls = jnp.take(tile_cls, n_valid - 1)
    valid = jnp.arange(n_tiles_e, dtype=jnp.int32) < n_valid
    tile_cls = jnp.where(valid, tile_cls, last_cls)
    pair_lo = jnp.asarray([a for a in range(EXPERTS_PER_GROUP) for b in range(a + 1, EXPERTS_PER_GROUP)], jnp.int32)
    pair_hi = jnp.asarray([b for a in range(EXPERTS_PER_GROUP) for b in range(a + 1, EXPERTS_PER_GROUP)], jnp.int32)
    grp = tile_cls // PAIRS_PER_GROUP
    e_lo = grp * EXPERTS_PER_GROUP + jnp.take(pair_lo, tile_cls % PAIRS_PER_GROUP)
    e_hi = grp * EXPERTS_PER_GROUP + jnp.take(pair_hi, tile_cls % PAIRS_PER_GROUP)
    return pos.astype(jnp.int32), e_lo, e_hi, valid.astype(jnp.int32)


def _permute_kernel(pos_ref, src_hbm, *rest, scatter, n_rows):
    dst_hbm, sem = rest[-2], rest[-1]
    unroll = 8

    def row_copy(t):
        p = pos_ref[t]
        if scatter:
            return pltpu.make_async_copy(src_hbm.at[pl.ds(t, 1)], dst_hbm.at[pl.ds(p, 1)], sem)
        return pltpu.make_async_copy(src_hbm.at[pl.ds(p, 1)], dst_hbm.at[pl.ds(t, 1)], sem)

    def wait_window():
        pltpu.make_async_copy(src_hbm.at[pl.ds(0, DMA_WINDOW)], dst_hbm.at[pl.ds(0, DMA_WINDOW)], sem).wait()

    def window_body(wi, carry):
        def issue(ui, c):
            for u in range(unroll):
                row_copy(wi * DMA_WINDOW + ui * unroll + u).start()
            return c
        lax.fori_loop(0, DMA_WINDOW // unroll, issue, 0)

        @pl.when(wi > 0)
        def _():
            wait_window()
        return carry

    lax.fori_loop(0, n_rows // DMA_WINDOW, window_body, 0)
    wait_window()


def _permute_rows(pos, src, n_dst, scatter):
    n_rows = pos.shape[0]
    width = src.shape[1]
    any_spec = pl.BlockSpec(memory_space=pl.ANY)
    operands = [pos, src]
    in_specs = [any_spec]
    aliases = {}
    if scatter:
        operands.append(jnp.zeros((n_dst, width), src.dtype))
        in_specs.append(any_spec)
        aliases = {2: 0}
    return pl.pallas_call(
        functools.partial(_permute_kernel, scatter=scatter, n_rows=n_rows),
        out_shape=jax.ShapeDtypeStruct((n_dst, width), src.dtype),
        grid_spec=pltpu.PrefetchScalarGridSpec(
            num_scalar_prefetch=1, grid=(1,), in_specs=in_specs, out_specs=any_spec,
            scratch_shapes=[pltpu.SemaphoreType.DMA]),
        input_output_aliases=aliases,
        compiler_params=pltpu.CompilerParams(dimension_semantics=("arbitrary",), disable_bounds_checks=True,
                                             has_side_effects=True),
        name="dispatch_rows" if scatter else "collect_rows",
    )(*operands)


def _moe_kernel(elo_ref, ehi_ref, valid_ref, x_ref, wgl_ref, wul_ref, wdl_ref, wgh_ref, wuh_ref, wdh_ref, y_ref):
    j = pl.program_id(0)

    @pl.when(valid_ref[j] == 0)
    def _():
        y_ref[...] = jnp.zeros_like(y_ref)

    @pl.when(valid_ref[j] != 0)
    def _():
        hb = x_ref[:, 0:D_MODEL].astype(BF16)
        route = x_ref[:, D_MODEL:D_MODEL + ROUTE_LANES]
        he_lo = _silu(_dot(hb, wgl_ref[0])) * _dot(hb, wul_ref[0])
        y = route[:, 0:1] * _dot(he_lo.astype(BF16), wdl_ref[0])
        he_hi = _silu(_dot(hb, wgh_ref[0])) * _dot(hb, wuh_ref[0])
        y_ref[...] = y + route[:, 1:2] * _dot(he_hi.astype(BF16), wdh_ref[0])


def _moe(rows_sorted, e_lo, e_hi, valid, w_gate, w_up, w_down):
    n_pad = rows_sorted.shape[0]
    d = D_MODEL
    lo3 = lambda j, el, eh, va: (el[j], 0, 0)
    hi3 = lambda j, el, eh, va: (eh[j], 0, 0)
    wg, wu, wd = w_gate.astype(BF16), w_up.astype(BF16), w_down.astype(BF16)
    return pl.pallas_call(
        _moe_kernel,
        out_shape=jax.ShapeDtypeStruct((n_pad, d), F32),
        grid_spec=pltpu.PrefetchScalarGridSpec(
            num_scalar_prefetch=3, grid=(n_pad // EXPERT_TILE,),
            in_specs=[pl.BlockSpec((EXPERT_TILE, ROW_WIDTH), lambda j, el, eh, va: (j, 0)),
                      pl.BlockSpec((1, d, D_EXPERT), lo3), pl.BlockSpec((1, d, D_EXPERT), lo3),
                      pl.BlockSpec((1, D_EXPERT, d), lo3),
                      pl.BlockSpec((1, d, D_EXPERT), hi3), pl.BlockSpec((1, d, D_EXPERT), hi3),
                      pl.BlockSpec((1, D_EXPERT, d), hi3)],
            out_specs=pl.BlockSpec((EXPERT_TILE, d), lambda j, el, eh, va: (j, 0))),
        compiler_params=_params("arbitrary"),
        name="moe_experts",
    )(e_lo, e_hi, valid, rows_sorted, wg, wu, wd, wg, wu, wd)


def _residual_kernel(x_ref, y_ref, mod_ref, fg_ref, o_ref, *, final):
    xo = x_ref[...] + mod_ref[0][5:6] * y_ref[...]
    if final:
        xo = _rms(xo, fg_ref[...])
    o_ref[...] = xo


def _moe_residual(xn, y, mod_l, final_g, final, seq):
    t, d = xn.shape
    tm = min(512, seq)
    rows = pl.BlockSpec((tm, d), lambda i: (i, 0))
    return pl.pallas_call(
        functools.partial(_residual_kernel, final=final),
        out_shape=jax.ShapeDtypeStruct((t, d), F32),
        grid=(t // tm,),
        in_specs=[rows, rows, pl.BlockSpec((1, 6, d), lambda i: ((i * tm) // seq, 0, 0)),
                  pl.BlockSpec((1, d), lambda i: (0, 0))],
        out_specs=rows,
        compiler_params=_params("parallel"),
        name="moe_residual",
    )(xn, y, mod_l, final_g[None, :])


def kernel(x, c, positions, ada_w, ada_b, norm1_g, w_in, hg_lb_logits, hg_norm_g, q_norm_g, w_q_up,
           kv_norm_g, w_kv_up, w_br_a, w_br_b, w_out, norm2_g, w_router, router_bias, w_gate, w_up,
           w_down, final_g):
    batch, seq, d = x.shape
    depth = ada_w.shape[0]
    t = batch * seq
    n_tiles_e = (t + N_CLASSES * (EXPERT_TILE - 1)) // EXPERT_TILE
    mod = _modulation(c, ada_w, ada_b).reshape(depth, batch, 6, d)
    cos_t, sin_t = _rope_tables(positions)
    x2 = x.reshape(t, d)
    for l in range(depth):
        hq, hf, hi, hgate, gsig, q, k, v = _input_projection(
            x2, mod[l], norm1_g[l], cos_t, sin_t, w_in[l], q_norm_g[l], w_q_up[l],
            kv_norm_g[l], w_kv_up[l], seq)
        oa = _hgrn(hg_lb_logits, hg_norm_g[l], hq, hf, hi, hgate, l, batch, seq)
        ob = _attention(q, k, v, batch, seq)
        xn, rows, meta, cnt = _merge(x2, oa, ob, gsig, mod[l], w_br_a[l], w_br_b[l], w_out[l], norm2_g[l],
                                     w_router, router_bias, seq)
        pos, e_lo, e_hi, valid = _dispatch_plan(meta, cnt, n_tiles_e)
        rows_sorted = _permute_rows(pos, rows, n_tiles_e * EXPERT_TILE, scatter=True)
        y_sorted = _moe(rows_sorted, e_lo, e_hi, valid, w_gate[l], w_up[l], w_down[l])
        y = _permute_rows(pos, y_sorted, t, scatter=False)
        x2 = _moe_residual(xn, y, mod[l], final_g, l == depth - 1, seq)
    return x2.reshape(batch, seq, d)
```

```python
import functools
import math

import jax
import jax.numpy as jnp
from jax import lax
from jax.experimental import pallas as pl
from jax.experimental.pallas import tpu as pltpu

F32 = jnp.float32
BF16 = jnp.bfloat16

D_MODEL = 1024
CHUNK = 64
EPS = 1e-6

HG_HEADS = 4
HG_DK = 128
HG_DV = 128
HG_WIDTH = HG_HEADS * HG_DV
HG_SUB = 16

MLA_HEADS = 8
MLA_NOPE = 64
MLA_ROPE = 32
MLA_V = 64
MLA_Q_LORA = 384
MLA_KV_LORA = 256
MLA_DQK = MLA_NOPE + MLA_ROPE
MLA_WIDTH = MLA_HEADS * MLA_V
ROPE_BASE = 10000.0
HEAD_PAD = 128
LOG2_E = math.log2(math.e)

N_EXPERTS = 16
N_GROUPS = 4
EXPERTS_PER_GROUP = N_EXPERTS // N_GROUPS
D_EXPERT = 512

IN_SIZES = (HG_HEADS * HG_DK, HG_HEADS * HG_DK, HG_HEADS * HG_DV, HG_WIDTH,
            MLA_Q_LORA, MLA_KV_LORA, MLA_ROPE, 2 * D_MODEL)

V7X_VMEM_LIMIT_BYTES = 56 * 1024 * 1024


def _params(*sem):
    return pltpu.CompilerParams(dimension_semantics=sem, vmem_limit_bytes=V7X_VMEM_LIMIT_BYTES)


def _resident(shape):
    nd = len(shape)
    return pl.BlockSpec(shape, lambda *_: (0,) * nd, pipeline_mode=pl.Buffered(1))


def _sigmoid(x):
    return 1.0 / (1.0 + jnp.exp(-x))


def _silu(x):
    return x * _sigmoid(x)


def _dot(a, b):
    return jnp.dot(a, b, preferred_element_type=F32)


def _dot_nt(a, b):
    return lax.dot_general(a, b, (((1,), (1,)), ((), ())), preferred_element_type=F32)


def _dot_tn(a, b):
    return lax.dot_general(a, b, (((0,), (0,)), ((), ())), preferred_element_type=F32)


def _split_bf16(x):
    hi = x.astype(BF16)
    lo = (x - hi.astype(F32)).astype(BF16)
    return hi, lo


def _rms(x, g):
    return x * lax.rsqrt(jnp.mean(x * x, axis=-1, keepdims=True) + EPS) * g


def _mod_kernel(c_ref, w_ref, b_ref, o_ref):
    ca = _silu(c_ref[...])
    o_ref[0] = _dot(ca.astype(BF16), w_ref[0].astype(BF16)) + b_ref[0]


def _modulation(c, ada_w, ada_b):
    depth, d, n = ada_w.shape
    b = c.shape[0]
    tn = 1536
    return pl.pallas_call(
        _mod_kernel,
        out_shape=jax.ShapeDtypeStruct((depth, b, n), F32),
        grid=(depth, n // tn),
        in_specs=[pl.BlockSpec((b, d), lambda l, j: (0, 0)),
                  pl.BlockSpec((1, d, tn), lambda l, j: (l, 0, j)),
                  pl.BlockSpec((1, 1, tn), lambda l, j: (l, 0, j))],
        out_specs=pl.BlockSpec((1, b, tn), lambda l, j: (l, 0, j)),
        compiler_params=_params("parallel", "parallel"),
        name="adaln_modulation",
    )(c, ada_w, ada_b.reshape(depth, 1, n))


def _rope_kernel(pos_ref, inv_ref, msk_ref, sgn_ref, cos_ref, sin_ref):
    ang = pos_ref[...].astype(F32) * inv_ref[...]
    cos_ref[...] = jnp.cos(ang) * msk_ref[...]
    sin_ref[...] = jnp.sin(ang) * sgn_ref[...]


def _rope_tables(positions):
    t = positions.size
    half = MLA_ROPE // 2
    inv = ROPE_BASE ** (-jnp.arange(half, dtype=F32) / half)
    z64, z32, one16 = jnp.zeros((MLA_NOPE,), F32), jnp.zeros((32,), F32), jnp.ones((half,), F32)
    inv_row = jnp.concatenate([z64, inv, inv, z32])[None, :]
    msk_row = jnp.concatenate([z64, one16, one16, z32])[None, :]
    sgn_row = jnp.concatenate([z64, -one16, one16, z32])[None, :]
    tr = min(2048, t)
    row = pl.BlockSpec((1, HEAD_PAD), lambda i: (0, 0))
    tab = pl.BlockSpec((tr, HEAD_PAD), lambda i: (i, 0))
    return pl.pallas_call(
        _rope_kernel,
        out_shape=(jax.ShapeDtypeStruct((t, HEAD_PAD), F32),) * 2,
        grid=(t // tr,),
        in_specs=[pl.BlockSpec((tr, 1), lambda i: (i, 0)), row, row, row],
        out_specs=(tab, tab),
        compiler_params=_params("parallel"),
        name="rope_tables",
    )(positions.reshape(t, 1), inv_row, msk_row, sgn_row)


def _proj_kernel(x_ref, mod_ref, n1_ref, cos_ref, sin_ref, wh_ref, wg_ref, wc_ref,
                 qn_ref, wqa_ref, wqs_ref, kn_ref, wk_ref, wv_ref, one_ref,
                 hq_ref, hf_ref, hi_ref, hgate_ref, gsig_ref, q_ref, k_ref, v_ref):
    x = x_ref[...]
    mod = mod_ref[0]
    h = _rms(x, n1_ref[...]) * (1.0 + mod[1:2]) + mod[0:1]
    hb = h.astype(BF16)

    ph = _dot(hb, wh_ref[...])
    w = HG_WIDTH
    hq_ref[...] = ph[:, 0:w].astype(BF16)
    hf_ref[...] = ph[:, w:2 * w]
    hi_ref[...] = ph[:, 2 * w:3 * w].astype(BF16)
    hgate_ref[...] = ph[:, 3 * w:4 * w].astype(BF16)

    gsig_ref[...] = _sigmoid(_dot(hb, wg_ref[...])).astype(BF16)

    pc = _dot(hb, wc_ref[...])
    cq = pc[:, 0:MLA_Q_LORA]
    ckv = pc[:, MLA_Q_LORA:MLA_Q_LORA + MLA_KV_LORA]
    kra = pc[:, MLA_Q_LORA + MLA_KV_LORA:MLA_Q_LORA + MLA_KV_LORA + HEAD_PAD]
    krb = pc[:, MLA_Q_LORA + MLA_KV_LORA + HEAD_PAD:]

    cos_t = cos_ref[...]
    sin_t = sin_ref[...]
    lane = lax.broadcasted_iota(jnp.int32, cos_t.shape, 1)
    scale = MLA_DQK ** -0.5 * LOG2_E
    cq_tab = jnp.tile(scale * (cos_t + jnp.where(lane < MLA_NOPE, 1.0, 0.0)), (1, MLA_HEADS))
    sq_tab = jnp.tile(scale * sin_t, (1, MLA_HEADS))

    cqn = _rms(cq, qn_ref[...]).astype(BF16)
    q = _dot(cqn, wqa_ref[...]) * cq_tab + _dot(cqn, wqs_ref[...]) * sq_tab
    q_ref[...] = q.astype(BF16)

    ckvn = _rms(ckv, kn_ref[...]).astype(BF16)
    kpe = kra * cos_t + krb * sin_t
    k_ref[...] = (_dot(ckvn, wk_ref[...]) + jnp.tile(kpe, (1, MLA_HEADS))).astype(BF16)
    v_ref[...] = (_dot(ckvn, wv_ref[...]) + one_ref[...]).astype(BF16)


def _pad_heads(w, lo, hi, at):
    k, nh, _ = w.shape
    out = jnp.zeros((k, nh, HEAD_PAD), w.dtype)
    out = out.at[:, :, at:at + (hi - lo)].set(w[:, :, lo:hi])
    return out


def _input_projection(x2, mod_l, n1, cos_t, sin_t, w_in, q_norm_g, w_q_up, kv_norm_g, w_kv_up, seq):
    t, d = x2.shape
    tm = 256
    splits = [0]
    for s in IN_SIZES:
        splits.append(splits[-1] + s)
    w_h = w_in[:, splits[0]:splits[4]].astype(BF16)
    w_g = w_in[:, splits[7]:splits[8]].astype(BF16)
    kr = w_in[:, splits[6]:splits[7]]
    half = MLA_ROPE // 2
    z64 = jnp.zeros((d, MLA_NOPE), F32)
    z32 = jnp.zeros((d, HEAD_PAD - MLA_NOPE - MLA_ROPE), F32)
    kr_a = jnp.concatenate([z64, kr, z32], axis=1)
    kr_b = jnp.concatenate([z64, kr[:, half:], kr[:, :half], z32], axis=1)
    w_c = jnp.concatenate([w_in[:, splits[4]:splits[6]], kr_a, kr_b], axis=1).astype(BF16)

    wq = w_q_up.reshape(MLA_Q_LORA, MLA_HEADS, MLA_DQK)
    wq_all = _pad_heads(wq, 0, MLA_DQK, 0).reshape(MLA_Q_LORA, -1).astype(BF16)
    wq_swap = (_pad_heads(wq, MLA_NOPE + half, MLA_DQK, MLA_NOPE)
               + _pad_heads(wq, MLA_NOPE, MLA_NOPE + half, MLA_NOPE + half))
    wq_swap = wq_swap.reshape(MLA_Q_LORA, -1).astype(BF16)
    wkv = w_kv_up.reshape(MLA_KV_LORA, MLA_HEADS, MLA_NOPE + MLA_V)
    wk_all = _pad_heads(wkv, 0, MLA_NOPE, 0).reshape(MLA_KV_LORA, -1).astype(BF16)
    wv_all = _pad_heads(wkv, MLA_NOPE, MLA_NOPE + MLA_V, 0).reshape(MLA_KV_LORA, -1).astype(BF16)
    ones_row = jnp.tile(jnp.zeros((HEAD_PAD,), F32).at[MLA_V].set(1.0), MLA_HEADS)[None, :]

    hp = MLA_HEADS * HEAD_PAD
    rows = lambda n: pl.BlockSpec((tm, n), lambda i: (i, 0))
    outs = [(HG_WIDTH, BF16), (HG_WIDTH, F32), (HG_WIDTH, BF16), (HG_WIDTH, BF16),
            (2 * d, BF16), (hp, BF16), (hp, BF16), (hp, BF16)]
    return pl.pallas_call(
        _proj_kernel,
        out_shape=tuple(jax.ShapeDtypeStruct((t, n), dt) for n, dt in outs),
        grid=(t // tm,),
        in_specs=[rows(d),
                  pl.BlockSpec((1, 6, d), lambda i: ((i * tm) // seq, 0, 0)),
                  _resident((1, d)),
                  rows(HEAD_PAD), rows(HEAD_PAD),
                  _resident(w_h.shape), _resident(w_g.shape), _resident(w_c.shape),
                  _resident((1, MLA_Q_LORA)), _resident(wq_all.shape), _resident(wq_swap.shape),
                  _resident((1, MLA_KV_LORA)), _resident(wk_all.shape), _resident(wv_all.shape),
                  _resident((1, hp))],
        out_specs=tuple(rows(n) for n, _ in outs),
        compiler_params=_params("parallel"),
        name="norm_input_projection",
    )(x2, mod_l, n1[None, :], cos_t, sin_t, w_h, w_g, w_c,
      q_norm_g[None, :], wq_all, wq_swap, kv_norm_g[None, :], wk_all, wv_all, ones_row)


def _hgrn_kernel(lbl_ref, gn_ref, q_ref, f_ref, v_ref, gate_ref, o_ref, st_ref, *, layer, n_chunks):
    @pl.when(pl.program_id(1) == 0)
    def _():
        st_ref[...] = jnp.zeros_like(st_ref)

    lg = lbl_ref[...]
    ex = jnp.exp(lg - jnp.max(lg, axis=0, keepdims=True))
    soft = ex / jnp.sum(ex, axis=0, keepdims=True)
    lb_all = jnp.zeros_like(soft[0:1])
    for i in range(1, layer + 1):
        lb_all = lb_all + soft[i:i + 1]

    r_i = lax.broadcasted_iota(jnp.int32, (CHUNK, CHUNK), 0)
    c_i = lax.broadcasted_iota(jnp.int32, (CHUNK, CHUNK), 1)
    tri = jnp.where(r_i >= c_i, 1.0, 0.0).astype(BF16)
    sub_row = lax.broadcasted_iota(jnp.int32, (HG_SUB, HG_DK), 0)
    n_sub = CHUNK // HG_SUB

    def chunk_body(ci, carry):
        r0 = pl.multiple_of(ci * CHUNK, CHUNK)
        rows = pl.ds(r0, CHUNK)
        for h in range(HG_HEADS):
            sl = slice(h * HG_DK, (h + 1) * HG_DK)
            lb = lb_all[:, sl]
            q = q_ref[rows, sl].astype(F32)
            vb = v_ref[rows, sl]
            v = vb.astype(F32)
            f = lb + (1.0 - lb) * _sigmoid(f_ref[rows, sl])
            g = jnp.log(f)
            k = 1.0 - f
            qf = _silu(q) * (HG_DK ** -0.5)
            g_hi, g_lo = _split_bf16(g)
            b = _dot(tri, g_hi) + _dot(tri, g_lo)

            blocks = []
            for i in range(n_sub):
                rs = slice(i * HG_SUB, (i + 1) * HG_SUB)
                b_i, q_i, k_i, v_i = b[rs], qf[rs], k[rs], v[rs]
                acc = jnp.zeros((HG_SUB, HG_DV), F32)
                for s in range(HG_SUB):
                    wgt = jnp.where(sub_row >= s, jnp.exp(b_i - b_i[s:s + 1]), 0.0)
                    a = jnp.sum(wgt * (q_i * k_i[s:s + 1]), axis=1, keepdims=True)
                    acc = acc + a * v_i[s:s + 1]
                if i > 0:
                    n_prev = i * HG_SUB
                    ref = b[n_prev - 1:n_prev]
                    qt = (q_i * jnp.exp(b_i - ref)).astype(BF16)
                    kt = (k[:n_prev] * jnp.exp(ref - b[:n_prev])).astype(BF16)
                    att = _dot_nt(qt, kt)
                    acc = acc + _dot(att.astype(BF16), vb[:n_prev])
                blocks.append(acc)
            o = jnp.concatenate(blocks, axis=0)

            st = st_ref[h]
            o = o + _dot_nt((qf * jnp.exp(b)).astype(BF16), st.astype(BF16))
            b_last = b[CHUNK - 1:CHUNK]
            kd = (k * jnp.exp(b_last - b)).astype(BF16)
            st_ref[h] = st * jnp.exp(b_last) + _dot_tn(vb, kd)

            gt = gate_ref[rows, sl].astype(F32)
            o_ref[rows, sl] = (_rms(o, gn_ref[:, sl]) * _silu(gt)).astype(BF16)
        return carry

    lax.fori_loop(0, n_chunks, chunk_body, 0)


def _hgrn(hg_lb_logits, hg_norm_g, hq, hf, hi, hgate, layer, batch, seq):
    t = hq.shape[0]
    lc = min(512, seq)
    nb = seq // lc
    rows = pl.BlockSpec((lc, HG_WIDTH), lambda b, j: (b * nb + j, 0))
    kern = functools.partial(_hgrn_kernel, layer=layer, n_chunks=lc // CHUNK)
    return pl.pallas_call(
        kern,
        out_shape=jax.ShapeDtypeStruct((t, HG_WIDTH), BF16),
        grid=(batch, nb),
        in_specs=[pl.BlockSpec(hg_lb_logits.shape, lambda b, j: (0, 0)),
                  pl.BlockSpec((1, HG_WIDTH), lambda b, j: (0, 0)),
                  rows, rows, rows, rows],
        out_specs=rows,
        scratch_shapes=[pltpu.VMEM((HG_HEADS, HG_DV, HG_DK), F32)],
        compiler_params=_params("parallel", "arbitrary"),
        name="hgrn2_chunkwise",
    )(hg_lb_logits, hg_norm_g[None, :], hq, hf, hi, hgate)


def _attn_kernel(qi_ref, kj_ref, q_ref, k_ref, v_ref, o_ref, m_ref, acc_ref, *, tq):
    p_id = pl.program_id(1)
    i = qi_ref[p_id]
    j = kj_ref[p_id]
    n_lane_tiles = tq // HEAD_PAD

    @pl.when(j == 0)
    def _():
        m_ref[...] = jnp.full_like(m_ref, -jnp.inf)
        acc_ref[...] = jnp.zeros_like(acc_ref)

    def step(diagonal):
        if diagonal:
            qc = lax.broadcasted_iota(jnp.int32, (tq, tq), 0) // CHUNK
            kc = lax.broadcasted_iota(jnp.int32, (tq, tq), 1) // CHUNK
            visible = kc <= qc
        for h in range(MLA_HEADS):
            sl = slice(h * HEAD_PAD, (h + 1) * HEAD_PAD)
            s = _dot_nt(q_ref[:, sl], k_ref[:, sl])
            if diagonal:
                s = jnp.where(visible, s, -jnp.inf)
            tiles = [s[:, t * HEAD_PAD:(t + 1) * HEAD_PAD] for t in range(n_lane_tiles)]
            m_tile = tiles[0]
            for t in range(1, n_lane_tiles):
                m_tile = jnp.maximum(m_tile, tiles[t])
            m_prev = m_ref[h]
            m_new = jnp.maximum(m_prev, jnp.max(m_tile, axis=1, keepdims=True))
            alpha = jnp.exp2(m_prev - m_new)
            p = jnp.concatenate([jnp.exp2((tl - m_new).astype(BF16)) for tl in tiles], axis=1)
            acc_ref[h] = acc_ref[h] * alpha + _dot(p, v_ref[:, sl])
            m_ref[h] = m_new

    @pl.when(j < i)
    def _():
        step(False)

    @pl.when(j == i)
    def _():
        step(True)
        outs = []
        for h in range(MLA_HEADS):
            acc = acc_ref[h]
            outs.append(acc[:, 0:MLA_V] / acc[:, MLA_V:MLA_V + 1])
        o_ref[...] = jnp.concatenate(outs, axis=1).astype(BF16)


def _attention(q, k, v, batch, seq):
    t, hp = q.shape
    tq = min(512, seq)
    nq = seq // tq
    pairs = [(i, j) for i in range(nq) for j in range(i + 1)]
    q_of = jnp.asarray([p[0] for p in pairs], jnp.int32)
    k_of = jnp.asarray([p[1] for p in pairs], jnp.int32)
    qspec = pl.BlockSpec((tq, hp), lambda b, p, qi, kj: (b * nq + qi[p], 0))
    kspec = pl.BlockSpec((tq, hp), lambda b, p, qi, kj: (b * nq + kj[p], 0))
    return pl.pallas_call(
        functools.partial(_attn_kernel, tq=tq),
        out_shape=jax.ShapeDtypeStruct((t, MLA_WIDTH), BF16),
        grid_spec=pltpu.PrefetchScalarGridSpec(
            num_scalar_prefetch=2,
            grid=(batch, len(pairs)),
            in_specs=[qspec, kspec, kspec],
            out_specs=pl.BlockSpec((tq, MLA_WIDTH), lambda b, p, qi, kj: (b * nq + qi[p], 0)),
            scratch_shapes=[pltpu.VMEM((MLA_HEADS, tq, HEAD_PAD), F32),
                            pltpu.VMEM((MLA_HEADS, tq, HEAD_PAD), F32)]),
        compiler_params=_params("parallel", "arbitrary"),
        name="mla_flash_attention",
    )(q_of, k_of, q, k, v)


PAIRS_PER_GROUP = EXPERTS_PER_GROUP * (EXPERTS_PER_GROUP - 1) // 2
N_CLASSES = N_GROUPS * PAIRS_PER_GROUP
LANES = 128
ROW_CHUNKS = D_MODEL // LANES
TOKEN_TILE = 256
EXPERT_TILE = 256
DMA_WINDOW = 256


def _route(scores, bias):
    biased = scores + bias
    col = [biased[:, e:e + 1] for e in range(N_EXPERTS)]
    gscore = []
    for g in range(N_GROUPS):
        a, b, c, d = col[g * EXPERTS_PER_GROUP:(g + 1) * EXPERTS_PER_GROUP]
        gscore.append(jnp.maximum(jnp.maximum(jnp.maximum(a + b, a + c), jnp.maximum(a + d, b + c)),
                                  jnp.maximum(b + d, c + d)))
    sel = []
    for g in range(N_GROUPS):
        ok = None
        for o in range(N_GROUPS):
            if o == g:
                continue
            cond = (gscore[g] > gscore[o]) if o < g else (gscore[g] >= gscore[o])
            ok = cond if ok is None else jnp.logical_and(ok, cond)
        grp = col[g * EXPERTS_PER_GROUP:(g + 1) * EXPERTS_PER_GROUP]
        for e in range(EXPERTS_PER_GROUP):
            beaten = jnp.zeros_like(grp[e])
            for o in range(EXPERTS_PER_GROUP):
                if o == e:
                    continue
                ahead = (grp[o] >= grp[e]) if o < e else (grp[o] > grp[e])
                beaten = beaten + jnp.where(ahead, 1.0, 0.0)
            sel.append(jnp.logical_and(ok, beaten < 1.5))
    lo = jnp.full_like(col[0], float(N_EXPERTS))
    hi = jnp.full_like(col[0], -1.0)
    for e in range(N_EXPERTS):
        lo = jnp.where(sel[e], jnp.minimum(lo, float(e)), lo)
        hi = jnp.where(sel[e], jnp.maximum(hi, float(e)), hi)
    return lo, hi


def _router_scores(h2, wr):
    h_hi, h_lo = _split_bf16(h2)
    w_hi, w_lo = _split_bf16(wr)
    return _sigmoid(_dot(h_hi, w_hi) + _dot(h_lo, w_hi) + _dot(h_hi, w_lo))


def _merge_kernel(x_ref, oa_ref, ob_ref, g_ref, mod_ref, wa_ref, wb_ref, wo_ref, n2_ref,
                  wr_ref, rb_ref, xn_ref, row_ref, meta_ref, cnt_ref):
    d = D_MODEL
    tm = x_ref.shape[0]
    mod = mod_ref[0]
    ya = _dot(oa_ref[...], wa_ref[...])
    yb = _dot(ob_ref[...], wb_ref[...])
    g = g_ref[...].astype(F32)
    merged = g[:, 0:d] * ya + g[:, d:2 * d] * yb
    xn = x_ref[...] + mod[2:3] * _dot(merged.astype(BF16), wo_ref[...])
    xn_ref[...] = xn
    h2 = _rms(xn, n2_ref[...]) * (1.0 + mod[4:5]) + mod[3:4]
    lo, hi = _route(_router_scores(h2, wr_ref[...]), rb_ref[...])

    for c in range(ROW_CHUNKS):
        row_ref[:, c, :] = h2[:, c * LANES:(c + 1) * LANES]
    lane = lax.broadcasted_iota(jnp.int32, (tm, LANES), 1)

    grp = jnp.floor(lo * (1.0 / EXPERTS_PER_GROUP))
    a = lo - grp * EXPERTS_PER_GROUP
    b = hi - grp * EXPERTS_PER_GROUP
    cls = grp * PAIRS_PER_GROUP + a * (2 * EXPERTS_PER_GROUP - 1 - a) * 0.5 + (b - a - 1.0)
    onehot = jnp.where(lane.astype(F32) == cls, 1.0, 0.0)
    r_i = lax.broadcasted_iota(jnp.int32, (tm, tm), 0)
    c_i = lax.broadcasted_iota(jnp.int32, (tm, tm), 1)
    before = jnp.where(r_i > c_i, 1.0, 0.0).astype(BF16)
    rank = jnp.sum(onehot * _dot(before, onehot.astype(BF16)), axis=1, keepdims=True)
    lane_m = lax.broadcasted_iota(jnp.int32, (tm, N_EXPERTS), 1)
    meta_ref[...] = jnp.where(lane_m == 0, cls, jnp.where(lane_m == 1, rank, 0.0))
    cnt_ref[...] = jnp.broadcast_to(jnp.sum(onehot, axis=0, keepdims=True), cnt_ref.shape)


def _merge(x2, oa, ob, gsig, mod_l, w_br_a, w_br_b, w_out, n2, w_router, router_bias, seq):
    t, d = x2.shape
    tm = TOKEN_TILE
    rows = lambda n: pl.BlockSpec((tm, n), lambda i: (i, 0))
    return pl.pallas_call(
        _merge_kernel,
        out_shape=(jax.ShapeDtypeStruct((t, d), F32), jax.ShapeDtypeStruct((t, ROW_CHUNKS, LANES), F32),
                   jax.ShapeDtypeStruct((t, N_EXPERTS), F32),
                   jax.ShapeDtypeStruct((t // tm * 8, LANES), F32)),
        grid=(t // tm,),
        in_specs=[rows(d), rows(HG_WIDTH), rows(MLA_WIDTH), rows(2 * d),
                  pl.BlockSpec((1, 6, d), lambda i: ((i * tm) // seq, 0, 0)),
                  _resident(w_br_a.shape), _resident(w_br_b.shape), _resident(w_out.shape),
                  _resident((1, d)), _resident(w_router.shape), _resident((1, N_EXPERTS))],
        out_specs=(rows(d), pl.BlockSpec((tm, ROW_CHUNKS, LANES), lambda i: (i, 0, 0)), rows(N_EXPERTS),
                   pl.BlockSpec((8, LANES), lambda i: (i, 0))),
        compiler_params=_params("parallel"),
        name="merge_outproj_router",
    )(x2, oa, ob, gsig, mod_l, w_br_a.astype(BF16), w_br_b.astype(BF16), w_out.astype(BF16),
      n2[None, :], w_router, router_bias[None, :])


def _dispatch_plan(meta, cnt, n_tiles_e):
    t = meta.shape[0]
    n_tok_tiles = t // TOKEN_TILE
    cls = meta[:, 0].astype(jnp.int32)
    rank = meta[:, 1].astype(jnp.int32)
    counts = cnt.reshape(n_tok_tiles, 8, LANES)[:, 0, :N_CLASSES].astype(jnp.int32)
    total = jnp.sum(counts, axis=0)
    total_pad = (total + EXPERT_TILE - 1) // EXPERT_TILE * EXPERT_TILE
    ends = jnp.cumsum(total_pad)
    base = (ends - total_pad)[None, :] + jnp.cumsum(counts, axis=0) - counts
    own = cls.reshape(n_tok_tiles, TOKEN_TILE, 1) == jnp.arange(N_CLASSES, dtype=jnp.int32)
    pos = jnp.sum(jnp.where(own, base[:, None, :], 0), axis=2).reshape(t) + rank

    tile_start = jnp.arange(n_tiles_e, dtype=jnp.int32) * EXPERT_TILE
    n_valid = ends[-1] // EXPERT_TILE
    tile_cls = jnp.sum((tile_start[:, None] >= ends[None, :]).astype(jnp.int32), axis=1)
    last_cls = jnp.take(tile_cls, n_valid - 1)
    valid = jnp.arange(n_tiles_e, dtype=jnp.int32) < n_valid
    tile_cls = jnp.where(valid, tile_cls, last_cls)
    pair_lo = jnp.asarray([a for a in range(EXPERTS_PER_GROUP) for b in range(a + 1, EXPERTS_PER_GROUP)], jnp.int32)
    pair_hi = jnp.asarray([b for a in range(EXPERTS_PER_GROUP) for b in range(a + 1, EXPERTS_PER_GROUP)], jnp.int32)
    grp = tile_cls // PAIRS_PER_GROUP
    e_lo = grp * EXPERTS_PER_GROUP + jnp.take(pair_lo, tile_cls % PAIRS_PER_GROUP)
    e_hi = grp * EXPERTS_PER_GROUP + jnp.take(pair_hi, tile_cls % PAIRS_PER_GROUP)
    return pos.astype(jnp.int32), e_lo, e_hi, valid.astype(jnp.int32)


def _permute_kernel(pos_ref, src_hbm, *rest, scatter, n_rows):
    dst_hbm, sem = rest[-2], rest[-1]
    unroll = 8

    def row_copy(t):
        p = pos_ref[t]
        if scatter:
            return pltpu.make_async_copy(src_hbm.at[pl.ds(t, 1)], dst_hbm.at[pl.ds(p, 1)], sem)
        return pltpu.make_async_copy(src_hbm.at[pl.ds(p, 1)], dst_hbm.at[pl.ds(t, 1)], sem)

    def wait_window():
        pltpu.make_async_copy(src_hbm.at[pl.ds(0, DMA_WINDOW)], dst_hbm.at[pl.ds(0, DMA_WINDOW)], sem).wait()

    def window_body(wi, carry):
        def issue(ui, c):
            for u in range(unroll):
                row_copy(wi * DMA_WINDOW + ui * unroll + u).start()
            return c
        lax.fori_loop(0, DMA_WINDOW // unroll, issue, 0)

        @pl.when(wi > 0)
        def _():
            wait_window()
        return carry

    lax.fori_loop(0, n_rows // DMA_WINDOW, window_body, 0)
    wait_window()


def _permute_rows(pos, src, n_dst, scatter):
    n_rows = pos.shape[0]
    any_spec = pl.BlockSpec(memory_space=pl.ANY)
    operands = [pos, src]
    in_specs = [any_spec]
    aliases = {}
    if scatter:
        operands.append(jnp.zeros((n_dst,) + src.shape[1:], src.dtype))
        in_specs.append(any_spec)
        aliases = {2: 0}
    return pl.pallas_call(
        functools.partial(_permute_kernel, scatter=scatter, n_rows=n_rows),
        out_shape=jax.ShapeDtypeStruct((n_dst,) + src.shape[1:], src.dtype),
        grid_spec=pltpu.PrefetchScalarGridSpec(
            num_scalar_prefetch=1, grid=(1,), in_specs=in_specs, out_specs=any_spec,
            scratch_shapes=[pltpu.SemaphoreType.DMA]),
        input_output_aliases=aliases,
        compiler_params=pltpu.CompilerParams(dimension_semantics=("arbitrary",), disable_bounds_checks=True,
                                             has_side_effects=True),
        name="dispatch_rows" if scatter else "collect_rows",
    )(*operands)


def _rows_to_2d(ref):
    return jnp.concatenate([ref[:, c, :] for c in range(ROW_CHUNKS)], axis=1)


def _moe_kernel(elo_ref, ehi_ref, valid_ref, x_ref, wr_ref, wgl_ref, wul_ref, wdl_ref, wgh_ref, wuh_ref,
                wdh_ref, y_ref):
    j = pl.program_id(0)

    @pl.when(valid_ref[j] == 0)
    def _():
        y_ref[...] = jnp.zeros_like(y_ref)

    @pl.when(valid_ref[j] != 0)
    def _():
        h2 = _rows_to_2d(x_ref)
        scores = _router_scores(h2, wr_ref[...])
        lane = lax.broadcasted_iota(jnp.int32, scores.shape, 1)
        s_lo = jnp.sum(jnp.where(lane == elo_ref[j], scores, 0.0), axis=1, keepdims=True)
        s_hi = jnp.sum(jnp.where(lane == ehi_ref[j], scores, 0.0), axis=1, keepdims=True)
        total = s_lo + s_hi
        hb = h2.astype(BF16)
        he_lo = _silu(_dot(hb, wgl_ref[0])) * _dot(hb, wul_ref[0])
        y = (s_lo / total) * _dot(he_lo.astype(BF16), wdl_ref[0])
        he_hi = _silu(_dot(hb, wgh_ref[0])) * _dot(hb, wuh_ref[0])
        y = y + (s_hi / total) * _dot(he_hi.astype(BF16), wdh_ref[0])
        for c in range(ROW_CHUNKS):
            y_ref[:, c, :] = y[:, c * LANES:(c + 1) * LANES]


def _moe(rows_sorted, e_lo, e_hi, valid, w_router, w_gate, w_up, w_down):
    n_pad = rows_sorted.shape[0]
    d = D_MODEL
    lo3 = lambda j, el, eh, va: (el[j], 0, 0)
    hi3 = lambda j, el, eh, va: (eh[j], 0, 0)
    tile = pl.BlockSpec((EXPERT_TILE, ROW_CHUNKS, LANES), lambda j, el, eh, va: (j, 0, 0))
    wg, wu, wd = w_gate.astype(BF16), w_up.astype(BF16), w_down.astype(BF16)
    return pl.pallas_call(
        _moe_kernel,
        out_shape=jax.ShapeDtypeStruct(rows_sorted.shape, F32),
        grid_spec=pltpu.PrefetchScalarGridSpec(
            num_scalar_prefetch=3, grid=(n_pad // EXPERT_TILE,),
            in_specs=[tile, _resident(w_router.shape),
                      pl.BlockSpec((1, d, D_EXPERT), lo3), pl.BlockSpec((1, d, D_EXPERT), lo3),
                      pl.BlockSpec((1, D_EXPERT, d), lo3),
                      pl.BlockSpec((1, d, D_EXPERT), hi3), pl.BlockSpec((1, d, D_EXPERT), hi3),
                      pl.BlockSpec((1, D_EXPERT, d), hi3)],
            out_specs=tile),
        compiler_params=_params("arbitrary"),
        name="moe_experts",
    )(e_lo, e_hi, valid, rows_sorted, w_router, wg, wu, wd, wg, wu, wd)


def _residual_kernel(x_ref, y_ref, mod_ref, fg_ref, o_ref, *, final):
    xo = x_ref[...] + mod_ref[0][5:6] * _rows_to_2d(y_ref)
    if final:
        xo = _rms(xo, fg_ref[...])
    o_ref[...] = xo


def _moe_residual(xn, y, mod_l, final_g, final, seq):
    t, d = xn.shape
    tm = min(512, seq)
    rows = pl.BlockSpec((tm, d), lambda i: (i, 0))
    return pl.pallas_call(
        functools.partial(_residual_kernel, final=final),
        out_shape=jax.ShapeDtypeStruct((t, d), F32),
        grid=(t // tm,),
        in_specs=[rows, pl.BlockSpec((tm, ROW_CHUNKS, LANES), lambda i: (i, 0, 0)),
                  pl.BlockSpec((1, 6, d), lambda i: ((i * tm) // seq, 0, 0)),
                  pl.BlockSpec((1, d), lambda i: (0, 0))],
        out_specs=rows,
        compiler_params=_params("parallel"),
        name="moe_residual",
    )(xn, y, mod_l, final_g[None, :])


def kernel(x, c, positions, ada_w, ada_b, norm1_g, w_in, hg_lb_logits, hg_norm_g, q_norm_g, w_q_up,
           kv_norm_g, w_kv_up, w_br_a, w_br_b, w_out, norm2_g, w_router, router_bias, w_gate, w_up,
           w_down, final_g):
    batch, seq, d = x.shape
    depth = ada_w.shape[0]
    t = batch * seq
    n_tiles_e = (t + N_CLASSES * (EXPERT_TILE - 1)) // EXPERT_TILE
    mod = _modulation(c, ada_w, ada_b).reshape(depth, batch, 6, d)
    cos_t, sin_t = _rope_tables(positions)
    x2 = x.reshape(t, d)
    for l in range(depth):
        hq, hf, hi, hgate, gsig, q, k, v = _input_projection(
            x2, mod[l], norm1_g[l], cos_t, sin_t, w_in[l], q_norm_g[l], w_q_up[l],
            kv_norm_g[l], w_kv_up[l], seq)
        oa = _hgrn(hg_lb_logits, hg_norm_g[l], hq, hf, hi, hgate, l, batch, seq)
        ob = _attention(q, k, v, batch, seq)
        xn, rows, meta, cnt = _merge(x2, oa, ob, gsig, mod[l], w_br_a[l], w_br_b[l], w_out[l], norm2_g[l],
                                     w_router, router_bias, seq)
        pos, e_lo, e_hi, valid = _dispatch_plan(meta, cnt, n_tiles_e)
        rows_sorted = _permute_rows(pos, rows, n_tiles_e * EXPERT_TILE, scatter=True)
        y_sorted = _moe(rows_sorted, e_lo, e_hi, valid, w_router, w_gate[l], w_up[l], w_down[l])
        y = _permute_rows(pos, y_sorted, t, scatter=False)
        x2 = _moe_residual(xn, y, mod[l], final_g, l == depth - 1, seq)
    return x2.reshape(batch, seq, d)
```

```python
import functools
import math

import jax
import jax.numpy as jnp
from jax import lax
from jax.experimental import pallas as pl
from jax.experimental.pallas import tpu as pltpu

F32 = jnp.float32
BF16 = jnp.bfloat16

D_MODEL = 1024
CHUNK = 64
EPS = 1e-6

HG_HEADS = 4
HG_DK = 128
HG_DV = 128
HG_WIDTH = HG_HEADS * HG_DV
HG_SUB = 16

MLA_HEADS = 8
MLA_NOPE = 64
MLA_ROPE = 32
MLA_V = 64
MLA_Q_LORA = 384
MLA_KV_LORA = 256
MLA_DQK = MLA_NOPE + MLA_ROPE
MLA_WIDTH = MLA_HEADS * MLA_V
ROPE_BASE = 10000.0
HEAD_PAD = 128
LOG2_E = math.log2(math.e)

N_EXPERTS = 16
N_GROUPS = 4
EXPERTS_PER_GROUP = N_EXPERTS // N_GROUPS
D_EXPERT = 512

IN_SIZES = (HG_HEADS * HG_DK, HG_HEADS * HG_DK, HG_HEADS * HG_DV, HG_WIDTH,
            MLA_Q_LORA, MLA_KV_LORA, MLA_ROPE, 2 * D_MODEL)

V7X_VMEM_LIMIT_BYTES = 56 * 1024 * 1024


def _params(*sem):
    return pltpu.CompilerParams(dimension_semantics=sem, vmem_limit_bytes=V7X_VMEM_LIMIT_BYTES)


def _resident(shape):
    nd = len(shape)
    return pl.BlockSpec(shape, lambda *_: (0,) * nd, pipeline_mode=pl.Buffered(1))


def _sigmoid(x):
    return 1.0 / (1.0 + jnp.exp(-x))


def _silu(x):
    return x * _sigmoid(x)


def _dot(a, b):
    return jnp.dot(a, b, preferred_element_type=F32)


def _dot_nt(a, b):
    return lax.dot_general(a, b, (((1,), (1,)), ((), ())), preferred_element_type=F32)


def _dot_tn(a, b):
    return lax.dot_general(a, b, (((0,), (0,)), ((), ())), preferred_element_type=F32)


def _split_bf16(x):
    hi = x.astype(BF16)
    lo = (x - hi.astype(F32)).astype(BF16)
    return hi, lo


def _rms(x, g):
    return x * lax.rsqrt(jnp.mean(x * x, axis=-1, keepdims=True) + EPS) * g


def _mod_kernel(c_ref, w_ref, b_ref, o_ref):
    ca = _silu(c_ref[...])
    o_ref[0] = _dot(ca.astype(BF16), w_ref[0].astype(BF16)) + b_ref[0]


def _modulation(c, ada_w, ada_b):
    depth, d, n = ada_w.shape
    b = c.shape[0]
    tn = 1536
    return pl.pallas_call(
        _mod_kernel,
        out_shape=jax.ShapeDtypeStruct((depth, b, n), F32),
        grid=(depth, n // tn),
        in_specs=[pl.BlockSpec((b, d), lambda l, j: (0, 0)),
                  pl.BlockSpec((1, d, tn), lambda l, j: (l, 0, j)),
                  pl.BlockSpec((1, 1, tn), lambda l, j: (l, 0, j))],
        out_specs=pl.BlockSpec((1, b, tn), lambda l, j: (l, 0, j)),
        compiler_params=_params("parallel", "parallel"),
        name="adaln_modulation",
    )(c, ada_w, ada_b.reshape(depth, 1, n))


def _rope_kernel(pos_ref, inv_ref, msk_ref, sgn_ref, cos_ref, sin_ref):
    ang = pos_ref[...].astype(F32) * inv_ref[...]
    cos_ref[...] = jnp.cos(ang) * msk_ref[...]
    sin_ref[...] = jnp.sin(ang) * sgn_ref[...]


def _rope_tables(positions):
    t = positions.size
    half = MLA_ROPE // 2
    inv = ROPE_BASE ** (-jnp.arange(half, dtype=F32) / half)
    z64, z32, one16 = jnp.zeros((MLA_NOPE,), F32), jnp.zeros((32,), F32), jnp.ones((half,), F32)
    inv_row = jnp.concatenate([z64, inv, inv, z32])[None, :]
    msk_row = jnp.concatenate([z64, one16, one16, z32])[None, :]
    sgn_row = jnp.concatenate([z64, -one16, one16, z32])[None, :]
    tr = min(2048, t)
    row = pl.BlockSpec((1, HEAD_PAD), lambda i: (0, 0))
    tab = pl.BlockSpec((tr, HEAD_PAD), lambda i: (i, 0))
    return pl.pallas_call(
        _rope_kernel,
        out_shape=(jax.ShapeDtypeStruct((t, HEAD_PAD), F32),) * 2,
        grid=(t // tr,),
        in_specs=[pl.BlockSpec((tr, 1), lambda i: (i, 0)), row, row, row],
        out_specs=(tab, tab),
        compiler_params=_params("parallel"),
        name="rope_tables",
    )(positions.reshape(t, 1), inv_row, msk_row, sgn_row)


def _proj_kernel(x_ref, mod_ref, n1_ref, cos_ref, sin_ref, wh_ref, wg_ref, wc_ref,
                 qn_ref, wqa_ref, wqs_ref, kn_ref, wk_ref, wv_ref, one_ref,
                 hq_ref, hf_ref, hi_ref, hgate_ref, gsig_ref, q_ref, k_ref, v_ref):
    x = x_ref[...]
    mod = mod_ref[0]
    h = _rms(x, n1_ref[...]) * (1.0 + mod[1:2]) + mod[0:1]
    hb = h.astype(BF16)

    ph = _dot(hb, wh_ref[...])
    w = HG_WIDTH
    hq_ref[...] = ph[:, 0:w].astype(BF16)
    hf_ref[...] = ph[:, w:2 * w]
    hi_ref[...] = ph[:, 2 * w:3 * w].astype(BF16)
    hgate_ref[...] = ph[:, 3 * w:4 * w].astype(BF16)

    gsig_ref[...] = _sigmoid(_dot(hb, wg_ref[...])).astype(BF16)

    pc = _dot(hb, wc_ref[...])
    cq = pc[:, 0:MLA_Q_LORA]
    ckv = pc[:, MLA_Q_LORA:MLA_Q_LORA + MLA_KV_LORA]
    kra = pc[:, MLA_Q_LORA + MLA_KV_LORA:MLA_Q_LORA + MLA_KV_LORA + HEAD_PAD]
    krb = pc[:, MLA_Q_LORA + MLA_KV_LORA + HEAD_PAD:]

    cos_t = cos_ref[...]
    sin_t = sin_ref[...]
    lane = lax.broadcasted_iota(jnp.int32, cos_t.shape, 1)
    scale = MLA_DQK ** -0.5 * LOG2_E
    cq_tab = jnp.tile(scale * (cos_t + jnp.where(lane < MLA_NOPE, 1.0, 0.0)), (1, MLA_HEADS))
    sq_tab = jnp.tile(scale * sin_t, (1, MLA_HEADS))

    cqn = _rms(cq, qn_ref[...]).astype(BF16)
    q = _dot(cqn, wqa_ref[...]) * cq_tab + _dot(cqn, wqs_ref[...]) * sq_tab
    q_ref[...] = q.astype(BF16)

    ckvn = _rms(ckv, kn_ref[...]).astype(BF16)
    kpe = kra * cos_t + krb * sin_t
    k_ref[...] = (_dot(ckvn, wk_ref[...]) + jnp.tile(kpe, (1, MLA_HEADS))).astype(BF16)
    v_ref[...] = (_dot(ckvn, wv_ref[...]) + one_ref[...]).astype(BF16)


def _pad_heads(w, lo, hi, at):
    k, nh, _ = w.shape
    out = jnp.zeros((k, nh, HEAD_PAD), w.dtype)
    out = out.at[:, :, at:at + (hi - lo)].set(w[:, :, lo:hi])
    return out


def _input_projection(x2, mod_l, n1, cos_t, sin_t, w_in, q_norm_g, w_q_up, kv_norm_g, w_kv_up, seq):
    t, d = x2.shape
    tm = 256
    splits = [0]
    for s in IN_SIZES:
        splits.append(splits[-1] + s)
    w_h = w_in[:, splits[0]:splits[4]].astype(BF16)
    w_g = w_in[:, splits[7]:splits[8]].astype(BF16)
    kr = w_in[:, splits[6]:splits[7]]
    half = MLA_ROPE // 2
    z64 = jnp.zeros((d, MLA_NOPE), F32)
    z32 = jnp.zeros((d, HEAD_PAD - MLA_NOPE - MLA_ROPE), F32)
    kr_a = jnp.concatenate([z64, kr, z32], axis=1)
    kr_b = jnp.concatenate([z64, kr[:, half:], kr[:, :half], z32], axis=1)
    w_c = jnp.concatenate([w_in[:, splits[4]:splits[6]], kr_a, kr_b], axis=1).astype(BF16)

    wq = w_q_up.reshape(MLA_Q_LORA, MLA_HEADS, MLA_DQK)
    wq_all = _pad_heads(wq, 0, MLA_DQK, 0).reshape(MLA_Q_LORA, -1).astype(BF16)
    wq_swap = (_pad_heads(wq, MLA_NOPE + half, MLA_DQK, MLA_NOPE)
               + _pad_heads(wq, MLA_NOPE, MLA_NOPE + half, MLA_NOPE + half))
    wq_swap = wq_swap.reshape(MLA_Q_LORA, -1).astype(BF16)
    wkv = w_kv_up.reshape(MLA_KV_LORA, MLA_HEADS, MLA_NOPE + MLA_V)
    wk_all = _pad_heads(wkv, 0, MLA_NOPE, 0).reshape(MLA_KV_LORA, -1).astype(BF16)
    wv_all = _pad_heads(wkv, MLA_NOPE, MLA_NOPE + MLA_V, 0).reshape(MLA_KV_LORA, -1).astype(BF16)
    ones_row = jnp.tile(jnp.zeros((HEAD_PAD,), F32).at[MLA_V].set(1.0), MLA_HEADS)[None, :]

    hp = MLA_HEADS * HEAD_PAD
    rows = lambda n: pl.BlockSpec((tm, n), lambda i: (i, 0))
    outs = [(HG_WIDTH, BF16), (HG_WIDTH, F32), (HG_WIDTH, BF16), (HG_WIDTH, BF16),
            (2 * d, BF16), (hp, BF16), (hp, BF16), (hp, BF16)]
    return pl.pallas_call(
        _proj_kernel,
        out_shape=tuple(jax.ShapeDtypeStruct((t, n), dt) for n, dt in outs),
        grid=(t // tm,),
        in_specs=[rows(d),
                  pl.BlockSpec((1, 6, d), lambda i: ((i * tm) // seq, 0, 0)),
                  _resident((1, d)),
                  rows(HEAD_PAD), rows(HEAD_PAD),
                  _resident(w_h.shape), _resident(w_g.shape), _resident(w_c.shape),
                  _resident((1, MLA_Q_LORA)), _resident(wq_all.shape), _resident(wq_swap.shape),
                  _resident((1, MLA_KV_LORA)), _resident(wk_all.shape), _resident(wv_all.shape),
                  _resident((1, hp))],
        out_specs=tuple(rows(n) for n, _ in outs),
        compiler_params=_params("parallel"),
        name="norm_input_projection",
    )(x2, mod_l, n1[None, :], cos_t, sin_t, w_h, w_g, w_c,
      q_norm_g[None, :], wq_all, wq_swap, kv_norm_g[None, :], wk_all, wv_all, ones_row)


def _hgrn_kernel(lbl_ref, gn_ref, q_ref, f_ref, v_ref, gate_ref, o_ref, st_ref, *, layer, n_chunks):
    @pl.when(pl.program_id(1) == 0)
    def _():
        st_ref[...] = jnp.zeros_like(st_ref)

    lg = lbl_ref[...]
    ex = jnp.exp(lg - jnp.max(lg, axis=0, keepdims=True))
    soft = ex / jnp.sum(ex, axis=0, keepdims=True)
    lb_all = jnp.zeros_like(soft[0:1])
    for i in range(1, layer + 1):
        lb_all = lb_all + soft[i:i + 1]

    r_i = lax.broadcasted_iota(jnp.int32, (CHUNK, CHUNK), 0)
    c_i = lax.broadcasted_iota(jnp.int32, (CHUNK, CHUNK), 1)
    tri = jnp.where(r_i >= c_i, 1.0, 0.0).astype(BF16)
    sub_row = lax.broadcasted_iota(jnp.int32, (HG_SUB, HG_DK), 0)
    n_sub = CHUNK // HG_SUB

    def chunk_body(ci, carry):
        r0 = pl.multiple_of(ci * CHUNK, CHUNK)
        rows = pl.ds(r0, CHUNK)
        for h in range(HG_HEADS):
            sl = slice(h * HG_DK, (h + 1) * HG_DK)
            lb = lb_all[:, sl]
            q = q_ref[rows, sl].astype(F32)
            vb = v_ref[rows, sl]
            v = vb.astype(F32)
            f = lb + (1.0 - lb) * _sigmoid(f_ref[rows, sl])
            g = jnp.log(f)
            k = 1.0 - f
            qf = _silu(q) * (HG_DK ** -0.5)
            g_hi, g_lo = _split_bf16(g)
            b = _dot(tri, g_hi) + _dot(tri, g_lo)

            blocks = []
            for i in range(n_sub):
                rs = slice(i * HG_SUB, (i + 1) * HG_SUB)
                b_i, q_i, k_i, v_i = b[rs], qf[rs], k[rs], v[rs]
                acc = jnp.zeros((HG_SUB, HG_DV), F32)
                for s in range(HG_SUB):
                    wgt = jnp.where(sub_row >= s, jnp.exp(b_i - b_i[s:s + 1]), 0.0)
                    a = jnp.sum(wgt * (q_i * k_i[s:s + 1]), axis=1, keepdims=True)
                    acc = acc + a * v_i[s:s + 1]
                if i > 0:
                    n_prev = i * HG_SUB
                    ref = b[n_prev - 1:n_prev]
                    qt = (q_i * jnp.exp(b_i - ref)).astype(BF16)
                    kt = (k[:n_prev] * jnp.exp(ref - b[:n_prev])).astype(BF16)
                    att = _dot_nt(qt, kt)
                    acc = acc + _dot(att.astype(BF16), vb[:n_prev])
                blocks.append(acc)
            o = jnp.concatenate(blocks, axis=0)

            st = st_ref[h]
            o = o + _dot_nt((qf * jnp.exp(b)).astype(BF16), st.astype(BF16))
            b_last = b[CHUNK - 1:CHUNK]
            kd = (k * jnp.exp(b_last - b)).astype(BF16)
            st_ref[h] = st * jnp.exp(b_last) + _dot_tn(vb, kd)

            gt = gate_ref[rows, sl].astype(F32)
            o_ref[rows, sl] = (_rms(o, gn_ref[:, sl]) * _silu(gt)).astype(BF16)
        return carry

    lax.fori_loop(0, n_chunks, chunk_body, 0)


def _hgrn(hg_lb_logits, hg_norm_g, hq, hf, hi, hgate, layer, batch, seq):
    t = hq.shape[0]
    lc = min(512, seq)
    nb = seq // lc
    rows = pl.BlockSpec((lc, HG_WIDTH), lambda b, j: (b * nb + j, 0))
    kern = functools.partial(_hgrn_kernel, layer=layer, n_chunks=lc // CHUNK)
    return pl.pallas_call(
        kern,
        out_shape=jax.ShapeDtypeStruct((t, HG_WIDTH), BF16),
        grid=(batch, nb),
        in_specs=[pl.BlockSpec(hg_lb_logits.shape, lambda b, j: (0, 0)),
                  pl.BlockSpec((1, HG_WIDTH), lambda b, j: (0, 0)),
                  rows, rows, rows, rows],
        out_specs=rows,
        scratch_shapes=[pltpu.VMEM((HG_HEADS, HG_DV, HG_DK), F32)],
        compiler_params=_params("parallel", "arbitrary"),
        name="hgrn2_chunkwise",
    )(hg_lb_logits, hg_norm_g[None, :], hq, hf, hi, hgate)


def _attn_kernel(qi_ref, kj_ref, q_ref, k_ref, v_ref, o_ref, m_ref, acc_ref, *, tq):
    p_id = pl.program_id(1)
    i = qi_ref[p_id]
    j = kj_ref[p_id]
    n_lane_tiles = tq // HEAD_PAD

    @pl.when(j == 0)
    def _():
        m_ref[...] = jnp.full_like(m_ref, -jnp.inf)
        acc_ref[...] = jnp.zeros_like(acc_ref)

    def step(diagonal):
        if diagonal:
            qc = lax.broadcasted_iota(jnp.int32, (tq, tq), 0) // CHUNK
            kc = lax.broadcasted_iota(jnp.int32, (tq, tq), 1) // CHUNK
            visible = kc <= qc
        for h in range(MLA_HEADS):
            sl = slice(h * HEAD_PAD, (h + 1) * HEAD_PAD)
            s = _dot_nt(q_ref[:, sl], k_ref[:, sl])
            if diagonal:
                s = jnp.where(visible, s, -jnp.inf)
            tiles = [s[:, t * HEAD_PAD:(t + 1) * HEAD_PAD] for t in range(n_lane_tiles)]
            m_tile = tiles[0]
            for t in range(1, n_lane_tiles):
                m_tile = jnp.maximum(m_tile, tiles[t])
            m_prev = m_ref[h]
            m_new = jnp.maximum(m_prev, jnp.max(m_tile, axis=1, keepdims=True))
            alpha = jnp.exp2(m_prev - m_new)
            p = jnp.concatenate([jnp.exp2((tl - m_new).astype(BF16)) for tl in tiles], axis=1)
            acc_ref[h] = acc_ref[h] * alpha + _dot(p, v_ref[:, sl])
            m_ref[h] = m_new

    @pl.when(j < i)
    def _():
        step(False)

    @pl.when(j == i)
    def _():
        step(True)
        outs = []
        for h in range(MLA_HEADS):
            acc = acc_ref[h]
            outs.append(acc[:, 0:MLA_V] / acc[:, MLA_V:MLA_V + 1])
        o_ref[...] = jnp.concatenate(outs, axis=1).astype(BF16)


def _attention(q, k, v, batch, seq):
    t, hp = q.shape
    tq = min(512, seq)
    nq = seq // tq
    pairs = [(i, j) for i in range(nq) for j in range(i + 1)]
    q_of = jnp.asarray([p[0] for p in pairs], jnp.int32)
    k_of = jnp.asarray([p[1] for p in pairs], jnp.int32)
    qspec = pl.BlockSpec((tq, hp), lambda b, p, qi, kj: (b * nq + qi[p], 0))
    kspec = pl.BlockSpec((tq, hp), lambda b, p, qi, kj: (b * nq + kj[p], 0))
    return pl.pallas_call(
        functools.partial(_attn_kernel, tq=tq),
        out_shape=jax.ShapeDtypeStruct((t, MLA_WIDTH), BF16),
        grid_spec=pltpu.PrefetchScalarGridSpec(
            num_scalar_prefetch=2,
            grid=(batch, len(pairs)),
            in_specs=[qspec, kspec, kspec],
            out_specs=pl.BlockSpec((tq, MLA_WIDTH), lambda b, p, qi, kj: (b * nq + qi[p], 0)),
            scratch_shapes=[pltpu.VMEM((MLA_HEADS, tq, HEAD_PAD), F32),
                            pltpu.VMEM((MLA_HEADS, tq, HEAD_PAD), F32)]),
        compiler_params=_params("parallel", "arbitrary"),
        name="mla_flash_attention",
    )(q_of, k_of, q, k, v)


PAIRS_PER_GROUP = EXPERTS_PER_GROUP * (EXPERTS_PER_GROUP - 1) // 2
N_CLASSES = N_GROUPS * PAIRS_PER_GROUP
LANES = 128
ROW_CHUNKS = D_MODEL // LANES
TOKEN_TILE = 256
EXPERT_TILE = 256


def _route(scores, bias):
    biased = scores + bias
    col = [biased[:, e:e + 1] for e in range(N_EXPERTS)]
    gscore = []
    for g in range(N_GROUPS):
        a, b, c, d = col[g * EXPERTS_PER_GROUP:(g + 1) * EXPERTS_PER_GROUP]
        gscore.append(jnp.maximum(jnp.maximum(jnp.maximum(a + b, a + c), jnp.maximum(a + d, b + c)),
                                  jnp.maximum(b + d, c + d)))
    sel = []
    for g in range(N_GROUPS):
        ok = None
        for o in range(N_GROUPS):
            if o == g:
                continue
            cond = (gscore[g] > gscore[o]) if o < g else (gscore[g] >= gscore[o])
            ok = cond if ok is None else jnp.logical_and(ok, cond)
        grp = col[g * EXPERTS_PER_GROUP:(g + 1) * EXPERTS_PER_GROUP]
        for e in range(EXPERTS_PER_GROUP):
            beaten = jnp.zeros_like(grp[e])
            for o in range(EXPERTS_PER_GROUP):
                if o == e:
                    continue
                ahead = (grp[o] >= grp[e]) if o < e else (grp[o] > grp[e])
                beaten = beaten + jnp.where(ahead, 1.0, 0.0)
            sel.append(jnp.logical_and(ok, beaten < 1.5))
    lo = jnp.full_like(col[0], float(N_EXPERTS))
    hi = jnp.full_like(col[0], -1.0)
    for e in range(N_EXPERTS):
        lo = jnp.where(sel[e], jnp.minimum(lo, float(e)), lo)
        hi = jnp.where(sel[e], jnp.maximum(hi, float(e)), hi)
    return lo, hi


def _router_scores(h2, wr):
    h_hi, h_lo = _split_bf16(h2)
    w_hi, w_lo = _split_bf16(wr)
    return _sigmoid(_dot(h_hi, w_hi) + _dot(h_lo, w_hi) + _dot(h_hi, w_lo))


def _merge_kernel(x_ref, oa_ref, ob_ref, g_ref, mod_ref, wa_ref, wb_ref, wo_ref, n2_ref,
                  wr_ref, rb_ref, xn_ref, row_ref, meta_ref, cnt_ref):
    d = D_MODEL
    tm = x_ref.shape[0]
    mod = mod_ref[0]
    ya = _dot(oa_ref[...], wa_ref[...])
    yb = _dot(ob_ref[...], wb_ref[...])
    g = g_ref[...].astype(F32)
    merged = g[:, 0:d] * ya + g[:, d:2 * d] * yb
    xn = x_ref[...] + mod[2:3] * _dot(merged.astype(BF16), wo_ref[...])
    xn_ref[...] = xn
    h2 = _rms(xn, n2_ref[...]) * (1.0 + mod[4:5]) + mod[3:4]
    lo, hi = _route(_router_scores(h2, wr_ref[...]), rb_ref[...])

    for c in range(ROW_CHUNKS):
        row_ref[:, c, :] = h2[:, c * LANES:(c + 1) * LANES]
    lane = lax.broadcasted_iota(jnp.int32, (tm, LANES), 1)

    grp = jnp.floor(lo * (1.0 / EXPERTS_PER_GROUP))
    a = lo - grp * EXPERTS_PER_GROUP
    b = hi - grp * EXPERTS_PER_GROUP
    cls = grp * PAIRS_PER_GROUP + a * (2 * EXPERTS_PER_GROUP - 1 - a) * 0.5 + (b - a - 1.0)
    onehot = jnp.where(lane.astype(F32) == cls, 1.0, 0.0)
    r_i = lax.broadcasted_iota(jnp.int32, (tm, tm), 0)
    c_i = lax.broadcasted_iota(jnp.int32, (tm, tm), 1)
    before = jnp.where(r_i > c_i, 1.0, 0.0).astype(BF16)
    rank = jnp.sum(onehot * _dot(before, onehot.astype(BF16)), axis=1, keepdims=True)
    lane_m = lax.broadcasted_iota(jnp.int32, (tm, N_EXPERTS), 1)
    meta_ref[...] = jnp.where(lane_m == 0, cls, jnp.where(lane_m == 1, rank, 0.0))
    cnt_ref[...] = jnp.broadcast_to(jnp.sum(onehot, axis=0, keepdims=True), cnt_ref.shape)


def _merge(x2, oa, ob, gsig, mod_l, w_br_a, w_br_b, w_out, n2, w_router, router_bias, seq):
    t, d = x2.shape
    tm = TOKEN_TILE
    rows = lambda n: pl.BlockSpec((tm, n), lambda i: (i, 0))
    return pl.pallas_call(
        _merge_kernel,
        out_shape=(jax.ShapeDtypeStruct((t, d), F32), jax.ShapeDtypeStruct((t, ROW_CHUNKS, LANES), F32),
                   jax.ShapeDtypeStruct((t, N_EXPERTS), F32),
                   jax.ShapeDtypeStruct((t // tm * 8, LANES), F32)),
        grid=(t // tm,),
        in_specs=[rows(d), rows(HG_WIDTH), rows(MLA_WIDTH), rows(2 * d),
                  pl.BlockSpec((1, 6, d), lambda i: ((i * tm) // seq, 0, 0)),
                  _resident(w_br_a.shape), _resident(w_br_b.shape), _resident(w_out.shape),
                  _resident((1, d)), _resident(w_router.shape), _resident((1, N_EXPERTS))],
        out_specs=(rows(d), pl.BlockSpec((tm, ROW_CHUNKS, LANES), lambda i: (i, 0, 0)), rows(N_EXPERTS),
                   pl.BlockSpec((8, LANES), lambda i: (i, 0))),
        compiler_params=_params("parallel"),
        name="merge_outproj_router",
    )(x2, oa, ob, gsig, mod_l, w_br_a.astype(BF16), w_br_b.astype(BF16), w_out.astype(BF16),
      n2[None, :], w_router, router_bias[None, :])


def _dispatch_plan(meta, cnt, n_tiles_e):
    t = meta.shape[0]
    n_tok_tiles = t // TOKEN_TILE
    cls = meta[:, 0].astype(jnp.int32)
    rank = meta[:, 1].astype(jnp.int32)
    counts = cnt.reshape(n_tok_tiles, 8, LANES)[:, 0, :N_CLASSES].astype(jnp.int32)
    total = jnp.sum(counts, axis=0)
    total_pad = (total + EXPERT_TILE - 1) // EXPERT_TILE * EXPERT_TILE
    ends = jnp.cumsum(total_pad)
    base = (ends - total_pad)[None, :] + jnp.cumsum(counts, axis=0) - counts
    own = cls.reshape(n_tok_tiles, TOKEN_TILE, 1) == jnp.arange(N_CLASSES, dtype=jnp.int32)
    pos = jnp.sum(jnp.where(own, base[:, None, :], 0), axis=2).reshape(t) + rank

    tile_start = jnp.arange(n_tiles_e, dtype=jnp.int32) * EXPERT_TILE
    n_valid = ends[-1] // EXPERT_TILE
    tile_cls = jnp.sum((tile_start[:, None] >= ends[None, :]).astype(jnp.int32), axis=1)
    last_cls = jnp.take(tile_cls, n_valid - 1)
    valid = jnp.arange(n_tiles_e, dtype=jnp.int32) < n_valid
    tile_cls = jnp.where(valid, tile_cls, last_cls)
    pair_lo = jnp.asarray([a for a in range(EXPERTS_PER_GROUP) for b in range(a + 1, EXPERTS_PER_GROUP)], jnp.int32)
    pair_hi = jnp.asarray([b for a in range(EXPERTS_PER_GROUP) for b in range(a + 1, EXPERTS_PER_GROUP)], jnp.int32)
    grp = tile_cls // PAIRS_PER_GROUP
    e_lo = grp * EXPERTS_PER_GROUP + jnp.take(pair_lo, tile_cls % PAIRS_PER_GROUP)
    e_hi = grp * EXPERTS_PER_GROUP + jnp.take(pair_hi, tile_cls % PAIRS_PER_GROUP)
    return pos.astype(jnp.int32), e_lo, e_hi, valid.astype(jnp.int32)


DMA_UNROLL = 8


def _row_copies(n_rows, make_copy):
    def issue(ui, carry):
        for u in range(DMA_UNROLL):
            make_copy(ui * DMA_UNROLL + u).start()
        return carry
    lax.fori_loop(0, n_rows // DMA_UNROLL, issue, 0)


def _dispatch_kernel(pos_ref, rows_ref, zero_hbm, dst_hbm, sem):
    del zero_hbm
    tm = rows_ref.shape[0]
    base = pl.program_id(0) * tm
    _row_copies(tm, lambda r: pltpu.make_async_copy(
        rows_ref.at[pl.ds(r, 1)], dst_hbm.at[pl.ds(pos_ref[base + r], 1)], sem))
    pltpu.make_async_copy(rows_ref, dst_hbm.at[pl.ds(0, tm)], sem).wait()


def _dispatch_rows(pos, rows, n_dst):
    t = rows.shape[0]
    tm = TOKEN_TILE
    any_spec = pl.BlockSpec(memory_space=pl.ANY)
    return pl.pallas_call(
        _dispatch_kernel,
        out_shape=jax.ShapeDtypeStruct((n_dst,) + rows.shape[1:], rows.dtype),
        grid_spec=pltpu.PrefetchScalarGridSpec(
            num_scalar_prefetch=1, grid=(t // tm,),
            in_specs=[pl.BlockSpec((tm, ROW_CHUNKS, LANES), lambda i, p: (i, 0, 0)), any_spec],
            out_specs=any_spec,
            scratch_shapes=[pltpu.SemaphoreType.DMA]),
        input_output_aliases={2: 0},
        compiler_params=pltpu.CompilerParams(dimension_semantics=("arbitrary",), disable_bounds_checks=True,
                                             has_side_effects=True),
        name="dispatch_rows",
    )(pos, rows, jnp.zeros((n_dst,) + rows.shape[1:], rows.dtype))


def _rows_to_2d(ref):
    return jnp.concatenate([ref[:, c, :] for c in range(ROW_CHUNKS)], axis=1)


def _moe_kernel(elo_ref, ehi_ref, valid_ref, x_ref, wr_ref, wgl_ref, wul_ref, wdl_ref, wgh_ref, wuh_ref,
                wdh_ref, y_ref):
    j = pl.program_id(0)

    @pl.when(valid_ref[j] == 0)
    def _():
        y_ref[...] = jnp.zeros_like(y_ref)

    @pl.when(valid_ref[j] != 0)
    def _():
        h2 = _rows_to_2d(x_ref)
        scores = _router_scores(h2, wr_ref[...])
        lane = lax.broadcasted_iota(jnp.int32, scores.shape, 1)
        s_lo = jnp.sum(jnp.where(lane == elo_ref[j], scores, 0.0), axis=1, keepdims=True)
        s_hi = jnp.sum(jnp.where(lane == ehi_ref[j], scores, 0.0), axis=1, keepdims=True)
        total = s_lo + s_hi
        hb = h2.astype(BF16)
        he_lo = _silu(_dot(hb, wgl_ref[0])) * _dot(hb, wul_ref[0])
        y = (s_lo / total) * _dot(he_lo.astype(BF16), wdl_ref[0])
        he_hi = _silu(_dot(hb, wgh_ref[0])) * _dot(hb, wuh_ref[0])
        y = y + (s_hi / total) * _dot(he_hi.astype(BF16), wdh_ref[0])
        for c in range(ROW_CHUNKS):
            y_ref[:, c, :] = y[:, c * LANES:(c + 1) * LANES]


def _moe(rows_sorted, e_lo, e_hi, valid, w_router, w_gate, w_up, w_down):
    n_pad = rows_sorted.shape[0]
    d = D_MODEL
    lo3 = lambda j, el, eh, va: (el[j], 0, 0)
    hi3 = lambda j, el, eh, va: (eh[j], 0, 0)
    tile = pl.BlockSpec((EXPERT_TILE, ROW_CHUNKS, LANES), lambda j, el, eh, va: (j, 0, 0))
    wg, wu, wd = w_gate.astype(BF16), w_up.astype(BF16), w_down.astype(BF16)
    return pl.pallas_call(
        _moe_kernel,
        out_shape=jax.ShapeDtypeStruct(rows_sorted.shape, F32),
        grid_spec=pltpu.PrefetchScalarGridSpec(
            num_scalar_prefetch=3, grid=(n_pad // EXPERT_TILE,),
            in_specs=[tile, _resident(w_router.shape),
                      pl.BlockSpec((1, d, D_EXPERT), lo3), pl.BlockSpec((1, d, D_EXPERT), lo3),
                      pl.BlockSpec((1, D_EXPERT, d), lo3),
                      pl.BlockSpec((1, d, D_EXPERT), hi3), pl.BlockSpec((1, d, D_EXPERT), hi3),
                      pl.BlockSpec((1, D_EXPERT, d), hi3)],
            out_specs=tile),
        compiler_params=_params("arbitrary"),
        name="moe_experts",
    )(e_lo, e_hi, valid, rows_sorted, w_router, wg, wu, wd, wg, wu, wd)


def _collect_kernel(pos_ref, x_ref, ys_hbm, mod_ref, fg_ref, o_ref, buf0, buf1, sems, *, final):
    tm = x_ref.shape[0]
    i = pl.program_id(0)
    n = pl.num_programs(0)
    bufs = (buf0, buf1)

    def fetch(tile, slot):
        base = tile * tm
        _row_copies(tm, lambda r: pltpu.make_async_copy(
            ys_hbm.at[pl.ds(pos_ref[base + r], 1)], bufs[slot].at[pl.ds(r, 1)], sems.at[slot]))

    def finish(slot):
        pltpu.make_async_copy(ys_hbm.at[pl.ds(0, tm)], bufs[slot], sems.at[slot]).wait()
        xo = x_ref[...] + mod_ref[0][5:6] * _rows_to_2d(bufs[slot])
        if final:
            xo = _rms(xo, fg_ref[...])
        o_ref[...] = xo

    @pl.when(i == 0)
    def _():
        fetch(0, 0)

    for slot in range(2):
        @pl.when(jnp.logical_and(i + 1 < n, (i + 1) % 2 == slot))
        def _():
            fetch(i + 1, slot)

    for slot in range(2):
        @pl.when(i % 2 == slot)
        def _():
            finish(slot)


def _collect_residual(pos, xn, y_sorted, mod_l, final_g, final, seq):
    t, d = xn.shape
    tm = TOKEN_TILE
    rows = pl.BlockSpec((tm, d), lambda i, p: (i, 0))
    return pl.pallas_call(
        functools.partial(_collect_kernel, final=final),
        out_shape=jax.ShapeDtypeStruct((t, d), F32),
        grid_spec=pltpu.PrefetchScalarGridSpec(
            num_scalar_prefetch=1, grid=(t // tm,),
            in_specs=[rows, pl.BlockSpec(memory_space=pl.ANY),
                      pl.BlockSpec((1, 6, d), lambda i, p: ((i * tm) // seq, 0, 0)),
                      pl.BlockSpec((1, d), lambda i, p: (0, 0))],
            out_specs=rows,
            scratch_shapes=[pltpu.VMEM((tm, ROW_CHUNKS, LANES), F32), pltpu.VMEM((tm, ROW_CHUNKS, LANES), F32),
                            pltpu.SemaphoreType.DMA((2,))]),
        compiler_params=pltpu.CompilerParams(dimension_semantics=("arbitrary",), disable_bounds_checks=True,
                                             vmem_limit_bytes=V7X_VMEM_LIMIT_BYTES),
        name="collect_residual",
    )(pos, xn, y_sorted, mod_l, final_g[None, :])


def kernel(x, c, positions, ada_w, ada_b, norm1_g, w_in, hg_lb_logits, hg_norm_g, q_norm_g, w_q_up,
           kv_norm_g, w_kv_up, w_br_a, w_br_b, w_out, norm2_g, w_router, router_bias, w_gate, w_up,
           w_down, final_g):
    batch, seq, d = x.shape
    depth = ada_w.shape[0]
    t = batch * seq
    n_tiles_e = (t + N_CLASSES * (EXPERT_TILE - 1)) // EXPERT_TILE
    mod = _modulation(c, ada_w, ada_b).reshape(depth, batch, 6, d)
    cos_t, sin_t = _rope_tables(positions)
    x2 = x.reshape(t, d)
    for l in range(depth):
        hq, hf, hi, hgate, gsig, q, k, v = _input_projection(
            x2, mod[l], norm1_g[l], cos_t, sin_t, w_in[l], q_norm_g[l], w_q_up[l],
            kv_norm_g[l], w_kv_up[l], seq)
        oa = _hgrn(hg_lb_logits, hg_norm_g[l], hq, hf, hi, hgate, l, batch, seq)
        ob = _attention(q, k, v, batch, seq)
        xn, rows, meta, cnt = _merge(x2, oa, ob, gsig, mod[l], w_br_a[l], w_br_b[l], w_out[l], norm2_g[l],
                                     w_router, router_bias, seq)
        pos, e_lo, e_hi, valid = _dispatch_plan(meta, cnt, n_tiles_e)
        rows_sorted = _dispatch_rows(pos, rows, n_tiles_e * EXPERT_TILE)
        y_sorted = _moe(rows_sorted, e_lo, e_hi, valid, w_router, w_gate[l], w_up[l], w_down[l])
        x2 = _collect_residual(pos, xn, y_sorted, mod[l], final_g, l == depth - 1, seq)
    return x2.reshape(batch, seq, d)
```

```python
import functools
import math

import jax
import jax.numpy as jnp
from jax import lax
from jax.experimental import pallas as pl
from jax.experimental.pallas import tpu as pltpu

F32 = jnp.float32
BF16 = jnp.bfloat16

D_MODEL = 1024
CHUNK = 64
EPS = 1e-6

HG_HEADS = 4
HG_DK = 128
HG_DV = 128
HG_WIDTH = HG_HEADS * HG_DV
HG_SUB = 8
HG_LEVELS = (8, 16, 32)

MLA_HEADS = 8
MLA_NOPE = 64
MLA_ROPE = 32
MLA_V = 64
MLA_Q_LORA = 384
MLA_KV_LORA = 256
MLA_DQK = MLA_NOPE + MLA_ROPE
MLA_WIDTH = MLA_HEADS * MLA_V
ROPE_BASE = 10000.0
HEAD_PAD = 128
LOG2_E = math.log2(math.e)

N_EXPERTS = 16
N_GROUPS = 4
EXPERTS_PER_GROUP = N_EXPERTS // N_GROUPS
D_EXPERT = 512

IN_SIZES = (HG_HEADS * HG_DK, HG_HEADS * HG_DK, HG_HEADS * HG_DV, HG_WIDTH,
            MLA_Q_LORA, MLA_KV_LORA, MLA_ROPE, 2 * D_MODEL)

V7X_VMEM_LIMIT_BYTES = 56 * 1024 * 1024


def _params(*sem):
    return pltpu.CompilerParams(dimension_semantics=sem, vmem_limit_bytes=V7X_VMEM_LIMIT_BYTES)


def _resident(shape):
    nd = len(shape)
    return pl.BlockSpec(shape, lambda *_: (0,) * nd, pipeline_mode=pl.Buffered(1))


def _sigmoid(x):
    return 1.0 / (1.0 + jnp.exp(-x))


def _silu(x):
    return x * _sigmoid(x)


def _dot(a, b):
    return jnp.dot(a, b, preferred_element_type=F32)


def _dot_nt(a, b):
    return lax.dot_general(a, b, (((1,), (1,)), ((), ())), preferred_element_type=F32)


def _dot_tn(a, b):
    return lax.dot_general(a, b, (((0,), (0,)), ((), ())), preferred_element_type=F32)


def _split_bf16(x):
    hi = x.astype(BF16)
    lo = (x - hi.astype(F32)).astype(BF16)
    return hi, lo


def _rms(x, g):
    return x * lax.rsqrt(jnp.mean(x * x, axis=-1, keepdims=True) + EPS) * g


def _mod_kernel(c_ref, w_ref, b_ref, o_ref):
    ca = _silu(c_ref[...])
    o_ref[0] = _dot(ca.astype(BF16), w_ref[0].astype(BF16)) + b_ref[0]


def _modulation(c, ada_w, ada_b):
    depth, d, n = ada_w.shape
    b = c.shape[0]
    tn = 1536
    return pl.pallas_call(
        _mod_kernel,
        out_shape=jax.ShapeDtypeStruct((depth, b, n), F32),
        grid=(depth, n // tn),
        in_specs=[pl.BlockSpec((b, d), lambda l, j: (0, 0)),
                  pl.BlockSpec((1, d, tn), lambda l, j: (l, 0, j)),
                  pl.BlockSpec((1, 1, tn), lambda l, j: (l, 0, j))],
        out_specs=pl.BlockSpec((1, b, tn), lambda l, j: (l, 0, j)),
        compiler_params=_params("parallel", "parallel"),
        name="adaln_modulation",
    )(c, ada_w, ada_b.reshape(depth, 1, n))


def _rope_kernel(pos_ref, inv_ref, msk_ref, sgn_ref, cos_ref, sin_ref):
    ang = pos_ref[...].astype(F32) * inv_ref[...]
    cos_ref[...] = jnp.cos(ang) * msk_ref[...]
    sin_ref[...] = jnp.sin(ang) * sgn_ref[...]


def _rope_tables(positions):
    t = positions.size
    half = MLA_ROPE // 2
    inv = ROPE_BASE ** (-jnp.arange(half, dtype=F32) / half)
    z64, z32, one16 = jnp.zeros((MLA_NOPE,), F32), jnp.zeros((32,), F32), jnp.ones((half,), F32)
    inv_row = jnp.concatenate([z64, inv, inv, z32])[None, :]
    msk_row = jnp.concatenate([z64, one16, one16, z32])[None, :]
    sgn_row = jnp.concatenate([z64, -one16, one16, z32])[None, :]
    tr = min(2048, t)
    row = pl.BlockSpec((1, HEAD_PAD), lambda i: (0, 0))
    tab = pl.BlockSpec((tr, HEAD_PAD), lambda i: (i, 0))
    return pl.pallas_call(
        _rope_kernel,
        out_shape=(jax.ShapeDtypeStruct((t, HEAD_PAD), F32),) * 2,
        grid=(t // tr,),
        in_specs=[pl.BlockSpec((tr, 1), lambda i: (i, 0)), row, row, row],
        out_specs=(tab, tab),
        compiler_params=_params("parallel"),
        name="rope_tables",
    )(positions.reshape(t, 1), inv_row, msk_row, sgn_row)


def _proj_kernel(x_ref, mod_ref, n1_ref, cos_ref, sin_ref, wh_ref, wg_ref, wc_ref,
                 qn_ref, wqa_ref, wqs_ref, kn_ref, wk_ref, wv_ref, one_ref,
                 hq_ref, hf_ref, hi_ref, hgate_ref, gsig_ref, q_ref, k_ref, v_ref):
    x = x_ref[...]
    mod = mod_ref[0]
    h = _rms(x, n1_ref[...]) * (1.0 + mod[1:2]) + mod[0:1]
    hb = h.astype(BF16)

    ph = _dot(hb, wh_ref[...])
    w = HG_WIDTH
    hq_ref[...] = ph[:, 0:w].astype(BF16)
    hf_ref[...] = ph[:, w:2 * w]
    hi_ref[...] = ph[:, 2 * w:3 * w].astype(BF16)
    hgate_ref[...] = ph[:, 3 * w:4 * w].astype(BF16)

    gsig_ref[...] = _sigmoid(_dot(hb, wg_ref[...])).astype(BF16)

    pc = _dot(hb, wc_ref[...])
    cq = pc[:, 0:MLA_Q_LORA]
    ckv = pc[:, MLA_Q_LORA:MLA_Q_LORA + MLA_KV_LORA]
    kra = pc[:, MLA_Q_LORA + MLA_KV_LORA:MLA_Q_LORA + MLA_KV_LORA + HEAD_PAD]
    krb = pc[:, MLA_Q_LORA + MLA_KV_LORA + HEAD_PAD:]

    cos_t = cos_ref[...]
    sin_t = sin_ref[...]
    lane = lax.broadcasted_iota(jnp.int32, cos_t.shape, 1)
    scale = MLA_DQK ** -0.5 * LOG2_E
    cq_tab = jnp.tile(scale * (cos_t + jnp.where(lane < MLA_NOPE, 1.0, 0.0)), (1, MLA_HEADS))
    sq_tab = jnp.tile(scale * sin_t, (1, MLA_HEADS))

    cqn = _rms(cq, qn_ref[...]).astype(BF16)
    q = _dot(cqn, wqa_ref[...]) * cq_tab + _dot(cqn, wqs_ref[...]) * sq_tab
    q_ref[...] = q.astype(BF16)

    ckvn = _rms(ckv, kn_ref[...]).astype(BF16)
    kpe = kra * cos_t + krb * sin_t
    k_ref[...] = (_dot(ckvn, wk_ref[...]) + jnp.tile(kpe, (1, MLA_HEADS))).astype(BF16)
    v_ref[...] = (_dot(ckvn, wv_ref[...]) + one_ref[...]).astype(BF16)


def _pad_heads(w, lo, hi, at):
    k, nh, _ = w.shape
    out = jnp.zeros((k, nh, HEAD_PAD), w.dtype)
    out = out.at[:, :, at:at + (hi - lo)].set(w[:, :, lo:hi])
    return out


def _input_projection(x2, mod_l, n1, cos_t, sin_t, w_in, q_norm_g, w_q_up, kv_norm_g, w_kv_up, seq):
    t, d = x2.shape
    tm = 256
    splits = [0]
    for s in IN_SIZES:
        splits.append(splits[-1] + s)
    w_h = w_in[:, splits[0]:splits[4]].astype(BF16)
    w_g = w_in[:, splits[7]:splits[8]].astype(BF16)
    kr = w_in[:, splits[6]:splits[7]]
    half = MLA_ROPE // 2
    z64 = jnp.zeros((d, MLA_NOPE), F32)
    z32 = jnp.zeros((d, HEAD_PAD - MLA_NOPE - MLA_ROPE), F32)
    kr_a = jnp.concatenate([z64, kr, z32], axis=1)
    kr_b = jnp.concatenate([z64, kr[:, half:], kr[:, :half], z32], axis=1)
    w_c = jnp.concatenate([w_in[:, splits[4]:splits[6]], kr_a, kr_b], axis=1).astype(BF16)

    wq = w_q_up.reshape(MLA_Q_LORA, MLA_HEADS, MLA_DQK)
    wq_all = _pad_heads(wq, 0, MLA_DQK, 0).reshape(MLA_Q_LORA, -1).astype(BF16)
    wq_swap = (_pad_heads(wq, MLA_NOPE + half, MLA_DQK, MLA_NOPE)
               + _pad_heads(wq, MLA_NOPE, MLA_NOPE + half, MLA_NOPE + half))
    wq_swap = wq_swap.reshape(MLA_Q_LORA, -1).astype(BF16)
    wkv = w_kv_up.reshape(MLA_KV_LORA, MLA_HEADS, MLA_NOPE + MLA_V)
    wk_all = _pad_heads(wkv, 0, MLA_NOPE, 0).reshape(MLA_KV_LORA, -1).astype(BF16)
    wv_all = _pad_heads(wkv, MLA_NOPE, MLA_NOPE + MLA_V, 0).reshape(MLA_KV_LORA, -1).astype(BF16)
    ones_row = jnp.tile(jnp.zeros((HEAD_PAD,), F32).at[MLA_V].set(1.0), MLA_HEADS)[None, :]

    hp = MLA_HEADS * HEAD_PAD
    rows = lambda n: pl.BlockSpec((tm, n), lambda i: (i, 0))
    outs = [(HG_WIDTH, BF16), (HG_WIDTH, F32), (HG_WIDTH, BF16), (HG_WIDTH, BF16),
            (2 * d, BF16), (hp, BF16), (hp, BF16), (hp, BF16)]
    return pl.pallas_call(
        _proj_kernel,
        out_shape=tuple(jax.ShapeDtypeStruct((t, n), dt) for n, dt in outs),
        grid=(t // tm,),
        in_specs=[rows(d),
                  pl.BlockSpec((1, 6, d), lambda i: ((i * tm) // seq, 0, 0)),
                  _resident((1, d)),
                  rows(HEAD_PAD), rows(HEAD_PAD),
                  _resident(w_h.shape), _resident(w_g.shape), _resident(w_c.shape),
                  _resident((1, MLA_Q_LORA)), _resident(wq_all.shape), _resident(wq_swap.shape),
                  _resident((1, MLA_KV_LORA)), _resident(wk_all.shape), _resident(wv_all.shape),
                  _resident((1, hp))],
        out_specs=tuple(rows(n) for n, _ in outs),
        compiler_params=_params("parallel"),
        name="norm_input_projection",
    )(x2, mod_l, n1[None, :], cos_t, sin_t, w_h, w_g, w_c,
      q_norm_g[None, :], wq_all, wq_swap, kv_norm_g[None, :], wk_all, wv_all, ones_row)


def _hgrn_kernel(lbl_ref, gn_ref, q_ref, f_ref, v_ref, gate_ref, o_ref, st_ref, *, layer, n_chunks):
    @pl.when(pl.program_id(1) == 0)
    def _():
        st_ref[...] = jnp.zeros_like(st_ref)

    lg = lbl_ref[...]
    ex = jnp.exp(lg - jnp.max(lg, axis=0, keepdims=True))
    soft = ex / jnp.sum(ex, axis=0, keepdims=True)
    lb_all = jnp.zeros_like(soft[0:1])
    for i in range(1, layer + 1):
        lb_all = lb_all + soft[i:i + 1]

    r_i = lax.broadcasted_iota(jnp.int32, (CHUNK, CHUNK), 0)
    c_i = lax.broadcasted_iota(jnp.int32, (CHUNK, CHUNK), 1)
    tri = jnp.where(r_i >= c_i, 1.0, 0.0).astype(BF16)
    level_masks = []
    for hs in HG_LEVELS:
        same = (r_i // (2 * hs)) == (c_i // (2 * hs))
        level_masks.append(jnp.logical_and(same, jnp.logical_and(r_i % (2 * hs) >= hs, c_i % (2 * hs) < hs)))
    sub_row = lax.broadcasted_iota(jnp.int32, (HG_SUB, 1), 0)
    n_sub = CHUNK // HG_SUB

    def chunk_body(ci, carry):
        r0 = pl.multiple_of(ci * CHUNK, CHUNK)
        rows = pl.ds(r0, CHUNK)
        for h in range(HG_HEADS):
            sl = slice(h * HG_DK, (h + 1) * HG_DK)
            lb = lb_all[:, sl]
            q = q_ref[rows, sl].astype(F32)
            vb = v_ref[rows, sl]
            v = vb.astype(F32)
            f = lb + (1.0 - lb) * _sigmoid(f_ref[rows, sl])
            g = jnp.log2(f)
            k = 1.0 - f
            qf = _silu(q) * (HG_DK ** -0.5)
            g_hi, g_lo = _split_bf16(g)
            b = _dot(tri, g_hi) + _dot(tri, g_lo)

            att = jnp.zeros((CHUNK, CHUNK), F32)
            for hs, mask in zip(HG_LEVELS, level_masks):
                ref = jnp.concatenate(
                    [jnp.broadcast_to(b[j + hs - 1:j + hs], (2 * hs, HG_DK)) for j in range(0, CHUNK, 2 * hs)],
                    axis=0)
                qt = (qf * jnp.exp2(b - ref)).astype(BF16)
                kt = (k * jnp.exp2(ref - b)).astype(BF16)
                att = att + jnp.where(mask, _dot_nt(qt, kt), 0.0)
            o = _dot(att.astype(BF16), vb)

            blocks = []
            for i in range(n_sub):
                rs = slice(i * HG_SUB, (i + 1) * HG_SUB)
                b_i, q_i, k_i, v_i = b[rs], qf[rs], k[rs], v[rs]
                acc = jnp.zeros((HG_SUB, HG_DV), F32)
                for s in range(HG_SUB):
                    a = jnp.sum(jnp.exp2(b_i - b_i[s:s + 1]) * (q_i * k_i[s:s + 1]), axis=1, keepdims=True)
                    acc = acc + jnp.where(sub_row >= s, a, 0.0) * v_i[s:s + 1]
                blocks.append(acc)
            o = o + jnp.concatenate(blocks, axis=0)

            st = st_ref[h]
            o = o + _dot_nt((qf * jnp.exp2(b)).astype(BF16), st.astype(BF16))
            b_last = b[CHUNK - 1:CHUNK]
            kd = (k * jnp.exp2(b_last - b)).astype(BF16)
            st_ref[h] = st * jnp.exp2(b_last) + _dot_tn(vb, kd)

            gt = gate_ref[rows, sl].astype(F32)
            o_ref[rows, sl] = (_rms(o, gn_ref[:, sl]) * _silu(gt)).astype(BF16)
        return carry

    lax.fori_loop(0, n_chunks, chunk_body, 0, unroll=2)


def _hgrn(hg_lb_logits, hg_norm_g, hq, hf, hi, hgate, layer, batch, seq):
    t = hq.shape[0]
    lc = min(512, seq)
    nb = seq // lc
    rows = pl.BlockSpec((lc, HG_WIDTH), lambda b, j: (b * nb + j, 0))
    kern = functools.partial(_hgrn_kernel, layer=layer, n_chunks=lc // CHUNK)
    return pl.pallas_call(
        kern,
        out_shape=jax.ShapeDtypeStruct((t, HG_WIDTH), BF16),
        grid=(batch, nb),
        in_specs=[pl.BlockSpec(hg_lb_logits.shape, lambda b, j: (0, 0)),
                  pl.BlockSpec((1, HG_WIDTH), lambda b, j: (0, 0)),
                  rows, rows, rows, rows],
        out_specs=rows,
        scratch_shapes=[pltpu.VMEM((HG_HEADS, HG_DV, HG_DK), F32)],
        compiler_params=_params("parallel", "arbitrary"),
        name="hgrn2_chunkwise",
    )(hg_lb_logits, hg_norm_g[None, :], hq, hf, hi, hgate)


def _attn_kernel(qi_ref, kj_ref, q_ref, k_ref, v_ref, o_ref, m_ref, acc_ref, *, tq):
    p_id = pl.program_id(1)
    i = qi_ref[p_id]
    j = kj_ref[p_id]
    n_lane_tiles = tq // HEAD_PAD

    @pl.when(j == 0)
    def _():
        m_ref[...] = jnp.full_like(m_ref, -jnp.inf)
        acc_ref[...] = jnp.zeros_like(acc_ref)

    def step(diagonal):
        if diagonal:
            qc = lax.broadcasted_iota(jnp.int32, (tq, tq), 0) // CHUNK
            kc = lax.broadcasted_iota(jnp.int32, (tq, tq), 1) // CHUNK
            visible = kc <= qc
        for h in range(MLA_HEADS):
            sl = slice(h * HEAD_PAD, (h + 1) * HEAD_PAD)
            s = _dot_nt(q_ref[:, sl], k_ref[:, sl])
            if diagonal:
                s = jnp.where(visible, s, -jnp.inf)
            tiles = [s[:, t * HEAD_PAD:(t + 1) * HEAD_PAD] for t in range(n_lane_tiles)]
            m_tile = tiles[0]
            for t in range(1, n_lane_tiles):
                m_tile = jnp.maximum(m_tile, tiles[t])
            m_prev = m_ref[h]
            m_new = jnp.maximum(m_prev, jnp.max(m_tile, axis=1, keepdims=True))
            alpha = jnp.exp2(m_prev - m_new)
            p = jnp.concatenate([jnp.exp2((tl - m_new).astype(BF16)) for tl in tiles], axis=1)
            acc_ref[h] = acc_ref[h] * alpha + _dot(p, v_ref[:, sl])
            m_ref[h] = m_new

    @pl.when(j < i)
    def _():
        step(False)

    @pl.when(j == i)
    def _():
        step(True)
        outs = []
        for h in range(MLA_HEADS):
            acc = acc_ref[h]
            outs.append(acc[:, 0:MLA_V] / acc[:, MLA_V:MLA_V + 1])
        o_ref[...] = jnp.concatenate(outs, axis=1).astype(BF16)


def _attention(q, k, v, batch, seq):
    t, hp = q.shape
    tq = min(512, seq)
    nq = seq // tq
    pairs = [(i, j) for i in range(nq) for j in range(i + 1)]
    q_of = jnp.asarray([p[0] for p in pairs], jnp.int32)
    k_of = jnp.asarray([p[1] for p in pairs], jnp.int32)
    qspec = pl.BlockSpec((tq, hp), lambda b, p, qi, kj: (b * nq + qi[p], 0))
    kspec = pl.BlockSpec((tq, hp), lambda b, p, qi, kj: (b * nq + kj[p], 0))
    return pl.pallas_call(
        functools.partial(_attn_kernel, tq=tq),
        out_shape=jax.ShapeDtypeStruct((t, MLA_WIDTH), BF16),
        grid_spec=pltpu.PrefetchScalarGridSpec(
            num_scalar_prefetch=2,
            grid=(batch, len(pairs)),
            in_specs=[qspec, kspec, kspec],
            out_specs=pl.BlockSpec((tq, MLA_WIDTH), lambda b, p, qi, kj: (b * nq + qi[p], 0)),
            scratch_shapes=[pltpu.VMEM((MLA_HEADS, tq, HEAD_PAD), F32),
                            pltpu.VMEM((MLA_HEADS, tq, HEAD_PAD), F32)]),
        compiler_params=_params("parallel", "arbitrary"),
        name="mla_flash_attention",
    )(q_of, k_of, q, k, v)


PAIRS_PER_GROUP = EXPERTS_PER_GROUP * (EXPERTS_PER_GROUP - 1) // 2
N_CLASSES = N_GROUPS * PAIRS_PER_GROUP
LANES = 128
ROW_CHUNKS = D_MODEL // LANES
TOKEN_TILE = 256
EXPERT_TILE = 256


N_CLASS_ROWS = 32


def _route(scores, bias):
    biased = scores + bias
    col = [biased[e:e + 1, :] for e in range(N_EXPERTS)]
    gscore = []
    for g in range(N_GROUPS):
        a, b, c, d = col[g * EXPERTS_PER_GROUP:(g + 1) * EXPERTS_PER_GROUP]
        gscore.append(jnp.maximum(jnp.maximum(jnp.maximum(a + b, a + c), jnp.maximum(a + d, b + c)),
                                  jnp.maximum(b + d, c + d)))
    sel = []
    for g in range(N_GROUPS):
        ok = None
        for o in range(N_GROUPS):
            if o == g:
                continue
            cond = (gscore[g] > gscore[o]) if o < g else (gscore[g] >= gscore[o])
            ok = cond if ok is None else jnp.logical_and(ok, cond)
        grp = col[g * EXPERTS_PER_GROUP:(g + 1) * EXPERTS_PER_GROUP]
        for e in range(EXPERTS_PER_GROUP):
            beaten = jnp.zeros_like(grp[e])
            for o in range(EXPERTS_PER_GROUP):
                if o == e:
                    continue
                ahead = (grp[o] >= grp[e]) if o < e else (grp[o] > grp[e])
                beaten = beaten + jnp.where(ahead, 1.0, 0.0)
            sel.append(jnp.logical_and(ok, beaten < 1.5))
    lo = jnp.full_like(col[0], float(N_EXPERTS))
    hi = jnp.full_like(col[0], -1.0)
    for e in range(N_EXPERTS):
        lo = jnp.where(sel[e], jnp.minimum(lo, float(e)), lo)
        hi = jnp.where(sel[e], jnp.maximum(hi, float(e)), hi)
    return lo, hi


def _router_scores(h2, wr):
    h_hi, h_lo = _split_bf16(h2)
    w_hi, w_lo = _split_bf16(wr)
    return _sigmoid(_dot(h_hi, w_hi) + _dot(h_lo, w_hi) + _dot(h_hi, w_lo))


def _store_rows(ref, x):
    n = x.shape[0]
    for c in range(ROW_CHUNKS):
        ref[pl.ds(c, n, stride=ROW_CHUNKS), :] = x[:, c * LANES:(c + 1) * LANES]


def _load_rows(ref, n):
    return jnp.concatenate([ref[pl.ds(c, n, stride=ROW_CHUNKS), :] for c in range(ROW_CHUNKS)], axis=1)


def _merge_kernel(x_ref, oa_ref, ob_ref, g_ref, mod_ref, wa_ref, wb_ref, wo_ref, n2_ref,
                  wrt_ref, rb_ref, xn_ref, row_ref, meta_ref):
    d = D_MODEL
    tm = x_ref.shape[0]
    mod = mod_ref[0]
    ya = _dot(oa_ref[...], wa_ref[...])
    yb = _dot(ob_ref[...], wb_ref[...])
    g = g_ref[...].astype(F32)
    merged = g[:, 0:d] * ya + g[:, d:2 * d] * yb
    xn = x_ref[...] + mod[2:3] * _dot(merged.astype(BF16), wo_ref[...])
    xn_ref[...] = xn
    h2 = _rms(xn, n2_ref[...]) * (1.0 + mod[4:5]) + mod[3:4]
    _store_rows(row_ref, h2)

    h_hi, h_lo = _split_bf16(h2)
    w_hi, w_lo = _split_bf16(wrt_ref[...])
    scores = _sigmoid(_dot_nt(w_hi, h_hi) + _dot_nt(w_hi, h_lo) + _dot_nt(w_lo, h_hi))
    lo, hi = _route(scores, rb_ref[...])

    grp = jnp.floor(lo * (1.0 / EXPERTS_PER_GROUP))
    a = lo - grp * EXPERTS_PER_GROUP
    b = hi - grp * EXPERTS_PER_GROUP
    cls = grp * PAIRS_PER_GROUP + a * (2 * EXPERTS_PER_GROUP - 1 - a) * 0.5 + (b - a - 1.0)
    c_row = lax.broadcasted_iota(jnp.int32, (N_CLASS_ROWS, tm), 0).astype(F32)
    onehot = jnp.where(c_row == cls, 1.0, 0.0)
    r_i = lax.broadcasted_iota(jnp.int32, (tm, tm), 0)
    c_i = lax.broadcasted_iota(jnp.int32, (tm, tm), 1)
    earlier = jnp.where(r_i < c_i, 1.0, 0.0).astype(BF16)
    rank = jnp.sum(onehot * _dot(onehot.astype(BF16), earlier), axis=0, keepdims=True)
    m_row = lax.broadcasted_iota(jnp.int32, (8, tm), 0)
    meta_ref[...] = jnp.where(m_row == 0, cls, jnp.where(m_row == 1, rank, 0.0))


def _merge(x2, oa, ob, gsig, mod_l, w_br_a, w_br_b, w_out, n2, w_router, router_bias, seq):
    t, d = x2.shape
    tm = TOKEN_TILE
    rows = lambda n: pl.BlockSpec((tm, n), lambda i: (i, 0))
    return pl.pallas_call(
        _merge_kernel,
        out_shape=(jax.ShapeDtypeStruct((t, d), F32), jax.ShapeDtypeStruct((t * ROW_CHUNKS, LANES), F32),
                   jax.ShapeDtypeStruct((8, t), F32)),
        grid=(t // tm,),
        in_specs=[rows(d), rows(HG_WIDTH), rows(MLA_WIDTH), rows(2 * d),
                  pl.BlockSpec((1, 6, d), lambda i: ((i * tm) // seq, 0, 0)),
                  _resident(w_br_a.shape), _resident(w_br_b.shape), _resident(w_out.shape),
                  _resident((1, d)), _resident((N_EXPERTS, d)), _resident((N_EXPERTS, 1))],
        out_specs=(rows(d), pl.BlockSpec((tm * ROW_CHUNKS, LANES), lambda i: (i, 0)),
                   pl.BlockSpec((8, tm), lambda i: (0, i))),
        compiler_params=_params("parallel"),
        name="merge_outproj_router",
    )(x2, oa, ob, gsig, mod_l, w_br_a.astype(BF16), w_br_b.astype(BF16), w_out.astype(BF16),
      n2[None, :], w_router.T, router_bias[:, None])


def _dispatch_plan(meta, n_tiles_e):
    t = meta.shape[1]
    n_tok_tiles = t // TOKEN_TILE
    cls = meta[0].astype(jnp.int32)
    rank = meta[1].astype(jnp.int32)
    own = cls.reshape(n_tok_tiles, TOKEN_TILE, 1) == jnp.arange(N_CLASSES, dtype=jnp.int32)
    counts = jnp.sum(own.astype(jnp.int32), axis=1)
    total = jnp.sum(counts, axis=0)
    total_pad = (total + EXPERT_TILE - 1) // EXPERT_TILE * EXPERT_TILE
    ends = jnp.cumsum(total_pad)
    base = (ends - total_pad)[None, :] + jnp.cumsum(counts, axis=0) - counts
    pos = jnp.sum(jnp.where(own, base[:, None, :], 0), axis=2).reshape(t) + rank

    tile_start = jnp.arange(n_tiles_e, dtype=jnp.int32) * EXPERT_TILE
    n_valid = ends[-1] // EXPERT_TILE
    tile_cls = jnp.sum((tile_start[:, None] >= ends[None, :]).astype(jnp.int32), axis=1)
    last_cls = jnp.take(tile_cls, n_valid - 1)
    valid = jnp.arange(n_tiles_e, dtype=jnp.int32) < n_valid
    tile_cls = jnp.where(valid, tile_cls, last_cls)
    pair_lo = jnp.asarray([a for a in range(EXPERTS_PER_GROUP) for b in range(a + 1, EXPERTS_PER_GROUP)], jnp.int32)
    pair_hi = jnp.asarray([b for a in range(EXPERTS_PER_GROUP) for b in range(a + 1, EXPERTS_PER_GROUP)], jnp.int32)
    grp = tile_cls // PAIRS_PER_GROUP
    e_lo = grp * EXPERTS_PER_GROUP + jnp.take(pair_lo, tile_cls % PAIRS_PER_GROUP)
    e_hi = grp * EXPERTS_PER_GROUP + jnp.take(pair_hi, tile_cls % PAIRS_PER_GROUP)
    return (pos * ROW_CHUNKS).astype(jnp.int32), e_lo, e_hi, valid.astype(jnp.int32)


DMA_UNROLL = 8


def _row_copies(n_rows, make_copy):
    def issue(ui, carry):
        for u in range(DMA_UNROLL):
            make_copy(ui * DMA_UNROLL + u).start()
        return carry
    lax.fori_loop(0, n_rows // DMA_UNROLL, issue, 0)


def _token_slab(ref, first_row):
    return ref.at[pl.ds(pl.multiple_of(first_row, ROW_CHUNKS), ROW_CHUNKS)]


def _dispatch_kernel(pos_ref, rows_ref, zero_hbm, dst_hbm, sem):
    del zero_hbm
    tm = rows_ref.shape[0] // ROW_CHUNKS
    base = pl.program_id(0) * tm
    _row_copies(tm, lambda r: pltpu.make_async_copy(
        _token_slab(rows_ref, r * ROW_CHUNKS), _token_slab(dst_hbm, pos_ref[base + r]), sem))
    pltpu.make_async_copy(rows_ref, dst_hbm.at[pl.ds(0, tm * ROW_CHUNKS)], sem).wait()


def _dispatch_rows(pos, rows, n_dst):
    t = rows.shape[0] // ROW_CHUNKS
    tm = TOKEN_TILE
    any_spec = pl.BlockSpec(memory_space=pl.ANY)
    return pl.pallas_call(
        _dispatch_kernel,
        out_shape=jax.ShapeDtypeStruct((n_dst * ROW_CHUNKS, LANES), rows.dtype),
        grid_spec=pltpu.PrefetchScalarGridSpec(
            num_scalar_prefetch=1, grid=(t // tm,),
            in_specs=[pl.BlockSpec((tm * ROW_CHUNKS, LANES), lambda i, p: (i, 0)), any_spec],
            out_specs=any_spec,
            scratch_shapes=[pltpu.SemaphoreType.DMA]),
        input_output_aliases={2: 0},
        compiler_params=pltpu.CompilerParams(dimension_semantics=("arbitrary",), disable_bounds_checks=True,
                                             has_side_effects=True),
        name="dispatch_rows",
    )(pos, rows, jnp.zeros((n_dst * ROW_CHUNKS, LANES), rows.dtype))


def _moe_kernel(elo_ref, ehi_ref, valid_ref, x_ref, wr_ref, wgl_ref, wul_ref, wdl_ref, wgh_ref, wuh_ref,
                wdh_ref, y_ref):
    j = pl.program_id(0)

    @pl.when(valid_ref[j] == 0)
    def _():
        y_ref[...] = jnp.zeros_like(y_ref)

    @pl.when(valid_ref[j] != 0)
    def _():
        h2 = _load_rows(x_ref, EXPERT_TILE)
        scores = _router_scores(h2, wr_ref[...])
        lane = lax.broadcasted_iota(jnp.int32, scores.shape, 1)
        s_lo = jnp.sum(jnp.where(lane == elo_ref[j], scores, 0.0), axis=1, keepdims=True)
        s_hi = jnp.sum(jnp.where(lane == ehi_ref[j], scores, 0.0), axis=1, keepdims=True)
        total = s_lo + s_hi
        hb = h2.astype(BF16)
        he_lo = _silu(_dot(hb, wgl_ref[0])) * _dot(hb, wul_ref[0])
        y = (s_lo / total) * _dot(he_lo.astype(BF16), wdl_ref[0])
        he_hi = _silu(_dot(hb, wgh_ref[0])) * _dot(hb, wuh_ref[0])
        _store_rows(y_ref, y + (s_hi / total) * _dot(he_hi.astype(BF16), wdh_ref[0]))


def _moe(rows_sorted, e_lo, e_hi, valid, w_router, w_gate, w_up, w_down):
    n_pad = rows_sorted.shape[0] // ROW_CHUNKS
    d = D_MODEL
    lo3 = lambda j, el, eh, va: (el[j], 0, 0)
    hi3 = lambda j, el, eh, va: (eh[j], 0, 0)
    tile = pl.BlockSpec((EXPERT_TILE * ROW_CHUNKS, LANES), lambda j, el, eh, va: (j, 0))
    wg, wu, wd = w_gate.astype(BF16), w_up.astype(BF16), w_down.astype(BF16)
    return pl.pallas_call(
        _moe_kernel,
        out_shape=jax.ShapeDtypeStruct(rows_sorted.shape, F32),
        grid_spec=pltpu.PrefetchScalarGridSpec(
            num_scalar_prefetch=3, grid=(n_pad // EXPERT_TILE,),
            in_specs=[tile, _resident(w_router.shape),
                      pl.BlockSpec((1, d, D_EXPERT), lo3), pl.BlockSpec((1, d, D_EXPERT), lo3),
                      pl.BlockSpec((1, D_EXPERT, d), lo3),
                      pl.BlockSpec((1, d, D_EXPERT), hi3), pl.BlockSpec((1, d, D_EXPERT), hi3),
                      pl.BlockSpec((1, D_EXPERT, d), hi3)],
            out_specs=tile),
        compiler_params=_params("arbitrary"),
        name="moe_experts",
    )(e_lo, e_hi, valid, rows_sorted, w_router, wg, wu, wd, wg, wu, wd)


def _collect_kernel(pos_ref, x_ref, ys_hbm, mod_ref, fg_ref, o_ref, buf0, buf1, sems, *, final):
    tm = x_ref.shape[0]
    i = pl.program_id(0)
    n = pl.num_programs(0)
    bufs = (buf0, buf1)

    def fetch(tile, slot):
        base = tile * tm
        _row_copies(tm, lambda r: pltpu.make_async_copy(
            _token_slab(ys_hbm, pos_ref[base + r]), _token_slab(bufs[slot], r * ROW_CHUNKS), sems.at[slot]))

    def finish(slot):
        pltpu.make_async_copy(ys_hbm.at[pl.ds(0, tm * ROW_CHUNKS)], bufs[slot], sems.at[slot]).wait()
        xo = x_ref[...] + mod_ref[0][5:6] * _load_rows(bufs[slot], tm)
        if final:
            xo = _rms(xo, fg_ref[...])
        o_ref[...] = xo

    @pl.when(i == 0)
    def _():
        fetch(0, 0)

    for slot in range(2):
        @pl.when(jnp.logical_and(i + 1 < n, (i + 1) % 2 == slot))
        def _():
            fetch(i + 1, slot)

    for slot in range(2):
        @pl.when(i % 2 == slot)
        def _():
            finish(slot)


def _collect_residual(pos, xn, y_sorted, mod_l, final_g, final, seq):
    t, d = xn.shape
    tm = TOKEN_TILE
    rows = pl.BlockSpec((tm, d), lambda i, p: (i, 0))
    return pl.pallas_call(
        functools.partial(_collect_kernel, final=final),
        out_shape=jax.ShapeDtypeStruct((t, d), F32),
        grid_spec=pltpu.PrefetchScalarGridSpec(
            num_scalar_prefetch=1, grid=(t // tm,),
            in_specs=[rows, pl.BlockSpec(memory_space=pl.ANY),
                      pl.BlockSpec((1, 6, d), lambda i, p: ((i * tm) // seq, 0, 0)),
                      pl.BlockSpec((1, d), lambda i, p: (0, 0))],
            out_specs=rows,
            scratch_shapes=[pltpu.VMEM((tm * ROW_CHUNKS, LANES), F32), pltpu.VMEM((tm * ROW_CHUNKS, LANES), F32),
                            pltpu.SemaphoreType.DMA((2,))]),
        compiler_params=pltpu.CompilerParams(dimension_semantics=("arbitrary",), disable_bounds_checks=True,
                                             vmem_limit_bytes=V7X_VMEM_LIMIT_BYTES),
        name="collect_residual",
    )(pos, xn, y_sorted, mod_l, final_g[None, :])


def kernel(x, c, positions, ada_w, ada_b, norm1_g, w_in, hg_lb_logits, hg_norm_g, q_norm_g, w_q_up,
           kv_norm_g, w_kv_up, w_br_a, w_br_b, w_out, norm2_g, w_router, router_bias, w_gate, w_up,
           w_down, final_g):
    batch, seq, d = x.shape
    depth = ada_w.shape[0]
    t = batch * seq
    n_tiles_e = (t + N_CLASSES * (EXPERT_TILE - 1)) // EXPERT_TILE
    mod = _modulation(c, ada_w, ada_b).reshape(depth, batch, 6, d)
    cos_t, sin_t = _rope_tables(positions)
    x2 = x.reshape(t, d)
    for l in range(depth):
        hq, hf, hi, hgate, gsig, q, k, v = _input_projection(
            x2, mod[l], norm1_g[l], cos_t, sin_t, w_in[l], q_norm_g[l], w_q_up[l],
            kv_norm_g[l], w_kv_up[l], seq)
        oa = _hgrn(hg_lb_logits, hg_norm_g[l], hq, hf, hi, hgate, l, batch, seq)
        ob = _attention(q, k, v, batch, seq)
        xn, rows, meta = _merge(x2, oa, ob, gsig, mod[l], w_br_a[l], w_br_b[l], w_out[l], norm2_g[l],
                                w_router, router_bias, seq)
        pos, e_lo, e_hi, valid = _dispatch_plan(meta, n_tiles_e)
        rows_sorted = _dispatch_rows(pos, rows, n_tiles_e * EXPERT_TILE)
        y_sorted = _moe(rows_sorted, e_lo, e_hi, valid, w_router, w_gate[l], w_up[l], w_down[l])
        x2 = _collect_residual(pos, xn, y_sorted, mod[l], final_g, l == depth - 1, seq)
    return x2.reshape(batch, seq, d)
```

```python
import functools
import math

import jax
import jax.numpy as jnp
from jax import lax
from jax.experimental import pallas as pl
from jax.experimental.pallas import tpu as pltpu

F32 = jnp.float32
BF16 = jnp.bfloat16

D_MODEL = 1024
CHUNK = 64
EPS = 1e-6

HG_HEADS = 4
HG_DK = 128
HG_DV = 128
HG_WIDTH = HG_HEADS * HG_DV
HG_SUB = 8
HG_LEVELS = (8, 16, 32)

MLA_HEADS = 8
MLA_NOPE = 64
MLA_ROPE = 32
MLA_V = 64
MLA_Q_LORA = 384
MLA_KV_LORA = 256
MLA_DQK = MLA_NOPE + MLA_ROPE
MLA_WIDTH = MLA_HEADS * MLA_V
ROPE_BASE = 10000.0
HEAD_PAD = 128
LOG2_E = math.log2(math.e)

N_EXPERTS = 16
N_GROUPS = 4
EXPERTS_PER_GROUP = N_EXPERTS // N_GROUPS
D_EXPERT = 512

IN_SIZES = (HG_HEADS * HG_DK, HG_HEADS * HG_DK, HG_HEADS * HG_DV, HG_WIDTH,
            MLA_Q_LORA, MLA_KV_LORA, MLA_ROPE, 2 * D_MODEL)

V7X_VMEM_LIMIT_BYTES = 56 * 1024 * 1024


def _params(*sem):
    return pltpu.CompilerParams(dimension_semantics=sem, vmem_limit_bytes=V7X_VMEM_LIMIT_BYTES)


def _resident(shape):
    nd = len(shape)
    return pl.BlockSpec(shape, lambda *_: (0,) * nd, pipeline_mode=pl.Buffered(1))


def _sigmoid(x):
    return 1.0 / (1.0 + jnp.exp(-x))


def _silu(x):
    return x * _sigmoid(x)


def _dot(a, b):
    return jnp.dot(a, b, preferred_element_type=F32)


def _dot_nt(a, b):
    return lax.dot_general(a, b, (((1,), (1,)), ((), ())), preferred_element_type=F32)


def _dot_tn(a, b):
    return lax.dot_general(a, b, (((0,), (0,)), ((), ())), preferred_element_type=F32)


def _split_bf16(x):
    hi = x.astype(BF16)
    lo = (x - hi.astype(F32)).astype(BF16)
    return hi, lo


def _rms(x, g):
    return x * lax.rsqrt(jnp.mean(x * x, axis=-1, keepdims=True) + EPS) * g


def _mod_kernel(c_ref, w_ref, b_ref, o_ref):
    ca = _silu(c_ref[...])
    o_ref[0] = _dot(ca.astype(BF16), w_ref[0].astype(BF16)) + b_ref[0]


def _modulation(c, ada_w, ada_b):
    depth, d, n = ada_w.shape
    b = c.shape[0]
    tn = 1536
    return pl.pallas_call(
        _mod_kernel,
        out_shape=jax.ShapeDtypeStruct((depth, b, n), F32),
        grid=(depth, n // tn),
        in_specs=[pl.BlockSpec((b, d), lambda l, j: (0, 0)),
                  pl.BlockSpec((1, d, tn), lambda l, j: (l, 0, j)),
                  pl.BlockSpec((1, 1, tn), lambda l, j: (l, 0, j))],
        out_specs=pl.BlockSpec((1, b, tn), lambda l, j: (l, 0, j)),
        compiler_params=_params("parallel", "parallel"),
        name="adaln_modulation",
    )(c, ada_w, ada_b.reshape(depth, 1, n))


def _rope_kernel(pos_ref, inv_ref, msk_ref, sgn_ref, cos_ref, sin_ref):
    ang = pos_ref[...].astype(F32) * inv_ref[...]
    cos_ref[...] = jnp.cos(ang) * msk_ref[...]
    sin_ref[...] = jnp.sin(ang) * sgn_ref[...]


def _rope_tables(positions):
    t = positions.size
    half = MLA_ROPE // 2
    inv = ROPE_BASE ** (-jnp.arange(half, dtype=F32) / half)
    z64, z32, one16 = jnp.zeros((MLA_NOPE,), F32), jnp.zeros((32,), F32), jnp.ones((half,), F32)
    inv_row = jnp.concatenate([z64, inv, inv, z32])[None, :]
    msk_row = jnp.concatenate([z64, one16, one16, z32])[None, :]
    sgn_row = jnp.concatenate([z64, -one16, one16, z32])[None, :]
    tr = min(2048, t)
    row = pl.BlockSpec((1, HEAD_PAD), lambda i: (0, 0))
    tab = pl.BlockSpec((tr, HEAD_PAD), lambda i: (i, 0))
    return pl.pallas_call(
        _rope_kernel,
        out_shape=(jax.ShapeDtypeStruct((t, HEAD_PAD), F32),) * 2,
        grid=(t // tr,),
        in_specs=[pl.BlockSpec((tr, 1), lambda i: (i, 0)), row, row, row],
        out_specs=(tab, tab),
        compiler_params=_params("parallel"),
        name="rope_tables",
    )(positions.reshape(t, 1), inv_row, msk_row, sgn_row)


def _proj_kernel(x_ref, mod_ref, n1_ref, cos_ref, sin_ref, wh_ref, wg_ref, wc_ref,
                 qn_ref, wqa_ref, wqs_ref, kn_ref, wk_ref, wv_ref, one_ref,
                 hq_ref, hf_ref, hi_ref, hgate_ref, gsig_ref, q_ref, k_ref, v_ref):
    x = x_ref[...]
    mod = mod_ref[0]
    h = _rms(x, n1_ref[...]) * (1.0 + mod[1:2]) + mod[0:1]
    hb = h.astype(BF16)

    ph = _dot(hb, wh_ref[...])
    w = HG_WIDTH
    hq_ref[...] = ph[:, 0:w].astype(BF16)
    hf_ref[...] = ph[:, w:2 * w]
    hi_ref[...] = ph[:, 2 * w:3 * w].astype(BF16)
    hgate_ref[...] = ph[:, 3 * w:4 * w].astype(BF16)

    gsig_ref[...] = _sigmoid(_dot(hb, wg_ref[...])).astype(BF16)

    pc = _dot(hb, wc_ref[...])
    cq = pc[:, 0:MLA_Q_LORA]
    ckv = pc[:, MLA_Q_LORA:MLA_Q_LORA + MLA_KV_LORA]
    kra = pc[:, MLA_Q_LORA + MLA_KV_LORA:MLA_Q_LORA + MLA_KV_LORA + HEAD_PAD]
    krb = pc[:, MLA_Q_LORA + MLA_KV_LORA + HEAD_PAD:]

    cos_t = cos_ref[...]
    sin_t = sin_ref[...]
    lane = lax.broadcasted_iota(jnp.int32, cos_t.shape, 1)
    scale = MLA_DQK ** -0.5 * LOG2_E
    cq_tab = jnp.tile(scale * (cos_t + jnp.where(lane < MLA_NOPE, 1.0, 0.0)), (1, MLA_HEADS))
    sq_tab = jnp.tile(scale * sin_t, (1, MLA_HEADS))

    cqn = _rms(cq, qn_ref[...]).astype(BF16)
    q = _dot(cqn, wqa_ref[...]) * cq_tab + _dot(cqn, wqs_ref[...]) * sq_tab
    q_ref[...] = q.astype(BF16)

    ckvn = _rms(ckv, kn_ref[...]).astype(BF16)
    kpe = kra * cos_t + krb * sin_t
    k_ref[...] = (_dot(ckvn, wk_ref[...]) + jnp.tile(kpe, (1, MLA_HEADS))).astype(BF16)
    v_ref[...] = (_dot(ckvn, wv_ref[...]) + one_ref[...]).astype(BF16)


def _pad_heads(w, lo, hi, at):
    k, nh, _ = w.shape
    out = jnp.zeros((k, nh, HEAD_PAD), w.dtype)
    out = out.at[:, :, at:at + (hi - lo)].set(w[:, :, lo:hi])
    return out


def _input_projection(x2, mod_l, n1, cos_t, sin_t, w_in, q_norm_g, w_q_up, kv_norm_g, w_kv_up, seq):
    t, d = x2.shape
    tm = 256
    splits = [0]
    for s in IN_SIZES:
        splits.append(splits[-1] + s)
    w_h = w_in[:, splits[0]:splits[4]].astype(BF16)
    w_g = w_in[:, splits[7]:splits[8]].astype(BF16)
    kr = w_in[:, splits[6]:splits[7]]
    half = MLA_ROPE // 2
    z64 = jnp.zeros((d, MLA_NOPE), F32)
    z32 = jnp.zeros((d, HEAD_PAD - MLA_NOPE - MLA_ROPE), F32)
    kr_a = jnp.concatenate([z64, kr, z32], axis=1)
    kr_b = jnp.concatenate([z64, kr[:, half:], kr[:, :half], z32], axis=1)
    w_c = jnp.concatenate([w_in[:, splits[4]:splits[6]], kr_a, kr_b], axis=1).astype(BF16)

    wq = w_q_up.reshape(MLA_Q_LORA, MLA_HEADS, MLA_DQK)
    wq_all = _pad_heads(wq, 0, MLA_DQK, 0).reshape(MLA_Q_LORA, -1).astype(BF16)
    wq_swap = (_pad_heads(wq, MLA_NOPE + half, MLA_DQK, MLA_NOPE)
               + _pad_heads(wq, MLA_NOPE, MLA_NOPE + half, MLA_NOPE + half))
    wq_swap = wq_swap.reshape(MLA_Q_LORA, -1).astype(BF16)
    wkv = w_kv_up.reshape(MLA_KV_LORA, MLA_HEADS, MLA_NOPE + MLA_V)
    wk_all = _pad_heads(wkv, 0, MLA_NOPE, 0).reshape(MLA_KV_LORA, -1).astype(BF16)
    wv_all = _pad_heads(wkv, MLA_NOPE, MLA_NOPE + MLA_V, 0).reshape(MLA_KV_LORA, -1).astype(BF16)
    ones_row = jnp.tile(jnp.zeros((HEAD_PAD,), F32).at[MLA_V].set(1.0), MLA_HEADS)[None, :]

    hp = MLA_HEADS * HEAD_PAD
    rows = lambda n: pl.BlockSpec((tm, n), lambda i: (i, 0))
    outs = [(HG_WIDTH, BF16), (HG_WIDTH, F32), (HG_WIDTH, BF16), (HG_WIDTH, BF16),
            (2 * d, BF16), (hp, BF16), (hp, BF16), (hp, BF16)]
    return pl.pallas_call(
        _proj_kernel,
        out_shape=tuple(jax.ShapeDtypeStruct((t, n), dt) for n, dt in outs),
        grid=(t // tm,),
        in_specs=[rows(d),
                  pl.BlockSpec((1, 6, d), lambda i: ((i * tm) // seq, 0, 0)),
                  _resident((1, d)),
                  rows(HEAD_PAD), rows(HEAD_PAD),
                  _resident(w_h.shape), _resident(w_g.shape), _resident(w_c.shape),
                  _resident((1, MLA_Q_LORA)), _resident(wq_all.shape), _resident(wq_swap.shape),
                  _resident((1, MLA_KV_LORA)), _resident(wk_all.shape), _resident(wv_all.shape),
                  _resident((1, hp))],
        out_specs=tuple(rows(n) for n, _ in outs),
        compiler_params=_params("parallel"),
        name="norm_input_projection",
    )(x2, mod_l, n1[None, :], cos_t, sin_t, w_h, w_g, w_c,
      q_norm_g[None, :], wq_all, wq_swap, kv_norm_g[None, :], wk_all, wv_all, ones_row)


def _hgrn_kernel(lbl_ref, gn_ref, q_ref, f_ref, v_ref, gate_ref, o_ref, st_ref, *, layer, n_chunks):
    @pl.when(pl.program_id(1) == 0)
    def _():
        st_ref[...] = jnp.zeros_like(st_ref)

    lg = lbl_ref[...]
    ex = jnp.exp(lg - jnp.max(lg, axis=0, keepdims=True))
    soft = ex / jnp.sum(ex, axis=0, keepdims=True)
    lb_all = jnp.zeros_like(soft[0:1])
    for i in range(1, layer + 1):
        lb_all = lb_all + soft[i:i + 1]

    r_i = lax.broadcasted_iota(jnp.int32, (CHUNK, CHUNK), 0)
    c_i = lax.broadcasted_iota(jnp.int32, (CHUNK, CHUNK), 1)
    tri = jnp.where(r_i >= c_i, 1.0, 0.0).astype(BF16)
    level_masks = []
    for hs in HG_LEVELS:
        same = (r_i // (2 * hs)) == (c_i // (2 * hs))
        level_masks.append(jnp.logical_and(same, jnp.logical_and(r_i % (2 * hs) >= hs, c_i % (2 * hs) < hs)))
    sub_row = lax.broadcasted_iota(jnp.int32, (HG_SUB, 1), 0)
    n_sub = CHUNK // HG_SUB

    def chunk_body(ci, carry):
        r0 = pl.multiple_of(ci * CHUNK, CHUNK)
        rows = pl.ds(r0, CHUNK)
        for h in range(HG_HEADS):
            sl = slice(h * HG_DK, (h + 1) * HG_DK)
            lb = lb_all[:, sl]
            q = q_ref[rows, sl].astype(F32)
            vb = v_ref[rows, sl]
            v = vb.astype(F32)
            f = lb + (1.0 - lb) * _sigmoid(f_ref[rows, sl])
            g = jnp.log2(f)
            k = 1.0 - f
            qf = _silu(q) * (HG_DK ** -0.5)
            g_hi, g_lo = _split_bf16(g)
            b = _dot(tri, g_hi) + _dot(tri, g_lo)

            att = jnp.zeros((CHUNK, CHUNK), F32)
            for hs, mask in zip(HG_LEVELS, level_masks):
                ref = jnp.concatenate(
                    [jnp.broadcast_to(b[j + hs - 1:j + hs], (2 * hs, HG_DK)) for j in range(0, CHUNK, 2 * hs)],
                    axis=0)
                qt = (qf * jnp.exp2(b - ref)).astype(BF16)
                kt = (k * jnp.exp2(ref - b)).astype(BF16)
                att = att + jnp.where(mask, _dot_nt(qt, kt), 0.0)
            o = _dot(att.astype(BF16), vb)

            blocks = []
            for i in range(n_sub):
                rs = slice(i * HG_SUB, (i + 1) * HG_SUB)
                b_i, q_i, k_i, v_i = b[rs], qf[rs], k[rs], v[rs]
                acc = jnp.zeros((HG_SUB, HG_DV), F32)
                for s in range(HG_SUB):
                    a = jnp.sum(jnp.exp2(b_i - b_i[s:s + 1]) * (q_i * k_i[s:s + 1]), axis=1, keepdims=True)
                    acc = acc + jnp.where(sub_row >= s, a, 0.0) * v_i[s:s + 1]
                blocks.append(acc)
            o = o + jnp.concatenate(blocks, axis=0)

            st = st_ref[h]
            o = o + _dot_nt((qf * jnp.exp2(b)).astype(BF16), st.astype(BF16))
            b_last = b[CHUNK - 1:CHUNK]
            kd = (k * jnp.exp2(b_last - b)).astype(BF16)
            st_ref[h] = st * jnp.exp2(b_last) + _dot_tn(vb, kd)

            gt = gate_ref[rows, sl].astype(F32)
            o_ref[rows, sl] = (_rms(o, gn_ref[:, sl]) * _silu(gt)).astype(BF16)
        return carry

    lax.fori_loop(0, n_chunks, chunk_body, 0, unroll=2)


def _hgrn(hg_lb_logits, hg_norm_g, hq, hf, hi, hgate, layer, batch, seq):
    t = hq.shape[0]
    lc = min(512, seq)
    nb = seq // lc
    rows = pl.BlockSpec((lc, HG_WIDTH), lambda b, j: (b * nb + j, 0))
    kern = functools.partial(_hgrn_kernel, layer=layer, n_chunks=lc // CHUNK)
    return pl.pallas_call(
        kern,
        out_shape=jax.ShapeDtypeStruct((t, HG_WIDTH), BF16),
        grid=(batch, nb),
        in_specs=[pl.BlockSpec(hg_lb_logits.shape, lambda b, j: (0, 0)),
                  pl.BlockSpec((1, HG_WIDTH), lambda b, j: (0, 0)),
                  rows, rows, rows, rows],
        out_specs=rows,
        scratch_shapes=[pltpu.VMEM((HG_HEADS, HG_DV, HG_DK), F32)],
        compiler_params=_params("parallel", "arbitrary"),
        name="hgrn2_chunkwise",
    )(hg_lb_logits, hg_norm_g[None, :], hq, hf, hi, hgate)


def _attn_kernel(qi_ref, kj_ref, q_ref, k_ref, v_ref, o_ref, m_ref, acc_ref, *, tq):
    p_id = pl.program_id(1)
    i = qi_ref[p_id]
    j = kj_ref[p_id]
    n_lane_tiles = tq // HEAD_PAD

    @pl.when(j == 0)
    def _():
        m_ref[...] = jnp.full_like(m_ref, -jnp.inf)
        acc_ref[...] = jnp.zeros_like(acc_ref)

    def step(diagonal):
        if diagonal:
            qc = lax.broadcasted_iota(jnp.int32, (tq, tq), 0) // CHUNK
            kc = lax.broadcasted_iota(jnp.int32, (tq, tq), 1) // CHUNK
            visible = kc <= qc
        for h in range(MLA_HEADS):
            sl = slice(h * HEAD_PAD, (h + 1) * HEAD_PAD)
            s = _dot_nt(q_ref[:, sl], k_ref[:, sl])
            if diagonal:
                s = jnp.where(visible, s, -jnp.inf)
            tiles = [s[:, t * HEAD_PAD:(t + 1) * HEAD_PAD] for t in range(n_lane_tiles)]
            m_tile = tiles[0]
            for t in range(1, n_lane_tiles):
                m_tile = jnp.maximum(m_tile, tiles[t])
            m_prev = m_ref[h]
            m_new = jnp.maximum(m_prev, jnp.max(m_tile, axis=1, keepdims=True))
            alpha = jnp.exp2(m_prev - m_new)
            p = jnp.concatenate([jnp.exp2((tl - m_new).astype(BF16)) for tl in tiles], axis=1)
            acc_ref[h] = acc_ref[h] * alpha + _dot(p, v_ref[:, sl])
            m_ref[h] = m_new

    @pl.when(j < i)
    def _():
        step(False)

    @pl.when(j == i)
    def _():
        step(True)
        outs = []
        for h in range(MLA_HEADS):
            acc = acc_ref[h]
            outs.append(acc[:, 0:MLA_V] / acc[:, MLA_V:MLA_V + 1])
        o_ref[...] = jnp.concatenate(outs, axis=1).astype(BF16)


def _attention(q, k, v, batch, seq):
    t, hp = q.shape
    tq = min(512, seq)
    nq = seq // tq
    pairs = [(i, j) for i in range(nq) for j in range(i + 1)]
    q_of = jnp.asarray([p[0] for p in pairs], jnp.int32)
    k_of = jnp.asarray([p[1] for p in pairs], jnp.int32)
    qspec = pl.BlockSpec((tq, hp), lambda b, p, qi, kj: (b * nq + qi[p], 0))
    kspec = pl.BlockSpec((tq, hp), lambda b, p, qi, kj: (b * nq + kj[p], 0))
    return pl.pallas_call(
        functools.partial(_attn_kernel, tq=tq),
        out_shape=jax.ShapeDtypeStruct((t, MLA_WIDTH), BF16),
        grid_spec=pltpu.PrefetchScalarGridSpec(
            num_scalar_prefetch=2,
            grid=(batch, len(pairs)),
            in_specs=[qspec, kspec, kspec],
            out_specs=pl.BlockSpec((tq, MLA_WIDTH), lambda b, p, qi, kj: (b * nq + qi[p], 0)),
            scratch_shapes=[pltpu.VMEM((MLA_HEADS, tq, HEAD_PAD), F32),
                            pltpu.VMEM((MLA_HEADS, tq, HEAD_PAD), F32)]),
        compiler_params=_params("parallel", "arbitrary"),
        name="mla_flash_attention",
    )(q_of, k_of, q, k, v)


PAIRS_PER_GROUP = EXPERTS_PER_GROUP * (EXPERTS_PER_GROUP - 1) // 2
N_CLASSES = N_GROUPS * PAIRS_PER_GROUP
LANES = 128
ROW_CHUNKS = D_MODEL // LANES
TOKEN_TILE = 512
EXPERT_TILE = 256


N_CLASS_ROWS = 32


def _route(scores, bias):
    biased = scores + bias
    col = [biased[e:e + 1, :] for e in range(N_EXPERTS)]
    gscore = []
    for g in range(N_GROUPS):
        a, b, c, d = col[g * EXPERTS_PER_GROUP:(g + 1) * EXPERTS_PER_GROUP]
        gscore.append(jnp.maximum(jnp.maximum(jnp.maximum(a + b, a + c), jnp.maximum(a + d, b + c)),
                                  jnp.maximum(b + d, c + d)))
    sel = []
    for g in range(N_GROUPS):
        ok = None
        for o in range(N_GROUPS):
            if o == g:
                continue
            cond = (gscore[g] > gscore[o]) if o < g else (gscore[g] >= gscore[o])
            ok = cond if ok is None else jnp.logical_and(ok, cond)
        grp = col[g * EXPERTS_PER_GROUP:(g + 1) * EXPERTS_PER_GROUP]
        for e in range(EXPERTS_PER_GROUP):
            beaten = jnp.zeros_like(grp[e])
            for o in range(EXPERTS_PER_GROUP):
                if o == e:
                    continue
                ahead = (grp[o] >= grp[e]) if o < e else (grp[o] > grp[e])
                beaten = beaten + jnp.where(ahead, 1.0, 0.0)
            sel.append(jnp.logical_and(ok, beaten < 1.5))
    lo = jnp.full_like(col[0], float(N_EXPERTS))
    hi = jnp.full_like(col[0], -1.0)
    for e in range(N_EXPERTS):
        lo = jnp.where(sel[e], jnp.minimum(lo, float(e)), lo)
        hi = jnp.where(sel[e], jnp.maximum(hi, float(e)), hi)
    return lo, hi


def _router_scores(h2, wr):
    h_hi, h_lo = _split_bf16(h2)
    w_hi, w_lo = _split_bf16(wr)
    return _sigmoid(_dot(h_hi, w_hi) + _dot(h_lo, w_hi) + _dot(h_hi, w_lo))


def _store_rows(ref, x):
    n = x.shape[0]
    for c in range(ROW_CHUNKS):
        ref[pl.ds(c, n, stride=ROW_CHUNKS), :] = x[:, c * LANES:(c + 1) * LANES]


def _load_rows(ref, n):
    return jnp.concatenate([ref[pl.ds(c, n, stride=ROW_CHUNKS), :] for c in range(ROW_CHUNKS)], axis=1)


def _merge_kernel(x_ref, oa_ref, ob_ref, g_ref, mod_ref, wa_ref, wb_ref, wo_ref, n2_ref,
                  wrt_ref, rb_ref, xn_ref, row_ref, meta_ref):
    d = D_MODEL
    tm = x_ref.shape[0]
    mod = mod_ref[0]
    ya = _dot(oa_ref[...], wa_ref[...])
    yb = _dot(ob_ref[...], wb_ref[...])
    g = g_ref[...].astype(F32)
    merged = g[:, 0:d] * ya + g[:, d:2 * d] * yb
    xn = x_ref[...] + mod[2:3] * _dot(merged.astype(BF16), wo_ref[...])
    xn_ref[...] = xn
    h2 = _rms(xn, n2_ref[...]) * (1.0 + mod[4:5]) + mod[3:4]
    _store_rows(row_ref, h2)

    h_hi, h_lo = _split_bf16(h2)
    w_hi, w_lo = _split_bf16(wrt_ref[...])
    scores = _sigmoid(_dot_nt(w_hi, h_hi) + _dot_nt(w_hi, h_lo) + _dot_nt(w_lo, h_hi))
    lo, hi = _route(scores, rb_ref[...])

    grp = jnp.floor(lo * (1.0 / EXPERTS_PER_GROUP))
    a = lo - grp * EXPERTS_PER_GROUP
    b = hi - grp * EXPERTS_PER_GROUP
    cls = grp * PAIRS_PER_GROUP + a * (2 * EXPERTS_PER_GROUP - 1 - a) * 0.5 + (b - a - 1.0)
    c_row = lax.broadcasted_iota(jnp.int32, (N_CLASS_ROWS, tm), 0).astype(F32)
    onehot = jnp.where(c_row == cls, 1.0, 0.0)
    r_i = lax.broadcasted_iota(jnp.int32, (tm, tm), 0)
    c_i = lax.broadcasted_iota(jnp.int32, (tm, tm), 1)
    earlier = jnp.where(r_i < c_i, 1.0, 0.0).astype(BF16)
    rank = jnp.sum(onehot * _dot(onehot.astype(BF16), earlier), axis=0, keepdims=True)
    m_row = lax.broadcasted_iota(jnp.int32, (8, tm), 0)
    meta_ref[...] = jnp.where(m_row == 0, cls, jnp.where(m_row == 1, rank, 0.0))


def _merge(x2, oa, ob, gsig, mod_l, w_br_a, w_br_b, w_out, n2, w_router, router_bias, seq):
    t, d = x2.shape
    tm = TOKEN_TILE
    rows = lambda n: pl.BlockSpec((tm, n), lambda i: (i, 0))
    return pl.pallas_call(
        _merge_kernel,
        out_shape=(jax.ShapeDtypeStruct((t, d), F32), jax.ShapeDtypeStruct((t * ROW_CHUNKS, LANES), F32),
                   jax.ShapeDtypeStruct((8, t), F32)),
        grid=(t // tm,),
        in_specs=[rows(d), rows(HG_WIDTH), rows(MLA_WIDTH), rows(2 * d),
                  pl.BlockSpec((1, 6, d), lambda i: ((i * tm) // seq, 0, 0)),
                  _resident(w_br_a.shape), _resident(w_br_b.shape), _resident(w_out.shape),
                  _resident((1, d)), _resident((N_EXPERTS, d)), _resident((N_EXPERTS, 1))],
        out_specs=(rows(d), pl.BlockSpec((tm * ROW_CHUNKS, LANES), lambda i: (i, 0)),
                   pl.BlockSpec((8, tm), lambda i: (0, i))),
        compiler_params=_params("parallel"),
        name="merge_outproj_router",
    )(x2, oa, ob, gsig, mod_l, w_br_a.astype(BF16), w_br_b.astype(BF16), w_out.astype(BF16),
      n2[None, :], w_router.T, router_bias[:, None])


def _dispatch_plan(meta, n_tiles_e):
    t = meta.shape[1]
    n_tok_tiles = t // TOKEN_TILE
    cls = meta[0].astype(jnp.int32)
    rank = meta[1].astype(jnp.int32)
    own = cls.reshape(n_tok_tiles, TOKEN_TILE, 1) == jnp.arange(N_CLASSES, dtype=jnp.int32)
    counts = jnp.sum(own.astype(jnp.int32), axis=1)
    total = jnp.sum(counts, axis=0)
    total_pad = (total + EXPERT_TILE - 1) // EXPERT_TILE * EXPERT_TILE
    ends = jnp.cumsum(total_pad)
    base = (ends - total_pad)[None, :] + jnp.cumsum(counts, axis=0) - counts
    pos = jnp.sum(jnp.where(own, base[:, None, :], 0), axis=2).reshape(t) + rank

    tile_start = jnp.arange(n_tiles_e, dtype=jnp.int32) * EXPERT_TILE
    n_valid = ends[-1] // EXPERT_TILE
    tile_cls = jnp.sum((tile_start[:, None] >= ends[None, :]).astype(jnp.int32), axis=1)
    last_cls = jnp.take(tile_cls, n_valid - 1)
    valid = jnp.arange(n_tiles_e, dtype=jnp.int32) < n_valid
    tile_cls = jnp.where(valid, tile_cls, last_cls)
    pair_lo = jnp.asarray([a for a in range(EXPERTS_PER_GROUP) for b in range(a + 1, EXPERTS_PER_GROUP)], jnp.int32)
    pair_hi = jnp.asarray([b for a in range(EXPERTS_PER_GROUP) for b in range(a + 1, EXPERTS_PER_GROUP)], jnp.int32)
    grp = tile_cls // PAIRS_PER_GROUP
    e_lo = grp * EXPERTS_PER_GROUP + jnp.take(pair_lo, tile_cls % PAIRS_PER_GROUP)
    e_hi = grp * EXPERTS_PER_GROUP + jnp.take(pair_hi, tile_cls % PAIRS_PER_GROUP)
    pad_tile = (ends // EXPERT_TILE - 1).astype(jnp.int32)
    used = jnp.concatenate([(total > 0).astype(jnp.int32), n_valid[None].astype(jnp.int32)])
    return (pos * ROW_CHUNKS).astype(jnp.int32), pad_tile, used, e_lo, e_hi, valid.astype(jnp.int32)


DMA_UNROLL = 8


def _row_copies(n_rows, make_copy):
    def issue(ui, carry):
        for u in range(DMA_UNROLL):
            make_copy(ui * DMA_UNROLL + u).start(priority=u % 2)
        return carry
    lax.fori_loop(0, n_rows // DMA_UNROLL, issue, 0)


def _token_slab(ref, first_row):
    return ref.at[pl.ds(pl.multiple_of(first_row, ROW_CHUNKS), ROW_CHUNKS)]


def _dispatch_kernel(pos_ref, pad_tile_ref, used_ref, rows_ref, dst_hbm, zero_ref, sem, zero_sem):
    tm = rows_ref.shape[0] // ROW_CHUNKS
    tile_rows = EXPERT_TILE * ROW_CHUNKS

    @pl.when(pl.program_id(0) == 0)
    def _():
        zero_ref[...] = jnp.zeros_like(zero_ref)
        n_tiles = dst_hbm.shape[0] // tile_rows

        def clear(tile):
            first = pl.multiple_of(tile * tile_rows, tile_rows)
            return pltpu.make_async_copy(zero_ref, dst_hbm.at[pl.ds(first, tile_rows)], zero_sem)

        def start_tail(tile, carry):
            clear(tile).start()
            return carry

        def wait_tail(tile, carry):
            clear(tile).wait()
            return carry
        for c in range(N_CLASSES):
            @pl.when(used_ref[c] != 0)
            def _():
                clear(pad_tile_ref[c]).start()
        lax.fori_loop(used_ref[N_CLASSES], n_tiles, start_tail, 0)
        for c in range(N_CLASSES):
            @pl.when(used_ref[c] != 0)
            def _():
                clear(pad_tile_ref[c]).wait()
        lax.fori_loop(used_ref[N_CLASSES], n_tiles, wait_tail, 0)

    base = pl.program_id(0) * tm
    _row_copies(tm, lambda r: pltpu.make_async_copy(
        _token_slab(rows_ref, r * ROW_CHUNKS), _token_slab(dst_hbm, pos_ref[base + r]), sem))
    pltpu.make_async_copy(rows_ref, dst_hbm.at[pl.ds(0, tm * ROW_CHUNKS)], sem).wait()


def _dispatch_rows(pos, pad_tile, used, rows, n_dst):
    t = rows.shape[0] // ROW_CHUNKS
    tm = TOKEN_TILE
    return pl.pallas_call(
        _dispatch_kernel,
        out_shape=jax.ShapeDtypeStruct((n_dst * ROW_CHUNKS, LANES), rows.dtype),
        grid_spec=pltpu.PrefetchScalarGridSpec(
            num_scalar_prefetch=3, grid=(t // tm,),
            in_specs=[pl.BlockSpec((tm * ROW_CHUNKS, LANES), lambda i, p, pt, us: (i, 0))],
            out_specs=pl.BlockSpec(memory_space=pl.ANY),
            scratch_shapes=[pltpu.VMEM((EXPERT_TILE * ROW_CHUNKS, LANES), rows.dtype),
                            pltpu.SemaphoreType.DMA, pltpu.SemaphoreType.DMA]),
        compiler_params=pltpu.CompilerParams(dimension_semantics=("arbitrary",), disable_bounds_checks=True,
                                             has_side_effects=True),
        name="dispatch_rows",
    )(pos, pad_tile, used, rows)


def _moe_kernel(elo_ref, ehi_ref, valid_ref, x_ref, wr_ref, wgl_ref, wul_ref, wdl_ref, wgh_ref, wuh_ref,
                wdh_ref, y_ref):
    j = pl.program_id(0)

    @pl.when(valid_ref[j] == 0)
    def _():
        y_ref[...] = jnp.zeros_like(y_ref)

    @pl.when(valid_ref[j] != 0)
    def _():
        h2 = _load_rows(x_ref, EXPERT_TILE)
        scores = _router_scores(h2, wr_ref[...])
        lane = lax.broadcasted_iota(jnp.int32, scores.shape, 1)
        s_lo = jnp.sum(jnp.where(lane == elo_ref[j], scores, 0.0), axis=1, keepdims=True)
        s_hi = jnp.sum(jnp.where(lane == ehi_ref[j], scores, 0.0), axis=1, keepdims=True)
        total = s_lo + s_hi
        hb = h2.astype(BF16)
        he_lo = _silu(_dot(hb, wgl_ref[0])) * _dot(hb, wul_ref[0])
        y = (s_lo / total) * _dot(he_lo.astype(BF16), wdl_ref[0])
        he_hi = _silu(_dot(hb, wgh_ref[0])) * _dot(hb, wuh_ref[0])
        _store_rows(y_ref, y + (s_hi / total) * _dot(he_hi.astype(BF16), wdh_ref[0]))


def _moe(rows_sorted, e_lo, e_hi, valid, w_router, w_gate, w_up, w_down):
    n_pad = rows_sorted.shape[0] // ROW_CHUNKS
    d = D_MODEL
    lo3 = lambda j, el, eh, va: (el[j], 0, 0)
    hi3 = lambda j, el, eh, va: (eh[j], 0, 0)
    tile = pl.BlockSpec((EXPERT_TILE * ROW_CHUNKS, LANES), lambda j, el, eh, va: (j, 0))
    wg, wu, wd = w_gate.astype(BF16), w_up.astype(BF16), w_down.astype(BF16)
    return pl.pallas_call(
        _moe_kernel,
        out_shape=jax.ShapeDtypeStruct(rows_sorted.shape, F32),
        grid_spec=pltpu.PrefetchScalarGridSpec(
            num_scalar_prefetch=3, grid=(n_pad // EXPERT_TILE,),
            in_specs=[tile, _resident(w_router.shape),
                      pl.BlockSpec((1, d, D_EXPERT), lo3), pl.BlockSpec((1, d, D_EXPERT), lo3),
                      pl.BlockSpec((1, D_EXPERT, d), lo3),
                      pl.BlockSpec((1, d, D_EXPERT), hi3), pl.BlockSpec((1, d, D_EXPERT), hi3),
                      pl.BlockSpec((1, D_EXPERT, d), hi3)],
            out_specs=tile),
        compiler_params=_params("arbitrary"),
        name="moe_experts",
    )(e_lo, e_hi, valid, rows_sorted, w_router, wg, wu, wd, wg, wu, wd)


def _collect_kernel(pos_ref, x_ref, ys_hbm, mod_ref, fg_ref, o_ref, buf0, buf1, sems, *, final):
    tm = x_ref.shape[0]
    i = pl.program_id(0)
    n = pl.num_programs(0)
    bufs = (buf0, buf1)

    def fetch(tile, slot):
        base = tile * tm
        _row_copies(tm, lambda r: pltpu.make_async_copy(
            _token_slab(ys_hbm, pos_ref[base + r]), _token_slab(bufs[slot], r * ROW_CHUNKS), sems.at[slot]))

    def finish(slot):
        pltpu.make_async_copy(ys_hbm.at[pl.ds(0, tm * ROW_CHUNKS)], bufs[slot], sems.at[slot]).wait()
        xo = x_ref[...] + mod_ref[0][5:6] * _load_rows(bufs[slot], tm)
        if final:
            xo = _rms(xo, fg_ref[...])
        o_ref[...] = xo

    @pl.when(i == 0)
    def _():
        fetch(0, 0)

    for slot in range(2):
        @pl.when(jnp.logical_and(i + 1 < n, (i + 1) % 2 == slot))
        def _():
            fetch(i + 1, slot)

    for slot in range(2):
        @pl.when(i % 2 == slot)
        def _():
            finish(slot)


def _collect_residual(pos, xn, y_sorted, mod_l, final_g, final, seq):
    t, d = xn.shape
    tm = TOKEN_TILE
    rows = pl.BlockSpec((tm, d), lambda i, p: (i, 0))
    return pl.pallas_call(
        functools.partial(_collect_kernel, final=final),
        out_shape=jax.ShapeDtypeStruct((t, d), F32),
        grid_spec=pltpu.PrefetchScalarGridSpec(
            num_scalar_prefetch=1, grid=(t // tm,),
            in_specs=[rows, pl.BlockSpec(memory_space=pl.ANY),
                      pl.BlockSpec((1, 6, d), lambda i, p: ((i * tm) // seq, 0, 0)),
                      pl.BlockSpec((1, d), lambda i, p: (0, 0))],
            out_specs=rows,
            scratch_shapes=[pltpu.VMEM((tm * ROW_CHUNKS, LANES), F32), pltpu.VMEM((tm * ROW_CHUNKS, LANES), F32),
                            pltpu.SemaphoreType.DMA((2,))]),
        compiler_params=pltpu.CompilerParams(dimension_semantics=("arbitrary",), disable_bounds_checks=True,
                                             vmem_limit_bytes=V7X_VMEM_LIMIT_BYTES),
        name="collect_residual",
    )(pos, xn, y_sorted, mod_l, final_g[None, :])


def kernel(x, c, positions, ada_w, ada_b, norm1_g, w_in, hg_lb_logits, hg_norm_g, q_norm_g, w_q_up,
           kv_norm_g, w_kv_up, w_br_a, w_br_b, w_out, norm2_g, w_router, router_bias, w_gate, w_up,
           w_down, final_g):
    batch, seq, d = x.shape
    depth = ada_w.shape[0]
    t = batch * seq
    n_tiles_e = (t + N_CLASSES * (EXPERT_TILE - 1)) // EXPERT_TILE
    mod = _modulation(c, ada_w, ada_b).reshape(depth, batch, 6, d)
    cos_t, sin_t = _rope_tables(positions)
    x2 = x.reshape(t, d)
    for l in range(depth):
        hq, hf, hi, hgate, gsig, q, k, v = _input_projection(
            x2, mod[l], norm1_g[l], cos_t, sin_t, w_in[l], q_norm_g[l], w_q_up[l],
            kv_norm_g[l], w_kv_up[l], seq)
        oa = _hgrn(hg_lb_logits, hg_norm_g[l], hq, hf, hi, hgate, l, batch, seq)
        ob = _attention(q, k, v, batch, seq)
        xn, rows, meta = _merge(x2, oa, ob, gsig, mod[l], w_br_a[l], w_br_b[l], w_out[l], norm2_g[l],
                                w_router, router_bias, seq)
        pos, pad_tile, used, e_lo, e_hi, valid = _dispatch_plan(meta, n_tiles_e)
        rows_sorted = _dispatch_rows(pos, pad_tile, used, rows, n_tiles_e * EXPERT_TILE)
        y_sorted = _moe(rows_sorted, e_lo, e_hi, valid, w_router, w_gate[l], w_up[l], w_down[l])
        x2 = _collect_residual(pos, xn, y_sorted, mod[l], final_g, l == depth - 1, seq)
    return x2.reshape(batch, seq, d)
```

```python
import functools
import math

import jax
import jax.numpy as jnp
from jax import lax
from jax.experimental import pallas as pl
from jax.experimental.pallas import tpu as pltpu

F32 = jnp.float32
BF16 = jnp.bfloat16

D_MODEL = 1024
CHUNK = 64
EPS = 1e-6

HG_HEADS = 4
HG_DK = 128
HG_DV = 128
HG_WIDTH = HG_HEADS * HG_DV
HG_SUB = 8
HG_LEVELS = (8, 16, 32)

MLA_HEADS = 8
MLA_NOPE = 64
MLA_ROPE = 32
MLA_V = 64
MLA_Q_LORA = 384
MLA_KV_LORA = 256
MLA_DQK = MLA_NOPE + MLA_ROPE
MLA_WIDTH = MLA_HEADS * MLA_V
ROPE_BASE = 10000.0
HEAD_PAD = 128
LOG2_E = math.log2(math.e)

N_EXPERTS = 16
N_GROUPS = 4
EXPERTS_PER_GROUP = N_EXPERTS // N_GROUPS
D_EXPERT = 512

IN_SIZES = (HG_HEADS * HG_DK, HG_HEADS * HG_DK, HG_HEADS * HG_DV, HG_WIDTH,
            MLA_Q_LORA, MLA_KV_LORA, MLA_ROPE, 2 * D_MODEL)

V7X_VMEM_LIMIT_BYTES = 56 * 1024 * 1024


def _params(*sem):
    return pltpu.CompilerParams(dimension_semantics=sem, vmem_limit_bytes=V7X_VMEM_LIMIT_BYTES)


def _resident(shape):
    nd = len(shape)
    return pl.BlockSpec(shape, lambda *_: (0,) * nd, pipeline_mode=pl.Buffered(1))


def _sigmoid(x):
    return 1.0 / (1.0 + jnp.exp(-x))


def _silu(x):
    return x * _sigmoid(x)


def _dot(a, b):
    return jnp.dot(a, b, preferred_element_type=F32)


def _dot_nt(a, b):
    return lax.dot_general(a, b, (((1,), (1,)), ((), ())), preferred_element_type=F32)


def _dot_tn(a, b):
    return lax.dot_general(a, b, (((0,), (0,)), ((), ())), preferred_element_type=F32)


def _split_bf16(x):
    hi = x.astype(BF16)
    lo = (x - hi.astype(F32)).astype(BF16)
    return hi, lo


def _rms(x, g):
    return x * lax.rsqrt(jnp.mean(x * x, axis=-1, keepdims=True) + EPS) * g


def _mod_kernel(c_ref, w_ref, b_ref, o_ref):
    ca = _silu(c_ref[...])
    o_ref[0] = _dot(ca.astype(BF16), w_ref[0].astype(BF16)) + b_ref[0]


def _modulation(c, ada_w, ada_b):
    depth, d, n = ada_w.shape
    b = c.shape[0]
    tn = 768
    return pl.pallas_call(
        _mod_kernel,
        out_shape=jax.ShapeDtypeStruct((depth, b, n), F32),
        grid=(depth, n // tn),
        in_specs=[pl.BlockSpec((b, d), lambda l, j: (0, 0)),
                  pl.BlockSpec((1, d, tn), lambda l, j: (l, 0, j)),
                  pl.BlockSpec((1, 1, tn), lambda l, j: (l, 0, j))],
        out_specs=pl.BlockSpec((1, b, tn), lambda l, j: (l, 0, j)),
        compiler_params=_params("parallel", "parallel"),
        name="adaln_modulation",
    )(c, ada_w, ada_b.reshape(depth, 1, n))


def _rope_kernel(pos_ref, inv_ref, msk_ref, sgn_ref, cos_ref, sin_ref):
    ang = pos_ref[...].astype(F32) * inv_ref[...]
    cos_ref[...] = jnp.cos(ang) * msk_ref[...]
    sin_ref[...] = jnp.sin(ang) * sgn_ref[...]


def _rope_tables(positions):
    t = positions.size
    half = MLA_ROPE // 2
    inv = ROPE_BASE ** (-jnp.arange(half, dtype=F32) / half)
    z64, z32, one16 = jnp.zeros((MLA_NOPE,), F32), jnp.zeros((32,), F32), jnp.ones((half,), F32)
    inv_row = jnp.concatenate([z64, inv, inv, z32])[None, :]
    msk_row = jnp.concatenate([z64, one16, one16, z32])[None, :]
    sgn_row = jnp.concatenate([z64, -one16, one16, z32])[None, :]
    tr = min(2048, t)
    row = pl.BlockSpec((1, HEAD_PAD), lambda i: (0, 0))
    tab = pl.BlockSpec((tr, HEAD_PAD), lambda i: (i, 0))
    return pl.pallas_call(
        _rope_kernel,
        out_shape=(jax.ShapeDtypeStruct((t, HEAD_PAD), F32),) * 2,
        grid=(t // tr,),
        in_specs=[pl.BlockSpec((tr, 1), lambda i: (i, 0)), row, row, row],
        out_specs=(tab, tab),
        compiler_params=_params("parallel"),
        name="rope_tables",
    )(positions.reshape(t, 1), inv_row, msk_row, sgn_row)


def _proj_kernel(x_ref, mod_ref, n1_ref, cos_ref, sin_ref, wcat_ref,
                 qn_ref, wqa_ref, wqs_ref, kn_ref, wk_ref, wv_ref, one_ref,
                 hq_ref, hf_ref, hi_ref, hgate_ref, gsig_ref, q_ref, k_ref, v_ref):
    x = x_ref[...]
    mod = mod_ref[0]
    h = _rms(x, n1_ref[...]) * (1.0 + mod[1:2]) + mod[0:1]
    hb = h.astype(BF16)
    n_h = 4 * HG_WIDTH
    n_g = 2 * D_MODEL

    ph = _dot(hb, wcat_ref[:, 0:n_h])
    w = HG_WIDTH
    hq_ref[...] = ph[:, 0:w].astype(BF16)
    hf_ref[...] = ph[:, w:2 * w]
    hi_ref[...] = ph[:, 2 * w:3 * w].astype(BF16)
    hgate_ref[...] = ph[:, 3 * w:4 * w].astype(BF16)

    gsig_ref[...] = _sigmoid(_dot(hb, wcat_ref[:, n_h:n_h + n_g])).astype(BF16)

    pc = _dot(hb, wcat_ref[:, n_h + n_g:])
    cq = pc[:, 0:MLA_Q_LORA]
    ckv = pc[:, MLA_Q_LORA:MLA_Q_LORA + MLA_KV_LORA]
    kra = pc[:, MLA_Q_LORA + MLA_KV_LORA:MLA_Q_LORA + MLA_KV_LORA + HEAD_PAD]
    krb = pc[:, MLA_Q_LORA + MLA_KV_LORA + HEAD_PAD:]

    cos_t = cos_ref[...]
    sin_t = sin_ref[...]
    lane = lax.broadcasted_iota(jnp.int32, cos_t.shape, 1)
    scale = MLA_DQK ** -0.5 * LOG2_E
    cq_tab = jnp.tile(scale * (cos_t + jnp.where(lane < MLA_NOPE, 1.0, 0.0)), (1, MLA_HEADS))
    sq_tab = jnp.tile(scale * sin_t, (1, MLA_HEADS))

    cqn = _rms(cq, qn_ref[...]).astype(BF16)
    q = _dot(cqn, wqa_ref[...]) * cq_tab + _dot(cqn, wqs_ref[...]) * sq_tab
    q_ref[...] = q.astype(BF16)

    ckvn = _rms(ckv, kn_ref[...]).astype(BF16)
    kpe = kra * cos_t + krb * sin_t
    k_ref[...] = (_dot(ckvn, wk_ref[...]) + jnp.tile(kpe, (1, MLA_HEADS))).astype(BF16)
    v_ref[...] = (_dot(ckvn, wv_ref[...]) + one_ref[...]).astype(BF16)


def _pad_heads(w, lo, hi, at):
    k, nh, _ = w.shape
    out = jnp.zeros((k, nh, HEAD_PAD), w.dtype)
    out = out.at[:, :, at:at + (hi - lo)].set(w[:, :, lo:hi])
    return out


def _input_projection(x2, mod_l, n1, cos_t, sin_t, w_in, q_norm_g, w_q_up, kv_norm_g, w_kv_up, seq):
    t, d = x2.shape
    tm = 256
    splits = [0]
    for s in IN_SIZES:
        splits.append(splits[-1] + s)
    kr = w_in[:, splits[6]:splits[7]]
    half = MLA_ROPE // 2
    z64 = jnp.zeros((d, MLA_NOPE), F32)
    z32 = jnp.zeros((d, HEAD_PAD - MLA_NOPE - MLA_ROPE), F32)
    kr_a = jnp.concatenate([z64, kr, z32], axis=1)
    kr_b = jnp.concatenate([z64, kr[:, half:], kr[:, :half], z32], axis=1)
    w_cat = jnp.concatenate([w_in[:, splits[0]:splits[4]], w_in[:, splits[7]:splits[8]],
                             w_in[:, splits[4]:splits[6]], kr_a, kr_b], axis=1).astype(BF16)

    wq = w_q_up.reshape(MLA_Q_LORA, MLA_HEADS, MLA_DQK)
    wq_all = _pad_heads(wq, 0, MLA_DQK, 0).reshape(MLA_Q_LORA, -1).astype(BF16)
    wq_swap = (_pad_heads(wq, MLA_NOPE + half, MLA_DQK, MLA_NOPE)
               + _pad_heads(wq, MLA_NOPE, MLA_NOPE + half, MLA_NOPE + half))
    wq_swap = wq_swap.reshape(MLA_Q_LORA, -1).astype(BF16)
    wkv = w_kv_up.reshape(MLA_KV_LORA, MLA_HEADS, MLA_NOPE + MLA_V)
    wk_all = _pad_heads(wkv, 0, MLA_NOPE, 0).reshape(MLA_KV_LORA, -1).astype(BF16)
    wv_all = _pad_heads(wkv, MLA_NOPE, MLA_NOPE + MLA_V, 0).reshape(MLA_KV_LORA, -1).astype(BF16)
    ones_row = jnp.tile(jnp.zeros((HEAD_PAD,), F32).at[MLA_V].set(1.0), MLA_HEADS)[None, :]

    hp = MLA_HEADS * HEAD_PAD
    rows = lambda n: pl.BlockSpec((tm, n), lambda i: (i, 0))
    outs = [(HG_WIDTH, BF16), (HG_WIDTH, F32), (HG_WIDTH, BF16), (HG_WIDTH, BF16),
            (2 * d, BF16), (hp, BF16), (hp, BF16), (hp, BF16)]
    return pl.pallas_call(
        _proj_kernel,
        out_shape=tuple(jax.ShapeDtypeStruct((t, n), dt) for n, dt in outs),
        grid=(t // tm,),
        in_specs=[rows(d),
                  pl.BlockSpec((1, 6, d), lambda i: ((i * tm) // seq, 0, 0)),
                  _resident((1, d)),
                  rows(HEAD_PAD), rows(HEAD_PAD),
                  _resident(w_cat.shape),
                  _resident((1, MLA_Q_LORA)), _resident(wq_all.shape), _resident(wq_swap.shape),
                  _resident((1, MLA_KV_LORA)), _resident(wk_all.shape), _resident(wv_all.shape),
                  _resident((1, hp))],
        out_specs=tuple(rows(n) for n, _ in outs),
        compiler_params=_params("parallel"),
        name="norm_input_projection",
    )(x2, mod_l, n1[None, :], cos_t, sin_t, w_cat,
      q_norm_g[None, :], wq_all, wq_swap, kv_norm_g[None, :], wk_all, wv_all, ones_row)


def _hgrn_kernel(lbl_ref, gn_ref, q_ref, f_ref, v_ref, gate_ref, o_ref, st_ref, *, layer, n_chunks):
    @pl.when(pl.program_id(1) == 0)
    def _():
        st_ref[...] = jnp.zeros_like(st_ref)

    lg = lbl_ref[...]
    ex = jnp.exp(lg - jnp.max(lg, axis=0, keepdims=True))
    soft = ex / jnp.sum(ex, axis=0, keepdims=True)
    lb_all = jnp.zeros_like(soft[0:1])
    for i in range(1, layer + 1):
        lb_all = lb_all + soft[i:i + 1]

    r_i = lax.broadcasted_iota(jnp.int32, (CHUNK, CHUNK), 0)
    c_i = lax.broadcasted_iota(jnp.int32, (CHUNK, CHUNK), 1)
    tri = jnp.where(r_i >= c_i, 1.0, 0.0).astype(BF16)
    level_masks = []
    for hs in HG_LEVELS:
        same = (r_i // (2 * hs)) == (c_i // (2 * hs))
        level_masks.append(jnp.logical_and(same, jnp.logical_and(r_i % (2 * hs) >= hs, c_i % (2 * hs) < hs)))
    sub_row = lax.broadcasted_iota(jnp.int32, (HG_SUB, 1), 0)
    n_sub = CHUNK // HG_SUB

    def chunk_body(ci, carry):
        r0 = pl.multiple_of(ci * CHUNK, CHUNK)
        rows = pl.ds(r0, CHUNK)
        for h in range(HG_HEADS):
            sl = slice(h * HG_DK, (h + 1) * HG_DK)
            lb = lb_all[:, sl]
            q = q_ref[rows, sl].astype(F32)
            vb = v_ref[rows, sl]
            v = vb.astype(F32)
            f = lb + (1.0 - lb) * _sigmoid(f_ref[rows, sl])
            g = jnp.log2(f)
            k = 1.0 - f
            qf = _silu(q) * (HG_DK ** -0.5)
            g_hi, g_lo = _split_bf16(g)
            b = _dot(tri, g_hi) + _dot(tri, g_lo)

            att = jnp.zeros((CHUNK, CHUNK), F32)
            for hs, mask in zip(HG_LEVELS, level_masks):
                ref = jnp.concatenate(
                    [jnp.broadcast_to(b[j + hs - 1:j + hs], (2 * hs, HG_DK)) for j in range(0, CHUNK, 2 * hs)],
                    axis=0)
                qt = (qf * jnp.exp2(b - ref)).astype(BF16)
                kt = (k * jnp.exp2(ref - b)).astype(BF16)
                att = att + jnp.where(mask, _dot_nt(qt, kt), 0.0)
            o = _dot(att.astype(BF16), vb)

            blocks = []
            for i in range(n_sub):
                rs = slice(i * HG_SUB, (i + 1) * HG_SUB)
                b_i, q_i, k_i, v_i = b[rs], qf[rs], k[rs], v[rs]
                acc = jnp.zeros((HG_SUB, HG_DV), F32)
                for s in range(HG_SUB):
                    a = jnp.sum(jnp.exp2(b_i - b_i[s:s + 1]) * (q_i * k_i[s:s + 1]), axis=1, keepdims=True)
                    acc = acc + jnp.where(sub_row >= s, a, 0.0) * v_i[s:s + 1]
                blocks.append(acc)
            o = o + jnp.concatenate(blocks, axis=0)

            st = st_ref[h]
            o = o + _dot_nt((qf * jnp.exp2(b)).astype(BF16), st.astype(BF16))
            b_last = b[CHUNK - 1:CHUNK]
            kd = (k * jnp.exp2(b_last - b)).astype(BF16)
            st_ref[h] = st * jnp.exp2(b_last) + _dot_tn(vb, kd)

            gt = gate_ref[rows, sl].astype(F32)
            o_ref[rows, sl] = (_rms(o, gn_ref[:, sl]) * _silu(gt)).astype(BF16)
        return carry

    lax.fori_loop(0, n_chunks, chunk_body, 0, unroll=2)


def _hgrn(hg_lb_logits, hg_norm_g, hq, hf, hi, hgate, layer, batch, seq):
    t = hq.shape[0]
    lc = min(512, seq)
    nb = seq // lc
    rows = pl.BlockSpec((lc, HG_WIDTH), lambda b, j: (b * nb + j, 0))
    kern = functools.partial(_hgrn_kernel, layer=layer, n_chunks=lc // CHUNK)
    return pl.pallas_call(
        kern,
        out_shape=jax.ShapeDtypeStruct((t, HG_WIDTH), BF16),
        grid=(batch, nb),
        in_specs=[pl.BlockSpec(hg_lb_logits.shape, lambda b, j: (0, 0)),
                  pl.BlockSpec((1, HG_WIDTH), lambda b, j: (0, 0)),
                  rows, rows, rows, rows],
        out_specs=rows,
        scratch_shapes=[pltpu.VMEM((HG_HEADS, HG_DV, HG_DK), F32)],
        compiler_params=_params("parallel", "arbitrary"),
        name="hgrn2_chunkwise",
    )(hg_lb_logits, hg_norm_g[None, :], hq, hf, hi, hgate)


def _attn_kernel(qi_ref, kj_ref, q_ref, k_ref, v_ref, o_ref, m_ref, acc_ref, *, tq):
    p_id = pl.program_id(1)
    i = qi_ref[p_id]
    j = kj_ref[p_id]
    n_lane_tiles = tq // HEAD_PAD

    @pl.when(j == 0)
    def _():
        m_ref[...] = jnp.full_like(m_ref, -jnp.inf)
        acc_ref[...] = jnp.zeros_like(acc_ref)

    def step(diagonal):
        if diagonal:
            qc = lax.broadcasted_iota(jnp.int32, (tq, tq), 0) // CHUNK
            kc = lax.broadcasted_iota(jnp.int32, (tq, tq), 1) // CHUNK
            visible = kc <= qc
        for h in range(MLA_HEADS):
            sl = slice(h * HEAD_PAD, (h + 1) * HEAD_PAD)
            s = _dot_nt(q_ref[:, sl], k_ref[:, sl])
            if diagonal:
                s = jnp.where(visible, s, -jnp.inf)
            tiles = [s[:, t * HEAD_PAD:(t + 1) * HEAD_PAD] for t in range(n_lane_tiles)]
            m_tile = tiles[0]
            for t in range(1, n_lane_tiles):
                m_tile = jnp.maximum(m_tile, tiles[t])
            m_prev = m_ref[h]
            m_new = jnp.maximum(m_prev, jnp.max(m_tile, axis=1, keepdims=True))
            alpha = jnp.exp2(m_prev - m_new)
            p = jnp.concatenate([jnp.exp2((tl - m_new).astype(BF16)) for tl in tiles], axis=1)
            acc_ref[h] = acc_ref[h] * alpha + _dot(p, v_ref[:, sl])
            m_ref[h] = m_new

    @pl.when(j < i)
    def _():
        step(False)

    @pl.when(j == i)
    def _():
        step(True)
        outs = []
        for h in range(MLA_HEADS):
            acc = acc_ref[h]
            outs.append(acc[:, 0:MLA_V] / acc[:, MLA_V:MLA_V + 1])
        o_ref[...] = jnp.concatenate(outs, axis=1).astype(BF16)


def _attention(q, k, v, batch, seq):
    t, hp = q.shape
    tq = min(512, seq)
    nq = seq // tq
    pairs = [(i, j) for i in range(nq) for j in range(i + 1)]
    q_of = jnp.asarray([p[0] for p in pairs], jnp.int32)
    k_of = jnp.asarray([p[1] for p in pairs], jnp.int32)
    qspec = pl.BlockSpec((tq, hp), lambda b, p, qi, kj: (b * nq + qi[p], 0))
    kspec = pl.BlockSpec((tq, hp), lambda b, p, qi, kj: (b * nq + kj[p], 0))
    return pl.pallas_call(
        functools.partial(_attn_kernel, tq=tq),
        out_shape=jax.ShapeDtypeStruct((t, MLA_WIDTH), BF16),
        grid_spec=pltpu.PrefetchScalarGridSpec(
            num_scalar_prefetch=2,
            grid=(batch, len(pairs)),
            in_specs=[qspec, kspec, kspec],
            out_specs=pl.BlockSpec((tq, MLA_WIDTH), lambda b, p, qi, kj: (b * nq + qi[p], 0)),
            scratch_shapes=[pltpu.VMEM((MLA_HEADS, tq, HEAD_PAD), F32),
                            pltpu.VMEM((MLA_HEADS, tq, HEAD_PAD), F32)]),
        compiler_params=_params("parallel", "arbitrary"),
        name="mla_flash_attention",
    )(q_of, k_of, q, k, v)


PAIRS_PER_GROUP = EXPERTS_PER_GROUP * (EXPERTS_PER_GROUP - 1) // 2
N_CLASSES = N_GROUPS * PAIRS_PER_GROUP
LANES = 128
ROW_CHUNKS = D_MODEL // LANES
TOKEN_TILE = 512
EXPERT_TILE = 256


N_CLASS_ROWS = 32


def _route(scores, bias):
    biased = scores + bias
    col = [biased[e:e + 1, :] for e in range(N_EXPERTS)]
    gscore = []
    for g in range(N_GROUPS):
        a, b, c, d = col[g * EXPERTS_PER_GROUP:(g + 1) * EXPERTS_PER_GROUP]
        gscore.append(jnp.maximum(jnp.maximum(jnp.maximum(a + b, a + c), jnp.maximum(a + d, b + c)),
                                  jnp.maximum(b + d, c + d)))
    sel = []
    for g in range(N_GROUPS):
        ok = None
        for o in range(N_GROUPS):
            if o == g:
                continue
            cond = (gscore[g] > gscore[o]) if o < g else (gscore[g] >= gscore[o])
            ok = cond if ok is None else jnp.logical_and(ok, cond)
        grp = col[g * EXPERTS_PER_GROUP:(g + 1) * EXPERTS_PER_GROUP]
        for e in range(EXPERTS_PER_GROUP):
            beaten = jnp.zeros_like(grp[e])
            for o in range(EXPERTS_PER_GROUP):
                if o == e:
                    continue
                ahead = (grp[o] >= grp[e]) if o < e else (grp[o] > grp[e])
                beaten = beaten + jnp.where(ahead, 1.0, 0.0)
            sel.append(jnp.logical_and(ok, beaten < 1.5))
    lo = jnp.full_like(col[0], float(N_EXPERTS))
    hi = jnp.full_like(col[0], -1.0)
    for e in range(N_EXPERTS):
        lo = jnp.where(sel[e], jnp.minimum(lo, float(e)), lo)
        hi = jnp.where(sel[e], jnp.maximum(hi, float(e)), hi)
    return lo, hi


def _store_rows(ref, x):
    n = x.shape[0]
    for c in range(ROW_CHUNKS):
        ref[pl.ds(c, n, stride=ROW_CHUNKS), :] = x[:, c * LANES:(c + 1) * LANES]


def _load_rows(ref, n):
    return jnp.concatenate([ref[pl.ds(c, n, stride=ROW_CHUNKS), :] for c in range(ROW_CHUNKS)], axis=1)


def _merge_kernel(x_ref, oa_ref, ob_ref, g_ref, mod_ref, wa_ref, wb_ref, wo_ref, n2_ref,
                  wrt_ref, rb_ref, xn_ref, row_ref, meta_ref):
    d = D_MODEL
    tm = x_ref.shape[0]
    mod = mod_ref[0]
    ya = _dot(oa_ref[...], wa_ref[...])
    yb = _dot(ob_ref[...], wb_ref[...])
    g = g_ref[...].astype(F32)
    merged = g[:, 0:d] * ya + g[:, d:2 * d] * yb
    xn = x_ref[...] + mod[2:3] * _dot(merged.astype(BF16), wo_ref[...])
    xn_ref[...] = xn
    h2 = _rms(xn, n2_ref[...]) * (1.0 + mod[4:5]) + mod[3:4]
    _store_rows(row_ref, h2)

    h_hi, h_lo = _split_bf16(h2)
    w_hi, w_lo = _split_bf16(wrt_ref[...])
    scores = _sigmoid(_dot_nt(w_hi, h_hi) + _dot_nt(w_hi, h_lo) + _dot_nt(w_lo, h_hi))
    lo, hi = _route(scores, rb_ref[...])

    grp = jnp.floor(lo * (1.0 / EXPERTS_PER_GROUP))
    a = lo - grp * EXPERTS_PER_GROUP
    b = hi - grp * EXPERTS_PER_GROUP
    cls = grp * PAIRS_PER_GROUP + a * (2 * EXPERTS_PER_GROUP - 1 - a) * 0.5 + (b - a - 1.0)
    c_row = lax.broadcasted_iota(jnp.int32, (N_CLASS_ROWS, tm), 0).astype(F32)
    onehot = jnp.where(c_row == cls, 1.0, 0.0)
    r_i = lax.broadcasted_iota(jnp.int32, (tm, tm), 0)
    c_i = lax.broadcasted_iota(jnp.int32, (tm, tm), 1)
    earlier = jnp.where(r_i < c_i, 1.0, 0.0).astype(BF16)
    rank = jnp.sum(onehot * _dot(onehot.astype(BF16), earlier), axis=0, keepdims=True)
    m_row = lax.broadcasted_iota(jnp.int32, (8, tm), 0)
    meta_ref[...] = jnp.where(m_row == 0, cls, jnp.where(m_row == 1, rank, 0.0))


def _merge(x2, oa, ob, gsig, mod_l, w_br_a, w_br_b, w_out, n2, w_router, router_bias, seq):
    t, d = x2.shape
    tm = TOKEN_TILE
    rows = lambda n: pl.BlockSpec((tm, n), lambda i: (i, 0))
    return pl.pallas_call(
        _merge_kernel,
        out_shape=(jax.ShapeDtypeStruct((t, d), F32), jax.ShapeDtypeStruct((t * ROW_CHUNKS, LANES), F32),
                   jax.ShapeDtypeStruct((8, t), F32)),
        grid=(t // tm,),
        in_specs=[rows(d), rows(HG_WIDTH), rows(MLA_WIDTH), rows(2 * d),
                  pl.BlockSpec((1, 6, d), lambda i: ((i * tm) // seq, 0, 0)),
                  _resident(w_br_a.shape), _resident(w_br_b.shape), _resident(w_out.shape),
                  _resident((1, d)), _resident((N_EXPERTS, d)), _resident((N_EXPERTS, 1))],
        out_specs=(rows(d), pl.BlockSpec((tm * ROW_CHUNKS, LANES), lambda i: (i, 0)),
                   pl.BlockSpec((8, tm), lambda i: (0, i))),
        compiler_params=_params("parallel"),
        name="merge_outproj_router",
    )(x2, oa, ob, gsig, mod_l, w_br_a.astype(BF16), w_br_b.astype(BF16), w_out.astype(BF16),
      n2[None, :], w_router.T, router_bias[:, None])


def _dispatch_plan(meta, n_tiles_e):
    t = meta.shape[1]
    n_tok_tiles = t // TOKEN_TILE
    cls = meta[0].astype(jnp.int32)
    rank = meta[1].astype(jnp.int32)
    own = cls.reshape(n_tok_tiles, TOKEN_TILE, 1) == jnp.arange(N_CLASSES, dtype=jnp.int32)
    counts = jnp.sum(own.astype(jnp.int32), axis=1)
    total = jnp.sum(counts, axis=0)
    total_pad = (total + EXPERT_TILE - 1) // EXPERT_TILE * EXPERT_TILE
    ends = jnp.cumsum(total_pad)
    base = (ends - total_pad)[None, :] + jnp.cumsum(counts, axis=0) - counts
    pos = jnp.sum(jnp.where(own, base[:, None, :], 0), axis=2).reshape(t) + rank

    tile_start = jnp.arange(n_tiles_e, dtype=jnp.int32) * EXPERT_TILE
    n_valid = ends[-1] // EXPERT_TILE
    tile_cls = jnp.sum((tile_start[:, None] >= ends[None, :]).astype(jnp.int32), axis=1)
    last_cls = jnp.take(tile_cls, n_valid - 1)
    valid = jnp.arange(n_tiles_e, dtype=jnp.int32) < n_valid
    tile_cls = jnp.where(valid, tile_cls, last_cls)
    pair_lo = jnp.asarray([a for a in range(EXPERTS_PER_GROUP) for b in range(a + 1, EXPERTS_PER_GROUP)], jnp.int32)
    pair_hi = jnp.asarray([b for a in range(EXPERTS_PER_GROUP) for b in range(a + 1, EXPERTS_PER_GROUP)], jnp.int32)
    grp = tile_cls // PAIRS_PER_GROUP
    e_lo = grp * EXPERTS_PER_GROUP + jnp.take(pair_lo, tile_cls % PAIRS_PER_GROUP)
    e_hi = grp * EXPERTS_PER_GROUP + jnp.take(pair_hi, tile_cls % PAIRS_PER_GROUP)
    pad_tile = (ends // EXPERT_TILE - 1).astype(jnp.int32)
    used = jnp.concatenate([(total > 0).astype(jnp.int32), n_valid[None].astype(jnp.int32)])
    return (pos * ROW_CHUNKS).astype(jnp.int32), pad_tile, used, e_lo, e_hi, valid.astype(jnp.int32)


DMA_UNROLL = 8


def _row_copies(n_rows, make_copy):
    def issue(ui, carry):
        for u in range(DMA_UNROLL):
            make_copy(ui * DMA_UNROLL + u).start(priority=u % 2)
        return carry
    lax.fori_loop(0, n_rows // DMA_UNROLL, issue, 0)


def _token_slab(ref, first_row):
    return ref.at[pl.ds(pl.multiple_of(first_row, ROW_CHUNKS), ROW_CHUNKS)]


def _dispatch_kernel(pos_ref, pad_tile_ref, used_ref, rows_ref, dst_hbm, zero_ref, sem, zero_sem):
    tm = rows_ref.shape[0] // ROW_CHUNKS
    tile_rows = EXPERT_TILE * ROW_CHUNKS

    @pl.when(pl.program_id(0) == 0)
    def _():
        zero_ref[...] = jnp.zeros_like(zero_ref)
        n_tiles = dst_hbm.shape[0] // tile_rows

        def clear(tile):
            first = pl.multiple_of(tile * tile_rows, tile_rows)
            return pltpu.make_async_copy(zero_ref, dst_hbm.at[pl.ds(first, tile_rows)], zero_sem)

        def start_tail(tile, carry):
            clear(tile).start()
            return carry

        def wait_tail(tile, carry):
            clear(tile).wait()
            return carry
        for c in range(N_CLASSES):
            @pl.when(used_ref[c] != 0)
            def _():
                clear(pad_tile_ref[c]).start()
        lax.fori_loop(used_ref[N_CLASSES], n_tiles, start_tail, 0)
        for c in range(N_CLASSES):
            @pl.when(used_ref[c] != 0)
            def _():
                clear(pad_tile_ref[c]).wait()
        lax.fori_loop(used_ref[N_CLASSES], n_tiles, wait_tail, 0)

    base = pl.program_id(0) * tm
    _row_copies(tm, lambda r: pltpu.make_async_copy(
        _token_slab(rows_ref, r * ROW_CHUNKS), _token_slab(dst_hbm, pos_ref[base + r]), sem))
    pltpu.make_async_copy(rows_ref, dst_hbm.at[pl.ds(0, tm * ROW_CHUNKS)], sem).wait()


def _dispatch_rows(pos, pad_tile, used, rows, n_dst):
    t = rows.shape[0] // ROW_CHUNKS
    tm = TOKEN_TILE
    return pl.pallas_call(
        _dispatch_kernel,
        out_shape=jax.ShapeDtypeStruct((n_dst * ROW_CHUNKS, LANES), rows.dtype),
        grid_spec=pltpu.PrefetchScalarGridSpec(
            num_scalar_prefetch=3, grid=(t // tm,),
            in_specs=[pl.BlockSpec((tm * ROW_CHUNKS, LANES), lambda i, p, pt, us: (i, 0))],
            out_specs=pl.BlockSpec(memory_space=pl.ANY),
            scratch_shapes=[pltpu.VMEM((EXPERT_TILE * ROW_CHUNKS, LANES), rows.dtype),
                            pltpu.SemaphoreType.DMA, pltpu.SemaphoreType.DMA]),
        compiler_params=pltpu.CompilerParams(dimension_semantics=("arbitrary",), disable_bounds_checks=True,
                                             has_side_effects=True),
        name="dispatch_rows",
    )(pos, pad_tile, used, rows)


def _moe_kernel(elo_ref, ehi_ref, valid_ref, x_ref, wrt_ref, wgl_ref, wul_ref, wdl_ref, wgh_ref, wuh_ref,
                wdh_ref, y_ref):
    j = pl.program_id(0)

    @pl.when(valid_ref[j] == 0)
    def _():
        y_ref[...] = jnp.zeros_like(y_ref)

    @pl.when(valid_ref[j] != 0)
    def _():
        h2 = _load_rows(x_ref, EXPERT_TILE)
        s_lo = _sigmoid(jnp.sum(h2 * wrt_ref[pl.ds(elo_ref[j], 1), :], axis=1, keepdims=True))
        s_hi = _sigmoid(jnp.sum(h2 * wrt_ref[pl.ds(ehi_ref[j], 1), :], axis=1, keepdims=True))
        total = s_lo + s_hi
        hb = h2.astype(BF16)
        he_lo = _silu(_dot(hb, wgl_ref[0])) * _dot(hb, wul_ref[0])
        y = (s_lo / total) * _dot(he_lo.astype(BF16), wdl_ref[0])
        he_hi = _silu(_dot(hb, wgh_ref[0])) * _dot(hb, wuh_ref[0])
        _store_rows(y_ref, y + (s_hi / total) * _dot(he_hi.astype(BF16), wdh_ref[0]))


def _moe(rows_sorted, e_lo, e_hi, valid, w_router, w_gate, w_up, w_down):
    n_pad = rows_sorted.shape[0] // ROW_CHUNKS
    d = D_MODEL
    lo3 = lambda j, el, eh, va: (el[j], 0, 0)
    hi3 = lambda j, el, eh, va: (eh[j], 0, 0)
    tile = pl.BlockSpec((EXPERT_TILE * ROW_CHUNKS, LANES), lambda j, el, eh, va: (j, 0))
    wg, wu, wd = w_gate.astype(BF16), w_up.astype(BF16), w_down.astype(BF16)
    return pl.pallas_call(
        _moe_kernel,
        out_shape=jax.ShapeDtypeStruct(rows_sorted.shape, F32),
        grid_spec=pltpu.PrefetchScalarGridSpec(
            num_scalar_prefetch=3, grid=(n_pad // EXPERT_TILE,),
            in_specs=[tile, _resident((N_EXPERTS, d)),
                      pl.BlockSpec((1, d, D_EXPERT), lo3), pl.BlockSpec((1, d, D_EXPERT), lo3),
                      pl.BlockSpec((1, D_EXPERT, d), lo3),
                      pl.BlockSpec((1, d, D_EXPERT), hi3), pl.BlockSpec((1, d, D_EXPERT), hi3),
                      pl.BlockSpec((1, D_EXPERT, d), hi3)],
            out_specs=tile),
        compiler_params=_params("arbitrary"),
        name="moe_experts",
    )(e_lo, e_hi, valid, rows_sorted, w_router.T, wg, wu, wd, wg, wu, wd)


def _collect_kernel(pos_ref, x_ref, ys_hbm, mod_ref, fg_ref, o_ref, buf0, buf1, sems, *, final):
    tm = x_ref.shape[0]
    i = pl.program_id(0)
    n = pl.num_programs(0)
    bufs = (buf0, buf1)

    def fetch(tile, slot):
        base = tile * tm
        _row_copies(tm, lambda r: pltpu.make_async_copy(
            _token_slab(ys_hbm, pos_ref[base + r]), _token_slab(bufs[slot], r * ROW_CHUNKS), sems.at[slot]))

    def finish(slot):
        pltpu.make_async_copy(ys_hbm.at[pl.ds(0, tm * ROW_CHUNKS)], bufs[slot], sems.at[slot]).wait()
        xo = x_ref[...] + mod_ref[0][5:6] * _load_rows(bufs[slot], tm)
        if final:
            xo = _rms(xo, fg_ref[...])
        o_ref[...] = xo

    @pl.when(i == 0)
    def _():
        fetch(0, 0)

    for slot in range(2):
        @pl.when(jnp.logical_and(i + 1 < n, (i + 1) % 2 == slot))
        def _():
            fetch(i + 1, slot)

    for slot in range(2):
        @pl.when(i % 2 == slot)
        def _():
            finish(slot)


def _collect_residual(pos, xn, y_sorted, mod_l, final_g, final, seq):
    t, d = xn.shape
    tm = TOKEN_TILE
    rows = pl.BlockSpec((tm, d), lambda i, p: (i, 0))
    return pl.pallas_call(
        functools.partial(_collect_kernel, final=final),
        out_shape=jax.ShapeDtypeStruct((t, d), F32),
        grid_spec=pltpu.PrefetchScalarGridSpec(
            num_scalar_prefetch=1, grid=(t // tm,),
            in_specs=[rows, pl.BlockSpec(memory_space=pl.ANY),
                      pl.BlockSpec((1, 6, d), lambda i, p: ((i * tm) // seq, 0, 0)),
                      pl.BlockSpec((1, d), lambda i, p: (0, 0))],
            out_specs=rows,
            scratch_shapes=[pltpu.VMEM((tm * ROW_CHUNKS, LANES), F32), pltpu.VMEM((tm * ROW_CHUNKS, LANES), F32),
                            pltpu.SemaphoreType.DMA((2,))]),
        compiler_params=pltpu.CompilerParams(dimension_semantics=("arbitrary",), disable_bounds_checks=True,
                                             vmem_limit_bytes=V7X_VMEM_LIMIT_BYTES),
        name="collect_residual",
    )(pos, xn, y_sorted, mod_l, final_g[None, :])


def kernel(x, c, positions, ada_w, ada_b, norm1_g, w_in, hg_lb_logits, hg_norm_g, q_norm_g, w_q_up,
           kv_norm_g, w_kv_up, w_br_a, w_br_b, w_out, norm2_g, w_router, router_bias, w_gate, w_up,
           w_down, final_g):
    batch, seq, d = x.shape
    depth = ada_w.shape[0]
    t = batch * seq
    n_tiles_e = (t + N_CLASSES * (EXPERT_TILE - 1)) // EXPERT_TILE
    mod = _modulation(c, ada_w, ada_b).reshape(depth, batch, 6, d)
    cos_t, sin_t = _rope_tables(positions)
    x2 = x.reshape(t, d)
    for l in range(depth):
        hq, hf, hi, hgate, gsig, q, k, v = _input_projection(
            x2, mod[l], norm1_g[l], cos_t, sin_t, w_in[l], q_norm_g[l], w_q_up[l],
            kv_norm_g[l], w_kv_up[l], seq)
        oa = _hgrn(hg_lb_logits, hg_norm_g[l], hq, hf, hi, hgate, l, batch, seq)
        ob = _attention(q, k, v, batch, seq)
        xn, rows, meta = _merge(x2, oa, ob, gsig, mod[l], w_br_a[l], w_br_b[l], w_out[l], norm2_g[l],
                                w_router, router_bias, seq)
        pos, pad_tile, used, e_lo, e_hi, valid = _dispatch_plan(meta, n_tiles_e)
        rows_sorted = _dispatch_rows(pos, pad_tile, used, rows, n_tiles_e * EXPERT_TILE)
        y_sorted = _moe(rows_sorted, e_lo, e_hi, valid, w_router, w_gate[l], w_up[l], w_down[l])
        x2 = _collect_residual(pos, xn, y_sorted, mod[l], final_g, l == depth - 1, seq)
    return x2.reshape(batch, seq, d)
```

```python
import functools
import math

import jax
import jax.numpy as jnp
from jax import lax
from jax.experimental import pallas as pl
from jax.experimental.pallas import tpu as pltpu

F32 = jnp.float32
BF16 = jnp.bfloat16

D_MODEL = 1024
CHUNK = 64
EPS = 1e-6

HG_HEADS = 4
HG_DK = 128
HG_DV = 128
HG_WIDTH = HG_HEADS * HG_DV
HG_SUB = 8
HG_LEVELS = (8, 16, 32)

MLA_HEADS = 8
MLA_NOPE = 64
MLA_ROPE = 32
MLA_V = 64
MLA_Q_LORA = 384
MLA_KV_LORA = 256
MLA_DQK = MLA_NOPE + MLA_ROPE
MLA_WIDTH = MLA_HEADS * MLA_V
ROPE_BASE = 10000.0
HEAD_PAD = 128
LOG2_E = math.log2(math.e)

N_EXPERTS = 16
N_GROUPS = 4
EXPERTS_PER_GROUP = N_EXPERTS // N_GROUPS
D_EXPERT = 512

IN_SIZES = (HG_HEADS * HG_DK, HG_HEADS * HG_DK, HG_HEADS * HG_DV, HG_WIDTH,
            MLA_Q_LORA, MLA_KV_LORA, MLA_ROPE, 2 * D_MODEL)

V7X_VMEM_LIMIT_BYTES = 56 * 1024 * 1024


def _params(*sem):
    return pltpu.CompilerParams(dimension_semantics=sem, vmem_limit_bytes=V7X_VMEM_LIMIT_BYTES)


def _resident(shape):
    nd = len(shape)
    return pl.BlockSpec(shape, lambda *_: (0,) * nd, pipeline_mode=pl.Buffered(1))


def _sigmoid(x):
    return 1.0 / (1.0 + jnp.exp(-x))


def _silu(x):
    return x * _sigmoid(x)


def _dot(a, b):
    return jnp.dot(a, b, preferred_element_type=F32)


def _dot_nt(a, b):
    return lax.dot_general(a, b, (((1,), (1,)), ((), ())), preferred_element_type=F32)


def _dot_tn(a, b):
    return lax.dot_general(a, b, (((0,), (0,)), ((), ())), preferred_element_type=F32)


def _split_bf16(x):
    hi = x.astype(BF16)
    lo = (x - hi.astype(F32)).astype(BF16)
    return hi, lo


def _rms(x, g):
    return x * lax.rsqrt(jnp.mean(x * x, axis=-1, keepdims=True) + EPS) * g


def _mod_kernel(c_ref, w_ref, b_ref, o_ref):
    ca = _silu(c_ref[...])
    o_ref[0] = _dot(ca.astype(BF16), w_ref[0].astype(BF16)) + b_ref[0]


def _modulation(c, ada_w, ada_b):
    depth, d, n = ada_w.shape
    b = c.shape[0]
    tn = 1536
    return pl.pallas_call(
        _mod_kernel,
        out_shape=jax.ShapeDtypeStruct((depth, b, n), F32),
        grid=(depth, n // tn),
        in_specs=[pl.BlockSpec((b, d), lambda l, j: (0, 0)),
                  pl.BlockSpec((1, d, tn), lambda l, j: (l, 0, j)),
                  pl.BlockSpec((1, 1, tn), lambda l, j: (l, 0, j))],
        out_specs=pl.BlockSpec((1, b, tn), lambda l, j: (l, 0, j)),
        compiler_params=_params("parallel", "parallel"),
        name="adaln_modulation",
    )(c, ada_w, ada_b.reshape(depth, 1, n))


def _rope_kernel(pos_ref, inv_ref, msk_ref, sgn_ref, cos_ref, sin_ref):
    ang = pos_ref[...].astype(F32) * inv_ref[...]
    cos_ref[...] = jnp.cos(ang) * msk_ref[...]
    sin_ref[...] = jnp.sin(ang) * sgn_ref[...]


def _rope_tables(positions):
    t = positions.size
    half = MLA_ROPE // 2
    inv = ROPE_BASE ** (-jnp.arange(half, dtype=F32) / half)
    z64, z32, one16 = jnp.zeros((MLA_NOPE,), F32), jnp.zeros((32,), F32), jnp.ones((half,), F32)
    inv_row = jnp.concatenate([z64, inv, inv, z32])[None, :]
    msk_row = jnp.concatenate([z64, one16, one16, z32])[None, :]
    sgn_row = jnp.concatenate([z64, -one16, one16, z32])[None, :]
    tr = min(2048, t)
    row = pl.BlockSpec((1, HEAD_PAD), lambda i: (0, 0))
    tab = pl.BlockSpec((tr, HEAD_PAD), lambda i: (i, 0))
    return pl.pallas_call(
        _rope_kernel,
        out_shape=(jax.ShapeDtypeStruct((t, HEAD_PAD), F32),) * 2,
        grid=(t // tr,),
        in_specs=[pl.BlockSpec((tr, 1), lambda i: (i, 0)), row, row, row],
        out_specs=(tab, tab),
        compiler_params=_params("parallel"),
        name="rope_tables",
    )(positions.reshape(t, 1), inv_row, msk_row, sgn_row)


def _proj_kernel(x_ref, mod_ref, n1_ref, cos_ref, sin_ref, wh_ref, wg_ref, wc_ref,
                 qn_ref, wqa_ref, wqs_ref, kn_ref, wk_ref, wv_ref, one_ref,
                 hq_ref, hf_ref, hi_ref, hgate_ref, gsig_ref, q_ref, k_ref, v_ref):
    x = x_ref[...]
    mod = mod_ref[0]
    h = _rms(x, n1_ref[...]) * (1.0 + mod[1:2]) + mod[0:1]
    hb = h.astype(BF16)

    ph = _dot(hb, wh_ref[...])
    w = HG_WIDTH
    hq_ref[...] = ph[:, 0:w].astype(BF16)
    hf_ref[...] = ph[:, w:2 * w]
    hi_ref[...] = ph[:, 2 * w:3 * w].astype(BF16)
    hgate_ref[...] = ph[:, 3 * w:4 * w].astype(BF16)

    gsig_ref[...] = _sigmoid(_dot(hb, wg_ref[...])).astype(BF16)

    pc = _dot(hb, wc_ref[...])
    cq = pc[:, 0:MLA_Q_LORA]
    ckv = pc[:, MLA_Q_LORA:MLA_Q_LORA + MLA_KV_LORA]
    kra = pc[:, MLA_Q_LORA + MLA_KV_LORA:MLA_Q_LORA + MLA_KV_LORA + HEAD_PAD]
    krb = pc[:, MLA_Q_LORA + MLA_KV_LORA + HEAD_PAD:]

    cos_t = cos_ref[...]
    sin_t = sin_ref[...]
    lane = lax.broadcasted_iota(jnp.int32, cos_t.shape, 1)
    scale = MLA_DQK ** -0.5 * LOG2_E
    cq_tab = jnp.tile(scale * (cos_t + jnp.where(lane < MLA_NOPE, 1.0, 0.0)), (1, MLA_HEADS))
    sq_tab = jnp.tile(scale * sin_t, (1, MLA_HEADS))

    cqn = _rms(cq, qn_ref[...]).astype(BF16)
    q = _dot(cqn, wqa_ref[...]) * cq_tab + _dot(cqn, wqs_ref[...]) * sq_tab
    q_ref[...] = q.astype(BF16)

    ckvn = _rms(ckv, kn_ref[...]).astype(BF16)
    kpe = kra * cos_t + krb * sin_t
    k_ref[...] = (_dot(ckvn, wk_ref[...]) + jnp.tile(kpe, (1, MLA_HEADS))).astype(BF16)
    v_ref[...] = (_dot(ckvn, wv_ref[...]) + one_ref[...]).astype(BF16)


def _pad_heads(w, lo, hi, at):
    k, nh, _ = w.shape
    out = jnp.zeros((k, nh, HEAD_PAD), w.dtype)
    out = out.at[:, :, at:at + (hi - lo)].set(w[:, :, lo:hi])
    return out


def _input_projection(x2, mod_l, n1, cos_t, sin_t, w_in, q_norm_g, w_q_up, kv_norm_g, w_kv_up, seq):
    t, d = x2.shape
    tm = 256
    splits = [0]
    for s in IN_SIZES:
        splits.append(splits[-1] + s)
    kr = w_in[:, splits[6]:splits[7]]
    half = MLA_ROPE // 2
    z64 = jnp.zeros((d, MLA_NOPE), F32)
    z32 = jnp.zeros((d, HEAD_PAD - MLA_NOPE - MLA_ROPE), F32)
    kr_a = jnp.concatenate([z64, kr, z32], axis=1)
    kr_b = jnp.concatenate([z64, kr[:, half:], kr[:, :half], z32], axis=1)
    w_h = w_in[:, splits[0]:splits[4]].astype(BF16)
    w_g = w_in[:, splits[7]:splits[8]].astype(BF16)
    w_c = jnp.concatenate([w_in[:, splits[4]:splits[6]], kr_a, kr_b], axis=1).astype(BF16)

    wq = w_q_up.reshape(MLA_Q_LORA, MLA_HEADS, MLA_DQK)
    wq_all = _pad_heads(wq, 0, MLA_DQK, 0).reshape(MLA_Q_LORA, -1).astype(BF16)
    wq_swap = (_pad_heads(wq, MLA_NOPE + half, MLA_DQK, MLA_NOPE)
               + _pad_heads(wq, MLA_NOPE, MLA_NOPE + half, MLA_NOPE + half))
    wq_swap = wq_swap.reshape(MLA_Q_LORA, -1).astype(BF16)
    wkv = w_kv_up.reshape(MLA_KV_LORA, MLA_HEADS, MLA_NOPE + MLA_V)
    wk_all = _pad_heads(wkv, 0, MLA_NOPE, 0).reshape(MLA_KV_LORA, -1).astype(BF16)
    wv_all = _pad_heads(wkv, MLA_NOPE, MLA_NOPE + MLA_V, 0).reshape(MLA_KV_LORA, -1).astype(BF16)
    ones_row = jnp.tile(jnp.zeros((HEAD_PAD,), F32).at[MLA_V].set(1.0), MLA_HEADS)[None, :]

    hp = MLA_HEADS * HEAD_PAD
    rows = lambda n: pl.BlockSpec((tm, n), lambda i: (i, 0))
    outs = [(HG_WIDTH, BF16), (HG_WIDTH, F32), (HG_WIDTH, BF16), (HG_WIDTH, BF16),
            (2 * d, BF16), (hp, BF16), (hp, BF16), (hp, BF16)]
    return pl.pallas_call(
        _proj_kernel,
        out_shape=tuple(jax.ShapeDtypeStruct((t, n), dt) for n, dt in outs),
        grid=(t // tm,),
        in_specs=[rows(d),
                  pl.BlockSpec((1, 6, d), lambda i: ((i * tm) // seq, 0, 0)),
                  _resident((1, d)),
                  rows(HEAD_PAD), rows(HEAD_PAD),
                  _resident(w_h.shape), _resident(w_g.shape), _resident(w_c.shape),
                  _resident((1, MLA_Q_LORA)), _resident(wq_all.shape), _resident(wq_swap.shape),
                  _resident((1, MLA_KV_LORA)), _resident(wk_all.shape), _resident(wv_all.shape),
                  _resident((1, hp))],
        out_specs=tuple(rows(n) for n, _ in outs),
        compiler_params=_params("parallel"),
        name="norm_input_projection",
    )(x2, mod_l, n1[None, :], cos_t, sin_t, w_h, w_g, w_c,
      q_norm_g[None, :], wq_all, wq_swap, kv_norm_g[None, :], wk_all, wv_all, ones_row)


def _hgrn_kernel(lbl_ref, gn_ref, q_ref, f_ref, v_ref, gate_ref, o_ref, st_ref, *, layer, n_chunks):
    @pl.when(pl.program_id(1) == 0)
    def _():
        st_ref[...] = jnp.zeros_like(st_ref)

    lg = lbl_ref[...]
    ex = jnp.exp(lg - jnp.max(lg, axis=0, keepdims=True))
    soft = ex / jnp.sum(ex, axis=0, keepdims=True)
    lb_all = jnp.zeros_like(soft[0:1])
    for i in range(1, layer + 1):
        lb_all = lb_all + soft[i:i + 1]

    r_i = lax.broadcasted_iota(jnp.int32, (CHUNK, CHUNK), 0)
    c_i = lax.broadcasted_iota(jnp.int32, (CHUNK, CHUNK), 1)
    tri = jnp.where(r_i >= c_i, 1.0, 0.0).astype(BF16)
    level_masks = []
    for hs in HG_LEVELS:
        same = (r_i // (2 * hs)) == (c_i // (2 * hs))
        level_masks.append(jnp.logical_and(same, jnp.logical_and(r_i % (2 * hs) >= hs, c_i % (2 * hs) < hs)))
    sub_row = lax.broadcasted_iota(jnp.int32, (HG_SUB, 1), 0)
    n_sub = CHUNK // HG_SUB

    def chunk_body(ci, carry):
        r0 = pl.multiple_of(ci * CHUNK, CHUNK)
        rows = pl.ds(r0, CHUNK)
        for h in range(HG_HEADS):
            sl = slice(h * HG_DK, (h + 1) * HG_DK)
            lb = lb_all[:, sl]
            q = q_ref[rows, sl].astype(F32)
            vb = v_ref[rows, sl]
            v = vb.astype(F32)
            f = lb + (1.0 - lb) * _sigmoid(f_ref[rows, sl])
            g = jnp.log2(f)
            k = 1.0 - f
            qf = _silu(q) * (HG_DK ** -0.5)
            g_hi, g_lo = _split_bf16(g)
            b = _dot(tri, g_hi) + _dot(tri, g_lo)

            att = jnp.zeros((CHUNK, CHUNK), F32)
            for hs, mask in zip(HG_LEVELS, level_masks):
                ref = jnp.concatenate(
                    [jnp.broadcast_to(b[j + hs - 1:j + hs], (2 * hs, HG_DK)) for j in range(0, CHUNK, 2 * hs)],
                    axis=0)
                qt = (qf * jnp.exp2(b - ref)).astype(BF16)
                kt = (k * jnp.exp2(ref - b)).astype(BF16)
                att = att + jnp.where(mask, _dot_nt(qt, kt), 0.0)
            o = _dot(att.astype(BF16), vb)

            blocks = []
            for i in range(n_sub):
                rs = slice(i * HG_SUB, (i + 1) * HG_SUB)
                b_i, q_i, k_i, v_i = b[rs], qf[rs], k[rs], v[rs]
                acc = jnp.zeros((HG_SUB, HG_DV), F32)
                for s in range(HG_SUB):
                    a = jnp.sum(jnp.exp2(b_i - b_i[s:s + 1]) * (q_i * k_i[s:s + 1]), axis=1, keepdims=True)
                    acc = acc + jnp.where(sub_row >= s, a, 0.0) * v_i[s:s + 1]
                blocks.append(acc)
            o = o + jnp.concatenate(blocks, axis=0)

            st = st_ref[h]
            o = o + _dot_nt((qf * jnp.exp2(b)).astype(BF16), st.astype(BF16))
            b_last = b[CHUNK - 1:CHUNK]
            kd = (k * jnp.exp2(b_last - b)).astype(BF16)
            st_ref[h] = st * jnp.exp2(b_last) + _dot_tn(vb, kd)

            gt = gate_ref[rows, sl].astype(F32)
            o_ref[rows, sl] = (_rms(o, gn_ref[:, sl]) * _silu(gt)).astype(BF16)
        return carry

    lax.fori_loop(0, n_chunks, chunk_body, 0, unroll=2)


def _hgrn(hg_lb_logits, hg_norm_g, hq, hf, hi, hgate, layer, batch, seq):
    t = hq.shape[0]
    lc = min(512, seq)
    nb = seq // lc
    rows = pl.BlockSpec((lc, HG_WIDTH), lambda b, j: (b * nb + j, 0))
    kern = functools.partial(_hgrn_kernel, layer=layer, n_chunks=lc // CHUNK)
    return pl.pallas_call(
        kern,
        out_shape=jax.ShapeDtypeStruct((t, HG_WIDTH), BF16),
        grid=(batch, nb),
        in_specs=[pl.BlockSpec(hg_lb_logits.shape, lambda b, j: (0, 0)),
                  pl.BlockSpec((1, HG_WIDTH), lambda b, j: (0, 0)),
                  rows, rows, rows, rows],
        out_specs=rows,
        scratch_shapes=[pltpu.VMEM((HG_HEADS, HG_DV, HG_DK), F32)],
        compiler_params=_params("parallel", "arbitrary"),
        name="hgrn2_chunkwise",
    )(hg_lb_logits, hg_norm_g[None, :], hq, hf, hi, hgate)


def _attn_kernel(qi_ref, kj_ref, q_ref, k_ref, v_ref, o_ref, m_ref, acc_ref, *, tq):
    p_id = pl.program_id(1)
    i = qi_ref[p_id]
    j = kj_ref[p_id]
    n_lane_tiles = tq // HEAD_PAD

    @pl.when(j == 0)
    def _():
        m_ref[...] = jnp.full_like(m_ref, -jnp.inf)
        acc_ref[...] = jnp.zeros_like(acc_ref)

    def step(diagonal):
        if diagonal:
            qc = lax.broadcasted_iota(jnp.int32, (tq, tq), 0) // CHUNK
            kc = lax.broadcasted_iota(jnp.int32, (tq, tq), 1) // CHUNK
            visible = kc <= qc
        for h in range(MLA_HEADS):
            sl = slice(h * HEAD_PAD, (h + 1) * HEAD_PAD)
            s = _dot_nt(q_ref[:, sl], k_ref[:, sl])
            if diagonal:
                s = jnp.where(visible, s, -jnp.inf)
            tiles = [s[:, t * HEAD_PAD:(t + 1) * HEAD_PAD] for t in range(n_lane_tiles)]
            m_tile = tiles[0]
            for t in range(1, n_lane_tiles):
                m_tile = jnp.maximum(m_tile, tiles[t])
            m_prev = m_ref[h]
            m_new = jnp.maximum(m_prev, jnp.max(m_tile, axis=1, keepdims=True))
            alpha = jnp.exp2(m_prev - m_new)
            p = jnp.concatenate([jnp.exp2((tl - m_new).astype(BF16)) for tl in tiles], axis=1)
            acc_ref[h] = acc_ref[h] * alpha + _dot(p, v_ref[:, sl])
            m_ref[h] = m_new

    @pl.when(j < i)
    def _():
        step(False)

    @pl.when(j == i)
    def _():
        step(True)
        outs = []
        for h in range(MLA_HEADS):
            acc = acc_ref[h]
            outs.append(acc[:, 0:MLA_V] / acc[:, MLA_V:MLA_V + 1])
        o_ref[...] = jnp.concatenate(outs, axis=1).astype(BF16)


def _attention(q, k, v, batch, seq):
    t, hp = q.shape
    tq = min(512, seq)
    nq = seq // tq
    pairs = [(i, j) for i in range(nq) for j in range(i + 1)]
    q_of = jnp.asarray([p[0] for p in pairs], jnp.int32)
    k_of = jnp.asarray([p[1] for p in pairs], jnp.int32)
    qspec = pl.BlockSpec((tq, hp), lambda b, p, qi, kj: (b * nq + qi[p], 0))
    kspec = pl.BlockSpec((tq, hp), lambda b, p, qi, kj: (b * nq + kj[p], 0))
    return pl.pallas_call(
        functools.partial(_attn_kernel, tq=tq),
        out_shape=jax.ShapeDtypeStruct((t, MLA_WIDTH), BF16),
        grid_spec=pltpu.PrefetchScalarGridSpec(
            num_scalar_prefetch=2,
            grid=(batch, len(pairs)),
            in_specs=[qspec, kspec, kspec],
            out_specs=pl.BlockSpec((tq, MLA_WIDTH), lambda b, p, qi, kj: (b * nq + qi[p], 0)),
            scratch_shapes=[pltpu.VMEM((MLA_HEADS, tq, HEAD_PAD), F32),
                            pltpu.VMEM((MLA_HEADS, tq, HEAD_PAD), F32)]),
        compiler_params=_params("parallel", "arbitrary"),
        name="mla_flash_attention",
    )(q_of, k_of, q, k, v)


PAIRS_PER_GROUP = EXPERTS_PER_GROUP * (EXPERTS_PER_GROUP - 1) // 2
N_CLASSES = N_GROUPS * PAIRS_PER_GROUP
LANES = 128
ROW_CHUNKS = D_MODEL // LANES
TOKEN_TILE = 512
EXPERT_TILE = 256


N_CLASS_ROWS = 32


def _route(scores, bias):
    biased = scores + bias
    col = [biased[e:e + 1, :] for e in range(N_EXPERTS)]
    gscore = []
    for g in range(N_GROUPS):
        a, b, c, d = col[g * EXPERTS_PER_GROUP:(g + 1) * EXPERTS_PER_GROUP]
        gscore.append(jnp.maximum(jnp.maximum(jnp.maximum(a + b, a + c), jnp.maximum(a + d, b + c)),
                                  jnp.maximum(b + d, c + d)))
    sel = []
    for g in range(N_GROUPS):
        ok = None
        for o in range(N_GROUPS):
            if o == g:
                continue
            cond = (gscore[g] > gscore[o]) if o < g else (gscore[g] >= gscore[o])
            ok = cond if ok is None else jnp.logical_and(ok, cond)
        grp = col[g * EXPERTS_PER_GROUP:(g + 1) * EXPERTS_PER_GROUP]
        for e in range(EXPERTS_PER_GROUP):
            beaten = jnp.zeros_like(grp[e])
            for o in range(EXPERTS_PER_GROUP):
                if o == e:
                    continue
                ahead = (grp[o] >= grp[e]) if o < e else (grp[o] > grp[e])
                beaten = beaten + jnp.where(ahead, 1.0, 0.0)
            sel.append(jnp.logical_and(ok, beaten < 1.5))
    lo = jnp.full_like(col[0], float(N_EXPERTS))
    hi = jnp.full_like(col[0], -1.0)
    for e in range(N_EXPERTS):
        lo = jnp.where(sel[e], jnp.minimum(lo, float(e)), lo)
        hi = jnp.where(sel[e], jnp.maximum(hi, float(e)), hi)
    return lo, hi


def _store_rows(ref, x):
    n = x.shape[0]
    for c in range(ROW_CHUNKS):
        ref[pl.ds(c, n, stride=ROW_CHUNKS), :] = x[:, c * LANES:(c + 1) * LANES]


def _load_rows(ref, n):
    return jnp.concatenate([ref[pl.ds(c, n, stride=ROW_CHUNKS), :] for c in range(ROW_CHUNKS)], axis=1)


def _merge_kernel(x_ref, oa_ref, ob_ref, g_ref, mod_ref, wa_ref, wb_ref, wo_ref, n2_ref,
                  wrt_ref, rb_ref, xn_ref, row_ref, meta_ref):
    d = D_MODEL
    tm = x_ref.shape[0]
    mod = mod_ref[0]
    ya = _dot(oa_ref[...], wa_ref[...])
    yb = _dot(ob_ref[...], wb_ref[...])
    g = g_ref[...].astype(F32)
    merged = g[:, 0:d] * ya + g[:, d:2 * d] * yb
    xn = x_ref[...] + mod[2:3] * _dot(merged.astype(BF16), wo_ref[...])
    xn_ref[...] = xn
    h2 = _rms(xn, n2_ref[...]) * (1.0 + mod[4:5]) + mod[3:4]
    _store_rows(row_ref, h2)

    h_hi, h_lo = _split_bf16(h2)
    w_hi, w_lo = _split_bf16(wrt_ref[...])
    scores = _sigmoid(_dot_nt(w_hi, h_hi) + _dot_nt(w_hi, h_lo) + _dot_nt(w_lo, h_hi))
    lo, hi = _route(scores, rb_ref[...])

    grp = jnp.floor(lo * (1.0 / EXPERTS_PER_GROUP))
    a = lo - grp * EXPERTS_PER_GROUP
    b = hi - grp * EXPERTS_PER_GROUP
    cls = grp * PAIRS_PER_GROUP + a * (2 * EXPERTS_PER_GROUP - 1 - a) * 0.5 + (b - a - 1.0)
    c_row = lax.broadcasted_iota(jnp.int32, (N_CLASS_ROWS, tm), 0).astype(F32)
    onehot = jnp.where(c_row == cls, 1.0, 0.0)
    r_i = lax.broadcasted_iota(jnp.int32, (tm, tm), 0)
    c_i = lax.broadcasted_iota(jnp.int32, (tm, tm), 1)
    earlier = jnp.where(r_i < c_i, 1.0, 0.0).astype(BF16)
    rank = jnp.sum(onehot * _dot(onehot.astype(BF16), earlier), axis=0, keepdims=True)
    m_row = lax.broadcasted_iota(jnp.int32, (8, tm), 0)
    meta_ref[...] = jnp.where(m_row == 0, cls, jnp.where(m_row == 1, rank, 0.0))


def _merge(x2, oa, ob, gsig, mod_l, w_br_a, w_br_b, w_out, n2, w_router, router_bias, seq):
    t, d = x2.shape
    tm = TOKEN_TILE
    rows = lambda n: pl.BlockSpec((tm, n), lambda i: (i, 0))
    return pl.pallas_call(
        _merge_kernel,
        out_shape=(jax.ShapeDtypeStruct((t, d), F32), jax.ShapeDtypeStruct((t * ROW_CHUNKS, LANES), F32),
                   jax.ShapeDtypeStruct((8, t), F32)),
        grid=(t // tm,),
        in_specs=[rows(d), rows(HG_WIDTH), rows(MLA_WIDTH), rows(2 * d),
                  pl.BlockSpec((1, 6, d), lambda i: ((i * tm) // seq, 0, 0)),
                  _resident(w_br_a.shape), _resident(w_br_b.shape), _resident(w_out.shape),
                  _resident((1, d)), _resident((N_EXPERTS, d)), _resident((N_EXPERTS, 1))],
        out_specs=(rows(d), pl.BlockSpec((tm * ROW_CHUNKS, LANES), lambda i: (i, 0)),
                   pl.BlockSpec((8, tm), lambda i: (0, i))),
        compiler_params=_params("parallel"),
        name="merge_outproj_router",
    )(x2, oa, ob, gsig, mod_l, w_br_a.astype(BF16), w_br_b.astype(BF16), w_out.astype(BF16),
      n2[None, :], w_router.T, router_bias[:, None])


def _dispatch_plan(meta, n_tiles_e):
    t = meta.shape[1]
    n_tok_tiles = t // TOKEN_TILE
    cls = meta[0].astype(jnp.int32)
    rank = meta[1].astype(jnp.int32)
    own = cls.reshape(n_tok_tiles, TOKEN_TILE, 1) == jnp.arange(N_CLASSES, dtype=jnp.int32)
    counts = jnp.sum(own.astype(jnp.int32), axis=1)
    total = jnp.sum(counts, axis=0)
    total_pad = (total + EXPERT_TILE - 1) // EXPERT_TILE * EXPERT_TILE
    ends = jnp.cumsum(total_pad)
    base = (ends - total_pad)[None, :] + jnp.cumsum(counts, axis=0) - counts
    pos = jnp.sum(jnp.where(own, base[:, None, :], 0), axis=2).reshape(t) + rank

    tile_start = jnp.arange(n_tiles_e, dtype=jnp.int32) * EXPERT_TILE
    n_valid = ends[-1] // EXPERT_TILE
    tile_cls = jnp.sum((tile_start[:, None] >= ends[None, :]).astype(jnp.int32), axis=1)
    last_cls = jnp.take(tile_cls, n_valid - 1)
    valid = jnp.arange(n_tiles_e, dtype=jnp.int32) < n_valid
    tile_cls = jnp.where(valid, tile_cls, last_cls)
    pair_lo = jnp.asarray([a for a in range(EXPERTS_PER_GROUP) for b in range(a + 1, EXPERTS_PER_GROUP)], jnp.int32)
    pair_hi = jnp.asarray([b for a in range(EXPERTS_PER_GROUP) for b in range(a + 1, EXPERTS_PER_GROUP)], jnp.int32)
    grp = tile_cls // PAIRS_PER_GROUP
    e_lo = grp * EXPERTS_PER_GROUP + jnp.take(pair_lo, tile_cls % PAIRS_PER_GROUP)
    e_hi = grp * EXPERTS_PER_GROUP + jnp.take(pair_hi, tile_cls % PAIRS_PER_GROUP)
    pad_tile = (ends // EXPERT_TILE - 1).astype(jnp.int32)
    used = jnp.concatenate([(total > 0).astype(jnp.int32), n_valid[None].astype(jnp.int32)])
    return (pos * ROW_CHUNKS).astype(jnp.int32), pad_tile, used, e_lo, e_hi, valid.astype(jnp.int32)


DMA_UNROLL = 8


def _row_copies(n_rows, make_copy):
    def issue(ui, carry):
        for u in range(DMA_UNROLL):
            make_copy(ui * DMA_UNROLL + u).start(priority=u % 2)
        return carry
    lax.fori_loop(0, n_rows // DMA_UNROLL, issue, 0)


def _token_slab(ref, first_row):
    return ref.at[pl.ds(pl.multiple_of(first_row, ROW_CHUNKS), ROW_CHUNKS)]


def _dispatch_kernel(pos_ref, pad_tile_ref, used_ref, rows_ref, dst_hbm, zero_ref, sem, zero_sem):
    tm = rows_ref.shape[0] // ROW_CHUNKS
    tile_rows = EXPERT_TILE * ROW_CHUNKS

    @pl.when(pl.program_id(0) == 0)
    def _():
        zero_ref[...] = jnp.zeros_like(zero_ref)
        n_tiles = dst_hbm.shape[0] // tile_rows

        def clear(tile):
            first = pl.multiple_of(tile * tile_rows, tile_rows)
            return pltpu.make_async_copy(zero_ref, dst_hbm.at[pl.ds(first, tile_rows)], zero_sem)

        def start_tail(tile, carry):
            clear(tile).start()
            return carry

        def wait_tail(tile, carry):
            clear(tile).wait()
            return carry
        for c in range(N_CLASSES):
            @pl.when(used_ref[c] != 0)
            def _():
                clear(pad_tile_ref[c]).start()
        lax.fori_loop(used_ref[N_CLASSES], n_tiles, start_tail, 0)
        for c in range(N_CLASSES):
            @pl.when(used_ref[c] != 0)
            def _():
                clear(pad_tile_ref[c]).wait()
        lax.fori_loop(used_ref[N_CLASSES], n_tiles, wait_tail, 0)

    base = pl.program_id(0) * tm
    _row_copies(tm, lambda r: pltpu.make_async_copy(
        _token_slab(rows_ref, r * ROW_CHUNKS), _token_slab(dst_hbm, pos_ref[base + r]), sem))
    pltpu.make_async_copy(rows_ref, dst_hbm.at[pl.ds(0, tm * ROW_CHUNKS)], sem).wait()


def _dispatch_rows(pos, pad_tile, used, rows, n_dst):
    t = rows.shape[0] // ROW_CHUNKS
    tm = TOKEN_TILE
    return pl.pallas_call(
        _dispatch_kernel,
        out_shape=jax.ShapeDtypeStruct((n_dst * ROW_CHUNKS, LANES), rows.dtype),
        grid_spec=pltpu.PrefetchScalarGridSpec(
            num_scalar_prefetch=3, grid=(t // tm,),
            in_specs=[pl.BlockSpec((tm * ROW_CHUNKS, LANES), lambda i, p, pt, us: (i, 0))],
            out_specs=pl.BlockSpec(memory_space=pl.ANY),
            scratch_shapes=[pltpu.VMEM((EXPERT_TILE * ROW_CHUNKS, LANES), rows.dtype),
                            pltpu.SemaphoreType.DMA, pltpu.SemaphoreType.DMA]),
        compiler_params=pltpu.CompilerParams(dimension_semantics=("arbitrary",), disable_bounds_checks=True,
                                             has_side_effects=True),
        name="dispatch_rows",
    )(pos, pad_tile, used, rows)


def _moe_kernel(elo_ref, ehi_ref, valid_ref, x_ref, wrt_ref, wgl_ref, wul_ref, wdl_ref, wgh_ref, wuh_ref,
                wdh_ref, y_ref):
    j = pl.program_id(0)

    @pl.when(valid_ref[j] == 0)
    def _():
        y_ref[...] = jnp.zeros_like(y_ref)

    @pl.when(valid_ref[j] != 0)
    def _():
        h2 = _load_rows(x_ref, EXPERT_TILE)
        s_lo = _sigmoid(jnp.sum(h2 * wrt_ref[pl.ds(elo_ref[j], 1), :], axis=1, keepdims=True))
        s_hi = _sigmoid(jnp.sum(h2 * wrt_ref[pl.ds(ehi_ref[j], 1), :], axis=1, keepdims=True))
        total = s_lo + s_hi
        hb = h2.astype(BF16)
        he_lo = _silu(_dot(hb, wgl_ref[0, 0])) * _dot(hb, wul_ref[0, 0])
        y = (s_lo / total) * _dot(he_lo.astype(BF16), wdl_ref[0, 0])
        he_hi = _silu(_dot(hb, wgh_ref[0, 0])) * _dot(hb, wuh_ref[0, 0])
        _store_rows(y_ref, y + (s_hi / total) * _dot(he_hi.astype(BF16), wdh_ref[0, 0]))


def _moe(rows_sorted, e_lo, e_hi, valid, w_router, wg, wu, wd, layer):
    n_pad = rows_sorted.shape[0] // ROW_CHUNKS
    d = D_MODEL
    lo4 = lambda j, el, eh, va: (layer, el[j], 0, 0)
    hi4 = lambda j, el, eh, va: (layer, eh[j], 0, 0)
    tile = pl.BlockSpec((EXPERT_TILE * ROW_CHUNKS, LANES), lambda j, el, eh, va: (j, 0))
    return pl.pallas_call(
        _moe_kernel,
        out_shape=jax.ShapeDtypeStruct(rows_sorted.shape, F32),
        grid_spec=pltpu.PrefetchScalarGridSpec(
            num_scalar_prefetch=3, grid=(n_pad // EXPERT_TILE,),
            in_specs=[tile, _resident((N_EXPERTS, d)),
                      pl.BlockSpec((1, 1, d, D_EXPERT), lo4), pl.BlockSpec((1, 1, d, D_EXPERT), lo4),
                      pl.BlockSpec((1, 1, D_EXPERT, d), lo4),
                      pl.BlockSpec((1, 1, d, D_EXPERT), hi4), pl.BlockSpec((1, 1, d, D_EXPERT), hi4),
                      pl.BlockSpec((1, 1, D_EXPERT, d), hi4)],
            out_specs=tile),
        compiler_params=_params("arbitrary"),
        name="moe_experts",
    )(e_lo, e_hi, valid, rows_sorted, w_router.T, wg, wu, wd, wg, wu, wd)


def _collect_kernel(pos_ref, x_ref, ys_hbm, mod_ref, fg_ref, o_ref, buf0, buf1, sems, *, final):
    tm = x_ref.shape[0]
    i = pl.program_id(0)
    n = pl.num_programs(0)
    bufs = (buf0, buf1)

    def fetch(tile, slot):
        base = tile * tm
        _row_copies(tm, lambda r: pltpu.make_async_copy(
            _token_slab(ys_hbm, pos_ref[base + r]), _token_slab(bufs[slot], r * ROW_CHUNKS), sems.at[slot]))

    def finish(slot):
        pltpu.make_async_copy(ys_hbm.at[pl.ds(0, tm * ROW_CHUNKS)], bufs[slot], sems.at[slot]).wait()
        xo = x_ref[...] + mod_ref[0][5:6] * _load_rows(bufs[slot], tm)
        if final:
            xo = _rms(xo, fg_ref[...])
        o_ref[...] = xo

    @pl.when(i == 0)
    def _():
        fetch(0, 0)

    for slot in range(2):
        @pl.when(jnp.logical_and(i + 1 < n, (i + 1) % 2 == slot))
        def _():
            fetch(i + 1, slot)

    for slot in range(2):
        @pl.when(i % 2 == slot)
        def _():
            finish(slot)


def _collect_residual(pos, xn, y_sorted, mod_l, final_g, final, seq):
    t, d = xn.shape
    tm = TOKEN_TILE
    rows = pl.BlockSpec((tm, d), lambda i, p: (i, 0))
    return pl.pallas_call(
        functools.partial(_collect_kernel, final=final),
        out_shape=jax.ShapeDtypeStruct((t, d), F32),
        grid_spec=pltpu.PrefetchScalarGridSpec(
            num_scalar_prefetch=1, grid=(t // tm,),
            in_specs=[rows, pl.BlockSpec(memory_space=pl.ANY),
                      pl.BlockSpec((1, 6, d), lambda i, p: ((i * tm) // seq, 0, 0)),
                      pl.BlockSpec((1, d), lambda i, p: (0, 0))],
            out_specs=rows,
            scratch_shapes=[pltpu.VMEM((tm * ROW_CHUNKS, LANES), F32), pltpu.VMEM((tm * ROW_CHUNKS, LANES), F32),
                            pltpu.SemaphoreType.DMA((2,))]),
        compiler_params=pltpu.CompilerParams(dimension_semantics=("arbitrary",), disable_bounds_checks=True,
                                             vmem_limit_bytes=V7X_VMEM_LIMIT_BYTES),
        name="collect_residual",
    )(pos, xn, y_sorted, mod_l, final_g[None, :])


def kernel(x, c, positions, ada_w, ada_b, norm1_g, w_in, hg_lb_logits, hg_norm_g, q_norm_g, w_q_up,
           kv_norm_g, w_kv_up, w_br_a, w_br_b, w_out, norm2_g, w_router, router_bias, w_gate, w_up,
           w_down, final_g):
    batch, seq, d = x.shape
    depth = ada_w.shape[0]
    t = batch * seq
    n_tiles_e = (t + N_CLASSES * (EXPERT_TILE - 1)) // EXPERT_TILE
    mod = _modulation(c, ada_w, ada_b).reshape(depth, batch, 6, d)
    cos_t, sin_t = _rope_tables(positions)
    x2 = x.reshape(t, d)
    wg_all, wu_all, wd_all = w_gate.astype(BF16), w_up.astype(BF16), w_down.astype(BF16)
    for l in range(depth):
        hq, hf, hi, hgate, gsig, q, k, v = _input_projection(
            x2, mod[l], norm1_g[l], cos_t, sin_t, w_in[l], q_norm_g[l], w_q_up[l],
            kv_norm_g[l], w_kv_up[l], seq)
        oa = _hgrn(hg_lb_logits, hg_norm_g[l], hq, hf, hi, hgate, l, batch, seq)
        ob = _attention(q, k, v, batch, seq)
        xn, rows, meta = _merge(x2, oa, ob, gsig, mod[l], w_br_a[l], w_br_b[l], w_out[l], norm2_g[l],
                                w_router, router_bias, seq)
        pos, pad_tile, used, e_lo, e_hi, valid = _dispatch_plan(meta, n_tiles_e)
        rows_sorted = _dispatch_rows(pos, pad_tile, used, rows, n_tiles_e * EXPERT_TILE)
        y_sorted = _moe(rows_sorted, e_lo, e_hi, valid, w_router, wg_all, wu_all, wd_all, l)
        x2 = _collect_residual(pos, xn, y_sorted, mod[l], final_g, l == depth - 1, seq)
    return x2.reshape(batch, seq, d)
```

```python
import functools
import math

import jax
import jax.numpy as jnp
from jax import lax
from jax.experimental import pallas as pl
from jax.experimental.pallas import tpu as pltpu

F32 = jnp.float32
BF16 = jnp.bfloat16

D_MODEL = 1024
CHUNK = 64
EPS = 1e-6

HG_HEADS = 4
HG_DK = 128
HG_DV = 128
HG_WIDTH = HG_HEADS * HG_DV
HG_SUB = 8
HG_LEVELS = (8, 16, 32)

MLA_HEADS = 8
MLA_NOPE = 64
MLA_ROPE = 32
MLA_V = 64
MLA_Q_LORA = 384
MLA_KV_LORA = 256
MLA_DQK = MLA_NOPE + MLA_ROPE
MLA_WIDTH = MLA_HEADS * MLA_V
ROPE_BASE = 10000.0
HEAD_PAD = 128
LOG2_E = math.log2(math.e)

N_EXPERTS = 16
N_GROUPS = 4
EXPERTS_PER_GROUP = N_EXPERTS // N_GROUPS
D_EXPERT = 512

IN_SIZES = (HG_HEADS * HG_DK, HG_HEADS * HG_DK, HG_HEADS * HG_DV, HG_WIDTH,
            MLA_Q_LORA, MLA_KV_LORA, MLA_ROPE, 2 * D_MODEL)

V7X_VMEM_LIMIT_BYTES = 56 * 1024 * 1024


def _params(*sem):
    return pltpu.CompilerParams(dimension_semantics=sem, vmem_limit_bytes=V7X_VMEM_LIMIT_BYTES)


def _resident(shape):
    nd = len(shape)
    return pl.BlockSpec(shape, lambda *_: (0,) * nd, pipeline_mode=pl.Buffered(1))


def _sigmoid(x):
    return 1.0 / (1.0 + jnp.exp(-x))


def _silu(x):
    return x * _sigmoid(x)


def _dot(a, b):
    return jnp.dot(a, b, preferred_element_type=F32)


def _dot_nt(a, b):
    return lax.dot_general(a, b, (((1,), (1,)), ((), ())), preferred_element_type=F32)


def _dot_tn(a, b):
    return lax.dot_general(a, b, (((0,), (0,)), ((), ())), preferred_element_type=F32)


def _split_bf16(x):
    hi = x.astype(BF16)
    lo = (x - hi.astype(F32)).astype(BF16)
    return hi, lo


def _rms(x, g):
    return x * lax.rsqrt(jnp.mean(x * x, axis=-1, keepdims=True) + EPS) * g


def _mod_kernel(c_ref, w_ref, b_ref, o_ref):
    ca = _silu(c_ref[...])
    o_ref[0] = _dot(ca.astype(BF16), w_ref[0].astype(BF16)) + b_ref[0]


def _modulation(c, ada_w, ada_b):
    depth, d, n = ada_w.shape
    b = c.shape[0]
    tn = 1536
    return pl.pallas_call(
        _mod_kernel,
        out_shape=jax.ShapeDtypeStruct((depth, b, n), F32),
        grid=(depth, n // tn),
        in_specs=[pl.BlockSpec((b, d), lambda l, j: (0, 0)),
                  pl.BlockSpec((1, d, tn), lambda l, j: (l, 0, j)),
                  pl.BlockSpec((1, 1, tn), lambda l, j: (l, 0, j))],
        out_specs=pl.BlockSpec((1, b, tn), lambda l, j: (l, 0, j)),
        compiler_params=_params("parallel", "parallel"),
        name="adaln_modulation",
    )(c, ada_w, ada_b.reshape(depth, 1, n))


def _rope_kernel(pos_ref, inv_ref, msk_ref, sgn_ref, cos_ref, sin_ref):
    ang = pos_ref[...].astype(F32) * inv_ref[...]
    cos_ref[...] = jnp.cos(ang) * msk_ref[...]
    sin_ref[...] = jnp.sin(ang) * sgn_ref[...]


def _rope_tables(positions):
    t = positions.size
    half = MLA_ROPE // 2
    inv = ROPE_BASE ** (-jnp.arange(half, dtype=F32) / half)
    z64, z32, one16 = jnp.zeros((MLA_NOPE,), F32), jnp.zeros((32,), F32), jnp.ones((half,), F32)
    inv_row = jnp.concatenate([z64, inv, inv, z32])[None, :]
    msk_row = jnp.concatenate([z64, one16, one16, z32])[None, :]
    sgn_row = jnp.concatenate([z64, -one16, one16, z32])[None, :]
    tr = min(2048, t)
    row = pl.BlockSpec((1, HEAD_PAD), lambda i: (0, 0))
    tab = pl.BlockSpec((tr, HEAD_PAD), lambda i: (i, 0))
    return pl.pallas_call(
        _rope_kernel,
        out_shape=(jax.ShapeDtypeStruct((t, HEAD_PAD), F32),) * 2,
        grid=(t // tr,),
        in_specs=[pl.BlockSpec((tr, 1), lambda i: (i, 0)), row, row, row],
        out_specs=(tab, tab),
        compiler_params=_params("parallel"),
        name="rope_tables",
    )(positions.reshape(t, 1), inv_row, msk_row, sgn_row)


def _proj_kernel(x_ref, mod_ref, n1_ref, cos_ref, sin_ref, wh_ref, wg_ref, wc_ref,
                 qn_ref, wqa_ref, wqs_ref, kn_ref, wk_ref, wv_ref, one_ref,
                 hq_ref, hf_ref, hi_ref, hgate_ref, gsig_ref, q_ref, k_ref, v_ref):
    x = x_ref[...]
    mod = mod_ref[0]
    h = _rms(x, n1_ref[...]) * (1.0 + mod[1:2]) + mod[0:1]
    hb = h.astype(BF16)

    ph = _dot(hb, wh_ref[...])
    w = HG_WIDTH
    hq_ref[...] = ph[:, 0:w].astype(BF16)
    hf_ref[...] = ph[:, w:2 * w]
    hi_ref[...] = ph[:, 2 * w:3 * w].astype(BF16)
    hgate_ref[...] = ph[:, 3 * w:4 * w].astype(BF16)

    gsig_ref[...] = _sigmoid(_dot(hb, wg_ref[...])).astype(BF16)

    pc = _dot(hb, wc_ref[...])
    cq = pc[:, 0:MLA_Q_LORA]
    ckv = pc[:, MLA_Q_LORA:MLA_Q_LORA + MLA_KV_LORA]
    kra = pc[:, MLA_Q_LORA + MLA_KV_LORA:MLA_Q_LORA + MLA_KV_LORA + HEAD_PAD]
    krb = pc[:, MLA_Q_LORA + MLA_KV_LORA + HEAD_PAD:]

    cos_t = cos_ref[...]
    sin_t = sin_ref[...]
    lane = lax.broadcasted_iota(jnp.int32, cos_t.shape, 1)
    scale = MLA_DQK ** -0.5 * LOG2_E
    cq_tab = jnp.tile(scale * (cos_t + jnp.where(lane < MLA_NOPE, 1.0, 0.0)), (1, MLA_HEADS))
    sq_tab = jnp.tile(scale * sin_t, (1, MLA_HEADS))

    cqn = _rms(cq, qn_ref[...]).astype(BF16)
    q = _dot(cqn, wqa_ref[...]) * cq_tab + _dot(cqn, wqs_ref[...]) * sq_tab
    q_ref[...] = q.astype(BF16)

    ckvn = _rms(ckv, kn_ref[...]).astype(BF16)
    kpe = kra * cos_t + krb * sin_t
    k_ref[...] = (_dot(ckvn, wk_ref[...]) + jnp.tile(kpe, (1, MLA_HEADS))).astype(BF16)
    v_ref[...] = (_dot(ckvn, wv_ref[...]) + one_ref[...]).astype(BF16)


def _pad_heads(w, lo, hi, at):
    k, nh, _ = w.shape
    out = jnp.zeros((k, nh, HEAD_PAD), w.dtype)
    out = out.at[:, :, at:at + (hi - lo)].set(w[:, :, lo:hi])
    return out


def _input_projection(x2, mod_l, n1, cos_t, sin_t, w_in, q_norm_g, w_q_up, kv_norm_g, w_kv_up, seq):
    t, d = x2.shape
    tm = 256
    splits = [0]
    for s in IN_SIZES:
        splits.append(splits[-1] + s)
    kr = w_in[:, splits[6]:splits[7]]
    half = MLA_ROPE // 2
    z64 = jnp.zeros((d, MLA_NOPE), F32)
    z32 = jnp.zeros((d, HEAD_PAD - MLA_NOPE - MLA_ROPE), F32)
    kr_a = jnp.concatenate([z64, kr, z32], axis=1)
    kr_b = jnp.concatenate([z64, kr[:, half:], kr[:, :half], z32], axis=1)
    w_h = w_in[:, splits[0]:splits[4]].astype(BF16)
    w_g = w_in[:, splits[7]:splits[8]].astype(BF16)
    w_c = jnp.concatenate([w_in[:, splits[4]:splits[6]], kr_a, kr_b], axis=1).astype(BF16)

    wq = w_q_up.reshape(MLA_Q_LORA, MLA_HEADS, MLA_DQK)
    wq_all = _pad_heads(wq, 0, MLA_DQK, 0).reshape(MLA_Q_LORA, -1).astype(BF16)
    wq_swap = (_pad_heads(wq, MLA_NOPE + half, MLA_DQK, MLA_NOPE)
               + _pad_heads(wq, MLA_NOPE, MLA_NOPE + half, MLA_NOPE + half))
    wq_swap = wq_swap.reshape(MLA_Q_LORA, -1).astype(BF16)
    wkv = w_kv_up.reshape(MLA_KV_LORA, MLA_HEADS, MLA_NOPE + MLA_V)
    wk_all = _pad_heads(wkv, 0, MLA_NOPE, 0).reshape(MLA_KV_LORA, -1).astype(BF16)
    wv_all = _pad_heads(wkv, MLA_NOPE, MLA_NOPE + MLA_V, 0).reshape(MLA_KV_LORA, -1).astype(BF16)
    ones_row = jnp.tile(jnp.zeros((HEAD_PAD,), F32).at[MLA_V].set(1.0), MLA_HEADS)[None, :]

    hp = MLA_HEADS * HEAD_PAD
    rows = lambda n: pl.BlockSpec((tm, n), lambda i: (i, 0))
    outs = [(HG_WIDTH, BF16), (HG_WIDTH, F32), (HG_WIDTH, BF16), (HG_WIDTH, BF16),
            (2 * d, BF16), (hp, BF16), (hp, BF16), (hp, BF16)]
    return pl.pallas_call(
        _proj_kernel,
        out_shape=tuple(jax.ShapeDtypeStruct((t, n), dt) for n, dt in outs),
        grid=(t // tm,),
        in_specs=[rows(d),
                  pl.BlockSpec((1, 6, d), lambda i: ((i * tm) // seq, 0, 0)),
                  _resident((1, d)),
                  rows(HEAD_PAD), rows(HEAD_PAD),
                  _resident(w_h.shape), _resident(w_g.shape), _resident(w_c.shape),
                  _resident((1, MLA_Q_LORA)), _resident(wq_all.shape), _resident(wq_swap.shape),
                  _resident((1, MLA_KV_LORA)), _resident(wk_all.shape), _resident(wv_all.shape),
                  _resident((1, hp))],
        out_specs=tuple(rows(n) for n, _ in outs),
        compiler_params=_params("parallel"),
        name="norm_input_projection",
    )(x2, mod_l, n1[None, :], cos_t, sin_t, w_h, w_g, w_c,
      q_norm_g[None, :], wq_all, wq_swap, kv_norm_g[None, :], wk_all, wv_all, ones_row)


def _hgrn_kernel(lbl_ref, gn_ref, q_ref, f_ref, v_ref, gate_ref, o_ref, st_ref, *, layer, n_chunks):
    @pl.when(pl.program_id(1) == 0)
    def _():
        st_ref[...] = jnp.zeros_like(st_ref)

    lg = lbl_ref[...]
    ex = jnp.exp(lg - jnp.max(lg, axis=0, keepdims=True))
    soft = ex / jnp.sum(ex, axis=0, keepdims=True)
    lb_all = jnp.zeros_like(soft[0:1])
    for i in range(1, layer + 1):
        lb_all = lb_all + soft[i:i + 1]

    r_i = lax.broadcasted_iota(jnp.int32, (CHUNK, CHUNK), 0)
    c_i = lax.broadcasted_iota(jnp.int32, (CHUNK, CHUNK), 1)
    tri = jnp.where(r_i >= c_i, 1.0, 0.0).astype(BF16)
    level_masks = []
    for hs in HG_LEVELS:
        same = (r_i // (2 * hs)) == (c_i // (2 * hs))
        level_masks.append(jnp.logical_and(same, jnp.logical_and(r_i % (2 * hs) >= hs, c_i % (2 * hs) < hs)))
    sub_row = lax.broadcasted_iota(jnp.int32, (HG_SUB, 1), 0)
    n_sub = CHUNK // HG_SUB

    def chunk_body(ci, carry):
        r0 = pl.multiple_of(ci * CHUNK, CHUNK)
        rows = pl.ds(r0, CHUNK)
        for h in range(HG_HEADS):
            sl = slice(h * HG_DK, (h + 1) * HG_DK)
            lb = lb_all[:, sl]
            q = q_ref[rows, sl].astype(F32)
            vb = v_ref[rows, sl]
            v = vb.astype(F32)
            f = lb + (1.0 - lb) * _sigmoid(f_ref[rows, sl])
            g = jnp.log2(f)
            k = 1.0 - f
            qf = _silu(q) * (HG_DK ** -0.5)
            g_hi, g_lo = _split_bf16(g)
            b = _dot(tri, g_hi) + _dot(tri, g_lo)

            att = jnp.zeros((CHUNK, CHUNK), F32)
            for hs, mask in zip(HG_LEVELS, level_masks):
                ref = jnp.concatenate(
                    [jnp.broadcast_to(b[j + hs - 1:j + hs], (2 * hs, HG_DK)) for j in range(0, CHUNK, 2 * hs)],
                    axis=0)
                qt = (qf * jnp.exp2(b - ref)).astype(BF16)
                kt = (k * jnp.exp2(ref - b)).astype(BF16)
                att = att + jnp.where(mask, _dot_nt(qt, kt), 0.0)
            o = _dot(att.astype(BF16), vb)

            blocks = []
            for i in range(n_sub):
                rs = slice(i * HG_SUB, (i + 1) * HG_SUB)
                b_i, q_i, k_i, v_i = b[rs], qf[rs], k[rs], v[rs]
                acc = jnp.zeros((HG_SUB, HG_DV), F32)
                for s in range(HG_SUB):
                    a = jnp.sum(jnp.exp2(b_i - b_i[s:s + 1]) * (q_i * k_i[s:s + 1]), axis=1, keepdims=True)
                    acc = acc + jnp.where(sub_row >= s, a, 0.0) * v_i[s:s + 1]
                blocks.append(acc)
            o = o + jnp.concatenate(blocks, axis=0)

            st = st_ref[h]
            o = o + _dot_nt((qf * jnp.exp2(b)).astype(BF16), st.astype(BF16))
            b_last = b[CHUNK - 1:CHUNK]
            kd = (k * jnp.exp2(b_last - b)).astype(BF16)
            st_ref[h] = st * jnp.exp2(b_last) + _dot_tn(vb, kd)

            gt = gate_ref[rows, sl].astype(F32)
            o_ref[rows, sl] = (_rms(o, gn_ref[:, sl]) * _silu(gt)).astype(BF16)
        return carry

    lax.fori_loop(0, n_chunks, chunk_body, 0, unroll=2)


def _hgrn(hg_lb_logits, hg_norm_g, hq, hf, hi, hgate, layer, batch, seq):
    t = hq.shape[0]
    lc = min(512, seq)
    nb = seq // lc
    rows = pl.BlockSpec((lc, HG_WIDTH), lambda b, j: (b * nb + j, 0))
    kern = functools.partial(_hgrn_kernel, layer=layer, n_chunks=lc // CHUNK)
    return pl.pallas_call(
        kern,
        out_shape=jax.ShapeDtypeStruct((t, HG_WIDTH), BF16),
        grid=(batch, nb),
        in_specs=[pl.BlockSpec(hg_lb_logits.shape, lambda b, j: (0, 0)),
                  pl.BlockSpec((1, HG_WIDTH), lambda b, j: (0, 0)),
                  rows, rows, rows, rows],
        out_specs=rows,
        scratch_shapes=[pltpu.VMEM((HG_HEADS, HG_DV, HG_DK), F32)],
        compiler_params=_params("parallel", "arbitrary"),
        name="hgrn2_chunkwise",
    )(hg_lb_logits, hg_norm_g[None, :], hq, hf, hi, hgate)


def _attn_kernel(qi_ref, kj_ref, q_ref, k_ref, v_ref, o_ref, m_ref, acc_ref, *, tq):
    p_id = pl.program_id(1)
    i = qi_ref[p_id]
    j = kj_ref[p_id]

    @pl.when(j == 0)
    def _():
        m_ref[...] = jnp.full_like(m_ref, -jnp.inf)
        acc_ref[...] = jnp.zeros_like(acc_ref)

    def step(diagonal):
        if diagonal:
            qc = lax.broadcasted_iota(jnp.int32, (tq, tq), 0) // CHUNK
            kc = lax.broadcasted_iota(jnp.int32, (tq, tq), 1) // CHUNK
            visible = kc <= qc
        for h in range(MLA_HEADS):
            sl = slice(h * HEAD_PAD, (h + 1) * HEAD_PAD)
            s = _dot_nt(q_ref[:, sl], k_ref[:, sl])
            if diagonal:
                s = jnp.where(visible, s, -jnp.inf)
            tiles = [s[:, t * HEAD_PAD:(t + 1) * HEAD_PAD] for t in range(tq // HEAD_PAD)]
            m_tile = tiles[0]
            for tl in tiles[1:]:
                m_tile = jnp.maximum(m_tile, tl)
            m_prev = m_ref[h]
            m_new = jnp.maximum(m_prev, jnp.max(m_tile, axis=1, keepdims=True))
            alpha = jnp.exp2(m_prev - m_new)
            p = jnp.concatenate([jnp.exp2((tl - m_new).astype(BF16)) for tl in tiles], axis=1)
            acc_ref[h] = acc_ref[h] * alpha + _dot(p, v_ref[:, sl])
            m_ref[h] = m_new

    @pl.when(j < i)
    def _():
        step(False)

    @pl.when(j == i)
    def _():
        step(True)
        outs = []
        for h in range(MLA_HEADS):
            acc = acc_ref[h]
            outs.append(acc[:, 0:MLA_V] / acc[:, MLA_V:MLA_V + 1])
        o_ref[...] = jnp.concatenate(outs, axis=1).astype(BF16)


def _attention(q, k, v, batch, seq):
    t, hp = q.shape
    tq = min(512, seq)
    nq = seq // tq
    pairs = [(i, j) for i in range(nq) for j in range(i + 1)]
    q_of = jnp.asarray([p[0] for p in pairs], jnp.int32)
    k_of = jnp.asarray([p[1] for p in pairs], jnp.int32)
    qspec = pl.BlockSpec((tq, hp), lambda b, p, qi, kj: (b * nq + qi[p], 0))
    kspec = pl.BlockSpec((tq, hp), lambda b, p, qi, kj: (b * nq + kj[p], 0))
    return pl.pallas_call(
        functools.partial(_attn_kernel, tq=tq),
        out_shape=jax.ShapeDtypeStruct((t, MLA_WIDTH), BF16),
        grid_spec=pltpu.PrefetchScalarGridSpec(
            num_scalar_prefetch=2,
            grid=(batch, len(pairs)),
            in_specs=[qspec, kspec, kspec],
            out_specs=pl.BlockSpec((tq, MLA_WIDTH), lambda b, p, qi, kj: (b * nq + qi[p], 0)),
            scratch_shapes=[pltpu.VMEM((MLA_HEADS, tq, HEAD_PAD), F32),
                            pltpu.VMEM((MLA_HEADS, tq, HEAD_PAD), F32)]),
        compiler_params=_params("parallel", "arbitrary"),
        name="mla_flash_attention",
    )(q_of, k_of, q, k, v)


PAIRS_PER_GROUP = EXPERTS_PER_GROUP * (EXPERTS_PER_GROUP - 1) // 2
N_CLASSES = N_GROUPS * PAIRS_PER_GROUP
LANES = 128
ROW_CHUNKS = D_MODEL // LANES
TOKEN_TILE = 512
EXPERT_TILE = 256


N_CLASS_ROWS = 32


def _route(scores, bias):
    biased = scores + bias
    col = [biased[e:e + 1, :] for e in range(N_EXPERTS)]
    gscore = []
    for g in range(N_GROUPS):
        a, b, c, d = col[g * EXPERTS_PER_GROUP:(g + 1) * EXPERTS_PER_GROUP]
        gscore.append(jnp.maximum(jnp.maximum(jnp.maximum(a + b, a + c), jnp.maximum(a + d, b + c)),
                                  jnp.maximum(b + d, c + d)))
    sel = []
    for g in range(N_GROUPS):
        ok = None
        for o in range(N_GROUPS):
            if o == g:
                continue
            cond = (gscore[g] > gscore[o]) if o < g else (gscore[g] >= gscore[o])
            ok = cond if ok is None else jnp.logical_and(ok, cond)
        grp = col[g * EXPERTS_PER_GROUP:(g + 1) * EXPERTS_PER_GROUP]
        for e in range(EXPERTS_PER_GROUP):
            beaten = jnp.zeros_like(grp[e])
            for o in range(EXPERTS_PER_GROUP):
                if o == e:
                    continue
                ahead = (grp[o] >= grp[e]) if o < e else (grp[o] > grp[e])
                beaten = beaten + jnp.where(ahead, 1.0, 0.0)
            sel.append(jnp.logical_and(ok, beaten < 1.5))
    lo = jnp.full_like(col[0], float(N_EXPERTS))
    hi = jnp.full_like(col[0], -1.0)
    for e in range(N_EXPERTS):
        lo = jnp.where(sel[e], jnp.minimum(lo, float(e)), lo)
        hi = jnp.where(sel[e], jnp.maximum(hi, float(e)), hi)
    return lo, hi


def _store_rows(ref, x):
    n = x.shape[0]
    for c in range(ROW_CHUNKS):
        ref[pl.ds(c, n, stride=ROW_CHUNKS), :] = x[:, c * LANES:(c + 1) * LANES]


def _load_rows(ref, n):
    return jnp.concatenate([ref[pl.ds(c, n, stride=ROW_CHUNKS), :] for c in range(ROW_CHUNKS)], axis=1)


def _merge_kernel(x_ref, oa_ref, ob_ref, g_ref, mod_ref, wa_ref, wb_ref, wo_ref, n2_ref,
                  wrt_ref, rb_ref, xn_ref, row_ref, meta_ref):
    d = D_MODEL
    tm = x_ref.shape[0]
    mod = mod_ref[0]
    ya = _dot(oa_ref[...], wa_ref[...])
    yb = _dot(ob_ref[...], wb_ref[...])
    g = g_ref[...].astype(F32)
    merged = g[:, 0:d] * ya + g[:, d:2 * d] * yb
    xn = x_ref[...] + mod[2:3] * _dot(merged.astype(BF16), wo_ref[...])
    xn_ref[...] = xn
    h2 = _rms(xn, n2_ref[...]) * (1.0 + mod[4:5]) + mod[3:4]
    _store_rows(row_ref, h2)

    h_hi, h_lo = _split_bf16(h2)
    w_hi, w_lo = _split_bf16(wrt_ref[...])
    scores = _sigmoid(_dot_nt(w_hi, h_hi) + _dot_nt(w_hi, h_lo) + _dot_nt(w_lo, h_hi))
    lo, hi = _route(scores, rb_ref[...])

    grp = jnp.floor(lo * (1.0 / EXPERTS_PER_GROUP))
    a = lo - grp * EXPERTS_PER_GROUP
    b = hi - grp * EXPERTS_PER_GROUP
    cls = grp * PAIRS_PER_GROUP + a * (2 * EXPERTS_PER_GROUP - 1 - a) * 0.5 + (b - a - 1.0)
    c_row = lax.broadcasted_iota(jnp.int32, (N_CLASS_ROWS, tm), 0).astype(F32)
    onehot = jnp.where(c_row == cls, 1.0, 0.0)
    r_i = lax.broadcasted_iota(jnp.int32, (tm, tm), 0)
    c_i = lax.broadcasted_iota(jnp.int32, (tm, tm), 1)
    earlier = jnp.where(r_i < c_i, 1.0, 0.0).astype(BF16)
    rank = jnp.sum(onehot * _dot(onehot.astype(BF16), earlier), axis=0, keepdims=True)
    m_row = lax.broadcasted_iota(jnp.int32, (8, tm), 0)
    meta_ref[...] = jnp.where(m_row == 0, cls, jnp.where(m_row == 1, rank, 0.0))


def _merge(x2, oa, ob, gsig, mod_l, w_br_a, w_br_b, w_out, n2, w_router, router_bias, seq):
    t, d = x2.shape
    tm = TOKEN_TILE
    rows = lambda n: pl.BlockSpec((tm, n), lambda i: (i, 0))
    return pl.pallas_call(
        _merge_kernel,
        out_shape=(jax.ShapeDtypeStruct((t, d), F32), jax.ShapeDtypeStruct((t * ROW_CHUNKS, LANES), F32),
                   jax.ShapeDtypeStruct((8, t), F32)),
        grid=(t // tm,),
        in_specs=[rows(d), rows(HG_WIDTH), rows(MLA_WIDTH), rows(2 * d),
                  pl.BlockSpec((1, 6, d), lambda i: ((i * tm) // seq, 0, 0)),
                  _resident(w_br_a.shape), _resident(w_br_b.shape), _resident(w_out.shape),
                  _resident((1, d)), _resident((N_EXPERTS, d)), _resident((N_EXPERTS, 1))],
        out_specs=(rows(d), pl.BlockSpec((tm * ROW_CHUNKS, LANES), lambda i: (i, 0)),
                   pl.BlockSpec((8, tm), lambda i: (0, i))),
        compiler_params=_params("parallel"),
        name="merge_outproj_router",
    )(x2, oa, ob, gsig, mod_l, w_br_a.astype(BF16), w_br_b.astype(BF16), w_out.astype(BF16),
      n2[None, :], w_router.T, router_bias[:, None])


def _dispatch_plan(meta, n_tiles_e):
    t = meta.shape[1]
    n_tok_tiles = t // TOKEN_TILE
    cls = meta[0].astype(jnp.int32)
    rank = meta[1].astype(jnp.int32)
    own = cls.reshape(n_tok_tiles, TOKEN_TILE, 1) == jnp.arange(N_CLASSES, dtype=jnp.int32)
    counts = jnp.sum(own.astype(jnp.int32), axis=1)
    total = jnp.sum(counts, axis=0)
    total_pad = (total + EXPERT_TILE - 1) // EXPERT_TILE * EXPERT_TILE
    ends = jnp.cumsum(total_pad)
    base = (ends - total_pad)[None, :] + jnp.cumsum(counts, axis=0) - counts
    pos = jnp.sum(jnp.where(own, base[:, None, :], 0), axis=2).reshape(t) + rank

    tile_start = jnp.arange(n_tiles_e, dtype=jnp.int32) * EXPERT_TILE
    n_valid = ends[-1] // EXPERT_TILE
    tile_cls = jnp.sum((tile_start[:, None] >= ends[None, :]).astype(jnp.int32), axis=1)
    last_cls = jnp.take(tile_cls, n_valid - 1)
    valid = jnp.arange(n_tiles_e, dtype=jnp.int32) < n_valid
    tile_cls = jnp.where(valid, tile_cls, last_cls)
    pair_lo = jnp.asarray([a for a in range(EXPERTS_PER_GROUP) for b in range(a + 1, EXPERTS_PER_GROUP)], jnp.int32)
    pair_hi = jnp.asarray([b for a in range(EXPERTS_PER_GROUP) for b in range(a + 1, EXPERTS_PER_GROUP)], jnp.int32)
    grp = tile_cls // PAIRS_PER_GROUP
    e_lo = grp * EXPERTS_PER_GROUP + jnp.take(pair_lo, tile_cls % PAIRS_PER_GROUP)
    e_hi = grp * EXPERTS_PER_GROUP + jnp.take(pair_hi, tile_cls % PAIRS_PER_GROUP)
    pad_tile = (ends // EXPERT_TILE - 1).astype(jnp.int32)
    used = jnp.concatenate([(total > 0).astype(jnp.int32), n_valid[None].astype(jnp.int32)])
    return (pos * ROW_CHUNKS).astype(jnp.int32), pad_tile, used, e_lo, e_hi, valid.astype(jnp.int32)


DMA_UNROLL = 8


def _row_copies(n_rows, make_copy):
    def issue(ui, carry):
        for u in range(DMA_UNROLL):
            make_copy(ui * DMA_UNROLL + u).start(priority=u % 2)
        return carry
    lax.fori_loop(0, n_rows // DMA_UNROLL, issue, 0)


def _token_slab(ref, first_row):
    return ref.at[pl.ds(pl.multiple_of(first_row, ROW_CHUNKS), ROW_CHUNKS)]


def _dispatch_kernel(pos_ref, pad_tile_ref, used_ref, rows_ref, dst_hbm, zero_ref, sem, zero_sem):
    tm = rows_ref.shape[0] // ROW_CHUNKS
    tile_rows = EXPERT_TILE * ROW_CHUNKS

    @pl.when(pl.program_id(0) == 0)
    def _():
        zero_ref[...] = jnp.zeros_like(zero_ref)
        n_tiles = dst_hbm.shape[0] // tile_rows

        def clear(tile):
            first = pl.multiple_of(tile * tile_rows, tile_rows)
            return pltpu.make_async_copy(zero_ref, dst_hbm.at[pl.ds(first, tile_rows)], zero_sem)

        def start_tail(tile, carry):
            clear(tile).start()
            return carry

        def wait_tail(tile, carry):
            clear(tile).wait()
            return carry
        for c in range(N_CLASSES):
            @pl.when(used_ref[c] != 0)
            def _():
                clear(pad_tile_ref[c]).start()
        lax.fori_loop(used_ref[N_CLASSES], n_tiles, start_tail, 0)
        for c in range(N_CLASSES):
            @pl.when(used_ref[c] != 0)
            def _():
                clear(pad_tile_ref[c]).wait()
        lax.fori_loop(used_ref[N_CLASSES], n_tiles, wait_tail, 0)

    base = pl.program_id(0) * tm
    _row_copies(tm, lambda r: pltpu.make_async_copy(
        _token_slab(rows_ref, r * ROW_CHUNKS), _token_slab(dst_hbm, pos_ref[base + r]), sem))
    pltpu.make_async_copy(rows_ref, dst_hbm.at[pl.ds(0, tm * ROW_CHUNKS)], sem).wait()


def _dispatch_rows(pos, pad_tile, used, rows, n_dst):
    t = rows.shape[0] // ROW_CHUNKS
    tm = TOKEN_TILE
    return pl.pallas_call(
        _dispatch_kernel,
        out_shape=jax.ShapeDtypeStruct((n_dst * ROW_CHUNKS, LANES), rows.dtype),
        grid_spec=pltpu.PrefetchScalarGridSpec(
            num_scalar_prefetch=3, grid=(t // tm,),
            in_specs=[pl.BlockSpec((tm * ROW_CHUNKS, LANES), lambda i, p, pt, us: (i, 0))],
            out_specs=pl.BlockSpec(memory_space=pl.ANY),
            scratch_shapes=[pltpu.VMEM((EXPERT_TILE * ROW_CHUNKS, LANES), rows.dtype),
                            pltpu.SemaphoreType.DMA, pltpu.SemaphoreType.DMA]),
        compiler_params=pltpu.CompilerParams(dimension_semantics=("arbitrary",), disable_bounds_checks=True,
                                             has_side_effects=True),
        name="dispatch_rows",
    )(pos, pad_tile, used, rows)


def _moe_kernel(elo_ref, ehi_ref, valid_ref, x_ref, wrt_ref, wgl_ref, wul_ref, wdl_ref, wgh_ref, wuh_ref,
                wdh_ref, y_ref):
    j = pl.program_id(0)

    @pl.when(valid_ref[j] == 0)
    def _():
        y_ref[...] = jnp.zeros_like(y_ref)

    @pl.when(valid_ref[j] != 0)
    def _():
        h2 = _load_rows(x_ref, EXPERT_TILE)
        s_lo = _sigmoid(jnp.sum(h2 * wrt_ref[pl.ds(elo_ref[j], 1), :], axis=1, keepdims=True))
        s_hi = _sigmoid(jnp.sum(h2 * wrt_ref[pl.ds(ehi_ref[j], 1), :], axis=1, keepdims=True))
        total = s_lo + s_hi
        hb = h2.astype(BF16)
        he_lo = _silu(_dot(hb, wgl_ref[0, 0])) * _dot(hb, wul_ref[0, 0])
        y = (s_lo / total) * _dot(he_lo.astype(BF16), wdl_ref[0, 0])
        he_hi = _silu(_dot(hb, wgh_ref[0, 0])) * _dot(hb, wuh_ref[0, 0])
        _store_rows(y_ref, y + (s_hi / total) * _dot(he_hi.astype(BF16), wdh_ref[0, 0]))


def _moe(rows_sorted, e_lo, e_hi, valid, w_router, wg, wu, wd, layer):
    n_pad = rows_sorted.shape[0] // ROW_CHUNKS
    d = D_MODEL
    lo4 = lambda j, el, eh, va: (layer, el[j], 0, 0)
    hi4 = lambda j, el, eh, va: (layer, eh[j], 0, 0)
    tile = pl.BlockSpec((EXPERT_TILE * ROW_CHUNKS, LANES), lambda j, el, eh, va: (j, 0))
    return pl.pallas_call(
        _moe_kernel,
        out_shape=jax.ShapeDtypeStruct(rows_sorted.shape, F32),
        grid_spec=pltpu.PrefetchScalarGridSpec(
            num_scalar_prefetch=3, grid=(n_pad // EXPERT_TILE,),
            in_specs=[tile, _resident((N_EXPERTS, d)),
                      pl.BlockSpec((1, 1, d, D_EXPERT), lo4), pl.BlockSpec((1, 1, d, D_EXPERT), lo4),
                      pl.BlockSpec((1, 1, D_EXPERT, d), lo4),
                      pl.BlockSpec((1, 1, d, D_EXPERT), hi4), pl.BlockSpec((1, 1, d, D_EXPERT), hi4),
                      pl.BlockSpec((1, 1, D_EXPERT, d), hi4)],
            out_specs=tile),
        compiler_params=_params("arbitrary"),
        name="moe_experts",
    )(e_lo, e_hi, valid, rows_sorted, w_router.T, wg, wu, wd, wg, wu, wd)


def _collect_kernel(pos_ref, x_ref, ys_hbm, mod_ref, fg_ref, o_ref, buf0, buf1, sems, *, final):
    tm = x_ref.shape[0]
    i = pl.program_id(0)
    n = pl.num_programs(0)
    bufs = (buf0, buf1)

    def fetch(tile, slot):
        base = tile * tm
        _row_copies(tm, lambda r: pltpu.make_async_copy(
            _token_slab(ys_hbm, pos_ref[base + r]), _token_slab(bufs[slot], r * ROW_CHUNKS), sems.at[slot]))

    def finish(slot):
        pltpu.make_async_copy(ys_hbm.at[pl.ds(0, tm * ROW_CHUNKS)], bufs[slot], sems.at[slot]).wait()
        xo = x_ref[...] + mod_ref[0][5:6] * _load_rows(bufs[slot], tm)
        if final:
            xo = _rms(xo, fg_ref[...])
        o_ref[...] = xo

    @pl.when(i == 0)
    def _():
        fetch(0, 0)

    for slot in range(2):
        @pl.when(jnp.logical_and(i + 1 < n, (i + 1) % 2 == slot))
        def _():
            fetch(i + 1, slot)

    for slot in range(2):
        @pl.when(i % 2 == slot)
        def _():
            finish(slot)


def _collect_residual(pos, xn, y_sorted, mod_l, final_g, final, seq):
    t, d = xn.shape
    tm = TOKEN_TILE
    rows = pl.BlockSpec((tm, d), lambda i, p: (i, 0))
    return pl.pallas_call(
        functools.partial(_collect_kernel, final=final),
        out_shape=jax.ShapeDtypeStruct((t, d), F32),
        grid_spec=pltpu.PrefetchScalarGridSpec(
            num_scalar_prefetch=1, grid=(t // tm,),
            in_specs=[rows, pl.BlockSpec(memory_space=pl.ANY),
                      pl.BlockSpec((1, 6, d), lambda i, p: ((i * tm) // seq, 0, 0)),
                      pl.BlockSpec((1, d), lambda i, p: (0, 0))],
            out_specs=rows,
            scratch_shapes=[pltpu.VMEM((tm * ROW_CHUNKS, LANES), F32), pltpu.VMEM((tm * ROW_CHUNKS, LANES), F32),
                            pltpu.SemaphoreType.DMA((2,))]),
        compiler_params=pltpu.CompilerParams(dimension_semantics=("arbitrary",), disable_bounds_checks=True,
                                             vmem_limit_bytes=V7X_VMEM_LIMIT_BYTES),
        name="collect_residual",
    )(pos, xn, y_sorted, mod_l, final_g[None, :])


def kernel(x, c, positions, ada_w, ada_b, norm1_g, w_in, hg_lb_logits, hg_norm_g, q_norm_g, w_q_up,
           kv_norm_g, w_kv_up, w_br_a, w_br_b, w_out, norm2_g, w_router, router_bias, w_gate, w_up,
           w_down, final_g):
    batch, seq, d = x.shape
    depth = ada_w.shape[0]
    t = batch * seq
    n_tiles_e = (t + N_CLASSES * (EXPERT_TILE - 1)) // EXPERT_TILE
    mod = _modulation(c, ada_w, ada_b).reshape(depth, batch, 6, d)
    cos_t, sin_t = _rope_tables(positions)
    x2 = x.reshape(t, d)
    for l in range(depth):
        hq, hf, hi, hgate, gsig, q, k, v = _input_projection(
            x2, mod[l], norm1_g[l], cos_t, sin_t, w_in[l], q_norm_g[l], w_q_up[l],
            kv_norm_g[l], w_kv_up[l], seq)
        oa = _hgrn(hg_lb_logits, hg_norm_g[l], hq, hf, hi, hgate, l, batch, seq)
        ob = _attention(q, k, v, batch, seq)
        xn, rows, meta = _merge(x2, oa, ob, gsig, mod[l], w_br_a[l], w_br_b[l], w_out[l], norm2_g[l],
                                w_router, router_bias, seq)
        pos, pad_tile, used, e_lo, e_hi, valid = _dispatch_plan(meta, n_tiles_e)
        rows_sorted = _dispatch_rows(pos, pad_tile, used, rows, n_tiles_e * EXPERT_TILE)
        y_sorted = _moe(rows_sorted, e_lo, e_hi, valid, w_router, w_gate, w_up, w_down, l)
        x2 = _collect_residual(pos, xn, y_sorted, mod[l], final_g, l == depth - 1, seq)
    return x2.reshape(batch, seq, d)
```

```python
import functools
import math

import jax
import jax.numpy as jnp
from jax import lax
from jax.experimental import pallas as pl
from jax.experimental.pallas import tpu as pltpu

F32 = jnp.float32
BF16 = jnp.bfloat16

D_MODEL = 1024
CHUNK = 64
EPS = 1e-6

HG_HEADS = 4
HG_DK = 128
HG_DV = 128
HG_WIDTH = HG_HEADS * HG_DV
HG_SUB = 8
HG_LEVELS = (8, 16, 32)

MLA_HEADS = 8
MLA_NOPE = 64
MLA_ROPE = 32
MLA_V = 64
MLA_Q_LORA = 384
MLA_KV_LORA = 256
MLA_DQK = MLA_NOPE + MLA_ROPE
MLA_WIDTH = MLA_HEADS * MLA_V
ROPE_BASE = 10000.0
HEAD_PAD = 128
LOG2_E = math.log2(math.e)

N_EXPERTS = 16
N_GROUPS = 4
EXPERTS_PER_GROUP = N_EXPERTS // N_GROUPS
D_EXPERT = 512

IN_SIZES = (HG_HEADS * HG_DK, HG_HEADS * HG_DK, HG_HEADS * HG_DV, HG_WIDTH,
            MLA_Q_LORA, MLA_KV_LORA, MLA_ROPE, 2 * D_MODEL)

V7X_VMEM_LIMIT_BYTES = 56 * 1024 * 1024


def _params(*sem):
    return pltpu.CompilerParams(dimension_semantics=sem, vmem_limit_bytes=V7X_VMEM_LIMIT_BYTES)


def _resident(shape):
    nd = len(shape)
    return pl.BlockSpec(shape, lambda *_: (0,) * nd, pipeline_mode=pl.Buffered(1))


def _sigmoid(x):
    return 1.0 / (1.0 + jnp.exp(-x))


def _silu(x):
    return x * _sigmoid(x)


def _dot(a, b):
    return jnp.dot(a, b, preferred_element_type=F32)


def _dot_nt(a, b):
    return lax.dot_general(a, b, (((1,), (1,)), ((), ())), preferred_element_type=F32)


def _dot_tn(a, b):
    return lax.dot_general(a, b, (((0,), (0,)), ((), ())), preferred_element_type=F32)


def _split_bf16(x):
    hi = x.astype(BF16)
    lo = (x - hi.astype(F32)).astype(BF16)
    return hi, lo


def _rms(x, g):
    return x * lax.rsqrt(jnp.mean(x * x, axis=-1, keepdims=True) + EPS) * g


def _mod_kernel(c_ref, w_ref, b_ref, o_ref):
    ca = _silu(c_ref[...])
    o_ref[0] = _dot(ca.astype(BF16), w_ref[0].astype(BF16)) + b_ref[0]


def _modulation(c, ada_w, ada_b):
    depth, d, n = ada_w.shape
    b = c.shape[0]
    tn = 1536
    return pl.pallas_call(
        _mod_kernel,
        out_shape=jax.ShapeDtypeStruct((depth, b, n), F32),
        grid=(depth, n // tn),
        in_specs=[pl.BlockSpec((b, d), lambda l, j: (0, 0)),
                  pl.BlockSpec((1, d, tn), lambda l, j: (l, 0, j)),
                  pl.BlockSpec((1, 1, tn), lambda l, j: (l, 0, j))],
        out_specs=pl.BlockSpec((1, b, tn), lambda l, j: (l, 0, j)),
        compiler_params=_params("parallel", "parallel"),
        name="adaln_modulation",
    )(c, ada_w, ada_b.reshape(depth, 1, n))


def _rope_kernel(pos_ref, inv_ref, msk_ref, sgn_ref, cos_ref, sin_ref):
    ang = pos_ref[...].astype(F32) * inv_ref[...]
    cos_ref[...] = jnp.cos(ang) * msk_ref[...]
    sin_ref[...] = jnp.sin(ang) * sgn_ref[...]


def _rope_tables(positions):
    t = positions.size
    half = MLA_ROPE // 2
    inv = ROPE_BASE ** (-jnp.arange(half, dtype=F32) / half)
    z64, z32, one16 = jnp.zeros((MLA_NOPE,), F32), jnp.zeros((32,), F32), jnp.ones((half,), F32)
    inv_row = jnp.concatenate([z64, inv, inv, z32])[None, :]
    msk_row = jnp.concatenate([z64, one16, one16, z32])[None, :]
    sgn_row = jnp.concatenate([z64, -one16, one16, z32])[None, :]
    tr = min(2048, t)
    row = pl.BlockSpec((1, HEAD_PAD), lambda i: (0, 0))
    tab = pl.BlockSpec((tr, HEAD_PAD), lambda i: (i, 0))
    return pl.pallas_call(
        _rope_kernel,
        out_shape=(jax.ShapeDtypeStruct((t, HEAD_PAD), F32),) * 2,
        grid=(t // tr,),
        in_specs=[pl.BlockSpec((tr, 1), lambda i: (i, 0)), row, row, row],
        out_specs=(tab, tab),
        compiler_params=_params("parallel"),
        name="rope_tables",
    )(positions.reshape(t, 1), inv_row, msk_row, sgn_row)


def _proj_kernel(x_ref, mod_ref, n1_ref, cos_ref, sin_ref, wh_ref, wg_ref, wc_ref,
                 qn_ref, wqa_ref, wqs_ref, kn_ref, wk_ref, wv_ref, one_ref,
                 hq_ref, hf_ref, hi_ref, hgate_ref, gsig_ref, q_ref, k_ref, v_ref):
    x = x_ref[...]
    mod = mod_ref[0]
    h = _rms(x, n1_ref[...]) * (1.0 + mod[1:2]) + mod[0:1]
    hb = h.astype(BF16)

    ph = _dot(hb, wh_ref[...])
    w = HG_WIDTH
    hq_ref[...] = ph[:, 0:w].astype(BF16)
    hf_ref[...] = ph[:, w:2 * w]
    hi_ref[...] = ph[:, 2 * w:3 * w].astype(BF16)
    hgate_ref[...] = ph[:, 3 * w:4 * w].astype(BF16)

    gsig_ref[...] = _sigmoid(_dot(hb, wg_ref[...])).astype(BF16)

    pc = _dot(hb, wc_ref[...])
    cq = pc[:, 0:MLA_Q_LORA]
    ckv = pc[:, MLA_Q_LORA:MLA_Q_LORA + MLA_KV_LORA]
    kra = pc[:, MLA_Q_LORA + MLA_KV_LORA:MLA_Q_LORA + MLA_KV_LORA + HEAD_PAD]
    krb = pc[:, MLA_Q_LORA + MLA_KV_LORA + HEAD_PAD:]

    cos_t = cos_ref[...]
    sin_t = sin_ref[...]
    lane = lax.broadcasted_iota(jnp.int32, cos_t.shape, 1)
    scale = MLA_DQK ** -0.5 * LOG2_E
    cq_tab = jnp.tile(scale * (cos_t + jnp.where(lane < MLA_NOPE, 1.0, 0.0)), (1, MLA_HEADS))
    sq_tab = jnp.tile(scale * sin_t, (1, MLA_HEADS))

    cqn = _rms(cq, qn_ref[...]).astype(BF16)
    q = _dot(cqn, wqa_ref[...]) * cq_tab + _dot(cqn, wqs_ref[...]) * sq_tab
    q_ref[...] = q.astype(BF16)

    ckvn = _rms(ckv, kn_ref[...]).astype(BF16)
    kpe = kra * cos_t + krb * sin_t
    k_ref[...] = (_dot(ckvn, wk_ref[...]) + jnp.tile(kpe, (1, MLA_HEADS))).astype(BF16)
    v_ref[...] = (_dot(ckvn, wv_ref[...]) + one_ref[...]).astype(BF16)


def _pad_heads(w, lo, hi, at):
    k, nh, _ = w.shape
    out = jnp.zeros((k, nh, HEAD_PAD), w.dtype)
    out = out.at[:, :, at:at + (hi - lo)].set(w[:, :, lo:hi])
    return out


def _input_projection(x2, mod_l, n1, cos_t, sin_t, w_in, q_norm_g, w_q_up, kv_norm_g, w_kv_up, seq):
    t, d = x2.shape
    tm = 256
    splits = [0]
    for s in IN_SIZES:
        splits.append(splits[-1] + s)
    kr = w_in[:, splits[6]:splits[7]]
    half = MLA_ROPE // 2
    z64 = jnp.zeros((d, MLA_NOPE), F32)
    z32 = jnp.zeros((d, HEAD_PAD - MLA_NOPE - MLA_ROPE), F32)
    kr_a = jnp.concatenate([z64, kr, z32], axis=1)
    kr_b = jnp.concatenate([z64, kr[:, half:], kr[:, :half], z32], axis=1)
    w_h = w_in[:, splits[0]:splits[4]].astype(BF16)
    w_g = w_in[:, splits[7]:splits[8]].astype(BF16)
    w_c = jnp.concatenate([w_in[:, splits[4]:splits[6]], kr_a, kr_b], axis=1).astype(BF16)

    wq = w_q_up.reshape(MLA_Q_LORA, MLA_HEADS, MLA_DQK)
    wq_all = _pad_heads(wq, 0, MLA_DQK, 0).reshape(MLA_Q_LORA, -1).astype(BF16)
    wq_swap = (_pad_heads(wq, MLA_NOPE + half, MLA_DQK, MLA_NOPE)
               + _pad_heads(wq, MLA_NOPE, MLA_NOPE + half, MLA_NOPE + half))
    wq_swap = wq_swap.reshape(MLA_Q_LORA, -1).astype(BF16)
    wkv = w_kv_up.reshape(MLA_KV_LORA, MLA_HEADS, MLA_NOPE + MLA_V)
    wk_all = _pad_heads(wkv, 0, MLA_NOPE, 0).reshape(MLA_KV_LORA, -1).astype(BF16)
    wv_all = _pad_heads(wkv, MLA_NOPE, MLA_NOPE + MLA_V, 0).reshape(MLA_KV_LORA, -1).astype(BF16)
    ones_row = jnp.tile(jnp.zeros((HEAD_PAD,), F32).at[MLA_V].set(1.0), MLA_HEADS)[None, :]

    hp = MLA_HEADS * HEAD_PAD
    rows = lambda n: pl.BlockSpec((tm, n), lambda i: (i, 0))
    outs = [(HG_WIDTH, BF16), (HG_WIDTH, F32), (HG_WIDTH, BF16), (HG_WIDTH, BF16),
            (2 * d, BF16), (hp, BF16), (hp, BF16), (hp, BF16)]
    return pl.pallas_call(
        _proj_kernel,
        out_shape=tuple(jax.ShapeDtypeStruct((t, n), dt) for n, dt in outs),
        grid=(t // tm,),
        in_specs=[rows(d),
                  pl.BlockSpec((1, 6, d), lambda i: ((i * tm) // seq, 0, 0)),
                  _resident((1, d)),
                  rows(HEAD_PAD), rows(HEAD_PAD),
                  _resident(w_h.shape), _resident(w_g.shape), _resident(w_c.shape),
                  _resident((1, MLA_Q_LORA)), _resident(wq_all.shape), _resident(wq_swap.shape),
                  _resident((1, MLA_KV_LORA)), _resident(wk_all.shape), _resident(wv_all.shape),
                  _resident((1, hp))],
        out_specs=tuple(rows(n) for n, _ in outs),
        compiler_params=_params("parallel"),
        name="norm_input_projection",
    )(x2, mod_l, n1[None, :], cos_t, sin_t, w_h, w_g, w_c,
      q_norm_g[None, :], wq_all, wq_swap, kv_norm_g[None, :], wk_all, wv_all, ones_row)


def _hgrn_kernel(lbl_ref, gn_ref, q_ref, f_ref, v_ref, gate_ref, o_ref, st_ref, *, layer, n_chunks):
    @pl.when(pl.program_id(1) == 0)
    def _():
        st_ref[...] = jnp.zeros_like(st_ref)

    lg = lbl_ref[...]
    ex = jnp.exp(lg - jnp.max(lg, axis=0, keepdims=True))
    soft = ex / jnp.sum(ex, axis=0, keepdims=True)
    lb_all = jnp.zeros_like(soft[0:1])
    for i in range(1, layer + 1):
        lb_all = lb_all + soft[i:i + 1]

    r_i = lax.broadcasted_iota(jnp.int32, (CHUNK, CHUNK), 0)
    c_i = lax.broadcasted_iota(jnp.int32, (CHUNK, CHUNK), 1)
    tri = jnp.where(r_i >= c_i, 1.0, 0.0).astype(BF16)
    level_masks = []
    for hs in HG_LEVELS:
        same = (r_i // (2 * hs)) == (c_i // (2 * hs))
        level_masks.append(jnp.logical_and(same, jnp.logical_and(r_i % (2 * hs) >= hs, c_i % (2 * hs) < hs)))
    sub_row = lax.broadcasted_iota(jnp.int32, (HG_SUB, 1), 0)
    n_sub = CHUNK // HG_SUB

    def chunk_body(ci, carry):
        r0 = pl.multiple_of(ci * CHUNK, CHUNK)
        rows = pl.ds(r0, CHUNK)
        for h in range(HG_HEADS):
            sl = slice(h * HG_DK, (h + 1) * HG_DK)
            lb = lb_all[:, sl]
            q = q_ref[rows, sl].astype(F32)
            vb = v_ref[rows, sl]
            v = vb.astype(F32)
            f = lb + (1.0 - lb) * _sigmoid(f_ref[rows, sl])
            g = jnp.log2(f)
            k = 1.0 - f
            qf = _silu(q) * (HG_DK ** -0.5)
            g_hi, g_lo = _split_bf16(g)
            b = _dot(tri, g_hi) + _dot(tri, g_lo)

            att = jnp.zeros((CHUNK, CHUNK), F32)
            for hs, mask in zip(HG_LEVELS, level_masks):
                ref = jnp.concatenate(
                    [jnp.broadcast_to(b[j + hs - 1:j + hs], (2 * hs, HG_DK)) for j in range(0, CHUNK, 2 * hs)],
                    axis=0)
                qt = (qf * jnp.exp2(b - ref)).astype(BF16)
                kt = (k * jnp.exp2(ref - b)).astype(BF16)
                att = att + jnp.where(mask, _dot_nt(qt, kt), 0.0)
            o = _dot(att.astype(BF16), vb)

            blocks = []
            for i in range(n_sub):
                rs = slice(i * HG_SUB, (i + 1) * HG_SUB)
                b_i, q_i, k_i, v_i = b[rs], qf[rs], k[rs], v[rs]
                acc = jnp.zeros((HG_SUB, HG_DV), F32)
                for s in range(HG_SUB):
                    a = jnp.sum(jnp.exp2(b_i - b_i[s:s + 1]) * (q_i * k_i[s:s + 1]), axis=1, keepdims=True)
                    acc = acc + jnp.where(sub_row >= s, a, 0.0) * v_i[s:s + 1]
                blocks.append(acc)
            o = o + jnp.concatenate(blocks, axis=0)

            st = st_ref[h]
            o = o + _dot_nt((qf * jnp.exp2(b)).astype(BF16), st.astype(BF16))
            b_last = b[CHUNK - 1:CHUNK]
            kd = (k * jnp.exp2(b_last - b)).astype(BF16)
            st_ref[h] = st * jnp.exp2(b_last) + _dot_tn(vb, kd)

            gt = gate_ref[rows, sl].astype(F32)
            o_ref[rows, sl] = (_rms(o, gn_ref[:, sl]) * _silu(gt)).astype(BF16)
        return carry

    lax.fori_loop(0, n_chunks, chunk_body, 0, unroll=2)


def _hgrn(hg_lb_logits, hg_norm_g, hq, hf, hi, hgate, layer, batch, seq):
    t = hq.shape[0]
    lc = min(512, seq)
    nb = seq // lc
    rows = pl.BlockSpec((lc, HG_WIDTH), lambda b, j: (b * nb + j, 0))
    kern = functools.partial(_hgrn_kernel, layer=layer, n_chunks=lc // CHUNK)
    return pl.pallas_call(
        kern,
        out_shape=jax.ShapeDtypeStruct((t, HG_WIDTH), BF16),
        grid=(batch, nb),
        in_specs=[pl.BlockSpec(hg_lb_logits.shape, lambda b, j: (0, 0)),
                  pl.BlockSpec((1, HG_WIDTH), lambda b, j: (0, 0)),
                  rows, rows, rows, rows],
        out_specs=rows,
        scratch_shapes=[pltpu.VMEM((HG_HEADS, HG_DV, HG_DK), F32)],
        compiler_params=_params("parallel", "arbitrary"),
        name="hgrn2_chunkwise",
    )(hg_lb_logits, hg_norm_g[None, :], hq, hf, hi, hgate)


def _attn_kernel(qi_ref, kj_ref, q_ref, k_ref, v_ref, o_ref, m_ref, acc_ref, *, tq):
    p_id = pl.program_id(1)
    i = qi_ref[p_id]
    j = kj_ref[p_id]

    @pl.when(j == 0)
    def _():
        m_ref[...] = jnp.full_like(m_ref, -jnp.inf)
        acc_ref[...] = jnp.zeros_like(acc_ref)

    def step(diagonal):
        if diagonal:
            qc = lax.broadcasted_iota(jnp.int32, (tq, tq), 0) // CHUNK
            kc = lax.broadcasted_iota(jnp.int32, (tq, tq), 1) // CHUNK
            visible = kc <= qc
        for h in range(MLA_HEADS):
            sl = slice(h * HEAD_PAD, (h + 1) * HEAD_PAD)
            s = _dot_nt(q_ref[:, sl], k_ref[:, sl])
            if diagonal:
                s = jnp.where(visible, s, -jnp.inf)
            tiles = [s[:, t * HEAD_PAD:(t + 1) * HEAD_PAD] for t in range(tq // HEAD_PAD)]
            m_tile = tiles[0]
            for tl in tiles[1:]:
                m_tile = jnp.maximum(m_tile, tl)
            m_prev = m_ref[h]
            m_new = jnp.maximum(m_prev, jnp.max(m_tile, axis=1, keepdims=True))
            alpha = jnp.exp2(m_prev - m_new)
            p = jnp.concatenate([jnp.exp2((tl - m_new).astype(BF16)) for tl in tiles], axis=1)
            acc_ref[h] = acc_ref[h] * alpha + _dot(p, v_ref[:, sl])
            m_ref[h] = m_new

    @pl.when(j < i)
    def _():
        step(False)

    @pl.when(j == i)
    def _():
        step(True)
        outs = []
        for h in range(MLA_HEADS):
            acc = acc_ref[h]
            outs.append(acc[:, 0:MLA_V] / acc[:, MLA_V:MLA_V + 1])
        o_ref[...] = jnp.concatenate(outs, axis=1).astype(BF16)


def _attention(q, k, v, batch, seq):
    t, hp = q.shape
    tq = min(512, seq)
    nq = seq // tq
    pairs = [(i, j) for i in range(nq) for j in range(i + 1)]
    q_of = jnp.asarray([p[0] for p in pairs], jnp.int32)
    k_of = jnp.asarray([p[1] for p in pairs], jnp.int32)
    qspec = pl.BlockSpec((tq, hp), lambda b, p, qi, kj: (b * nq + qi[p], 0))
    kspec = pl.BlockSpec((tq, hp), lambda b, p, qi, kj: (b * nq + kj[p], 0))
    return pl.pallas_call(
        functools.partial(_attn_kernel, tq=tq),
        out_shape=jax.ShapeDtypeStruct((t, MLA_WIDTH), BF16),
        grid_spec=pltpu.PrefetchScalarGridSpec(
            num_scalar_prefetch=2,
            grid=(batch, len(pairs)),
            in_specs=[qspec, kspec, kspec],
            out_specs=pl.BlockSpec((tq, MLA_WIDTH), lambda b, p, qi, kj: (b * nq + qi[p], 0)),
            scratch_shapes=[pltpu.VMEM((MLA_HEADS, tq, HEAD_PAD), F32),
                            pltpu.VMEM((MLA_HEADS, tq, HEAD_PAD), F32)]),
        compiler_params=_params("parallel", "arbitrary"),
        name="mla_flash_attention",
    )(q_of, k_of, q, k, v)


PAIR_ORDER = ((0, 1), (0, 2), (0, 3), (1, 3), (1, 2), (2, 3))
PAIRS_PER_GROUP = len(PAIR_ORDER)
N_CLASSES = N_GROUPS * PAIRS_PER_GROUP
LANES = 128
ROW_CHUNKS = D_MODEL // LANES
TOKEN_TILE = 512
EXPERT_TILE = 256


N_CLASS_ROWS = 32


def _route(scores, bias):
    biased = scores + bias
    col = [biased[e:e + 1, :] for e in range(N_EXPERTS)]
    gscore = []
    for g in range(N_GROUPS):
        a, b, c, d = col[g * EXPERTS_PER_GROUP:(g + 1) * EXPERTS_PER_GROUP]
        gscore.append(jnp.maximum(jnp.maximum(jnp.maximum(a + b, a + c), jnp.maximum(a + d, b + c)),
                                  jnp.maximum(b + d, c + d)))
    sel = []
    for g in range(N_GROUPS):
        ok = None
        for o in range(N_GROUPS):
            if o == g:
                continue
            cond = (gscore[g] > gscore[o]) if o < g else (gscore[g] >= gscore[o])
            ok = cond if ok is None else jnp.logical_and(ok, cond)
        grp = col[g * EXPERTS_PER_GROUP:(g + 1) * EXPERTS_PER_GROUP]
        for e in range(EXPERTS_PER_GROUP):
            beaten = jnp.zeros_like(grp[e])
            for o in range(EXPERTS_PER_GROUP):
                if o == e:
                    continue
                ahead = (grp[o] >= grp[e]) if o < e else (grp[o] > grp[e])
                beaten = beaten + jnp.where(ahead, 1.0, 0.0)
            sel.append(jnp.logical_and(ok, beaten < 1.5))
    lo = jnp.full_like(col[0], float(N_EXPERTS))
    hi = jnp.full_like(col[0], -1.0)
    for e in range(N_EXPERTS):
        lo = jnp.where(sel[e], jnp.minimum(lo, float(e)), lo)
        hi = jnp.where(sel[e], jnp.maximum(hi, float(e)), hi)
    return lo, hi


def _store_rows(ref, x):
    n = x.shape[0]
    for c in range(ROW_CHUNKS):
        ref[pl.ds(c, n, stride=ROW_CHUNKS), :] = x[:, c * LANES:(c + 1) * LANES]


def _load_rows(ref, n):
    return jnp.concatenate([ref[pl.ds(c, n, stride=ROW_CHUNKS), :] for c in range(ROW_CHUNKS)], axis=1)


def _merge_kernel(x_ref, oa_ref, ob_ref, g_ref, mod_ref, wa_ref, wb_ref, wo_ref, n2_ref,
                  wrt_ref, rb_ref, xn_ref, row_ref, meta_ref):
    d = D_MODEL
    tm = x_ref.shape[0]
    mod = mod_ref[0]
    ya = _dot(oa_ref[...], wa_ref[...])
    yb = _dot(ob_ref[...], wb_ref[...])
    g = g_ref[...].astype(F32)
    merged = g[:, 0:d] * ya + g[:, d:2 * d] * yb
    xn = x_ref[...] + mod[2:3] * _dot(merged.astype(BF16), wo_ref[...])
    xn_ref[...] = xn
    h2 = _rms(xn, n2_ref[...]) * (1.0 + mod[4:5]) + mod[3:4]
    _store_rows(row_ref, h2)

    h_hi, h_lo = _split_bf16(h2)
    w_hi, w_lo = _split_bf16(wrt_ref[...])
    scores = _sigmoid(_dot_nt(w_hi, h_hi) + _dot_nt(w_hi, h_lo) + _dot_nt(w_lo, h_hi))
    lo, hi = _route(scores, rb_ref[...])

    grp = jnp.floor(lo * (1.0 / EXPERTS_PER_GROUP))
    a = lo - grp * EXPERTS_PER_GROUP
    b = hi - grp * EXPERTS_PER_GROUP
    pair = jnp.zeros_like(a)
    for p, (p_lo, p_hi) in enumerate(PAIR_ORDER):
        pair = pair + jnp.where(jnp.logical_and(a == p_lo, b == p_hi), float(p), 0.0)
    cls = grp * PAIRS_PER_GROUP + pair
    c_row = lax.broadcasted_iota(jnp.int32, (N_CLASS_ROWS, tm), 0).astype(F32)
    onehot = jnp.where(c_row == cls, 1.0, 0.0)
    r_i = lax.broadcasted_iota(jnp.int32, (tm, tm), 0)
    c_i = lax.broadcasted_iota(jnp.int32, (tm, tm), 1)
    earlier = jnp.where(r_i < c_i, 1.0, 0.0).astype(BF16)
    rank = jnp.sum(onehot * _dot(onehot.astype(BF16), earlier), axis=0, keepdims=True)
    m_row = lax.broadcasted_iota(jnp.int32, (8, tm), 0)
    meta_ref[...] = jnp.where(m_row == 0, cls, jnp.where(m_row == 1, rank, 0.0))


def _merge(x2, oa, ob, gsig, mod_l, w_br_a, w_br_b, w_out, n2, w_router, router_bias, seq):
    t, d = x2.shape
    tm = TOKEN_TILE
    rows = lambda n: pl.BlockSpec((tm, n), lambda i: (i, 0))
    return pl.pallas_call(
        _merge_kernel,
        out_shape=(jax.ShapeDtypeStruct((t, d), F32), jax.ShapeDtypeStruct((t * ROW_CHUNKS, LANES), F32),
                   jax.ShapeDtypeStruct((8, t), F32)),
        grid=(t // tm,),
        in_specs=[rows(d), rows(HG_WIDTH), rows(MLA_WIDTH), rows(2 * d),
                  pl.BlockSpec((1, 6, d), lambda i: ((i * tm) // seq, 0, 0)),
                  _resident(w_br_a.shape), _resident(w_br_b.shape), _resident(w_out.shape),
                  _resident((1, d)), _resident((N_EXPERTS, d)), _resident((N_EXPERTS, 1))],
        out_specs=(rows(d), pl.BlockSpec((tm * ROW_CHUNKS, LANES), lambda i: (i, 0)),
                   pl.BlockSpec((8, tm), lambda i: (0, i))),
        compiler_params=_params("parallel"),
        name="merge_outproj_router",
    )(x2, oa, ob, gsig, mod_l, w_br_a.astype(BF16), w_br_b.astype(BF16), w_out.astype(BF16),
      n2[None, :], w_router.T, router_bias[:, None])


def _dispatch_plan(meta, n_tiles_e):
    t = meta.shape[1]
    n_tok_tiles = t // TOKEN_TILE
    cls = meta[0].astype(jnp.int32)
    rank = meta[1].astype(jnp.int32)
    own = cls.reshape(n_tok_tiles, TOKEN_TILE, 1) == jnp.arange(N_CLASSES, dtype=jnp.int32)
    counts = jnp.sum(own.astype(jnp.int32), axis=1)
    total = jnp.sum(counts, axis=0)
    total_pad = (total + EXPERT_TILE - 1) // EXPERT_TILE * EXPERT_TILE
    ends = jnp.cumsum(total_pad)
    base = (ends - total_pad)[None, :] + jnp.cumsum(counts, axis=0) - counts
    pos = jnp.sum(jnp.where(own, base[:, None, :], 0), axis=2).reshape(t) + rank

    tile_start = jnp.arange(n_tiles_e, dtype=jnp.int32) * EXPERT_TILE
    n_valid = ends[-1] // EXPERT_TILE
    tile_cls = jnp.sum((tile_start[:, None] >= ends[None, :]).astype(jnp.int32), axis=1)
    last_cls = jnp.take(tile_cls, n_valid - 1)
    valid = jnp.arange(n_tiles_e, dtype=jnp.int32) < n_valid
    tile_cls = jnp.where(valid, tile_cls, last_cls)
    pair_lo = jnp.asarray([p[0] for p in PAIR_ORDER], jnp.int32)
    pair_hi = jnp.asarray([p[1] for p in PAIR_ORDER], jnp.int32)
    grp = tile_cls // PAIRS_PER_GROUP
    e_lo = grp * EXPERTS_PER_GROUP + jnp.take(pair_lo, tile_cls % PAIRS_PER_GROUP)
    e_hi = grp * EXPERTS_PER_GROUP + jnp.take(pair_hi, tile_cls % PAIRS_PER_GROUP)
    pad_tile = (ends // EXPERT_TILE - 1).astype(jnp.int32)
    used = jnp.concatenate([(total > 0).astype(jnp.int32), n_valid[None].astype(jnp.int32)])
    return (pos * ROW_CHUNKS).astype(jnp.int32), pad_tile, used, e_lo, e_hi, valid.astype(jnp.int32)


DMA_UNROLL = 8


def _row_copies(n_rows, make_copy):
    def issue(ui, carry):
        for u in range(DMA_UNROLL):
            make_copy(ui * DMA_UNROLL + u).start(priority=u % 2)
        return carry
    lax.fori_loop(0, n_rows // DMA_UNROLL, issue, 0)


def _token_slab(ref, first_row):
    return ref.at[pl.ds(pl.multiple_of(first_row, ROW_CHUNKS), ROW_CHUNKS)]


def _dispatch_kernel(pos_ref, pad_tile_ref, used_ref, rows_ref, dst_hbm, zero_ref, sem, zero_sem):
    tm = rows_ref.shape[0] // ROW_CHUNKS
    tile_rows = EXPERT_TILE * ROW_CHUNKS

    @pl.when(pl.program_id(0) == 0)
    def _():
        zero_ref[...] = jnp.zeros_like(zero_ref)
        n_tiles = dst_hbm.shape[0] // tile_rows

        def clear(tile):
            first = pl.multiple_of(tile * tile_rows, tile_rows)
            return pltpu.make_async_copy(zero_ref, dst_hbm.at[pl.ds(first, tile_rows)], zero_sem)

        def start_tail(tile, carry):
            clear(tile).start()
            return carry

        def wait_tail(tile, carry):
            clear(tile).wait()
            return carry
        for c in range(N_CLASSES):
            @pl.when(used_ref[c] != 0)
            def _():
                clear(pad_tile_ref[c]).start()
        lax.fori_loop(used_ref[N_CLASSES], n_tiles, start_tail, 0)
        for c in range(N_CLASSES):
            @pl.when(used_ref[c] != 0)
            def _():
                clear(pad_tile_ref[c]).wait()
        lax.fori_loop(used_ref[N_CLASSES], n_tiles, wait_tail, 0)

    base = pl.program_id(0) * tm
    _row_copies(tm, lambda r: pltpu.make_async_copy(
        _token_slab(rows_ref, r * ROW_CHUNKS), _token_slab(dst_hbm, pos_ref[base + r]), sem))
    pltpu.make_async_copy(rows_ref, dst_hbm.at[pl.ds(0, tm * ROW_CHUNKS)], sem).wait()


def _dispatch_rows(pos, pad_tile, used, rows, n_dst):
    t = rows.shape[0] // ROW_CHUNKS
    tm = TOKEN_TILE
    return pl.pallas_call(
        _dispatch_kernel,
        out_shape=jax.ShapeDtypeStruct((n_dst * ROW_CHUNKS, LANES), rows.dtype),
        grid_spec=pltpu.PrefetchScalarGridSpec(
            num_scalar_prefetch=3, grid=(t // tm,),
            in_specs=[pl.BlockSpec((tm * ROW_CHUNKS, LANES), lambda i, p, pt, us: (i, 0))],
            out_specs=pl.BlockSpec(memory_space=pl.ANY),
            scratch_shapes=[pltpu.VMEM((EXPERT_TILE * ROW_CHUNKS, LANES), rows.dtype),
                            pltpu.SemaphoreType.DMA, pltpu.SemaphoreType.DMA]),
        compiler_params=pltpu.CompilerParams(dimension_semantics=("arbitrary",), disable_bounds_checks=True,
                                             has_side_effects=True),
        name="dispatch_rows",
    )(pos, pad_tile, used, rows)


def _moe_kernel(elo_ref, ehi_ref, valid_ref, x_ref, wrt_ref, wgl_ref, wul_ref, wdl_ref, wgh_ref, wuh_ref,
                wdh_ref, y_ref):
    j = pl.program_id(0)

    @pl.when(valid_ref[j] == 0)
    def _():
        y_ref[...] = jnp.zeros_like(y_ref)

    @pl.when(valid_ref[j] != 0)
    def _():
        h2 = _load_rows(x_ref, EXPERT_TILE)
        s_lo = _sigmoid(jnp.sum(h2 * wrt_ref[pl.ds(elo_ref[j], 1), :], axis=1, keepdims=True))
        s_hi = _sigmoid(jnp.sum(h2 * wrt_ref[pl.ds(ehi_ref[j], 1), :], axis=1, keepdims=True))
        total = s_lo + s_hi
        hb = h2.astype(BF16)
        he_lo = _silu(_dot(hb, wgl_ref[0, 0])) * _dot(hb, wul_ref[0, 0])
        y = (s_lo / total) * _dot(he_lo.astype(BF16), wdl_ref[0, 0])
        he_hi = _silu(_dot(hb, wgh_ref[0, 0])) * _dot(hb, wuh_ref[0, 0])
        _store_rows(y_ref, y + (s_hi / total) * _dot(he_hi.astype(BF16), wdh_ref[0, 0]))


def _moe(rows_sorted, e_lo, e_hi, valid, w_router, wg, wu, wd, layer):
    n_pad = rows_sorted.shape[0] // ROW_CHUNKS
    d = D_MODEL
    lo4 = lambda j, el, eh, va: (layer, el[j], 0, 0)
    hi4 = lambda j, el, eh, va: (layer, eh[j], 0, 0)
    tile = pl.BlockSpec((EXPERT_TILE * ROW_CHUNKS, LANES), lambda j, el, eh, va: (j, 0))
    return pl.pallas_call(
        _moe_kernel,
        out_shape=jax.ShapeDtypeStruct(rows_sorted.shape, F32),
        grid_spec=pltpu.PrefetchScalarGridSpec(
            num_scalar_prefetch=3, grid=(n_pad // EXPERT_TILE,),
            in_specs=[tile, _resident((N_EXPERTS, d)),
                      pl.BlockSpec((1, 1, d, D_EXPERT), lo4), pl.BlockSpec((1, 1, d, D_EXPERT), lo4),
                      pl.BlockSpec((1, 1, D_EXPERT, d), lo4),
                      pl.BlockSpec((1, 1, d, D_EXPERT), hi4), pl.BlockSpec((1, 1, d, D_EXPERT), hi4),
                      pl.BlockSpec((1, 1, D_EXPERT, d), hi4)],
            out_specs=tile),
        compiler_params=_params("arbitrary"),
        name="moe_experts",
    )(e_lo, e_hi, valid, rows_sorted, w_router.T, wg, wu, wd, wg, wu, wd)


def _collect_kernel(pos_ref, x_ref, ys_hbm, mod_ref, fg_ref, o_ref, buf0, buf1, sems, *, final):
    tm = x_ref.shape[0]
    i = pl.program_id(0)
    n = pl.num_programs(0)
    bufs = (buf0, buf1)

    def fetch(tile, slot):
        base = tile * tm
        _row_copies(tm, lambda r: pltpu.make_async_copy(
            _token_slab(ys_hbm, pos_ref[base + r]), _token_slab(bufs[slot], r * ROW_CHUNKS), sems.at[slot]))

    def finish(slot):
        pltpu.make_async_copy(ys_hbm.at[pl.ds(0, tm * ROW_CHUNKS)], bufs[slot], sems.at[slot]).wait()
        xo = x_ref[...] + mod_ref[0][5:6] * _load_rows(bufs[slot], tm)
        if final:
            xo = _rms(xo, fg_ref[...])
        o_ref[...] = xo

    @pl.when(i == 0)
    def _():
        fetch(0, 0)

    for slot in range(2):
        @pl.when(jnp.logical_and(i + 1 < n, (i + 1) % 2 == slot))
        def _():
            fetch(i + 1, slot)

    for slot in range(2):
        @pl.when(i % 2 == slot)
        def _():
            finish(slot)


def _collect_residual(pos, xn, y_sorted, mod_l, final_g, final, seq):
    t, d = xn.shape
    tm = TOKEN_TILE
    rows = pl.BlockSpec((tm, d), lambda i, p: (i, 0))
    return pl.pallas_call(
        functools.partial(_collect_kernel, final=final),
        out_shape=jax.ShapeDtypeStruct((t, d), F32),
        grid_spec=pltpu.PrefetchScalarGridSpec(
            num_scalar_prefetch=1, grid=(t // tm,),
            in_specs=[rows, pl.BlockSpec(memory_space=pl.ANY),
                      pl.BlockSpec((1, 6, d), lambda i, p: ((i * tm) // seq, 0, 0)),
                      pl.BlockSpec((1, d), lambda i, p: (0, 0))],
            out_specs=rows,
            scratch_shapes=[pltpu.VMEM((tm * ROW_CHUNKS, LANES), F32), pltpu.VMEM((tm * ROW_CHUNKS, LANES), F32),
                            pltpu.SemaphoreType.DMA((2,))]),
        compiler_params=pltpu.CompilerParams(dimension_semantics=("arbitrary",), disable_bounds_checks=True,
                                             vmem_limit_bytes=V7X_VMEM_LIMIT_BYTES),
        name="collect_residual",
    )(pos, xn, y_sorted, mod_l, final_g[None, :])


def kernel(x, c, positions, ada_w, ada_b, norm1_g, w_in, hg_lb_logits, hg_norm_g, q_norm_g, w_q_up,
           kv_norm_g, w_kv_up, w_br_a, w_br_b, w_out, norm2_g, w_router, router_bias, w_gate, w_up,
           w_down, final_g):
    batch, seq, d = x.shape
    depth = ada_w.shape[0]
    t = batch * seq
    n_tiles_e = (t + N_CLASSES * (EXPERT_TILE - 1)) // EXPERT_TILE
    mod = _modulation(c, ada_w, ada_b).reshape(depth, batch, 6, d)
    cos_t, sin_t = _rope_tables(positions)
    x2 = x.reshape(t, d)
    for l in range(depth):
        hq, hf, hi, hgate, gsig, q, k, v = _input_projection(
            x2, mod[l], norm1_g[l], cos_t, sin_t, w_in[l], q_norm_g[l], w_q_up[l],
            kv_norm_g[l], w_kv_up[l], seq)
        oa = _hgrn(hg_lb_logits, hg_norm_g[l], hq, hf, hi, hgate, l, batch, seq)
        ob = _attention(q, k, v, batch, seq)
        xn, rows, meta = _merge(x2, oa, ob, gsig, mod[l], w_br_a[l], w_br_b[l], w_out[l], norm2_g[l],
                                w_router, router_bias, seq)
        pos, pad_tile, used, e_lo, e_hi, valid = _dispatch_plan(meta, n_tiles_e)
        rows_sorted = _dispatch_rows(pos, pad_tile, used, rows, n_tiles_e * EXPERT_TILE)
        y_sorted = _moe(rows_sorted, e_lo, e_hi, valid, w_router, w_gate, w_up, w_down, l)
        x2 = _collect_residual(pos, xn, y_sorted, mod[l], final_g, l == depth - 1, seq)
    return x2.reshape(batch, seq, d)
```

```python
import functools
import math

import jax
import jax.numpy as jnp
from jax import lax
from jax.experimental import pallas as pl
from jax.experimental.pallas import tpu as pltpu

F32 = jnp.float32
BF16 = jnp.bfloat16

D_MODEL = 1024
CHUNK = 64
EPS = 1e-6

HG_HEADS = 4
HG_DK = 128
HG_DV = 128
HG_WIDTH = HG_HEADS * HG_DV
HG_SUB = 8
HG_LEVELS = (8, 16, 32)

MLA_HEADS = 8
MLA_NOPE = 64
MLA_ROPE = 32
MLA_V = 64
MLA_Q_LORA = 384
MLA_KV_LORA = 256
MLA_DQK = MLA_NOPE + MLA_ROPE
MLA_WIDTH = MLA_HEADS * MLA_V
ROPE_BASE = 10000.0
HEAD_PAD = 128
LOG2_E = math.log2(math.e)

N_EXPERTS = 16
N_GROUPS = 4
EXPERTS_PER_GROUP = N_EXPERTS // N_GROUPS
D_EXPERT = 512

IN_SIZES = (HG_HEADS * HG_DK, HG_HEADS * HG_DK, HG_HEADS * HG_DV, HG_WIDTH,
            MLA_Q_LORA, MLA_KV_LORA, MLA_ROPE, 2 * D_MODEL)

V7X_VMEM_LIMIT_BYTES = 56 * 1024 * 1024


def _params(*sem):
    return pltpu.CompilerParams(dimension_semantics=sem, vmem_limit_bytes=V7X_VMEM_LIMIT_BYTES)


def _resident(shape):
    nd = len(shape)
    return pl.BlockSpec(shape, lambda *_: (0,) * nd, pipeline_mode=pl.Buffered(1))


def _sigmoid(x):
    return 1.0 / (1.0 + jnp.exp(-x))


def _silu(x):
    return x * _sigmoid(x)


def _dot(a, b):
    return jnp.dot(a, b, preferred_element_type=F32)


def _dot_nt(a, b):
    return lax.dot_general(a, b, (((1,), (1,)), ((), ())), preferred_element_type=F32)


def _dot_tn(a, b):
    return lax.dot_general(a, b, (((0,), (0,)), ((), ())), preferred_element_type=F32)


def _split_bf16(x):
    hi = x.astype(BF16)
    lo = (x - hi.astype(F32)).astype(BF16)
    return hi, lo


def _rms(x, g):
    return x * lax.rsqrt(jnp.mean(x * x, axis=-1, keepdims=True) + EPS) * g


def _mod_kernel(c_ref, w_ref, b_ref, o_ref):
    ca = _silu(c_ref[...])
    o_ref[0] = _dot(ca.astype(BF16), w_ref[0].astype(BF16)) + b_ref[0]


def _modulation(c, ada_w, ada_b):
    depth, d, n = ada_w.shape
    b = c.shape[0]
    tn = 1536
    return pl.pallas_call(
        _mod_kernel,
        out_shape=jax.ShapeDtypeStruct((depth, b, n), F32),
        grid=(depth, n // tn),
        in_specs=[pl.BlockSpec((b, d), lambda l, j: (0, 0)),
                  pl.BlockSpec((1, d, tn), lambda l, j: (l, 0, j)),
                  pl.BlockSpec((1, 1, tn), lambda l, j: (l, 0, j))],
        out_specs=pl.BlockSpec((1, b, tn), lambda l, j: (l, 0, j)),
        compiler_params=_params("parallel", "parallel"),
        name="adaln_modulation",
    )(c, ada_w, ada_b.reshape(depth, 1, n))


def _rope_kernel(pos_ref, inv_ref, msk_ref, sgn_ref, cos_ref, sin_ref):
    ang = pos_ref[...].astype(F32) * inv_ref[...]
    cos_ref[...] = jnp.cos(ang) * msk_ref[...]
    sin_ref[...] = jnp.sin(ang) * sgn_ref[...]


def _rope_tables(positions):
    t = positions.size
    half = MLA_ROPE // 2
    inv = ROPE_BASE ** (-jnp.arange(half, dtype=F32) / half)
    z64, z32, one16 = jnp.zeros((MLA_NOPE,), F32), jnp.zeros((32,), F32), jnp.ones((half,), F32)
    inv_row = jnp.concatenate([z64, inv, inv, z32])[None, :]
    msk_row = jnp.concatenate([z64, one16, one16, z32])[None, :]
    sgn_row = jnp.concatenate([z64, -one16, one16, z32])[None, :]
    tr = min(2048, t)
    row = pl.BlockSpec((1, HEAD_PAD), lambda i: (0, 0))
    tab = pl.BlockSpec((tr, HEAD_PAD), lambda i: (i, 0))
    return pl.pallas_call(
        _rope_kernel,
        out_shape=(jax.ShapeDtypeStruct((t, HEAD_PAD), F32),) * 2,
        grid=(t // tr,),
        in_specs=[pl.BlockSpec((tr, 1), lambda i: (i, 0)), row, row, row],
        out_specs=(tab, tab),
        compiler_params=_params("parallel"),
        name="rope_tables",
    )(positions.reshape(t, 1), inv_row, msk_row, sgn_row)


def _proj_kernel(x_ref, mod_ref, n1_ref, cos_ref, sin_ref, wh_ref, wg_ref, wc_ref,
                 qn_ref, wqa_ref, wqs_ref, kn_ref, wk_ref, wv_ref, one_ref,
                 hq_ref, hf_ref, hi_ref, hgate_ref, gsig_ref, q_ref, k_ref, v_ref):
    x = x_ref[...]
    mod = mod_ref[0]
    h = _rms(x, n1_ref[...]) * (1.0 + mod[1:2]) + mod[0:1]
    hb = h.astype(BF16)

    ph = _dot(hb, wh_ref[...])
    w = HG_WIDTH
    hq_ref[...] = ph[:, 0:w].astype(BF16)
    hf_ref[...] = ph[:, w:2 * w]
    hi_ref[...] = ph[:, 2 * w:3 * w].astype(BF16)
    hgate_ref[...] = ph[:, 3 * w:4 * w].astype(BF16)

    gsig_ref[...] = _sigmoid(_dot(hb, wg_ref[...])).astype(BF16)

    pc = _dot(hb, wc_ref[...])
    cq = pc[:, 0:MLA_Q_LORA]
    ckv = pc[:, MLA_Q_LORA:MLA_Q_LORA + MLA_KV_LORA]
    kra = pc[:, MLA_Q_LORA + MLA_KV_LORA:MLA_Q_LORA + MLA_KV_LORA + HEAD_PAD]
    krb = pc[:, MLA_Q_LORA + MLA_KV_LORA + HEAD_PAD:]

    cos_t = cos_ref[...]
    sin_t = sin_ref[...]
    lane = lax.broadcasted_iota(jnp.int32, cos_t.shape, 1)
    scale = MLA_DQK ** -0.5 * LOG2_E
    cq_tab = jnp.tile(scale * (cos_t + jnp.where(lane < MLA_NOPE, 1.0, 0.0)), (1, MLA_HEADS))
    sq_tab = jnp.tile(scale * sin_t, (1, MLA_HEADS))

    cqn = _rms(cq, qn_ref[...]).astype(BF16)
    q = _dot(cqn, wqa_ref[...]) * cq_tab + _dot(cqn, wqs_ref[...]) * sq_tab
    q_ref[...] = q.astype(BF16)

    ckvn = _rms(ckv, kn_ref[...]).astype(BF16)
    kpe = kra * cos_t + krb * sin_t
    k_ref[...] = (_dot(ckvn, wk_ref[...]) + jnp.tile(kpe, (1, MLA_HEADS))).astype(BF16)
    v_ref[...] = (_dot(ckvn, wv_ref[...]) + one_ref[...]).astype(BF16)


def _pad_heads(w, lo, hi, at):
    k, nh, _ = w.shape
    out = jnp.zeros((k, nh, HEAD_PAD), w.dtype)
    out = out.at[:, :, at:at + (hi - lo)].set(w[:, :, lo:hi])
    return out


def _input_projection(x2, mod_l, n1, cos_t, sin_t, w_in, q_norm_g, w_q_up, kv_norm_g, w_kv_up, seq):
    t, d = x2.shape
    tm = 256
    splits = [0]
    for s in IN_SIZES:
        splits.append(splits[-1] + s)
    kr = w_in[:, splits[6]:splits[7]]
    half = MLA_ROPE // 2
    z64 = jnp.zeros((d, MLA_NOPE), F32)
    z32 = jnp.zeros((d, HEAD_PAD - MLA_NOPE - MLA_ROPE), F32)
    kr_a = jnp.concatenate([z64, kr, z32], axis=1)
    kr_b = jnp.concatenate([z64, kr[:, half:], kr[:, :half], z32], axis=1)
    w_h = w_in[:, splits[0]:splits[4]].astype(BF16)
    w_g = w_in[:, splits[7]:splits[8]].astype(BF16)
    w_c = jnp.concatenate([w_in[:, splits[4]:splits[6]], kr_a, kr_b], axis=1).astype(BF16)

    wq = w_q_up.reshape(MLA_Q_LORA, MLA_HEADS, MLA_DQK)
    wq_all = _pad_heads(wq, 0, MLA_DQK, 0).reshape(MLA_Q_LORA, -1).astype(BF16)
    wq_swap = (_pad_heads(wq, MLA_NOPE + half, MLA_DQK, MLA_NOPE)
               + _pad_heads(wq, MLA_NOPE, MLA_NOPE + half, MLA_NOPE + half))
    wq_swap = wq_swap.reshape(MLA_Q_LORA, -1).astype(BF16)
    wkv = w_kv_up.reshape(MLA_KV_LORA, MLA_HEADS, MLA_NOPE + MLA_V)
    wk_all = _pad_heads(wkv, 0, MLA_NOPE, 0).reshape(MLA_KV_LORA, -1).astype(BF16)
    wv_all = _pad_heads(wkv, MLA_NOPE, MLA_NOPE + MLA_V, 0).reshape(MLA_KV_LORA, -1).astype(BF16)
    ones_row = jnp.tile(jnp.zeros((HEAD_PAD,), F32).at[MLA_V].set(1.0), MLA_HEADS)[None, :]

    hp = MLA_HEADS * HEAD_PAD
    rows = lambda n: pl.BlockSpec((tm, n), lambda i: (i, 0))
    outs = [(HG_WIDTH, BF16), (HG_WIDTH, F32), (HG_WIDTH, BF16), (HG_WIDTH, BF16),
            (2 * d, BF16), (hp, BF16), (hp, BF16), (hp, BF16)]
    return pl.pallas_call(
        _proj_kernel,
        out_shape=tuple(jax.ShapeDtypeStruct((t, n), dt) for n, dt in outs),
        grid=(t // tm,),
        in_specs=[rows(d),
                  pl.BlockSpec((1, 6, d), lambda i: ((i * tm) // seq, 0, 0)),
                  _resident((1, d)),
                  rows(HEAD_PAD), rows(HEAD_PAD),
                  _resident(w_h.shape), _resident(w_g.shape), _resident(w_c.shape),
                  _resident((1, MLA_Q_LORA)), _resident(wq_all.shape), _resident(wq_swap.shape),
                  _resident((1, MLA_KV_LORA)), _resident(wk_all.shape), _resident(wv_all.shape),
                  _resident((1, hp))],
        out_specs=tuple(rows(n) for n, _ in outs),
        compiler_params=_params("parallel"),
        name="norm_input_projection",
    )(x2, mod_l, n1[None, :], cos_t, sin_t, w_h, w_g, w_c,
      q_norm_g[None, :], wq_all, wq_swap, kv_norm_g[None, :], wk_all, wv_all, ones_row)


def _hgrn_kernel(lbl_ref, gn_ref, q_ref, f_ref, v_ref, gate_ref, o_ref, st_ref, *, layer, n_chunks):
    @pl.when(pl.program_id(1) == 0)
    def _():
        st_ref[...] = jnp.zeros_like(st_ref)

    lg = lbl_ref[...]
    ex = jnp.exp(lg - jnp.max(lg, axis=0, keepdims=True))
    soft = ex / jnp.sum(ex, axis=0, keepdims=True)
    lb_all = jnp.zeros_like(soft[0:1])
    for i in range(1, layer + 1):
        lb_all = lb_all + soft[i:i + 1]

    r_i = lax.broadcasted_iota(jnp.int32, (CHUNK, CHUNK), 0)
    c_i = lax.broadcasted_iota(jnp.int32, (CHUNK, CHUNK), 1)
    tri = jnp.where(r_i >= c_i, 1.0, 0.0).astype(BF16)
    level_masks = []
    for hs in HG_LEVELS:
        same = (r_i // (2 * hs)) == (c_i // (2 * hs))
        level_masks.append(jnp.logical_and(same, jnp.logical_and(r_i % (2 * hs) >= hs, c_i % (2 * hs) < hs)))
    sub_row = lax.broadcasted_iota(jnp.int32, (HG_SUB, 1), 0)
    n_sub = CHUNK // HG_SUB

    def chunk_body(ci, carry):
        r0 = pl.multiple_of(ci * CHUNK, CHUNK)
        rows = pl.ds(r0, CHUNK)
        for h in range(HG_HEADS):
            sl = slice(h * HG_DK, (h + 1) * HG_DK)
            lb = lb_all[:, sl]
            q = q_ref[rows, sl].astype(F32)
            vb = v_ref[rows, sl]
            v = vb.astype(F32)
            f = lb + (1.0 - lb) * _sigmoid(f_ref[rows, sl])
            g = jnp.log2(f)
            k = 1.0 - f
            qf = _silu(q) * (HG_DK ** -0.5)
            g_hi, g_lo = _split_bf16(g)
            b = _dot(tri, g_hi) + _dot(tri, g_lo)

            att = jnp.zeros((CHUNK, CHUNK), F32)
            for hs, mask in zip(HG_LEVELS, level_masks):
                ref = jnp.concatenate(
                    [jnp.broadcast_to(b[j + hs - 1:j + hs], (2 * hs, HG_DK)) for j in range(0, CHUNK, 2 * hs)],
                    axis=0)
                qt = (qf * jnp.exp2(b - ref)).astype(BF16)
                kt = (k * jnp.exp2(ref - b)).astype(BF16)
                att = att + jnp.where(mask, _dot_nt(qt, kt), 0.0)
            o = _dot(att.astype(BF16), vb)

            blocks = []
            for i in range(n_sub):
                rs = slice(i * HG_SUB, (i + 1) * HG_SUB)
                b_i, q_i, k_i, v_i = b[rs], qf[rs], k[rs], v[rs]
                acc = jnp.zeros((HG_SUB, HG_DV), F32)
                for s in range(HG_SUB):
                    a = jnp.sum(jnp.exp2(b_i - b_i[s:s + 1]) * (q_i * k_i[s:s + 1]), axis=1, keepdims=True)
                    acc = acc + jnp.where(sub_row >= s, a, 0.0) * v_i[s:s + 1]
                blocks.append(acc)
            o = o + jnp.concatenate(blocks, axis=0)

            st = st_ref[h]
            o = o + _dot_nt((qf * jnp.exp2(b)).astype(BF16), st.astype(BF16))
            b_last = b[CHUNK - 1:CHUNK]
            kd = (k * jnp.exp2(b_last - b)).astype(BF16)
            st_ref[h] = st * jnp.exp2(b_last) + _dot_tn(vb, kd)

            gt = gate_ref[rows, sl].astype(F32)
            o_ref[rows, sl] = (_rms(o, gn_ref[:, sl]) * _silu(gt)).astype(BF16)
        return carry

    lax.fori_loop(0, n_chunks, chunk_body, 0, unroll=2)


def _hgrn(hg_lb_logits, hg_norm_g, hq, hf, hi, hgate, layer, batch, seq):
    t = hq.shape[0]
    lc = min(512, seq)
    nb = seq // lc
    rows = pl.BlockSpec((lc, HG_WIDTH), lambda b, j: (b * nb + j, 0))
    kern = functools.partial(_hgrn_kernel, layer=layer, n_chunks=lc // CHUNK)
    return pl.pallas_call(
        kern,
        out_shape=jax.ShapeDtypeStruct((t, HG_WIDTH), BF16),
        grid=(batch, nb),
        in_specs=[pl.BlockSpec(hg_lb_logits.shape, lambda b, j: (0, 0)),
                  pl.BlockSpec((1, HG_WIDTH), lambda b, j: (0, 0)),
                  rows, rows, rows, rows],
        out_specs=rows,
        scratch_shapes=[pltpu.VMEM((HG_HEADS, HG_DV, HG_DK), F32)],
        compiler_params=_params("parallel", "arbitrary"),
        name="hgrn2_chunkwise",
    )(hg_lb_logits, hg_norm_g[None, :], hq, hf, hi, hgate)


ATTN_KEY_BLOCK = 512
STEP_FULL, STEP_TOP_DIAG, STEP_BOTTOM_FULL, STEP_BOTTOM_DIAG = range(4)


def _attn_kernel(qi_ref, kj_ref, kind_ref, first_ref, q_ref, k_ref, v_ref, o_ref, m_ref, acc_ref, *, tk):
    del qi_ref, kj_ref
    p_id = pl.program_id(1)
    kind = kind_ref[p_id]

    @pl.when(first_ref[p_id] != 0)
    def _():
        m_ref[...] = jnp.full_like(m_ref, -jnp.inf)
        acc_ref[...] = jnp.zeros_like(acc_ref)

    def step(r0, nr, diagonal):
        rows = slice(r0, r0 + nr)
        if diagonal:
            qc = lax.broadcasted_iota(jnp.int32, (nr, tk), 0) // CHUNK
            kc = lax.broadcasted_iota(jnp.int32, (nr, tk), 1) // CHUNK
            visible = kc <= qc
        for h in range(MLA_HEADS):
            sl = slice(h * HEAD_PAD, (h + 1) * HEAD_PAD)
            s = _dot_nt(q_ref[rows, sl], k_ref[:, sl])
            if diagonal:
                s = jnp.where(visible, s, -jnp.inf)
            tiles = [s[:, t * HEAD_PAD:(t + 1) * HEAD_PAD] for t in range(tk // HEAD_PAD)]
            m_tile = tiles[0]
            for tl in tiles[1:]:
                m_tile = jnp.maximum(m_tile, tl)
            m_prev = m_ref[h, rows]
            m_new = jnp.maximum(m_prev, jnp.max(m_tile, axis=1, keepdims=True))
            alpha = jnp.exp2(m_prev - m_new)
            p = jnp.concatenate([jnp.exp2((tl - m_new).astype(BF16)) for tl in tiles], axis=1)
            acc_ref[h, rows] = acc_ref[h, rows] * alpha + _dot(p, v_ref[:, sl])
            m_ref[h, rows] = m_new

    @pl.when(kind == STEP_FULL)
    def _():
        step(0, 2 * tk, False)

    @pl.when(kind == STEP_TOP_DIAG)
    def _():
        step(0, tk, True)

    @pl.when(kind == STEP_BOTTOM_FULL)
    def _():
        step(tk, tk, False)

    @pl.when(kind == STEP_BOTTOM_DIAG)
    def _():
        step(tk, tk, True)
        outs = []
        for h in range(MLA_HEADS):
            acc = acc_ref[h]
            outs.append(acc[:, 0:MLA_V] / acc[:, MLA_V:MLA_V + 1])
        o_ref[...] = jnp.concatenate(outs, axis=1).astype(BF16)


def _attention(q, k, v, batch, seq):
    t, hp = q.shape
    tk = ATTN_KEY_BLOCK
    tq = 2 * tk
    nq, nk = seq // tq, seq // tk
    steps = []
    for i in range(nq):
        steps += [(i, j, STEP_FULL) for j in range(2 * i)]
        steps += [(i, 2 * i, STEP_TOP_DIAG), (i, 2 * i, STEP_BOTTOM_FULL), (i, 2 * i + 1, STEP_BOTTOM_DIAG)]
    q_of, k_of, kinds = (jnp.asarray([st[c] for st in steps], jnp.int32) for c in range(3))
    first = jnp.asarray([int(n == 0 or steps[n - 1][0] != st[0]) for n, st in enumerate(steps)], jnp.int32)
    qspec = pl.BlockSpec((tq, hp), lambda b, p, qi, kj, kd, fs: (b * nq + qi[p], 0))
    kspec = pl.BlockSpec((tk, hp), lambda b, p, qi, kj, kd, fs: (b * nk + kj[p], 0))
    return pl.pallas_call(
        functools.partial(_attn_kernel, tk=tk),
        out_shape=jax.ShapeDtypeStruct((t, MLA_WIDTH), BF16),
        grid_spec=pltpu.PrefetchScalarGridSpec(
            num_scalar_prefetch=4,
            grid=(batch, len(steps)),
            in_specs=[qspec, kspec, kspec],
            out_specs=pl.BlockSpec((tq, MLA_WIDTH), lambda b, p, qi, kj, kd, fs: (b * nq + qi[p], 0)),
            scratch_shapes=[pltpu.VMEM((MLA_HEADS, tq, HEAD_PAD), F32),
                            pltpu.VMEM((MLA_HEADS, tq, HEAD_PAD), F32)]),
        compiler_params=_params("parallel", "arbitrary"),
        name="mla_flash_attention",
    )(q_of, k_of, kinds, first, q, k, v)


PAIR_ORDER = ((0, 1), (0, 2), (0, 3), (1, 3), (1, 2), (2, 3))
PAIRS_PER_GROUP = len(PAIR_ORDER)
N_CLASSES = N_GROUPS * PAIRS_PER_GROUP
LANES = 128
ROW_CHUNKS = D_MODEL // LANES
TOKEN_TILE = 512
EXPERT_TILE = 256


N_CLASS_ROWS = 32


def _route(scores, bias):
    biased = scores + bias
    col = [biased[e:e + 1, :] for e in range(N_EXPERTS)]
    gscore = []
    for g in range(N_GROUPS):
        a, b, c, d = col[g * EXPERTS_PER_GROUP:(g + 1) * EXPERTS_PER_GROUP]
        gscore.append(jnp.maximum(jnp.maximum(jnp.maximum(a + b, a + c), jnp.maximum(a + d, b + c)),
                                  jnp.maximum(b + d, c + d)))
    sel = []
    for g in range(N_GROUPS):
        ok = None
        for o in range(N_GROUPS):
            if o == g:
                continue
            cond = (gscore[g] > gscore[o]) if o < g else (gscore[g] >= gscore[o])
            ok = cond if ok is None else jnp.logical_and(ok, cond)
        grp = col[g * EXPERTS_PER_GROUP:(g + 1) * EXPERTS_PER_GROUP]
        for e in range(EXPERTS_PER_GROUP):
            beaten = jnp.zeros_like(grp[e])
            for o in range(EXPERTS_PER_GROUP):
                if o == e:
                    continue
                ahead = (grp[o] >= grp[e]) if o < e else (grp[o] > grp[e])
                beaten = beaten + jnp.where(ahead, 1.0, 0.0)
            sel.append(jnp.logical_and(ok, beaten < 1.5))
    lo = jnp.full_like(col[0], float(N_EXPERTS))
    hi = jnp.full_like(col[0], -1.0)
    for e in range(N_EXPERTS):
        lo = jnp.where(sel[e], jnp.minimum(lo, float(e)), lo)
        hi = jnp.where(sel[e], jnp.maximum(hi, float(e)), hi)
    return lo, hi


def _store_rows(ref, x):
    n = x.shape[0]
    for c in range(ROW_CHUNKS):
        ref[pl.ds(c, n, stride=ROW_CHUNKS), :] = x[:, c * LANES:(c + 1) * LANES]


def _load_rows(ref, n):
    return jnp.concatenate([ref[pl.ds(c, n, stride=ROW_CHUNKS), :] for c in range(ROW_CHUNKS)], axis=1)


def _merge_kernel(x_ref, oa_ref, ob_ref, g_ref, mod_ref, wa_ref, wb_ref, wo_ref, n2_ref,
                  wrt_ref, rb_ref, xn_ref, row_ref, meta_ref):
    d = D_MODEL
    tm = x_ref.shape[0]
    mod = mod_ref[0]
    ya = _dot(oa_ref[...], wa_ref[...])
    yb = _dot(ob_ref[...], wb_ref[...])
    g = g_ref[...].astype(F32)
    merged = g[:, 0:d] * ya + g[:, d:2 * d] * yb
    xn = x_ref[...] + mod[2:3] * _dot(merged.astype(BF16), wo_ref[...])
    xn_ref[...] = xn
    h2 = _rms(xn, n2_ref[...]) * (1.0 + mod[4:5]) + mod[3:4]
    _store_rows(row_ref, h2)

    h_hi, h_lo = _split_bf16(h2)
    w_hi, w_lo = _split_bf16(wrt_ref[...])
    scores = _sigmoid(_dot_nt(w_hi, h_hi) + _dot_nt(w_hi, h_lo) + _dot_nt(w_lo, h_hi))
    lo, hi = _route(scores, rb_ref[...])

    grp = jnp.floor(lo * (1.0 / EXPERTS_PER_GROUP))
    a = lo - grp * EXPERTS_PER_GROUP
    b = hi - grp * EXPERTS_PER_GROUP
    pair = jnp.zeros_like(a)
    for p, (p_lo, p_hi) in enumerate(PAIR_ORDER):
        pair = pair + jnp.where(jnp.logical_and(a == p_lo, b == p_hi), float(p), 0.0)
    cls = grp * PAIRS_PER_GROUP + pair
    c_row = lax.broadcasted_iota(jnp.int32, (N_CLASS_ROWS, tm), 0).astype(F32)
    onehot = jnp.where(c_row == cls, 1.0, 0.0)
    r_i = lax.broadcasted_iota(jnp.int32, (tm, tm), 0)
    c_i = lax.broadcasted_iota(jnp.int32, (tm, tm), 1)
    earlier = jnp.where(r_i < c_i, 1.0, 0.0).astype(BF16)
    rank = jnp.sum(onehot * _dot(onehot.astype(BF16), earlier), axis=0, keepdims=True)
    m_row = lax.broadcasted_iota(jnp.int32, (8, tm), 0)
    meta_ref[...] = jnp.where(m_row == 0, cls, jnp.where(m_row == 1, rank, 0.0))


def _merge(x2, oa, ob, gsig, mod_l, w_br_a, w_br_b, w_out, n2, w_router, router_bias, seq):
    t, d = x2.shape
    tm = TOKEN_TILE
    rows = lambda n: pl.BlockSpec((tm, n), lambda i: (i, 0))
    return pl.pallas_call(
        _merge_kernel,
        out_shape=(jax.ShapeDtypeStruct((t, d), F32), jax.ShapeDtypeStruct((t * ROW_CHUNKS, LANES), F32),
                   jax.ShapeDtypeStruct((8, t), F32)),
        grid=(t // tm,),
        in_specs=[rows(d), rows(HG_WIDTH), rows(MLA_WIDTH), rows(2 * d),
                  pl.BlockSpec((1, 6, d), lambda i: ((i * tm) // seq, 0, 0)),
                  _resident(w_br_a.shape), _resident(w_br_b.shape), _resident(w_out.shape),
                  _resident((1, d)), _resident((N_EXPERTS, d)), _resident((N_EXPERTS, 1))],
        out_specs=(rows(d), pl.BlockSpec((tm * ROW_CHUNKS, LANES), lambda i: (i, 0)),
                   pl.BlockSpec((8, tm), lambda i: (0, i))),
        compiler_params=_params("parallel"),
        name="merge_outproj_router",
    )(x2, oa, ob, gsig, mod_l, w_br_a.astype(BF16), w_br_b.astype(BF16), w_out.astype(BF16),
      n2[None, :], w_router.T, router_bias[:, None])


def _dispatch_plan(meta, n_tiles_e):
    t = meta.shape[1]
    n_tok_tiles = t // TOKEN_TILE
    cls = meta[0].astype(jnp.int32)
    rank = meta[1].astype(jnp.int32)
    own = cls.reshape(n_tok_tiles, TOKEN_TILE, 1) == jnp.arange(N_CLASSES, dtype=jnp.int32)
    counts = jnp.sum(own.astype(jnp.int32), axis=1)
    total = jnp.sum(counts, axis=0)
    total_pad = (total + EXPERT_TILE - 1) // EXPERT_TILE * EXPERT_TILE
    ends = jnp.cumsum(total_pad)
    base = (ends - total_pad)[None, :] + jnp.cumsum(counts, axis=0) - counts
    pos = jnp.sum(jnp.where(own, base[:, None, :], 0), axis=2).reshape(t) + rank

    tile_start = jnp.arange(n_tiles_e, dtype=jnp.int32) * EXPERT_TILE
    n_valid = ends[-1] // EXPERT_TILE
    tile_cls = jnp.sum((tile_start[:, None] >= ends[None, :]).astype(jnp.int32), axis=1)
    last_cls = jnp.take(tile_cls, n_valid - 1)
    valid = jnp.arange(n_tiles_e, dtype=jnp.int32) < n_valid
    tile_cls = jnp.where(valid, tile_cls, last_cls)
    pair_lo = jnp.asarray([p[0] for p in PAIR_ORDER], jnp.int32)
    pair_hi = jnp.asarray([p[1] for p in PAIR_ORDER], jnp.int32)
    grp = tile_cls // PAIRS_PER_GROUP
    e_lo = grp * EXPERTS_PER_GROUP + jnp.take(pair_lo, tile_cls % PAIRS_PER_GROUP)
    e_hi = grp * EXPERTS_PER_GROUP + jnp.take(pair_hi, tile_cls % PAIRS_PER_GROUP)
    pad_tile = (ends // EXPERT_TILE - 1).astype(jnp.int32)
    used = jnp.concatenate([(total > 0).astype(jnp.int32), n_valid[None].astype(jnp.int32)])
    return (pos * ROW_CHUNKS).astype(jnp.int32), pad_tile, used, e_lo, e_hi, valid.astype(jnp.int32)


DMA_UNROLL = 8


def _row_copies(n_rows, make_copy):
    def issue(ui, carry):
        for u in range(DMA_UNROLL):
            make_copy(ui * DMA_UNROLL + u).start(priority=u % 2)
        return carry
    lax.fori_loop(0, n_rows // DMA_UNROLL, issue, 0)


def _token_slab(ref, first_row):
    return ref.at[pl.ds(pl.multiple_of(first_row, ROW_CHUNKS), ROW_CHUNKS)]


def _dispatch_kernel(pos_ref, pad_tile_ref, used_ref, rows_ref, dst_hbm, zero_ref, sem, zero_sem):
    tm = rows_ref.shape[0] // ROW_CHUNKS
    tile_rows = EXPERT_TILE * ROW_CHUNKS

    @pl.when(pl.program_id(0) == 0)
    def _():
        zero_ref[...] = jnp.zeros_like(zero_ref)
        n_tiles = dst_hbm.shape[0] // tile_rows

        def clear(tile):
            first = pl.multiple_of(tile * tile_rows, tile_rows)
            return pltpu.make_async_copy(zero_ref, dst_hbm.at[pl.ds(first, tile_rows)], zero_sem)

        def start_tail(tile, carry):
            clear(tile).start()
            return carry

        def wait_tail(tile, carry):
            clear(tile).wait()
            return carry
        for c in range(N_CLASSES):
            @pl.when(used_ref[c] != 0)
            def _():
                clear(pad_tile_ref[c]).start()
        lax.fori_loop(used_ref[N_CLASSES], n_tiles, start_tail, 0)
        for c in range(N_CLASSES):
            @pl.when(used_ref[c] != 0)
            def _():
                clear(pad_tile_ref[c]).wait()
        lax.fori_loop(used_ref[N_CLASSES], n_tiles, wait_tail, 0)

    base = pl.program_id(0) * tm
    _row_copies(tm, lambda r: pltpu.make_async_copy(
        _token_slab(rows_ref, r * ROW_CHUNKS), _token_slab(dst_hbm, pos_ref[base + r]), sem))
    pltpu.make_async_copy(rows_ref, dst_hbm.at[pl.ds(0, tm * ROW_CHUNKS)], sem).wait()


def _dispatch_rows(pos, pad_tile, used, rows, n_dst):
    t = rows.shape[0] // ROW_CHUNKS
    tm = TOKEN_TILE
    return pl.pallas_call(
        _dispatch_kernel,
        out_shape=jax.ShapeDtypeStruct((n_dst * ROW_CHUNKS, LANES), rows.dtype),
        grid_spec=pltpu.PrefetchScalarGridSpec(
            num_scalar_prefetch=3, grid=(t // tm,),
            in_specs=[pl.BlockSpec((tm * ROW_CHUNKS, LANES), lambda i, p, pt, us: (i, 0))],
            out_specs=pl.BlockSpec(memory_space=pl.ANY),
            scratch_shapes=[pltpu.VMEM((EXPERT_TILE * ROW_CHUNKS, LANES), rows.dtype),
                            pltpu.SemaphoreType.DMA, pltpu.SemaphoreType.DMA]),
        compiler_params=pltpu.CompilerParams(dimension_semantics=("arbitrary",), disable_bounds_checks=True,
                                             has_side_effects=True),
        name="dispatch_rows",
    )(pos, pad_tile, used, rows)


def _moe_kernel(elo_ref, ehi_ref, valid_ref, x_ref, wrt_ref, wgl_ref, wul_ref, wdl_ref, wgh_ref, wuh_ref,
                wdh_ref, y_ref):
    j = pl.program_id(0)

    @pl.when(valid_ref[j] == 0)
    def _():
        y_ref[...] = jnp.zeros_like(y_ref)

    @pl.when(valid_ref[j] != 0)
    def _():
        h2 = _load_rows(x_ref, EXPERT_TILE)
        s_lo = _sigmoid(jnp.sum(h2 * wrt_ref[pl.ds(elo_ref[j], 1), :], axis=1, keepdims=True))
        s_hi = _sigmoid(jnp.sum(h2 * wrt_ref[pl.ds(ehi_ref[j], 1), :], axis=1, keepdims=True))
        total = s_lo + s_hi
        hb = h2.astype(BF16)
        he_lo = _silu(_dot(hb, wgl_ref[0, 0])) * _dot(hb, wul_ref[0, 0])
        y = (s_lo / total) * _dot(he_lo.astype(BF16), wdl_ref[0, 0])
        he_hi = _silu(_dot(hb, wgh_ref[0, 0])) * _dot(hb, wuh_ref[0, 0])
        _store_rows(y_ref, y + (s_hi / total) * _dot(he_hi.astype(BF16), wdh_ref[0, 0]))


def _moe(rows_sorted, e_lo, e_hi, valid, w_router, wg, wu, wd, layer):
    n_pad = rows_sorted.shape[0] // ROW_CHUNKS
    d = D_MODEL
    lo4 = lambda j, el, eh, va: (layer, el[j], 0, 0)
    hi4 = lambda j, el, eh, va: (layer, eh[j], 0, 0)
    tile = pl.BlockSpec((EXPERT_TILE * ROW_CHUNKS, LANES), lambda j, el, eh, va: (j, 0))
    return pl.pallas_call(
        _moe_kernel,
        out_shape=jax.ShapeDtypeStruct(rows_sorted.shape, F32),
        grid_spec=pltpu.PrefetchScalarGridSpec(
            num_scalar_prefetch=3, grid=(n_pad // EXPERT_TILE,),
            in_specs=[tile, _resident((N_EXPERTS, d)),
                      pl.BlockSpec((1, 1, d, D_EXPERT), lo4), pl.BlockSpec((1, 1, d, D_EXPERT), lo4),
                      pl.BlockSpec((1, 1, D_EXPERT, d), lo4),
                      pl.BlockSpec((1, 1, d, D_EXPERT), hi4), pl.BlockSpec((1, 1, d, D_EXPERT), hi4),
                      pl.BlockSpec((1, 1, D_EXPERT, d), hi4)],
            out_specs=tile),
        compiler_params=_params("arbitrary"),
        name="moe_experts",
    )(e_lo, e_hi, valid, rows_sorted, w_router.T, wg, wu, wd, wg, wu, wd)


def _collect_kernel(pos_ref, x_ref, ys_hbm, mod_ref, fg_ref, o_ref, buf0, buf1, sems, *, final):
    tm = x_ref.shape[0]
    i = pl.program_id(0)
    n = pl.num_programs(0)
    bufs = (buf0, buf1)

    def fetch(tile, slot):
        base = tile * tm
        _row_copies(tm, lambda r: pltpu.make_async_copy(
            _token_slab(ys_hbm, pos_ref[base + r]), _token_slab(bufs[slot], r * ROW_CHUNKS), sems.at[slot]))

    def finish(slot):
        pltpu.make_async_copy(ys_hbm.at[pl.ds(0, tm * ROW_CHUNKS)], bufs[slot], sems.at[slot]).wait()
        xo = x_ref[...] + mod_ref[0][5:6] * _load_rows(bufs[slot], tm)
        if final:
            xo = _rms(xo, fg_ref[...])
        o_ref[...] = xo

    @pl.when(i == 0)
    def _():
        fetch(0, 0)

    for slot in range(2):
        @pl.when(jnp.logical_and(i + 1 < n, (i + 1) % 2 == slot))
        def _():
            fetch(i + 1, slot)

    for slot in range(2):
        @pl.when(i % 2 == slot)
        def _():
            finish(slot)


def _collect_residual(pos, xn, y_sorted, mod_l, final_g, final, seq):
    t, d = xn.shape
    tm = TOKEN_TILE
    rows = pl.BlockSpec((tm, d), lambda i, p: (i, 0))
    return pl.pallas_call(
        functools.partial(_collect_kernel, final=final),
        out_shape=jax.ShapeDtypeStruct((t, d), F32),
        grid_spec=pltpu.PrefetchScalarGridSpec(
            num_scalar_prefetch=1, grid=(t // tm,),
            in_specs=[rows, pl.BlockSpec(memory_space=pl.ANY),
                      pl.BlockSpec((1, 6, d), lambda i, p: ((i * tm) // seq, 0, 0)),
                      pl.BlockSpec((1, d), lambda i, p: (0, 0))],
            out_specs=rows,
            scratch_shapes=[pltpu.VMEM((tm * ROW_CHUNKS, LANES), F32), pltpu.VMEM((tm * ROW_CHUNKS, LANES), F32),
                            pltpu.SemaphoreType.DMA((2,))]),
        compiler_params=pltpu.CompilerParams(dimension_semantics=("arbitrary",), disable_bounds_checks=True,
                                             vmem_limit_bytes=V7X_VMEM_LIMIT_BYTES),
        name="collect_residual",
    )(pos, xn, y_sorted, mod_l, final_g[None, :])


def kernel(x, c, positions, ada_w, ada_b, norm1_g, w_in, hg_lb_logits, hg_norm_g, q_norm_g, w_q_up,
           kv_norm_g, w_kv_up, w_br_a, w_br_b, w_out, norm2_g, w_router, router_bias, w_gate, w_up,
           w_down, final_g):
    batch, seq, d = x.shape
    depth = ada_w.shape[0]
    t = batch * seq
    n_tiles_e = (t + N_CLASSES * (EXPERT_TILE - 1)) // EXPERT_TILE
    mod = _modulation(c, ada_w, ada_b).reshape(depth, batch, 6, d)
    cos_t, sin_t = _rope_tables(positions)
    x2 = x.reshape(t, d)
    wg_all, wu_all, wd_all = w_gate.astype(BF16), w_up.astype(BF16), w_down.astype(BF16)
    for l in range(depth):
        hq, hf, hi, hgate, gsig, q, k, v = _input_projection(
            x2, mod[l], norm1_g[l], cos_t, sin_t, w_in[l], q_norm_g[l], w_q_up[l],
            kv_norm_g[l], w_kv_up[l], seq)
        oa = _hgrn(hg_lb_logits, hg_norm_g[l], hq, hf, hi, hgate, l, batch, seq)
        ob = _attention(q, k, v, batch, seq)
        xn, rows, meta = _merge(x2, oa, ob, gsig, mod[l], w_br_a[l], w_br_b[l], w_out[l], norm2_g[l],
                                w_router, router_bias, seq)
        pos, pad_tile, used, e_lo, e_hi, valid = _dispatch_plan(meta, n_tiles_e)
        rows_sorted = _dispatch_rows(pos, pad_tile, used, rows, n_tiles_e * EXPERT_TILE)
        y_sorted = _moe(rows_sorted, e_lo, e_hi, valid, w_router, wg_all, wu_all, wd_all, l)
        x2 = _collect_residual(pos, xn, y_sorted, mod[l], final_g, l == depth - 1, seq)
    return x2.reshape(batch, seq, d)
```

```python
import functools
import math

import jax
import jax.numpy as jnp
from jax import lax
from jax.experimental import pallas as pl
from jax.experimental.pallas import tpu as pltpu

F32 = jnp.float32
BF16 = jnp.bfloat16

D_MODEL = 1024
CHUNK = 64
EPS = 1e-6

HG_HEADS = 4
HG_DK = 128
HG_DV = 128
HG_WIDTH = HG_HEADS * HG_DV
HG_SUB = 8
HG_LEVELS = (8, 16, 32)

MLA_HEADS = 8
MLA_NOPE = 64
MLA_ROPE = 32
MLA_V = 64
MLA_Q_LORA = 384
MLA_KV_LORA = 256
MLA_DQK = MLA_NOPE + MLA_ROPE
MLA_WIDTH = MLA_HEADS * MLA_V
ROPE_BASE = 10000.0
HEAD_PAD = 128
LOG2_E = math.log2(math.e)

N_EXPERTS = 16
N_GROUPS = 4
EXPERTS_PER_GROUP = N_EXPERTS // N_GROUPS
D_EXPERT = 512

IN_SIZES = (HG_HEADS * HG_DK, HG_HEADS * HG_DK, HG_HEADS * HG_DV, HG_WIDTH,
            MLA_Q_LORA, MLA_KV_LORA, MLA_ROPE, 2 * D_MODEL)

V7X_VMEM_LIMIT_BYTES = 56 * 1024 * 1024


def _params(*sem):
    return pltpu.CompilerParams(dimension_semantics=sem, vmem_limit_bytes=V7X_VMEM_LIMIT_BYTES)


def _resident(shape):
    nd = len(shape)
    return pl.BlockSpec(shape, lambda *_: (0,) * nd, pipeline_mode=pl.Buffered(1))


def _sigmoid(x):
    return 1.0 / (1.0 + jnp.exp(-x))


def _silu(x):
    return x * _sigmoid(x)


def _dot(a, b):
    return jnp.dot(a, b, preferred_element_type=F32)


def _dot_nt(a, b):
    return lax.dot_general(a, b, (((1,), (1,)), ((), ())), preferred_element_type=F32)


def _dot_tn(a, b):
    return lax.dot_general(a, b, (((0,), (0,)), ((), ())), preferred_element_type=F32)


def _split_bf16(x):
    hi = x.astype(BF16)
    lo = (x - hi.astype(F32)).astype(BF16)
    return hi, lo


def _rms(x, g):
    return x * lax.rsqrt(jnp.mean(x * x, axis=-1, keepdims=True) + EPS) * g


def _mod_kernel(c_ref, w_ref, b_ref, o_ref):
    ca = _silu(c_ref[...])
    o_ref[0] = _dot(ca.astype(BF16), w_ref[0].astype(BF16)) + b_ref[0]


def _modulation(c, ada_w, ada_b):
    depth, d, n = ada_w.shape
    b = c.shape[0]
    tn = 1536
    return pl.pallas_call(
        _mod_kernel,
        out_shape=jax.ShapeDtypeStruct((depth, b, n), F32),
        grid=(depth, n // tn),
        in_specs=[pl.BlockSpec((b, d), lambda l, j: (0, 0)),
                  pl.BlockSpec((1, d, tn), lambda l, j: (l, 0, j)),
                  pl.BlockSpec((1, 1, tn), lambda l, j: (l, 0, j))],
        out_specs=pl.BlockSpec((1, b, tn), lambda l, j: (l, 0, j)),
        compiler_params=_params("parallel", "parallel"),
        name="adaln_modulation",
    )(c, ada_w, ada_b.reshape(depth, 1, n))


def _rope_kernel(pos_ref, inv_ref, msk_ref, sgn_ref, cos_ref, sin_ref):
    ang = pos_ref[...].astype(F32) * inv_ref[...]
    cos_ref[...] = jnp.cos(ang) * msk_ref[...]
    sin_ref[...] = jnp.sin(ang) * sgn_ref[...]


def _rope_tables(positions):
    t = positions.size
    half = MLA_ROPE // 2
    inv = ROPE_BASE ** (-jnp.arange(half, dtype=F32) / half)
    z64, z32, one16 = jnp.zeros((MLA_NOPE,), F32), jnp.zeros((32,), F32), jnp.ones((half,), F32)
    inv_row = jnp.concatenate([z64, inv, inv, z32])[None, :]
    msk_row = jnp.concatenate([z64, one16, one16, z32])[None, :]
    sgn_row = jnp.concatenate([z64, -one16, one16, z32])[None, :]
    tr = min(2048, t)
    row = pl.BlockSpec((1, HEAD_PAD), lambda i: (0, 0))
    tab = pl.BlockSpec((tr, HEAD_PAD), lambda i: (i, 0))
    return pl.pallas_call(
        _rope_kernel,
        out_shape=(jax.ShapeDtypeStruct((t, HEAD_PAD), F32),) * 2,
        grid=(t // tr,),
        in_specs=[pl.BlockSpec((tr, 1), lambda i: (i, 0)), row, row, row],
        out_specs=(tab, tab),
        compiler_params=_params("parallel"),
        name="rope_tables",
    )(positions.reshape(t, 1), inv_row, msk_row, sgn_row)


def _proj_kernel(x_ref, mod_ref, n1_ref, cos_ref, sin_ref, wh_ref, wg_ref, wc_ref,
                 qn_ref, wqa_ref, wqs_ref, kn_ref, wk_ref, wv_ref, one_ref,
                 hq_ref, hf_ref, hi_ref, hgate_ref, gsig_ref, q_ref, k_ref, v_ref):
    x = x_ref[...]
    mod = mod_ref[0]
    h = _rms(x, n1_ref[...]) * (1.0 + mod[1:2]) + mod[0:1]
    hb = h.astype(BF16)

    ph = _dot(hb, wh_ref[...])
    w = HG_WIDTH
    hq_ref[...] = ph[:, 0:w].astype(BF16)
    hf_ref[...] = ph[:, w:2 * w]
    hi_ref[...] = ph[:, 2 * w:3 * w].astype(BF16)
    hgate_ref[...] = ph[:, 3 * w:4 * w].astype(BF16)

    gsig_ref[...] = _sigmoid(_dot(hb, wg_ref[...])).astype(BF16)

    pc = _dot(hb, wc_ref[...])
    cq = pc[:, 0:MLA_Q_LORA]
    ckv = pc[:, MLA_Q_LORA:MLA_Q_LORA + MLA_KV_LORA]
    kra = pc[:, MLA_Q_LORA + MLA_KV_LORA:MLA_Q_LORA + MLA_KV_LORA + HEAD_PAD]
    krb = pc[:, MLA_Q_LORA + MLA_KV_LORA + HEAD_PAD:]

    cos_t = cos_ref[...]
    sin_t = sin_ref[...]
    lane = lax.broadcasted_iota(jnp.int32, cos_t.shape, 1)
    scale = MLA_DQK ** -0.5 * LOG2_E
    cq_tab = jnp.tile(scale * (cos_t + jnp.where(lane < MLA_NOPE, 1.0, 0.0)), (1, MLA_HEADS))
    sq_tab = jnp.tile(scale * sin_t, (1, MLA_HEADS))

    cqn = _rms(cq, qn_ref[...]).astype(BF16)
    q = _dot(cqn, wqa_ref[...]) * cq_tab + _dot(cqn, wqs_ref[...]) * sq_tab
    q_ref[...] = q.astype(BF16)

    ckvn = _rms(ckv, kn_ref[...]).astype(BF16)
    kpe = kra * cos_t + krb * sin_t
    k_ref[...] = (_dot(ckvn, wk_ref[...]) + jnp.tile(kpe, (1, MLA_HEADS))).astype(BF16)
    v_ref[...] = (_dot(ckvn, wv_ref[...]) + one_ref[...]).astype(BF16)


def _pad_heads(w, lo, hi, at):
    k, nh, _ = w.shape
    out = jnp.zeros((k, nh, HEAD_PAD), w.dtype)
    out = out.at[:, :, at:at + (hi - lo)].set(w[:, :, lo:hi])
    return out


def _input_projection(x2, mod_l, n1, cos_t, sin_t, w_in, q_norm_g, w_q_up, kv_norm_g, w_kv_up, seq):
    t, d = x2.shape
    tm = 256
    splits = [0]
    for s in IN_SIZES:
        splits.append(splits[-1] + s)
    kr = w_in[:, splits[6]:splits[7]]
    half = MLA_ROPE // 2
    z64 = jnp.zeros((d, MLA_NOPE), F32)
    z32 = jnp.zeros((d, HEAD_PAD - MLA_NOPE - MLA_ROPE), F32)
    kr_a = jnp.concatenate([z64, kr, z32], axis=1)
    kr_b = jnp.concatenate([z64, kr[:, half:], kr[:, :half], z32], axis=1)
    w_h = w_in[:, splits[0]:splits[4]].astype(BF16)
    w_g = w_in[:, splits[7]:splits[8]].astype(BF16)
    w_c = jnp.concatenate([w_in[:, splits[4]:splits[6]], kr_a, kr_b], axis=1).astype(BF16)

    wq = w_q_up.reshape(MLA_Q_LORA, MLA_HEADS, MLA_DQK)
    wq_all = _pad_heads(wq, 0, MLA_DQK, 0).reshape(MLA_Q_LORA, -1).astype(BF16)
    wq_swap = (_pad_heads(wq, MLA_NOPE + half, MLA_DQK, MLA_NOPE)
               + _pad_heads(wq, MLA_NOPE, MLA_NOPE + half, MLA_NOPE + half))
    wq_swap = wq_swap.reshape(MLA_Q_LORA, -1).astype(BF16)
    wkv = w_kv_up.reshape(MLA_KV_LORA, MLA_HEADS, MLA_NOPE + MLA_V)
    wk_all = _pad_heads(wkv, 0, MLA_NOPE, 0).reshape(MLA_KV_LORA, -1).astype(BF16)
    wv_all = _pad_heads(wkv, MLA_NOPE, MLA_NOPE + MLA_V, 0).reshape(MLA_KV_LORA, -1).astype(BF16)
    ones_row = jnp.tile(jnp.zeros((HEAD_PAD,), F32).at[MLA_V].set(1.0), MLA_HEADS)[None, :]

    hp = MLA_HEADS * HEAD_PAD
    rows = lambda n: pl.BlockSpec((tm, n), lambda i: (i, 0))
    outs = [(HG_WIDTH, BF16), (HG_WIDTH, F32), (HG_WIDTH, BF16), (HG_WIDTH, BF16),
            (2 * d, BF16), (hp, BF16), (hp, BF16), (hp, BF16)]
    return pl.pallas_call(
        _proj_kernel,
        out_shape=tuple(jax.ShapeDtypeStruct((t, n), dt) for n, dt in outs),
        grid=(t // tm,),
        in_specs=[rows(d),
                  pl.BlockSpec((1, 6, d), lambda i: ((i * tm) // seq, 0, 0)),
                  _resident((1, d)),
                  rows(HEAD_PAD), rows(HEAD_PAD),
                  _resident(w_h.shape), _resident(w_g.shape), _resident(w_c.shape),
                  _resident((1, MLA_Q_LORA)), _resident(wq_all.shape), _resident(wq_swap.shape),
                  _resident((1, MLA_KV_LORA)), _resident(wk_all.shape), _resident(wv_all.shape),
                  _resident((1, hp))],
        out_specs=tuple(rows(n) for n, _ in outs),
        compiler_params=_params("parallel"),
        name="norm_input_projection",
    )(x2, mod_l, n1[None, :], cos_t, sin_t, w_h, w_g, w_c,
      q_norm_g[None, :], wq_all, wq_swap, kv_norm_g[None, :], wk_all, wv_all, ones_row)


def _hgrn_kernel(lbl_ref, gn_ref, q_ref, f_ref, v_ref, gate_ref, o_ref, st_ref, *, layer, n_chunks):
    @pl.when(pl.program_id(1) == 0)
    def _():
        st_ref[...] = jnp.zeros_like(st_ref)

    lg = lbl_ref[...]
    ex = jnp.exp(lg - jnp.max(lg, axis=0, keepdims=True))
    soft = ex / jnp.sum(ex, axis=0, keepdims=True)
    lb_all = jnp.zeros_like(soft[0:1])
    for i in range(1, layer + 1):
        lb_all = lb_all + soft[i:i + 1]

    r_i = lax.broadcasted_iota(jnp.int32, (CHUNK, CHUNK), 0)
    c_i = lax.broadcasted_iota(jnp.int32, (CHUNK, CHUNK), 1)
    tri = jnp.where(r_i >= c_i, 1.0, 0.0).astype(BF16)
    level_masks = []
    for hs in HG_LEVELS:
        same = (r_i // (2 * hs)) == (c_i // (2 * hs))
        level_masks.append(jnp.logical_and(same, jnp.logical_and(r_i % (2 * hs) >= hs, c_i % (2 * hs) < hs)))
    sub_row = lax.broadcasted_iota(jnp.int32, (HG_SUB, 1), 0)
    n_sub = CHUNK // HG_SUB

    def chunk_body(ci, carry):
        r0 = pl.multiple_of(ci * CHUNK, CHUNK)
        rows = pl.ds(r0, CHUNK)
        for h in range(HG_HEADS):
            sl = slice(h * HG_DK, (h + 1) * HG_DK)
            lb = lb_all[:, sl]
            q = q_ref[rows, sl].astype(F32)
            vb = v_ref[rows, sl]
            v = vb.astype(F32)
            f = lb + (1.0 - lb) * _sigmoid(f_ref[rows, sl])
            g = jnp.log2(f)
            k = 1.0 - f
            qf = _silu(q) * (HG_DK ** -0.5)
            g_hi, g_lo = _split_bf16(g)
            b = _dot(tri, g_hi) + _dot(tri, g_lo)

            att = jnp.zeros((CHUNK, CHUNK), F32)
            for hs, mask in zip(HG_LEVELS, level_masks):
                ref = jnp.concatenate(
                    [jnp.broadcast_to(b[j + hs - 1:j + hs], (2 * hs, HG_DK)) for j in range(0, CHUNK, 2 * hs)],
                    axis=0)
                qt = (qf * jnp.exp2(b - ref)).astype(BF16)
                kt = (k * jnp.exp2(ref - b)).astype(BF16)
                att = att + jnp.where(mask, _dot_nt(qt, kt), 0.0)
            o = _dot(att.astype(BF16), vb)

            blocks = []
            for i in range(n_sub):
                rs = slice(i * HG_SUB, (i + 1) * HG_SUB)
                b_i, q_i, k_i, v_i = b[rs], qf[rs], k[rs], v[rs]
                acc = jnp.zeros((HG_SUB, HG_DV), F32)
                for s in range(HG_SUB):
                    a = jnp.sum(jnp.exp2(b_i - b_i[s:s + 1]) * (q_i * k_i[s:s + 1]), axis=1, keepdims=True)
                    acc = acc + jnp.where(sub_row >= s, a, 0.0) * v_i[s:s + 1]
                blocks.append(acc)
            o = o + jnp.concatenate(blocks, axis=0)

            st = st_ref[h]
            o = o + _dot_nt((qf * jnp.exp2(b)).astype(BF16), st.astype(BF16))
            b_last = b[CHUNK - 1:CHUNK]
            kd = (k * jnp.exp2(b_last - b)).astype(BF16)
            st_ref[h] = st * jnp.exp2(b_last) + _dot_tn(vb, kd)

            gt = gate_ref[rows, sl].astype(F32)
            o_ref[rows, sl] = (_rms(o, gn_ref[:, sl]) * _silu(gt)).astype(BF16)
        return carry

    lax.fori_loop(0, n_chunks, chunk_body, 0, unroll=2)


def _hgrn(hg_lb_logits, hg_norm_g, hq, hf, hi, hgate, layer, batch, seq):
    t = hq.shape[0]
    lc = min(512, seq)
    nb = seq // lc
    rows = pl.BlockSpec((lc, HG_WIDTH), lambda b, j: (b * nb + j, 0))
    kern = functools.partial(_hgrn_kernel, layer=layer, n_chunks=lc // CHUNK)
    return pl.pallas_call(
        kern,
        out_shape=jax.ShapeDtypeStruct((t, HG_WIDTH), BF16),
        grid=(batch, nb),
        in_specs=[pl.BlockSpec(hg_lb_logits.shape, lambda b, j: (0, 0)),
                  pl.BlockSpec((1, HG_WIDTH), lambda b, j: (0, 0)),
                  rows, rows, rows, rows],
        out_specs=rows,
        scratch_shapes=[pltpu.VMEM((HG_HEADS, HG_DV, HG_DK), F32)],
        compiler_params=_params("parallel", "arbitrary"),
        name="hgrn2_chunkwise",
    )(hg_lb_logits, hg_norm_g[None, :], hq, hf, hi, hgate)


ATTN_BLOCK = 1024
STEP_FULL, STEP_TOP_DIAG, STEP_BOTTOM_DIAG = range(3)


def _attn_kernel(qi_ref, kj_ref, kind_ref, q_ref, k_ref, v_ref, o_ref, m_ref, acc_ref, *, tb):
    del qi_ref, kj_ref
    p_id = pl.program_id(1)
    kind = kind_ref[p_id]
    half = tb // 2

    def reset():
        m_ref[...] = jnp.full_like(m_ref, -jnp.inf)
        acc_ref[...] = jnp.zeros_like(acc_ref)

    def step(r0, nr, nk, diagonal):
        rows = slice(r0, r0 + nr)
        if diagonal:
            qc = (lax.broadcasted_iota(jnp.int32, (nr, nk), 0) + r0) // CHUNK
            kc = lax.broadcasted_iota(jnp.int32, (nr, nk), 1) // CHUNK
            visible = kc <= qc
        for h in range(MLA_HEADS):
            sl = slice(h * HEAD_PAD, (h + 1) * HEAD_PAD)
            s = _dot_nt(q_ref[rows, sl], k_ref[0:nk, sl])
            if diagonal:
                s = jnp.where(visible, s, -jnp.inf)
            tiles = [s[:, t * HEAD_PAD:(t + 1) * HEAD_PAD] for t in range(nk // HEAD_PAD)]
            m_tile = tiles[0]
            for tl in tiles[1:]:
                m_tile = jnp.maximum(m_tile, tl)
            m_prev = m_ref[h, rows]
            m_new = jnp.maximum(m_prev, jnp.max(m_tile, axis=1, keepdims=True))
            alpha = jnp.exp2(m_prev - m_new)
            p = jnp.concatenate([jnp.exp2((tl - m_new).astype(BF16)) for tl in tiles], axis=1)
            acc_ref[h, rows] = acc_ref[h, rows] * alpha + _dot(p, v_ref[0:nk, sl])
            m_ref[h, rows] = m_new

    @pl.when(p_id == 0)
    def _():
        reset()

    @pl.when(kind == STEP_FULL)
    def _():
        step(0, tb, tb, False)

    @pl.when(kind == STEP_TOP_DIAG)
    def _():
        step(0, half, half, True)

    @pl.when(kind == STEP_BOTTOM_DIAG)
    def _():
        step(half, half, tb, True)
        outs = []
        for h in range(MLA_HEADS):
            acc = acc_ref[h]
            outs.append(acc[:, 0:MLA_V] / acc[:, MLA_V:MLA_V + 1])
        o_ref[...] = jnp.concatenate(outs, axis=1).astype(BF16)
        reset()


def _attention(q, k, v, batch, seq):
    t, hp = q.shape
    tb = min(ATTN_BLOCK, seq)
    nb = seq // tb
    steps = []
    for i in range(nb):
        steps += [(i, j, STEP_FULL) for j in range(i)]
        steps += [(i, i, STEP_TOP_DIAG), (i, i, STEP_BOTTOM_DIAG)]
    q_of, k_of, kinds = (jnp.asarray([st[c] for st in steps], jnp.int32) for c in range(3))
    qspec = pl.BlockSpec((tb, hp), lambda b, p, qi, kj, kd: (b * nb + qi[p], 0))
    kspec = pl.BlockSpec((tb, hp), lambda b, p, qi, kj, kd: (b * nb + kj[p], 0))
    return pl.pallas_call(
        functools.partial(_attn_kernel, tb=tb),
        out_shape=jax.ShapeDtypeStruct((t, MLA_WIDTH), BF16),
        grid_spec=pltpu.PrefetchScalarGridSpec(
            num_scalar_prefetch=3,
            grid=(batch, len(steps)),
            in_specs=[qspec, kspec, kspec],
            out_specs=pl.BlockSpec((tb, MLA_WIDTH), lambda b, p, qi, kj, kd: (b * nb + qi[p], 0)),
            scratch_shapes=[pltpu.VMEM((MLA_HEADS, tb, HEAD_PAD), F32),
                            pltpu.VMEM((MLA_HEADS, tb, HEAD_PAD), F32)]),
        compiler_params=_params("parallel", "arbitrary"),
        name="mla_flash_attention",
    )(q_of, k_of, kinds, q, k, v)


PAIR_ORDER = ((0, 1), (0, 2), (0, 3), (1, 3), (1, 2), (2, 3))
PAIRS_PER_GROUP = len(PAIR_ORDER)
N_CLASSES = N_GROUPS * PAIRS_PER_GROUP
LANES = 128
ROW_CHUNKS = D_MODEL // LANES
TOKEN_TILE = 1024
EXPERT_TILE = 256


N_CLASS_ROWS = 32


def _route(scores, bias):
    biased = scores + bias
    col = [biased[e:e + 1, :] for e in range(N_EXPERTS)]
    gscore = []
    for g in range(N_GROUPS):
        a, b, c, d = col[g * EXPERTS_PER_GROUP:(g + 1) * EXPERTS_PER_GROUP]
        gscore.append(jnp.maximum(jnp.maximum(jnp.maximum(a + b, a + c), jnp.maximum(a + d, b + c)),
                                  jnp.maximum(b + d, c + d)))
    sel = []
    for g in range(N_GROUPS):
        ok = None
        for o in range(N_GROUPS):
            if o == g:
                continue
            cond = (gscore[g] > gscore[o]) if o < g else (gscore[g] >= gscore[o])
            ok = cond if ok is None else jnp.logical_and(ok, cond)
        grp = col[g * EXPERTS_PER_GROUP:(g + 1) * EXPERTS_PER_GROUP]
        for e in range(EXPERTS_PER_GROUP):
            beaten = jnp.zeros_like(grp[e])
            for o in range(EXPERTS_PER_GROUP):
                if o == e:
                    continue
                ahead = (grp[o] >= grp[e]) if o < e else (grp[o] > grp[e])
                beaten = beaten + jnp.where(ahead, 1.0, 0.0)
            sel.append(jnp.logical_and(ok, beaten < 1.5))
    lo = jnp.full_like(col[0], float(N_EXPERTS))
    hi = jnp.full_like(col[0], -1.0)
    for e in range(N_EXPERTS):
        lo = jnp.where(sel[e], jnp.minimum(lo, float(e)), lo)
        hi = jnp.where(sel[e], jnp.maximum(hi, float(e)), hi)
    return lo, hi


def _store_rows(ref, x):
    n = x.shape[0]
    for c in range(ROW_CHUNKS):
        ref[pl.ds(c, n, stride=ROW_CHUNKS), :] = x[:, c * LANES:(c + 1) * LANES]


def _load_rows(ref, n):
    return jnp.concatenate([ref[pl.ds(c, n, stride=ROW_CHUNKS), :] for c in range(ROW_CHUNKS)], axis=1)


def _merge_kernel(x_ref, oa_ref, ob_ref, g_ref, mod_ref, wa_ref, wb_ref, wo_ref, n2_ref,
                  wrt_ref, rb_ref, xn_ref, row_ref, meta_ref):
    d = D_MODEL
    tm = x_ref.shape[0]
    mod = mod_ref[0]
    ya = _dot(oa_ref[...], wa_ref[...])
    yb = _dot(ob_ref[...], wb_ref[...])
    g = g_ref[...].astype(F32)
    merged = g[:, 0:d] * ya + g[:, d:2 * d] * yb
    xn = x_ref[...] + mod[2:3] * _dot(merged.astype(BF16), wo_ref[...])
    xn_ref[...] = xn
    h2 = _rms(xn, n2_ref[...]) * (1.0 + mod[4:5]) + mod[3:4]
    _store_rows(row_ref, h2)

    h_hi, h_lo = _split_bf16(h2)
    w_hi, w_lo = _split_bf16(wrt_ref[...])
    scores = _sigmoid(_dot_nt(w_hi, h_hi) + _dot_nt(w_hi, h_lo) + _dot_nt(w_lo, h_hi))
    lo, hi = _route(scores, rb_ref[...])

    grp = jnp.floor(lo * (1.0 / EXPERTS_PER_GROUP))
    a = lo - grp * EXPERTS_PER_GROUP
    b = hi - grp * EXPERTS_PER_GROUP
    pair = jnp.zeros_like(a)
    for p, (p_lo, p_hi) in enumerate(PAIR_ORDER):
        pair = pair + jnp.where(jnp.logical_and(a == p_lo, b == p_hi), float(p), 0.0)
    cls = grp * PAIRS_PER_GROUP + pair
    c_row = lax.broadcasted_iota(jnp.int32, (N_CLASS_ROWS, tm), 0).astype(F32)
    onehot = jnp.where(c_row == cls, 1.0, 0.0)
    r_i = lax.broadcasted_iota(jnp.int32, (tm, tm), 0)
    c_i = lax.broadcasted_iota(jnp.int32, (tm, tm), 1)
    earlier = jnp.where(r_i < c_i, 1.0, 0.0).astype(BF16)
    rank = jnp.sum(onehot * _dot(onehot.astype(BF16), earlier), axis=0, keepdims=True)
    m_row = lax.broadcasted_iota(jnp.int32, (8, tm), 0)
    meta_ref[...] = jnp.where(m_row == 0, cls, jnp.where(m_row == 1, rank, 0.0))


def _merge(x2, oa, ob, gsig, mod_l, w_br_a, w_br_b, w_out, n2, w_router, router_bias, seq):
    t, d = x2.shape
    tm = TOKEN_TILE
    rows = lambda n: pl.BlockSpec((tm, n), lambda i: (i, 0))
    return pl.pallas_call(
        _merge_kernel,
        out_shape=(jax.ShapeDtypeStruct((t, d), F32), jax.ShapeDtypeStruct((t * ROW_CHUNKS, LANES), F32),
                   jax.ShapeDtypeStruct((8, t), F32)),
        grid=(t // tm,),
        in_specs=[rows(d), rows(HG_WIDTH), rows(MLA_WIDTH), rows(2 * d),
                  pl.BlockSpec((1, 6, d), lambda i: ((i * tm) // seq, 0, 0)),
                  _resident(w_br_a.shape), _resident(w_br_b.shape), _resident(w_out.shape),
                  _resident((1, d)), _resident((N_EXPERTS, d)), _resident((N_EXPERTS, 1))],
        out_specs=(rows(d), pl.BlockSpec((tm * ROW_CHUNKS, LANES), lambda i: (i, 0)),
                   pl.BlockSpec((8, tm), lambda i: (0, i))),
        compiler_params=_params("parallel"),
        name="merge_outproj_router",
    )(x2, oa, ob, gsig, mod_l, w_br_a.astype(BF16), w_br_b.astype(BF16), w_out.astype(BF16),
      n2[None, :], w_router.T, router_bias[:, None])


def _dispatch_plan(meta, n_tiles_e):
    t = meta.shape[1]
    n_tok_tiles = t // TOKEN_TILE
    cls = meta[0].astype(jnp.int32)
    rank = meta[1].astype(jnp.int32)
    own = cls.reshape(n_tok_tiles, TOKEN_TILE, 1) == jnp.arange(N_CLASSES, dtype=jnp.int32)
    counts = jnp.sum(own.astype(jnp.int32), axis=1)
    total = jnp.sum(counts, axis=0)
    total_pad = (total + EXPERT_TILE - 1) // EXPERT_TILE * EXPERT_TILE
    ends = jnp.cumsum(total_pad)
    base = (ends - total_pad)[None, :] + jnp.cumsum(counts, axis=0) - counts
    pos = jnp.sum(jnp.where(own, base[:, None, :], 0), axis=2).reshape(t) + rank

    tile_start = jnp.arange(n_tiles_e, dtype=jnp.int32) * EXPERT_TILE
    n_valid = ends[-1] // EXPERT_TILE
    tile_cls = jnp.sum((tile_start[:, None] >= ends[None, :]).astype(jnp.int32), axis=1)
    last_cls = jnp.take(tile_cls, n_valid - 1)
    valid = jnp.arange(n_tiles_e, dtype=jnp.int32) < n_valid
    tile_cls = jnp.where(valid, tile_cls, last_cls)
    pair_lo = jnp.asarray([p[0] for p in PAIR_ORDER], jnp.int32)
    pair_hi = jnp.asarray([p[1] for p in PAIR_ORDER], jnp.int32)
    grp = tile_cls // PAIRS_PER_GROUP
    e_lo = grp * EXPERTS_PER_GROUP + jnp.take(pair_lo, tile_cls % PAIRS_PER_GROUP)
    e_hi = grp * EXPERTS_PER_GROUP + jnp.take(pair_hi, tile_cls % PAIRS_PER_GROUP)
    pad_tile = (ends // EXPERT_TILE - 1).astype(jnp.int32)
    used = jnp.concatenate([(total > 0).astype(jnp.int32), n_valid[None].astype(jnp.int32)])
    return (pos * ROW_CHUNKS).astype(jnp.int32), pad_tile, used, e_lo, e_hi, valid.astype(jnp.int32)


DMA_UNROLL = 8


def _row_copies(n_rows, make_copy):
    def issue(ui, carry):
        for u in range(DMA_UNROLL):
            make_copy(ui * DMA_UNROLL + u).start(priority=u % 2)
        return carry
    lax.fori_loop(0, n_rows // DMA_UNROLL, issue, 0)


def _token_slab(ref, first_row):
    return ref.at[pl.ds(pl.multiple_of(first_row, ROW_CHUNKS), ROW_CHUNKS)]


def _dispatch_kernel(pos_ref, pad_tile_ref, used_ref, rows_ref, dst_hbm, zero_ref, sem, zero_sem):
    tm = rows_ref.shape[0] // ROW_CHUNKS
    tile_rows = EXPERT_TILE * ROW_CHUNKS

    @pl.when(pl.program_id(0) == 0)
    def _():
        zero_ref[...] = jnp.zeros_like(zero_ref)
        n_tiles = dst_hbm.shape[0] // tile_rows

        def clear(tile):
            first = pl.multiple_of(tile * tile_rows, tile_rows)
            return pltpu.make_async_copy(zero_ref, dst_hbm.at[pl.ds(first, tile_rows)], zero_sem)

        def start_tail(tile, carry):
            clear(tile).start()
            return carry

        def wait_tail(tile, carry):
            clear(tile).wait()
            return carry
        for c in range(N_CLASSES):
            @pl.when(used_ref[c] != 0)
            def _():
                clear(pad_tile_ref[c]).start()
        lax.fori_loop(used_ref[N_CLASSES], n_tiles, start_tail, 0)
        for c in range(N_CLASSES):
            @pl.when(used_ref[c] != 0)
            def _():
                clear(pad_tile_ref[c]).wait()
        lax.fori_loop(used_ref[N_CLASSES], n_tiles, wait_tail, 0)

    base = pl.program_id(0) * tm
    _row_copies(tm, lambda r: pltpu.make_async_copy(
        _token_slab(rows_ref, r * ROW_CHUNKS), _token_slab(dst_hbm, pos_ref[base + r]), sem))
    pltpu.make_async_copy(rows_ref, dst_hbm.at[pl.ds(0, tm * ROW_CHUNKS)], sem).wait()


def _dispatch_rows(pos, pad_tile, used, rows, n_dst):
    t = rows.shape[0] // ROW_CHUNKS
    tm = TOKEN_TILE
    return pl.pallas_call(
        _dispatch_kernel,
        out_shape=jax.ShapeDtypeStruct((n_dst * ROW_CHUNKS, LANES), rows.dtype),
        grid_spec=pltpu.PrefetchScalarGridSpec(
            num_scalar_prefetch=3, grid=(t // tm,),
            in_specs=[pl.BlockSpec((tm * ROW_CHUNKS, LANES), lambda i, p, pt, us: (i, 0))],
            out_specs=pl.BlockSpec(memory_space=pl.ANY),
            scratch_shapes=[pltpu.VMEM((EXPERT_TILE * ROW_CHUNKS, LANES), rows.dtype),
                            pltpu.SemaphoreType.DMA, pltpu.SemaphoreType.DMA]),
        compiler_params=pltpu.CompilerParams(dimension_semantics=("arbitrary",), disable_bounds_checks=True,
                                             has_side_effects=True),
        name="dispatch_rows",
    )(pos, pad_tile, used, rows)


def _moe_kernel(elo_ref, ehi_ref, valid_ref, x_ref, wrt_ref, wgl_ref, wul_ref, wdl_ref, wgh_ref, wuh_ref,
                wdh_ref, y_ref):
    j = pl.program_id(0)

    @pl.when(valid_ref[j] == 0)
    def _():
        y_ref[...] = jnp.zeros_like(y_ref)

    @pl.when(valid_ref[j] != 0)
    def _():
        h2 = _load_rows(x_ref, EXPERT_TILE)
        s_lo = _sigmoid(jnp.sum(h2 * wrt_ref[pl.ds(elo_ref[j], 1), :], axis=1, keepdims=True))
        s_hi = _sigmoid(jnp.sum(h2 * wrt_ref[pl.ds(ehi_ref[j], 1), :], axis=1, keepdims=True))
        total = s_lo + s_hi
        hb = h2.astype(BF16)
        he_lo = _silu(_dot(hb, wgl_ref[0, 0])) * _dot(hb, wul_ref[0, 0])
        y = (s_lo / total) * _dot(he_lo.astype(BF16), wdl_ref[0, 0])
        he_hi = _silu(_dot(hb, wgh_ref[0, 0])) * _dot(hb, wuh_ref[0, 0])
        _store_rows(y_ref, y + (s_hi / total) * _dot(he_hi.astype(BF16), wdh_ref[0, 0]))


def _moe(rows_sorted, e_lo, e_hi, valid, w_router, wg, wu, wd, layer):
    n_pad = rows_sorted.shape[0] // ROW_CHUNKS
    d = D_MODEL
    lo4 = lambda j, el, eh, va: (layer, el[j], 0, 0)
    hi4 = lambda j, el, eh, va: (layer, eh[j], 0, 0)
    tile = pl.BlockSpec((EXPERT_TILE * ROW_CHUNKS, LANES), lambda j, el, eh, va: (j, 0))
    return pl.pallas_call(
        _moe_kernel,
        out_shape=jax.ShapeDtypeStruct(rows_sorted.shape, F32),
        grid_spec=pltpu.PrefetchScalarGridSpec(
            num_scalar_prefetch=3, grid=(n_pad // EXPERT_TILE,),
            in_specs=[tile, _resident((N_EXPERTS, d)),
                      pl.BlockSpec((1, 1, d, D_EXPERT), lo4), pl.BlockSpec((1, 1, d, D_EXPERT), lo4),
                      pl.BlockSpec((1, 1, D_EXPERT, d), lo4),
                      pl.BlockSpec((1, 1, d, D_EXPERT), hi4), pl.BlockSpec((1, 1, d, D_EXPERT), hi4),
                      pl.BlockSpec((1, 1, D_EXPERT, d), hi4)],
            out_specs=tile),
        compiler_params=_params("arbitrary"),
        name="moe_experts",
    )(e_lo, e_hi, valid, rows_sorted, w_router.T, wg, wu, wd, wg, wu, wd)


def _collect_kernel(pos_ref, x_ref, ys_hbm, mod_ref, fg_ref, o_ref, buf0, buf1, sems, *, final):
    tm = x_ref.shape[0]
    i = pl.program_id(0)
    n = pl.num_programs(0)
    bufs = (buf0, buf1)

    def fetch(tile, slot):
        base = tile * tm
        _row_copies(tm, lambda r: pltpu.make_async_copy(
            _token_slab(ys_hbm, pos_ref[base + r]), _token_slab(bufs[slot], r * ROW_CHUNKS), sems.at[slot]))

    def finish(slot):
        pltpu.make_async_copy(ys_hbm.at[pl.ds(0, tm * ROW_CHUNKS)], bufs[slot], sems.at[slot]).wait()
        xo = x_ref[...] + mod_ref[0][5:6] * _load_rows(bufs[slot], tm)
        if final:
            xo = _rms(xo, fg_ref[...])
        o_ref[...] = xo

    @pl.when(i == 0)
    def _():
        fetch(0, 0)

    for slot in range(2):
        @pl.when(jnp.logical_and(i + 1 < n, (i + 1) % 2 == slot))
        def _():
            fetch(i + 1, slot)

    for slot in range(2):
        @pl.when(i % 2 == slot)
        def _():
            finish(slot)


def _collect_residual(pos, xn, y_sorted, mod_l, final_g, final, seq):
    t, d = xn.shape
    tm = TOKEN_TILE
    rows = pl.BlockSpec((tm, d), lambda i, p: (i, 0))
    return pl.pallas_call(
        functools.partial(_collect_kernel, final=final),
        out_shape=jax.ShapeDtypeStruct((t, d), F32),
        grid_spec=pltpu.PrefetchScalarGridSpec(
            num_scalar_prefetch=1, grid=(t // tm,),
            in_specs=[rows, pl.BlockSpec(memory_space=pl.ANY),
                      pl.BlockSpec((1, 6, d), lambda i, p: ((i * tm) // seq, 0, 0)),
                      pl.BlockSpec((1, d), lambda i, p: (0, 0))],
            out_specs=rows,
            scratch_shapes=[pltpu.VMEM((tm * ROW_CHUNKS, LANES), F32), pltpu.VMEM((tm * ROW_CHUNKS, LANES), F32),
                            pltpu.SemaphoreType.DMA((2,))]),
        compiler_params=pltpu.CompilerParams(dimension_semantics=("arbitrary",), disable_bounds_checks=True,
                                             vmem_limit_bytes=V7X_VMEM_LIMIT_BYTES),
        name="collect_residual",
    )(pos, xn, y_sorted, mod_l, final_g[None, :])


def kernel(x, c, positions, ada_w, ada_b, norm1_g, w_in, hg_lb_logits, hg_norm_g, q_norm_g, w_q_up,
           kv_norm_g, w_kv_up, w_br_a, w_br_b, w_out, norm2_g, w_router, router_bias, w_gate, w_up,
           w_down, final_g):
    batch, seq, d = x.shape
    depth = ada_w.shape[0]
    t = batch * seq
    n_tiles_e = (t + N_CLASSES * (EXPERT_TILE - 1)) // EXPERT_TILE
    mod = _modulation(c, ada_w, ada_b).reshape(depth, batch, 6, d)
    cos_t, sin_t = _rope_tables(positions)
    x2 = x.reshape(t, d)
    wg_all, wu_all, wd_all = w_gate.astype(BF16), w_up.astype(BF16), w_down.astype(BF16)
    for l in range(depth):
        hq, hf, hi, hgate, gsig, q, k, v = _input_projection(
            x2, mod[l], norm1_g[l], cos_t, sin_t, w_in[l], q_norm_g[l], w_q_up[l],
            kv_norm_g[l], w_kv_up[l], seq)
        oa = _hgrn(hg_lb_logits, hg_norm_g[l], hq, hf, hi, hgate, l, batch, seq)
        ob = _attention(q, k, v, batch, seq)
        xn, rows, meta = _merge(x2, oa, ob, gsig, mod[l], w_br_a[l], w_br_b[l], w_out[l], norm2_g[l],
                                w_router, router_bias, seq)
        pos, pad_tile, used, e_lo, e_hi, valid = _dispatch_plan(meta, n_tiles_e)
        rows_sorted = _dispatch_rows(pos, pad_tile, used, rows, n_tiles_e * EXPERT_TILE)
        y_sorted = _moe(rows_sorted, e_lo, e_hi, valid, w_router, wg_all, wu_all, wd_all, l)
        x2 = _collect_residual(pos, xn, y_sorted, mod[l], final_g, l == depth - 1, seq)
    return x2.reshape(batch, seq, d)
```

```python
import functools
import math

import jax
import jax.numpy as jnp
from jax import lax
from jax.experimental import pallas as pl
from jax.experimental.pallas import tpu as pltpu

F32 = jnp.float32
BF16 = jnp.bfloat16

D_MODEL = 1024
CHUNK = 64
EPS = 1e-6

HG_HEADS = 4
HG_DK = 128
HG_DV = 128
HG_WIDTH = HG_HEADS * HG_DV
HG_SUB = 8
HG_LEVELS = (8, 16, 32)

MLA_HEADS = 8
MLA_NOPE = 64
MLA_ROPE = 32
MLA_V = 64
MLA_Q_LORA = 384
MLA_KV_LORA = 256
MLA_DQK = MLA_NOPE + MLA_ROPE
MLA_WIDTH = MLA_HEADS * MLA_V
ROPE_BASE = 10000.0
HEAD_PAD = 128
LOG2_E = math.log2(math.e)

N_EXPERTS = 16
N_GROUPS = 4
EXPERTS_PER_GROUP = N_EXPERTS // N_GROUPS
D_EXPERT = 512

IN_SIZES = (HG_HEADS * HG_DK, HG_HEADS * HG_DK, HG_HEADS * HG_DV, HG_WIDTH,
            MLA_Q_LORA, MLA_KV_LORA, MLA_ROPE, 2 * D_MODEL)

V7X_VMEM_LIMIT_BYTES = 56 * 1024 * 1024


def _params(*sem):
    return pltpu.CompilerParams(dimension_semantics=sem, vmem_limit_bytes=V7X_VMEM_LIMIT_BYTES)


def _resident(shape):
    nd = len(shape)
    return pl.BlockSpec(shape, lambda *_: (0,) * nd, pipeline_mode=pl.Buffered(1))


def _sigmoid(x):
    return 1.0 / (1.0 + jnp.exp(-x))


def _silu(x):
    return x * _sigmoid(x)


def _dot(a, b):
    return jnp.dot(a, b, preferred_element_type=F32)


def _dot_nt(a, b):
    return lax.dot_general(a, b, (((1,), (1,)), ((), ())), preferred_element_type=F32)


def _dot_tn(a, b):
    return lax.dot_general(a, b, (((0,), (0,)), ((), ())), preferred_element_type=F32)


def _split_bf16(x):
    hi = x.astype(BF16)
    lo = (x - hi.astype(F32)).astype(BF16)
    return hi, lo


def _rms(x, g):
    return x * lax.rsqrt(jnp.mean(x * x, axis=-1, keepdims=True) + EPS) * g


def _mod_kernel(c_ref, w_ref, b_ref, o_ref):
    @pl.when(pl.program_id(1) == 0)
    def _():
        o_ref[0] = jnp.broadcast_to(b_ref[0], o_ref.shape[1:])
    ca = _silu(c_ref[...])
    o_ref[0] += _dot(ca.astype(BF16), w_ref[0].astype(BF16))


def _modulation(c, ada_w, ada_b):
    depth, d, n = ada_w.shape
    b = c.shape[0]
    tk = 256
    return pl.pallas_call(
        _mod_kernel,
        out_shape=jax.ShapeDtypeStruct((depth, b, n), F32),
        grid=(depth, d // tk),
        in_specs=[pl.BlockSpec((b, tk), lambda l, k: (0, k)),
                  pl.BlockSpec((1, tk, n), lambda l, k: (l, k, 0)),
                  pl.BlockSpec((1, 1, n), lambda l, k: (l, 0, 0))],
        out_specs=pl.BlockSpec((1, b, n), lambda l, k: (l, 0, 0)),
        compiler_params=_params("parallel", "arbitrary"),
        name="adaln_modulation",
    )(c, ada_w, ada_b.reshape(depth, 1, n))


def _rope_kernel(pos_ref, inv_ref, cos_ref, sin_ref):
    ang = pos_ref[...].astype(F32) * inv_ref[...]
    cos_ref[...] = jnp.cos(ang)
    sin_ref[...] = jnp.sin(ang)


def _rope_tables(positions):
    t = positions.size
    half = MLA_ROPE // 2
    per_row = HEAD_PAD // half
    inv = ROPE_BASE ** (-jnp.arange(half, dtype=F32) / half)
    pos_dense = jnp.repeat(positions.reshape(t // per_row, per_row), half, axis=1)
    dense = pl.BlockSpec((t // per_row, HEAD_PAD), lambda i: (0, 0))
    cos_d, sin_d = pl.pallas_call(
        _rope_kernel,
        out_shape=(jax.ShapeDtypeStruct((t // per_row, HEAD_PAD), F32),) * 2,
        grid=(1,),
        in_specs=[dense, pl.BlockSpec((1, HEAD_PAD), lambda i: (0, 0))],
        out_specs=(dense, dense),
        compiler_params=_params("arbitrary"),
        name="rope_tables",
    )(pos_dense, jnp.tile(inv, per_row)[None, :])
    cos16, sin16 = cos_d.reshape(t, half), sin_d.reshape(t, half)
    z64 = jnp.zeros((t, MLA_NOPE), F32)
    z32 = jnp.zeros((t, HEAD_PAD - MLA_NOPE - MLA_ROPE), F32)
    return (jnp.concatenate([z64, cos16, cos16, z32], axis=1),
            jnp.concatenate([z64, -sin16, sin16, z32], axis=1))


def _proj_kernel(x_ref, mod_ref, n1_ref, cos_ref, sin_ref, wh_ref, wg_ref, wc_ref,
                 qn_ref, wqa_ref, wqs_ref, kn_ref, wk_ref, wv_ref, one_ref,
                 hq_ref, hf_ref, hi_ref, hgate_ref, gsig_ref, q_ref, k_ref, v_ref):
    x = x_ref[...]
    mod = mod_ref[0]
    h = _rms(x, n1_ref[...]) * (1.0 + mod[1:2]) + mod[0:1]
    hb = h.astype(BF16)

    ph = _dot(hb, wh_ref[...])
    w = HG_WIDTH
    hq_ref[...] = ph[:, 0:w].astype(BF16)
    hf_ref[...] = ph[:, w:2 * w]
    hi_ref[...] = ph[:, 2 * w:3 * w].astype(BF16)
    hgate_ref[...] = ph[:, 3 * w:4 * w].astype(BF16)

    gsig_ref[...] = _sigmoid(_dot(hb, wg_ref[...])).astype(BF16)

    pc = _dot(hb, wc_ref[...])
    cq = pc[:, 0:MLA_Q_LORA]
    ckv = pc[:, MLA_Q_LORA:MLA_Q_LORA + MLA_KV_LORA]
    kra = pc[:, MLA_Q_LORA + MLA_KV_LORA:MLA_Q_LORA + MLA_KV_LORA + HEAD_PAD]
    krb = pc[:, MLA_Q_LORA + MLA_KV_LORA + HEAD_PAD:]

    cos_t = cos_ref[...]
    sin_t = sin_ref[...]
    lane = lax.broadcasted_iota(jnp.int32, cos_t.shape, 1)
    scale = MLA_DQK ** -0.5 * LOG2_E
    cq_tab = jnp.tile(scale * (cos_t + jnp.where(lane < MLA_NOPE, 1.0, 0.0)), (1, MLA_HEADS))
    sq_tab = jnp.tile(scale * sin_t, (1, MLA_HEADS))

    cqn = _rms(cq, qn_ref[...]).astype(BF16)
    q = _dot(cqn, wqa_ref[...]) * cq_tab + _dot(cqn, wqs_ref[...]) * sq_tab
    q_ref[...] = q.astype(BF16)

    ckvn = _rms(ckv, kn_ref[...]).astype(BF16)
    kpe = kra * cos_t + krb * sin_t
    k_ref[...] = (_dot(ckvn, wk_ref[...]) + jnp.tile(kpe, (1, MLA_HEADS))).astype(BF16)
    v_ref[...] = (_dot(ckvn, wv_ref[...]) + one_ref[...]).astype(BF16)


def _pad_heads(w, lo, hi, at):
    k, nh, _ = w.shape
    out = jnp.zeros((k, nh, HEAD_PAD), w.dtype)
    out = out.at[:, :, at:at + (hi - lo)].set(w[:, :, lo:hi])
    return out


def _input_projection(x2, mod_l, n1, cos_t, sin_t, w_in, q_norm_g, w_q_up, kv_norm_g, w_kv_up, seq):
    t, d = x2.shape
    tm = 256
    splits = [0]
    for s in IN_SIZES:
        splits.append(splits[-1] + s)
    kr = w_in[:, splits[6]:splits[7]]
    half = MLA_ROPE // 2
    z64 = jnp.zeros((d, MLA_NOPE), F32)
    z32 = jnp.zeros((d, HEAD_PAD - MLA_NOPE - MLA_ROPE), F32)
    kr_a = jnp.concatenate([z64, kr, z32], axis=1)
    kr_b = jnp.concatenate([z64, kr[:, half:], kr[:, :half], z32], axis=1)
    w_h = w_in[:, splits[0]:splits[4]].astype(BF16)
    w_g = w_in[:, splits[7]:splits[8]].astype(BF16)
    w_c = jnp.concatenate([w_in[:, splits[4]:splits[6]], kr_a, kr_b], axis=1).astype(BF16)

    wq = w_q_up.reshape(MLA_Q_LORA, MLA_HEADS, MLA_DQK)
    wq_all = _pad_heads(wq, 0, MLA_DQK, 0).reshape(MLA_Q_LORA, -1).astype(BF16)
    wq_swap = (_pad_heads(wq, MLA_NOPE + half, MLA_DQK, MLA_NOPE)
               + _pad_heads(wq, MLA_NOPE, MLA_NOPE + half, MLA_NOPE + half))
    wq_swap = wq_swap.reshape(MLA_Q_LORA, -1).astype(BF16)
    wkv = w_kv_up.reshape(MLA_KV_LORA, MLA_HEADS, MLA_NOPE + MLA_V)
    wk_all = _pad_heads(wkv, 0, MLA_NOPE, 0).reshape(MLA_KV_LORA, -1).astype(BF16)
    wv_all = _pad_heads(wkv, MLA_NOPE, MLA_NOPE + MLA_V, 0).reshape(MLA_KV_LORA, -1).astype(BF16)
    ones_row = jnp.tile(jnp.zeros((HEAD_PAD,), F32).at[MLA_V].set(1.0), MLA_HEADS)[None, :]

    hp = MLA_HEADS * HEAD_PAD
    rows = lambda n: pl.BlockSpec((tm, n), lambda i: (i, 0))
    outs = [(HG_WIDTH, BF16), (HG_WIDTH, F32), (HG_WIDTH, BF16), (HG_WIDTH, BF16),
            (2 * d, BF16), (hp, BF16), (hp, BF16), (hp, BF16)]
    return pl.pallas_call(
        _proj_kernel,
        out_shape=tuple(jax.ShapeDtypeStruct((t, n), dt) for n, dt in outs),
        grid=(t // tm,),
        in_specs=[rows(d),
                  pl.BlockSpec((1, 6, d), lambda i: ((i * tm) // seq, 0, 0)),
                  _resident((1, d)),
                  rows(HEAD_PAD), rows(HEAD_PAD),
                  _resident(w_h.shape), _resident(w_g.shape), _resident(w_c.shape),
                  _resident((1, MLA_Q_LORA)), _resident(wq_all.shape), _resident(wq_swap.shape),
                  _resident((1, MLA_KV_LORA)), _resident(wk_all.shape), _resident(wv_all.shape),
                  _resident((1, hp))],
        out_specs=tuple(rows(n) for n, _ in outs),
        compiler_params=_params("parallel"),
        name="norm_input_projection",
    )(x2, mod_l, n1[None, :], cos_t, sin_t, w_h, w_g, w_c,
      q_norm_g[None, :], wq_all, wq_swap, kv_norm_g[None, :], wk_all, wv_all, ones_row)


def _hgrn_kernel(lbl_ref, gn_ref, q_ref, f_ref, v_ref, gate_ref, o_ref, st_ref, *, layer, n_chunks):
    @pl.when(pl.program_id(1) == 0)
    def _():
        st_ref[...] = jnp.zeros_like(st_ref)

    lg = lbl_ref[...]
    ex = jnp.exp(lg - jnp.max(lg, axis=0, keepdims=True))
    soft = ex / jnp.sum(ex, axis=0, keepdims=True)
    lb_all = jnp.zeros_like(soft[0:1])
    for i in range(1, layer + 1):
        lb_all = lb_all + soft[i:i + 1]

    r_i = lax.broadcasted_iota(jnp.int32, (CHUNK, CHUNK), 0)
    c_i = lax.broadcasted_iota(jnp.int32, (CHUNK, CHUNK), 1)
    tri = jnp.where(r_i >= c_i, 1.0, 0.0).astype(BF16)
    level_masks = []
    for hs in HG_LEVELS:
        same = (r_i // (2 * hs)) == (c_i // (2 * hs))
        level_masks.append(jnp.logical_and(same, jnp.logical_and(r_i % (2 * hs) >= hs, c_i % (2 * hs) < hs)))
    sub_row = lax.broadcasted_iota(jnp.int32, (HG_SUB, 1), 0)
    n_sub = CHUNK // HG_SUB

    def chunk_body(ci, carry):
        r0 = pl.multiple_of(ci * CHUNK, CHUNK)
        rows = pl.ds(r0, CHUNK)
        for h in range(HG_HEADS):
            sl = slice(h * HG_DK, (h + 1) * HG_DK)
            lb = lb_all[:, sl]
            q = q_ref[rows, sl].astype(F32)
            vb = v_ref[rows, sl]
            v = vb.astype(F32)
            f = lb + (1.0 - lb) * _sigmoid(f_ref[rows, sl])
            g = jnp.log2(f)
            k = 1.0 - f
            qf = _silu(q) * (HG_DK ** -0.5)
            g_hi, g_lo = _split_bf16(g)
            b = _dot(tri, g_hi) + _dot(tri, g_lo)

            att = jnp.zeros((CHUNK, CHUNK), F32)
            for hs, mask in zip(HG_LEVELS, level_masks):
                ref = jnp.concatenate(
                    [jnp.broadcast_to(b[j + hs - 1:j + hs], (2 * hs, HG_DK)) for j in range(0, CHUNK, 2 * hs)],
                    axis=0)
                qt = (qf * jnp.exp2(b - ref)).astype(BF16)
                kt = (k * jnp.exp2(ref - b)).astype(BF16)
                att = att + jnp.where(mask, _dot_nt(qt, kt), 0.0)
            o = _dot(att.astype(BF16), vb)

            blocks = []
            for i in range(n_sub):
                rs = slice(i * HG_SUB, (i + 1) * HG_SUB)
                b_i, q_i, k_i, v_i = b[rs], qf[rs], k[rs], v[rs]
                acc = jnp.zeros((HG_SUB, HG_DV), F32)
                for s in range(HG_SUB):
                    a = jnp.sum(jnp.exp2(b_i - b_i[s:s + 1]) * (q_i * k_i[s:s + 1]), axis=1, keepdims=True)
                    acc = acc + jnp.where(sub_row >= s, a, 0.0) * v_i[s:s + 1]
                blocks.append(acc)
            o = o + jnp.concatenate(blocks, axis=0)

            st = st_ref[h]
            o = o + _dot_nt((qf * jnp.exp2(b)).astype(BF16), st.astype(BF16))
            b_last = b[CHUNK - 1:CHUNK]
            kd = (k * jnp.exp2(b_last - b)).astype(BF16)
            st_ref[h] = st * jnp.exp2(b_last) + _dot_tn(vb, kd)

            gt = gate_ref[rows, sl].astype(F32)
            o_ref[rows, sl] = (_rms(o, gn_ref[:, sl]) * _silu(gt)).astype(BF16)
        return carry

    lax.fori_loop(0, n_chunks, chunk_body, 0, unroll=2)


def _hgrn(hg_lb_logits, hg_norm_g, hq, hf, hi, hgate, layer, batch, seq):
    t = hq.shape[0]
    lc = min(512, seq)
    nb = seq // lc
    rows = pl.BlockSpec((lc, HG_WIDTH), lambda b, j: (b * nb + j, 0))
    kern = functools.partial(_hgrn_kernel, layer=layer, n_chunks=lc // CHUNK)
    return pl.pallas_call(
        kern,
        out_shape=jax.ShapeDtypeStruct((t, HG_WIDTH), BF16),
        grid=(batch, nb),
        in_specs=[pl.BlockSpec(hg_lb_logits.shape, lambda b, j: (0, 0)),
                  pl.BlockSpec((1, HG_WIDTH), lambda b, j: (0, 0)),
                  rows, rows, rows, rows],
        out_specs=rows,
        scratch_shapes=[pltpu.VMEM((HG_HEADS, HG_DV, HG_DK), F32)],
        compiler_params=_params("parallel", "arbitrary"),
        name="hgrn2_chunkwise",
    )(hg_lb_logits, hg_norm_g[None, :], hq, hf, hi, hgate)


ATTN_BLOCK = 1024
STEP_FULL, STEP_TOP_DIAG, STEP_BOTTOM_DIAG = range(3)


def _attn_kernel(qi_ref, kj_ref, kind_ref, q_ref, k_ref, v_ref, o_ref, m_ref, acc_ref, *, tb):
    del qi_ref, kj_ref
    p_id = pl.program_id(1)
    kind = kind_ref[p_id]
    half = tb // 2

    def reset():
        m_ref[...] = jnp.full_like(m_ref, -jnp.inf)
        acc_ref[...] = jnp.zeros_like(acc_ref)

    def step(r0, nr, nk, diagonal):
        rows = slice(r0, r0 + nr)
        if diagonal:
            qc = (lax.broadcasted_iota(jnp.int32, (nr, nk), 0) + r0) // CHUNK
            kc = lax.broadcasted_iota(jnp.int32, (nr, nk), 1) // CHUNK
            visible = kc <= qc
        for h in range(MLA_HEADS):
            sl = slice(h * HEAD_PAD, (h + 1) * HEAD_PAD)
            s = _dot_nt(q_ref[rows, sl], k_ref[0:nk, sl])
            if diagonal:
                s = jnp.where(visible, s, -jnp.inf)
            tiles = [s[:, t * HEAD_PAD:(t + 1) * HEAD_PAD] for t in range(nk // HEAD_PAD)]
            m_tile = tiles[0]
            for tl in tiles[1:]:
                m_tile = jnp.maximum(m_tile, tl)
            m_prev = m_ref[h, rows]
            m_new = jnp.maximum(m_prev, jnp.max(m_tile, axis=1, keepdims=True))
            alpha = jnp.exp2(m_prev - m_new)
            p = jnp.concatenate([jnp.exp2((tl - m_new).astype(BF16)) for tl in tiles], axis=1)
            acc_ref[h, rows] = acc_ref[h, rows] * alpha + _dot(p, v_ref[0:nk, sl])
            m_ref[h, rows] = m_new

    @pl.when(p_id == 0)
    def _():
        reset()

    @pl.when(kind == STEP_FULL)
    def _():
        step(0, tb, tb, False)

    @pl.when(kind == STEP_TOP_DIAG)
    def _():
        step(0, half, half, True)

    @pl.when(kind == STEP_BOTTOM_DIAG)
    def _():
        step(half, half, tb, True)
        outs = []
        for h in range(MLA_HEADS):
            acc = acc_ref[h]
            outs.append(acc[:, 0:MLA_V] / acc[:, MLA_V:MLA_V + 1])
        o_ref[...] = jnp.concatenate(outs, axis=1).astype(BF16)
        reset()


def _attention(q, k, v, batch, seq):
    t, hp = q.shape
    tb = min(ATTN_BLOCK, seq)
    nb = seq // tb
    steps = []
    for i in range(nb):
        steps += [(i, j, STEP_FULL) for j in range(i)]
        steps += [(i, i, STEP_TOP_DIAG), (i, i, STEP_BOTTOM_DIAG)]
    q_of, k_of, kinds = (jnp.asarray([st[c] for st in steps], jnp.int32) for c in range(3))
    qspec = pl.BlockSpec((tb, hp), lambda b, p, qi, kj, kd: (b * nb + qi[p], 0))
    kspec = pl.BlockSpec((tb, hp), lambda b, p, qi, kj, kd: (b * nb + kj[p], 0))
    return pl.pallas_call(
        functools.partial(_attn_kernel, tb=tb),
        out_shape=jax.ShapeDtypeStruct((t, MLA_WIDTH), BF16),
        grid_spec=pltpu.PrefetchScalarGridSpec(
            num_scalar_prefetch=3,
            grid=(batch, len(steps)),
            in_specs=[qspec, kspec, kspec],
            out_specs=pl.BlockSpec((tb, MLA_WIDTH), lambda b, p, qi, kj, kd: (b * nb + qi[p], 0)),
            scratch_shapes=[pltpu.VMEM((MLA_HEADS, tb, HEAD_PAD), F32),
                            pltpu.VMEM((MLA_HEADS, tb, HEAD_PAD), F32)]),
        compiler_params=_params("parallel", "arbitrary"),
        name="mla_flash_attention",
    )(q_of, k_of, kinds, q, k, v)


PAIR_ORDER = ((0, 1), (0, 2), (0, 3), (1, 3), (1, 2), (2, 3))
PAIRS_PER_GROUP = len(PAIR_ORDER)
N_CLASSES = N_GROUPS * PAIRS_PER_GROUP
LANES = 128
ROW_CHUNKS = D_MODEL // LANES
TOKEN_TILE = 1024
COLLECT_TILE = 512
EXPERT_TILE = 256


N_CLASS_ROWS = 32


def _route(scores, bias):
    biased = scores + bias
    col = [biased[e:e + 1, :] for e in range(N_EXPERTS)]
    gscore = []
    for g in range(N_GROUPS):
        a, b, c, d = col[g * EXPERTS_PER_GROUP:(g + 1) * EXPERTS_PER_GROUP]
        gscore.append(jnp.maximum(jnp.maximum(jnp.maximum(a + b, a + c), jnp.maximum(a + d, b + c)),
                                  jnp.maximum(b + d, c + d)))
    sel = []
    for g in range(N_GROUPS):
        ok = None
        for o in range(N_GROUPS):
            if o == g:
                continue
            cond = (gscore[g] > gscore[o]) if o < g else (gscore[g] >= gscore[o])
            ok = cond if ok is None else jnp.logical_and(ok, cond)
        grp = col[g * EXPERTS_PER_GROUP:(g + 1) * EXPERTS_PER_GROUP]
        for e in range(EXPERTS_PER_GROUP):
            beaten = jnp.zeros_like(grp[e])
            for o in range(EXPERTS_PER_GROUP):
                if o == e:
                    continue
                ahead = (grp[o] >= grp[e]) if o < e else (grp[o] > grp[e])
                beaten = beaten + jnp.where(ahead, 1.0, 0.0)
            sel.append(jnp.logical_and(ok, beaten < 1.5))
    lo = jnp.full_like(col[0], float(N_EXPERTS))
    hi = jnp.full_like(col[0], -1.0)
    for e in range(N_EXPERTS):
        lo = jnp.where(sel[e], jnp.minimum(lo, float(e)), lo)
        hi = jnp.where(sel[e], jnp.maximum(hi, float(e)), hi)
    return lo, hi


def _store_rows(ref, x):
    n = x.shape[0]
    for c in range(ROW_CHUNKS):
        ref[pl.ds(c, n, stride=ROW_CHUNKS), :] = x[:, c * LANES:(c + 1) * LANES]


def _load_rows(ref, n):
    return jnp.concatenate([ref[pl.ds(c, n, stride=ROW_CHUNKS), :] for c in range(ROW_CHUNKS)], axis=1)


def _merge_kernel(x_ref, oa_ref, ob_ref, g_ref, mod_ref, wa_ref, wb_ref, wo_ref, n2_ref,
                  wrt_ref, rb_ref, xn_ref, row_ref, meta_ref):
    d = D_MODEL
    tm = x_ref.shape[0]
    mod = mod_ref[0]
    ya = _dot(oa_ref[...], wa_ref[...])
    yb = _dot(ob_ref[...], wb_ref[...])
    g = g_ref[...].astype(F32)
    merged = g[:, 0:d] * ya + g[:, d:2 * d] * yb
    xn = x_ref[...] + mod[2:3] * _dot(merged.astype(BF16), wo_ref[...])
    xn_ref[...] = xn
    h2 = _rms(xn, n2_ref[...]) * (1.0 + mod[4:5]) + mod[3:4]
    _store_rows(row_ref, h2)

    h_hi, h_lo = _split_bf16(h2)
    w_hi, w_lo = _split_bf16(wrt_ref[...])
    scores = _sigmoid(_dot_nt(w_hi, h_hi) + _dot_nt(w_hi, h_lo) + _dot_nt(w_lo, h_hi))
    lo, hi = _route(scores, rb_ref[...])

    grp = jnp.floor(lo * (1.0 / EXPERTS_PER_GROUP))
    a = lo - grp * EXPERTS_PER_GROUP
    b = hi - grp * EXPERTS_PER_GROUP
    pair = jnp.zeros_like(a)
    for p, (p_lo, p_hi) in enumerate(PAIR_ORDER):
        pair = pair + jnp.where(jnp.logical_and(a == p_lo, b == p_hi), float(p), 0.0)
    cls = grp * PAIRS_PER_GROUP + pair
    c_row = lax.broadcasted_iota(jnp.int32, (N_CLASS_ROWS, tm), 0).astype(F32)
    onehot = jnp.where(c_row == cls, 1.0, 0.0)
    r_i = lax.broadcasted_iota(jnp.int32, (tm, tm), 0)
    c_i = lax.broadcasted_iota(jnp.int32, (tm, tm), 1)
    earlier = jnp.where(r_i < c_i, 1.0, 0.0).astype(BF16)
    rank = jnp.sum(onehot * _dot(onehot.astype(BF16), earlier), axis=0, keepdims=True)
    m_row = lax.broadcasted_iota(jnp.int32, (8, tm), 0)
    meta_ref[...] = jnp.where(m_row == 0, cls, jnp.where(m_row == 1, rank, 0.0))


def _merge(x2, oa, ob, gsig, mod_l, w_br_a, w_br_b, w_out, n2, w_router, router_bias, seq):
    t, d = x2.shape
    tm = TOKEN_TILE
    rows = lambda n: pl.BlockSpec((tm, n), lambda i: (i, 0))
    return pl.pallas_call(
        _merge_kernel,
        out_shape=(jax.ShapeDtypeStruct((t, d), F32), jax.ShapeDtypeStruct((t * ROW_CHUNKS, LANES), F32),
                   jax.ShapeDtypeStruct((8, t), F32)),
        grid=(t // tm,),
        in_specs=[rows(d), rows(HG_WIDTH), rows(MLA_WIDTH), rows(2 * d),
                  pl.BlockSpec((1, 6, d), lambda i: ((i * tm) // seq, 0, 0)),
                  _resident(w_br_a.shape), _resident(w_br_b.shape), _resident(w_out.shape),
                  _resident((1, d)), _resident((N_EXPERTS, d)), _resident((N_EXPERTS, 1))],
        out_specs=(rows(d), pl.BlockSpec((tm * ROW_CHUNKS, LANES), lambda i: (i, 0)),
                   pl.BlockSpec((8, tm), lambda i: (0, i))),
        compiler_params=_params("parallel"),
        name="merge_outproj_router",
    )(x2, oa, ob, gsig, mod_l, w_br_a.astype(BF16), w_br_b.astype(BF16), w_out.astype(BF16),
      n2[None, :], w_router.T, router_bias[:, None])


def _dispatch_plan(meta, n_tiles_e):
    t = meta.shape[1]
    n_tok_tiles = t // TOKEN_TILE
    cls = meta[0].astype(jnp.int32)
    rank = meta[1].astype(jnp.int32)
    own = cls.reshape(n_tok_tiles, TOKEN_TILE, 1) == jnp.arange(N_CLASSES, dtype=jnp.int32)
    counts = jnp.sum(own.astype(jnp.int32), axis=1)
    total = jnp.sum(counts, axis=0)
    total_pad = (total + EXPERT_TILE - 1) // EXPERT_TILE * EXPERT_TILE
    ends = jnp.cumsum(total_pad)
    base = (ends - total_pad)[None, :] + jnp.cumsum(counts, axis=0) - counts
    pos = jnp.sum(jnp.where(own, base[:, None, :], 0), axis=2).reshape(t) + rank

    tile_start = jnp.arange(n_tiles_e, dtype=jnp.int32) * EXPERT_TILE
    n_valid = ends[-1] // EXPERT_TILE
    tile_cls = jnp.sum((tile_start[:, None] >= ends[None, :]).astype(jnp.int32), axis=1)
    last_cls = jnp.take(tile_cls, n_valid - 1)
    valid = jnp.arange(n_tiles_e, dtype=jnp.int32) < n_valid
    tile_cls = jnp.where(valid, tile_cls, last_cls)
    pair_lo = jnp.asarray([p[0] for p in PAIR_ORDER], jnp.int32)
    pair_hi = jnp.asarray([p[1] for p in PAIR_ORDER], jnp.int32)
    grp = tile_cls // PAIRS_PER_GROUP
    e_lo = grp * EXPERTS_PER_GROUP + jnp.take(pair_lo, tile_cls % PAIRS_PER_GROUP)
    e_hi = grp * EXPERTS_PER_GROUP + jnp.take(pair_hi, tile_cls % PAIRS_PER_GROUP)
    pad_tile = (ends // EXPERT_TILE - 1).astype(jnp.int32)
    used = jnp.concatenate([(total > 0).astype(jnp.int32), n_valid[None].astype(jnp.int32)])
    return (pos * ROW_CHUNKS).astype(jnp.int32), pad_tile, used, e_lo, e_hi, valid.astype(jnp.int32)


DMA_UNROLL = 8


def _row_copies(n_rows, make_copy):
    def issue(ui, carry):
        for u in range(DMA_UNROLL):
            make_copy(ui * DMA_UNROLL + u).start(priority=u % 2)
        return carry
    lax.fori_loop(0, n_rows // DMA_UNROLL, issue, 0)


def _token_slab(ref, first_row):
    return ref.at[pl.ds(pl.multiple_of(first_row, ROW_CHUNKS), ROW_CHUNKS)]


def _dispatch_kernel(pos_ref, pad_tile_ref, used_ref, rows_ref, dst_hbm, zero_ref, sem, zero_sem):
    tm = rows_ref.shape[0] // ROW_CHUNKS
    tile_rows = EXPERT_TILE * ROW_CHUNKS

    @pl.when(pl.program_id(0) == 0)
    def _():
        zero_ref[...] = jnp.zeros_like(zero_ref)
        n_tiles = dst_hbm.shape[0] // tile_rows

        def clear(tile):
            first = pl.multiple_of(tile * tile_rows, tile_rows)
            return pltpu.make_async_copy(zero_ref, dst_hbm.at[pl.ds(first, tile_rows)], zero_sem)

        def start_tail(tile, carry):
            clear(tile).start()
            return carry

        def wait_tail(tile, carry):
            clear(tile).wait()
            return carry
        for c in range(N_CLASSES):
            @pl.when(used_ref[c] != 0)
            def _():
                clear(pad_tile_ref[c]).start()
        lax.fori_loop(used_ref[N_CLASSES], n_tiles, start_tail, 0)
        for c in range(N_CLASSES):
            @pl.when(used_ref[c] != 0)
            def _():
                clear(pad_tile_ref[c]).wait()
        lax.fori_loop(used_ref[N_CLASSES], n_tiles, wait_tail, 0)

    base = pl.program_id(0) * tm
    _row_copies(tm, lambda r: pltpu.make_async_copy(
        _token_slab(rows_ref, r * ROW_CHUNKS), _token_slab(dst_hbm, pos_ref[base + r]), sem))
    pltpu.make_async_copy(rows_ref, dst_hbm.at[pl.ds(0, tm * ROW_CHUNKS)], sem).wait()


def _dispatch_rows(pos, pad_tile, used, rows, n_dst):
    t = rows.shape[0] // ROW_CHUNKS
    tm = TOKEN_TILE
    return pl.pallas_call(
        _dispatch_kernel,
        out_shape=jax.ShapeDtypeStruct((n_dst * ROW_CHUNKS, LANES), rows.dtype),
        grid_spec=pltpu.PrefetchScalarGridSpec(
            num_scalar_prefetch=3, grid=(t // tm,),
            in_specs=[pl.BlockSpec((tm * ROW_CHUNKS, LANES), lambda i, p, pt, us: (i, 0))],
            out_specs=pl.BlockSpec(memory_space=pl.ANY),
            scratch_shapes=[pltpu.VMEM((EXPERT_TILE * ROW_CHUNKS, LANES), rows.dtype),
                            pltpu.SemaphoreType.DMA, pltpu.SemaphoreType.DMA]),
        compiler_params=pltpu.CompilerParams(dimension_semantics=("arbitrary",), disable_bounds_checks=True,
                                             has_side_effects=True),
        name="dispatch_rows",
    )(pos, pad_tile, used, rows)


def _moe_kernel(elo_ref, ehi_ref, valid_ref, x_ref, wrt_ref, wgl_ref, wul_ref, wdl_ref, wgh_ref, wuh_ref,
                wdh_ref, y_ref):
    j = pl.program_id(0)

    @pl.when(valid_ref[j] == 0)
    def _():
        y_ref[...] = jnp.zeros_like(y_ref)

    @pl.when(valid_ref[j] != 0)
    def _():
        h2 = _load_rows(x_ref, EXPERT_TILE)
        s_lo = _sigmoid(jnp.sum(h2 * wrt_ref[pl.ds(elo_ref[j], 1), :], axis=1, keepdims=True))
        s_hi = _sigmoid(jnp.sum(h2 * wrt_ref[pl.ds(ehi_ref[j], 1), :], axis=1, keepdims=True))
        total = s_lo + s_hi
        hb = h2.astype(BF16)
        he_lo = _silu(_dot(hb, wgl_ref[0, 0])) * _dot(hb, wul_ref[0, 0])
        y = (s_lo / total) * _dot(he_lo.astype(BF16), wdl_ref[0, 0])
        he_hi = _silu(_dot(hb, wgh_ref[0, 0])) * _dot(hb, wuh_ref[0, 0])
        _store_rows(y_ref, y + (s_hi / total) * _dot(he_hi.astype(BF16), wdh_ref[0, 0]))


def _moe(rows_sorted, e_lo, e_hi, valid, w_router, wg, wu, wd, layer):
    n_pad = rows_sorted.shape[0] // ROW_CHUNKS
    d = D_MODEL
    lo4 = lambda j, el, eh, va: (layer, el[j], 0, 0)
    hi4 = lambda j, el, eh, va: (layer, eh[j], 0, 0)
    tile = pl.BlockSpec((EXPERT_TILE * ROW_CHUNKS, LANES), lambda j, el, eh, va: (j, 0))
    return pl.pallas_call(
        _moe_kernel,
        out_shape=jax.ShapeDtypeStruct(rows_sorted.shape, F32),
        grid_spec=pltpu.PrefetchScalarGridSpec(
            num_scalar_prefetch=3, grid=(n_pad // EXPERT_TILE,),
            in_specs=[tile, _resident((N_EXPERTS, d)),
                      pl.BlockSpec((1, 1, d, D_EXPERT), lo4), pl.BlockSpec((1, 1, d, D_EXPERT), lo4),
                      pl.BlockSpec((1, 1, D_EXPERT, d), lo4),
                      pl.BlockSpec((1, 1, d, D_EXPERT), hi4), pl.BlockSpec((1, 1, d, D_EXPERT), hi4),
                      pl.BlockSpec((1, 1, D_EXPERT, d), hi4)],
            out_specs=tile),
        compiler_params=_params("arbitrary"),
        name="moe_experts",
    )(e_lo, e_hi, valid, rows_sorted, w_router.T, wg, wu, wd, wg, wu, wd)


def _collect_kernel(pos_ref, x_ref, ys_hbm, mod_ref, fg_ref, o_ref, buf0, buf1, sems, *, final):
    tm = x_ref.shape[0]
    i = pl.program_id(0)
    n = pl.num_programs(0)
    bufs = (buf0, buf1)

    def fetch(tile, slot):
        base = tile * tm
        _row_copies(tm, lambda r: pltpu.make_async_copy(
            _token_slab(ys_hbm, pos_ref[base + r]), _token_slab(bufs[slot], r * ROW_CHUNKS), sems.at[slot]))

    def finish(slot):
        pltpu.make_async_copy(ys_hbm.at[pl.ds(0, tm * ROW_CHUNKS)], bufs[slot], sems.at[slot]).wait()
        xo = x_ref[...] + mod_ref[0][5:6] * _load_rows(bufs[slot], tm)
        if final:
            xo = _rms(xo, fg_ref[...])
        o_ref[...] = xo

    @pl.when(i == 0)
    def _():
        fetch(0, 0)

    for slot in range(2):
        @pl.when(jnp.logical_and(i + 1 < n, (i + 1) % 2 == slot))
        def _():
            fetch(i + 1, slot)

    for slot in range(2):
        @pl.when(i % 2 == slot)
        def _():
            finish(slot)


def _collect_residual(pos, xn, y_sorted, mod_l, final_g, final, seq):
    t, d = xn.shape
    tm = COLLECT_TILE
    rows = pl.BlockSpec((tm, d), lambda i, p: (i, 0))
    return pl.pallas_call(
        functools.partial(_collect_kernel, final=final),
        out_shape=jax.ShapeDtypeStruct((t, d), F32),
        grid_spec=pltpu.PrefetchScalarGridSpec(
            num_scalar_prefetch=1, grid=(t // tm,),
            in_specs=[rows, pl.BlockSpec(memory_space=pl.ANY),
                      pl.BlockSpec((1, 6, d), lambda i, p: ((i * tm) // seq, 0, 0)),
                      pl.BlockSpec((1, d), lambda i, p: (0, 0))],
            out_specs=rows,
            scratch_shapes=[pltpu.VMEM((tm * ROW_CHUNKS, LANES), F32), pltpu.VMEM((tm * ROW_CHUNKS, LANES), F32),
                            pltpu.SemaphoreType.DMA((2,))]),
        compiler_params=pltpu.CompilerParams(dimension_semantics=("arbitrary",), disable_bounds_checks=True,
                                             vmem_limit_bytes=V7X_VMEM_LIMIT_BYTES),
        name="collect_residual",
    )(pos, xn, y_sorted, mod_l, final_g[None, :])


def kernel(x, c, positions, ada_w, ada_b, norm1_g, w_in, hg_lb_logits, hg_norm_g, q_norm_g, w_q_up,
           kv_norm_g, w_kv_up, w_br_a, w_br_b, w_out, norm2_g, w_router, router_bias, w_gate, w_up,
           w_down, final_g):
    batch, seq, d = x.shape
    depth = ada_w.shape[0]
    t = batch * seq
    n_tiles_e = (t + N_CLASSES * (EXPERT_TILE - 1)) // EXPERT_TILE
    mod = _modulation(c, ada_w, ada_b).reshape(depth, batch, 6, d)
    cos_t, sin_t = _rope_tables(positions)
    x2 = x.reshape(t, d)
    wg_all, wu_all, wd_all = w_gate.astype(BF16), w_up.astype(BF16), w_down.astype(BF16)
    for l in range(depth):
        hq, hf, hi, hgate, gsig, q, k, v = _input_projection(
            x2, mod[l], norm1_g[l], cos_t, sin_t, w_in[l], q_norm_g[l], w_q_up[l],
            kv_norm_g[l], w_kv_up[l], seq)
        oa = _hgrn(hg_lb_logits, hg_norm_g[l], hq, hf, hi, hgate, l, batch, seq)
        ob = _attention(q, k, v, batch, seq)
        xn, rows, meta = _merge(x2, oa, ob, gsig, mod[l], w_br_a[l], w_br_b[l], w_out[l], norm2_g[l],
                                w_router, router_bias, seq)
        pos, pad_tile, used, e_lo, e_hi, valid = _dispatch_plan(meta, n_tiles_e)
        rows_sorted = _dispatch_rows(pos, pad_tile, used, rows, n_tiles_e * EXPERT_TILE)
        y_sorted = _moe(rows_sorted, e_lo, e_hi, valid, w_router, wg_all, wu_all, wd_all, l)
        x2 = _collect_residual(pos, xn, y_sorted, mod[l], final_g, l == depth - 1, seq)
    return x2.reshape(batch, seq, d)
```

```python
import functools
import math

import jax
import jax.numpy as jnp
from jax import lax
from jax.experimental import pallas as pl
from jax.experimental.pallas import tpu as pltpu

F32 = jnp.float32
BF16 = jnp.bfloat16

D_MODEL = 1024
CHUNK = 64
EPS = 1e-6

HG_HEADS = 4
HG_DK = 128
HG_DV = 128
HG_WIDTH = HG_HEADS * HG_DV
HG_SUB = 8
HG_LEVELS = (8, 16, 32)

MLA_HEADS = 8
MLA_NOPE = 64
MLA_ROPE = 32
MLA_V = 64
MLA_Q_LORA = 384
MLA_KV_LORA = 256
MLA_DQK = MLA_NOPE + MLA_ROPE
MLA_WIDTH = MLA_HEADS * MLA_V
ROPE_BASE = 10000.0
HEAD_PAD = 128
LOG2_E = math.log2(math.e)

N_EXPERTS = 16
N_GROUPS = 4
EXPERTS_PER_GROUP = N_EXPERTS // N_GROUPS
D_EXPERT = 512

IN_SIZES = (HG_HEADS * HG_DK, HG_HEADS * HG_DK, HG_HEADS * HG_DV, HG_WIDTH,
            MLA_Q_LORA, MLA_KV_LORA, MLA_ROPE, 2 * D_MODEL)

V7X_VMEM_LIMIT_BYTES = 56 * 1024 * 1024


def _params(*sem):
    return pltpu.CompilerParams(dimension_semantics=sem, vmem_limit_bytes=V7X_VMEM_LIMIT_BYTES)


def _resident(shape):
    nd = len(shape)
    return pl.BlockSpec(shape, lambda *_: (0,) * nd, pipeline_mode=pl.Buffered(1))


def _sigmoid(x):
    return 1.0 / (1.0 + jnp.exp(-x))


def _silu(x):
    return x * _sigmoid(x)


def _dot(a, b):
    return jnp.dot(a, b, preferred_element_type=F32)


def _dot_nt(a, b):
    return lax.dot_general(a, b, (((1,), (1,)), ((), ())), preferred_element_type=F32)


def _dot_tn(a, b):
    return lax.dot_general(a, b, (((0,), (0,)), ((), ())), preferred_element_type=F32)


def _split_bf16(x):
    hi = x.astype(BF16)
    lo = (x - hi.astype(F32)).astype(BF16)
    return hi, lo


def _rms(x, g):
    return x * lax.rsqrt(jnp.mean(x * x, axis=-1, keepdims=True) + EPS) * g


def _mod_kernel(c_ref, w_ref, b_ref, o_ref):
    @pl.when(pl.program_id(1) == 0)
    def _():
        o_ref[0] = jnp.broadcast_to(b_ref[0], o_ref.shape[1:])
    ca = _silu(c_ref[...])
    o_ref[0] += _dot(ca.astype(BF16), w_ref[0].astype(BF16))


def _modulation(c, ada_w, ada_b):
    depth, d, n = ada_w.shape
    b = c.shape[0]
    tk = 256
    return pl.pallas_call(
        _mod_kernel,
        out_shape=jax.ShapeDtypeStruct((depth, b, n), F32),
        grid=(depth, d // tk),
        in_specs=[pl.BlockSpec((b, tk), lambda l, k: (0, k)),
                  pl.BlockSpec((1, tk, n), lambda l, k: (l, k, 0)),
                  pl.BlockSpec((1, 1, n), lambda l, k: (l, 0, 0))],
        out_specs=pl.BlockSpec((1, b, n), lambda l, k: (l, 0, 0)),
        compiler_params=_params("parallel", "arbitrary"),
        name="adaln_modulation",
    )(c, ada_w, ada_b.reshape(depth, 1, n))


def _rope_kernel(pos_ref, inv_ref, cos_ref, sin_ref):
    ang = pos_ref[...].astype(F32) * inv_ref[...]
    cos_ref[...] = jnp.cos(ang)
    sin_ref[...] = jnp.sin(ang)


def _rope_tables(positions):
    t = positions.size
    half = MLA_ROPE // 2
    per_row = HEAD_PAD // half
    inv = ROPE_BASE ** (-jnp.arange(half, dtype=F32) / half)
    pos_dense = jnp.repeat(positions.reshape(t // per_row, per_row), half, axis=1)
    dense = pl.BlockSpec((t // per_row, HEAD_PAD), lambda i: (0, 0))
    cos_d, sin_d = pl.pallas_call(
        _rope_kernel,
        out_shape=(jax.ShapeDtypeStruct((t // per_row, HEAD_PAD), F32),) * 2,
        grid=(1,),
        in_specs=[dense, pl.BlockSpec((1, HEAD_PAD), lambda i: (0, 0))],
        out_specs=(dense, dense),
        compiler_params=_params("arbitrary"),
        name="rope_tables",
    )(pos_dense, jnp.tile(inv, per_row)[None, :])
    cos16, sin16 = cos_d.reshape(t, half), sin_d.reshape(t, half)
    z64 = jnp.zeros((t, MLA_NOPE), F32)
    z32 = jnp.zeros((t, HEAD_PAD - MLA_NOPE - MLA_ROPE), F32)
    return (jnp.concatenate([z64, cos16, cos16, z32], axis=1),
            jnp.concatenate([z64, -sin16, sin16, z32], axis=1))


def _proj_kernel(x_ref, mod_ref, n1_ref, cos_ref, sin_ref, wh_ref, wg_ref, wc_ref,
                 qn_ref, wqa_ref, wqs_ref, kn_ref, wk_ref, wv_ref, one_ref,
                 hq_ref, hf_ref, hi_ref, hgate_ref, gsig_ref, q_ref, k_ref, v_ref):
    x = x_ref[...]
    mod = mod_ref[0]
    h = _rms(x, n1_ref[...]) * (1.0 + mod[1:2]) + mod[0:1]
    hb = h.astype(BF16)

    ph = _dot(hb, wh_ref[...])
    w = HG_WIDTH
    hq_ref[...] = ph[:, 0:w].astype(BF16)
    hf_ref[...] = ph[:, w:2 * w]
    hi_ref[...] = ph[:, 2 * w:3 * w].astype(BF16)
    hgate_ref[...] = ph[:, 3 * w:4 * w].astype(BF16)

    gsig_ref[...] = _sigmoid(_dot(hb, wg_ref[...])).astype(BF16)

    pc = _dot(hb, wc_ref[...])
    cq = pc[:, 0:MLA_Q_LORA]
    ckv = pc[:, MLA_Q_LORA:MLA_Q_LORA + MLA_KV_LORA]
    kra = pc[:, MLA_Q_LORA + MLA_KV_LORA:MLA_Q_LORA + MLA_KV_LORA + HEAD_PAD]
    krb = pc[:, MLA_Q_LORA + MLA_KV_LORA + HEAD_PAD:]

    cos_t = cos_ref[...]
    sin_t = sin_ref[...]
    lane = lax.broadcasted_iota(jnp.int32, cos_t.shape, 1)
    scale = MLA_DQK ** -0.5 * LOG2_E
    cq_tab = jnp.tile(scale * (cos_t + jnp.where(lane < MLA_NOPE, 1.0, 0.0)), (1, MLA_HEADS))
    sq_tab = jnp.tile(scale * sin_t, (1, MLA_HEADS))

    cqn = _rms(cq, qn_ref[...]).astype(BF16)
    q = _dot(cqn, wqa_ref[...]) * cq_tab + _dot(cqn, wqs_ref[...]) * sq_tab
    q_ref[...] = q.astype(BF16)

    ckvn = _rms(ckv, kn_ref[...]).astype(BF16)
    kpe = kra * cos_t + krb * sin_t
    k_ref[...] = (_dot(ckvn, wk_ref[...]) + jnp.tile(kpe, (1, MLA_HEADS))).astype(BF16)
    v_ref[...] = (_dot(ckvn, wv_ref[...]) + one_ref[...]).astype(BF16)


def _pad_heads(w, lo, hi, at):
    k, nh, _ = w.shape
    out = jnp.zeros((k, nh, HEAD_PAD), w.dtype)
    out = out.at[:, :, at:at + (hi - lo)].set(w[:, :, lo:hi])
    return out


def _input_projection(x2, mod_l, n1, cos_t, sin_t, w_in, q_norm_g, w_q_up, kv_norm_g, w_kv_up, seq):
    t, d = x2.shape
    tm = 256
    splits = [0]
    for s in IN_SIZES:
        splits.append(splits[-1] + s)
    kr = w_in[:, splits[6]:splits[7]]
    half = MLA_ROPE // 2
    z64 = jnp.zeros((d, MLA_NOPE), F32)
    z32 = jnp.zeros((d, HEAD_PAD - MLA_NOPE - MLA_ROPE), F32)
    kr_a = jnp.concatenate([z64, kr, z32], axis=1)
    kr_b = jnp.concatenate([z64, kr[:, half:], kr[:, :half], z32], axis=1)
    w_h = w_in[:, splits[0]:splits[4]].astype(BF16)
    w_g = w_in[:, splits[7]:splits[8]].astype(BF16)
    w_c = jnp.concatenate([w_in[:, splits[4]:splits[6]], kr_a, kr_b], axis=1).astype(BF16)

    wq = w_q_up.reshape(MLA_Q_LORA, MLA_HEADS, MLA_DQK)
    wq_all = _pad_heads(wq, 0, MLA_DQK, 0).reshape(MLA_Q_LORA, -1).astype(BF16)
    wq_swap = (_pad_heads(wq, MLA_NOPE + half, MLA_DQK, MLA_NOPE)
               + _pad_heads(wq, MLA_NOPE, MLA_NOPE + half, MLA_NOPE + half))
    wq_swap = wq_swap.reshape(MLA_Q_LORA, -1).astype(BF16)
    wkv = w_kv_up.reshape(MLA_KV_LORA, MLA_HEADS, MLA_NOPE + MLA_V)
    wk_all = _pad_heads(wkv, 0, MLA_NOPE, 0).reshape(MLA_KV_LORA, -1).astype(BF16)
    wv_all = _pad_heads(wkv, MLA_NOPE, MLA_NOPE + MLA_V, 0).reshape(MLA_KV_LORA, -1).astype(BF16)
    ones_row = jnp.tile(jnp.zeros((HEAD_PAD,), F32).at[MLA_V].set(1.0), MLA_HEADS)[None, :]

    hp = MLA_HEADS * HEAD_PAD
    rows = lambda n: pl.BlockSpec((tm, n), lambda i: (i, 0))
    outs = [(HG_WIDTH, BF16), (HG_WIDTH, F32), (HG_WIDTH, BF16), (HG_WIDTH, BF16),
            (2 * d, BF16), (hp, BF16), (hp, BF16), (hp, BF16)]
    return pl.pallas_call(
        _proj_kernel,
        out_shape=tuple(jax.ShapeDtypeStruct((t, n), dt) for n, dt in outs),
        grid=(t // tm,),
        in_specs=[rows(d),
                  pl.BlockSpec((1, 6, d), lambda i: ((i * tm) // seq, 0, 0)),
                  _resident((1, d)),
                  rows(HEAD_PAD), rows(HEAD_PAD),
                  _resident(w_h.shape), _resident(w_g.shape), _resident(w_c.shape),
                  _resident((1, MLA_Q_LORA)), _resident(wq_all.shape), _resident(wq_swap.shape),
                  _resident((1, MLA_KV_LORA)), _resident(wk_all.shape), _resident(wv_all.shape),
                  _resident((1, hp))],
        out_specs=tuple(rows(n) for n, _ in outs),
        compiler_params=_params("parallel"),
        name="norm_input_projection",
    )(x2, mod_l, n1[None, :], cos_t, sin_t, w_h, w_g, w_c,
      q_norm_g[None, :], wq_all, wq_swap, kv_norm_g[None, :], wk_all, wv_all, ones_row)


def _hgrn_kernel(lbl_ref, gn_ref, q_ref, f_ref, v_ref, gate_ref, o_ref, st_ref, *, layer, n_chunks):
    @pl.when(pl.program_id(1) == 0)
    def _():
        st_ref[...] = jnp.zeros_like(st_ref)

    lg = lbl_ref[...]
    ex = jnp.exp(lg - jnp.max(lg, axis=0, keepdims=True))
    soft = ex / jnp.sum(ex, axis=0, keepdims=True)
    lb_all = jnp.zeros_like(soft[0:1])
    for i in range(1, layer + 1):
        lb_all = lb_all + soft[i:i + 1]

    r_i = lax.broadcasted_iota(jnp.int32, (CHUNK, CHUNK), 0)
    c_i = lax.broadcasted_iota(jnp.int32, (CHUNK, CHUNK), 1)
    tri = jnp.where(r_i >= c_i, 1.0, 0.0).astype(BF16)
    level_masks = []
    for hs in HG_LEVELS:
        same = (r_i // (2 * hs)) == (c_i // (2 * hs))
        level_masks.append(jnp.logical_and(same, jnp.logical_and(r_i % (2 * hs) >= hs, c_i % (2 * hs) < hs)))
    sub_row = lax.broadcasted_iota(jnp.int32, (HG_SUB, 1), 0)
    n_sub = CHUNK // HG_SUB

    def chunk_body(ci, carry):
        r0 = pl.multiple_of(ci * CHUNK, CHUNK)
        rows = pl.ds(r0, CHUNK)
        for h in range(HG_HEADS):
            sl = slice(h * HG_DK, (h + 1) * HG_DK)
            lb = lb_all[:, sl]
            q = q_ref[rows, sl].astype(F32)
            vb = v_ref[rows, sl]
            v = vb.astype(F32)
            f = lb + (1.0 - lb) * _sigmoid(f_ref[rows, sl])
            g = jnp.log2(f)
            k = 1.0 - f
            qf = _silu(q) * (HG_DK ** -0.5)
            g_hi, g_lo = _split_bf16(g)
            b = _dot(tri, g_hi) + _dot(tri, g_lo)

            att = jnp.zeros((CHUNK, CHUNK), F32)
            for hs, mask in zip(HG_LEVELS, level_masks):
                ref = jnp.concatenate(
                    [jnp.broadcast_to(b[j + hs - 1:j + hs], (2 * hs, HG_DK)) for j in range(0, CHUNK, 2 * hs)],
                    axis=0)
                qt = (qf * jnp.exp2(b - ref)).astype(BF16)
                kt = (k * jnp.exp2(ref - b)).astype(BF16)
                att = att + jnp.where(mask, _dot_nt(qt, kt), 0.0)
            o = _dot(att.astype(BF16), vb)

            blocks = []
            for i in range(n_sub):
                rs = slice(i * HG_SUB, (i + 1) * HG_SUB)
                b_i, q_i, k_i, v_i = b[rs], qf[rs], k[rs], v[rs]
                acc = jnp.zeros((HG_SUB, HG_DV), F32)
                for s in range(HG_SUB):
                    a = jnp.sum(jnp.exp2(b_i - b_i[s:s + 1]) * (q_i * k_i[s:s + 1]), axis=1, keepdims=True)
                    acc = acc + jnp.where(sub_row >= s, a, 0.0) * v_i[s:s + 1]
                blocks.append(acc)
            o = o + jnp.concatenate(blocks, axis=0)

            st = st_ref[h]
            o = o + _dot_nt((qf * jnp.exp2(b)).astype(BF16), st.astype(BF16))
            b_last = b[CHUNK - 1:CHUNK]
            kd = (k * jnp.exp2(b_last - b)).astype(BF16)
            st_ref[h] = st * jnp.exp2(b_last) + _dot_tn(vb, kd)

            gt = gate_ref[rows, sl].astype(F32)
            o_ref[rows, sl] = (_rms(o, gn_ref[:, sl]) * _silu(gt)).astype(BF16)
        return carry

    lax.fori_loop(0, n_chunks, chunk_body, 0, unroll=2)


def _hgrn(hg_lb_logits, hg_norm_g, hq, hf, hi, hgate, layer, batch, seq):
    t = hq.shape[0]
    lc = min(512, seq)
    nb = seq // lc
    rows = pl.BlockSpec((lc, HG_WIDTH), lambda b, j: (b * nb + j, 0))
    kern = functools.partial(_hgrn_kernel, layer=layer, n_chunks=lc // CHUNK)
    return pl.pallas_call(
        kern,
        out_shape=jax.ShapeDtypeStruct((t, HG_WIDTH), BF16),
        grid=(batch, nb),
        in_specs=[pl.BlockSpec(hg_lb_logits.shape, lambda b, j: (0, 0)),
                  pl.BlockSpec((1, HG_WIDTH), lambda b, j: (0, 0)),
                  rows, rows, rows, rows],
        out_specs=rows,
        scratch_shapes=[pltpu.VMEM((HG_HEADS, HG_DV, HG_DK), F32)],
        compiler_params=_params("parallel", "arbitrary"),
        name="hgrn2_chunkwise",
    )(hg_lb_logits, hg_norm_g[None, :], hq, hf, hi, hgate)


ATTN_BLOCK = 1024
STEP_FULL, STEP_TOP_DIAG, STEP_BOTTOM_DIAG = range(3)


def _attn_kernel(qi_ref, kj_ref, kind_ref, q_ref, k_ref, v_ref, o_ref, m_ref, acc_ref, *, tb):
    del qi_ref, kj_ref
    p_id = pl.program_id(1)
    kind = kind_ref[p_id]
    half = tb // 2

    def reset():
        m_ref[...] = jnp.full_like(m_ref, -jnp.inf)
        acc_ref[...] = jnp.zeros_like(acc_ref)

    def step(r0, nr, nk, diagonal):
        rows = slice(r0, r0 + nr)
        if diagonal:
            qc = (lax.broadcasted_iota(jnp.int32, (nr, nk), 0) + r0) // CHUNK
            kc = lax.broadcasted_iota(jnp.int32, (nr, nk), 1) // CHUNK
            visible = kc <= qc
        for h in range(MLA_HEADS):
            sl = slice(h * HEAD_PAD, (h + 1) * HEAD_PAD)
            s = _dot_nt(q_ref[rows, sl], k_ref[0:nk, sl])
            if diagonal:
                s = jnp.where(visible, s, -jnp.inf)
            tiles = [s[:, t * HEAD_PAD:(t + 1) * HEAD_PAD] for t in range(nk // HEAD_PAD)]
            m_tile = tiles[0]
            for tl in tiles[1:]:
                m_tile = jnp.maximum(m_tile, tl)
            m_prev = m_ref[h, rows]
            m_new = jnp.maximum(m_prev, jnp.max(m_tile, axis=1, keepdims=True))
            alpha = jnp.exp2(m_prev - m_new)
            p = jnp.concatenate([jnp.exp2((tl - m_new).astype(BF16)) for tl in tiles], axis=1)
            acc_ref[h, rows] = acc_ref[h, rows] * alpha + _dot(p, v_ref[0:nk, sl])
            m_ref[h, rows] = m_new

    @pl.when(p_id == 0)
    def _():
        reset()

    @pl.when(kind == STEP_FULL)
    def _():
        step(0, tb, tb, False)

    @pl.when(kind == STEP_TOP_DIAG)
    def _():
        step(0, half, half, True)

    @pl.when(kind == STEP_BOTTOM_DIAG)
    def _():
        step(half, half, tb, True)
        outs = []
        for h in range(MLA_HEADS):
            acc = acc_ref[h]
            outs.append(acc[:, 0:MLA_V] / acc[:, MLA_V:MLA_V + 1])
        o_ref[...] = jnp.concatenate(outs, axis=1).astype(BF16)
        reset()


def _attention(q, k, v, batch, seq):
    t, hp = q.shape
    tb = min(ATTN_BLOCK, seq)
    nb = seq // tb
    steps = []
    for i in range(nb):
        steps += [(i, j, STEP_FULL) for j in range(i)]
        steps += [(i, i, STEP_TOP_DIAG), (i, i, STEP_BOTTOM_DIAG)]
    q_of, k_of, kinds = (jnp.asarray([st[c] for st in steps], jnp.int32) for c in range(3))
    qspec = pl.BlockSpec((tb, hp), lambda b, p, qi, kj, kd: (b * nb + qi[p], 0))
    kspec = pl.BlockSpec((tb, hp), lambda b, p, qi, kj, kd: (b * nb + kj[p], 0))
    return pl.pallas_call(
        functools.partial(_attn_kernel, tb=tb),
        out_shape=jax.ShapeDtypeStruct((t, MLA_WIDTH), BF16),
        grid_spec=pltpu.PrefetchScalarGridSpec(
            num_scalar_prefetch=3,
            grid=(batch, len(steps)),
            in_specs=[qspec, kspec, kspec],
            out_specs=pl.BlockSpec((tb, MLA_WIDTH), lambda b, p, qi, kj, kd: (b * nb + qi[p], 0)),
            scratch_shapes=[pltpu.VMEM((MLA_HEADS, tb, HEAD_PAD), F32),
                            pltpu.VMEM((MLA_HEADS, tb, HEAD_PAD), F32)]),
        compiler_params=_params("parallel", "arbitrary"),
        name="mla_flash_attention",
    )(q_of, k_of, kinds, q, k, v)


PAIR_ORDER = ((0, 1), (0, 2), (0, 3), (1, 3), (1, 2), (2, 3))
PAIRS_PER_GROUP = len(PAIR_ORDER)
N_CLASSES = N_GROUPS * PAIRS_PER_GROUP
LANES = 128
ROW_CHUNKS = D_MODEL // LANES
TOKEN_TILE = 1024
COLLECT_TILE = 512
EXPERT_TILE = 256


N_CLASS_ROWS = 32


def _route(scores, bias):
    biased = scores + bias
    col = [biased[e:e + 1, :] for e in range(N_EXPERTS)]
    gscore = []
    for g in range(N_GROUPS):
        a, b, c, d = col[g * EXPERTS_PER_GROUP:(g + 1) * EXPERTS_PER_GROUP]
        gscore.append(jnp.maximum(jnp.maximum(jnp.maximum(a + b, a + c), jnp.maximum(a + d, b + c)),
                                  jnp.maximum(b + d, c + d)))
    sel = []
    for g in range(N_GROUPS):
        ok = None
        for o in range(N_GROUPS):
            if o == g:
                continue
            cond = (gscore[g] > gscore[o]) if o < g else (gscore[g] >= gscore[o])
            ok = cond if ok is None else jnp.logical_and(ok, cond)
        grp = col[g * EXPERTS_PER_GROUP:(g + 1) * EXPERTS_PER_GROUP]
        for e in range(EXPERTS_PER_GROUP):
            beaten = jnp.zeros_like(grp[e])
            for o in range(EXPERTS_PER_GROUP):
                if o == e:
                    continue
                ahead = (grp[o] >= grp[e]) if o < e else (grp[o] > grp[e])
                beaten = beaten + jnp.where(ahead, 1.0, 0.0)
            sel.append(jnp.logical_and(ok, beaten < 1.5))
    lo = jnp.full_like(col[0], float(N_EXPERTS))
    hi = jnp.full_like(col[0], -1.0)
    for e in range(N_EXPERTS):
        lo = jnp.where(sel[e], jnp.minimum(lo, float(e)), lo)
        hi = jnp.where(sel[e], jnp.maximum(hi, float(e)), hi)
    return lo, hi


def _store_rows(ref, x):
    n = x.shape[0]
    for c in range(ROW_CHUNKS):
        ref[pl.ds(c, n, stride=ROW_CHUNKS), :] = x[:, c * LANES:(c + 1) * LANES]


def _load_rows(ref, n):
    return jnp.concatenate([ref[pl.ds(c, n, stride=ROW_CHUNKS), :] for c in range(ROW_CHUNKS)], axis=1)


def _merge_kernel(x_ref, oa_ref, ob_ref, g_ref, mod_ref, wa_ref, wb_ref, wo_ref, n2_ref,
                  wrt_ref, rb_ref, xn_ref, row_ref, meta_ref):
    d = D_MODEL
    tm = x_ref.shape[0]
    mod = mod_ref[0]
    ya = _dot(oa_ref[...], wa_ref[...])
    yb = _dot(ob_ref[...], wb_ref[...])
    g = g_ref[...].astype(F32)
    merged = g[:, 0:d] * ya + g[:, d:2 * d] * yb
    xn = x_ref[...] + mod[2:3] * _dot(merged.astype(BF16), wo_ref[...])
    xn_ref[...] = xn
    h2 = _rms(xn, n2_ref[...]) * (1.0 + mod[4:5]) + mod[3:4]
    _store_rows(row_ref, h2)

    h_hi, h_lo = _split_bf16(h2)
    w_hi, w_lo = _split_bf16(wrt_ref[...])
    scores = _sigmoid(_dot_nt(w_hi, h_hi) + _dot_nt(w_hi, h_lo) + _dot_nt(w_lo, h_hi))
    lo, hi = _route(scores, rb_ref[...])

    grp = jnp.floor(lo * (1.0 / EXPERTS_PER_GROUP))
    a = lo - grp * EXPERTS_PER_GROUP
    b = hi - grp * EXPERTS_PER_GROUP
    pair = jnp.zeros_like(a)
    for p, (p_lo, p_hi) in enumerate(PAIR_ORDER):
        pair = pair + jnp.where(jnp.logical_and(a == p_lo, b == p_hi), float(p), 0.0)
    cls = grp * PAIRS_PER_GROUP + pair
    c_row = lax.broadcasted_iota(jnp.int32, (N_CLASS_ROWS, tm), 0).astype(F32)
    onehot = jnp.where(c_row == cls, 1.0, 0.0)
    r_i = lax.broadcasted_iota(jnp.int32, (tm, tm), 0)
    c_i = lax.broadcasted_iota(jnp.int32, (tm, tm), 1)
    earlier = jnp.where(r_i < c_i, 1.0, 0.0).astype(BF16)
    rank = jnp.sum(onehot * _dot(onehot.astype(BF16), earlier), axis=0, keepdims=True)
    m_row = lax.broadcasted_iota(jnp.int32, (8, tm), 0)
    meta_ref[...] = jnp.where(m_row == 0, cls, jnp.where(m_row == 1, rank, 0.0))


def _merge(x2, oa, ob, gsig, mod_l, w_br_a, w_br_b, w_out, n2, w_router, router_bias, seq):
    t, d = x2.shape
    tm = TOKEN_TILE
    rows = lambda n: pl.BlockSpec((tm, n), lambda i: (i, 0))
    return pl.pallas_call(
        _merge_kernel,
        out_shape=(jax.ShapeDtypeStruct((t, d), F32), jax.ShapeDtypeStruct((t * ROW_CHUNKS, LANES), F32),
                   jax.ShapeDtypeStruct((8, t), F32)),
        grid=(t // tm,),
        in_specs=[rows(d), rows(HG_WIDTH), rows(MLA_WIDTH), rows(2 * d),
                  pl.BlockSpec((1, 6, d), lambda i: ((i * tm) // seq, 0, 0)),
                  _resident(w_br_a.shape), _resident(w_br_b.shape), _resident(w_out.shape),
                  _resident((1, d)), _resident((N_EXPERTS, d)), _resident((N_EXPERTS, 1))],
        out_specs=(rows(d), pl.BlockSpec((tm * ROW_CHUNKS, LANES), lambda i: (i, 0)),
                   pl.BlockSpec((8, tm), lambda i: (0, i))),
        compiler_params=_params("parallel"),
        name="merge_outproj_router",
    )(x2, oa, ob, gsig, mod_l, w_br_a.astype(BF16), w_br_b.astype(BF16), w_out.astype(BF16),
      n2[None, :], w_router.T, router_bias[:, None])


def _dispatch_plan(meta, n_tiles_e):
    t = meta.shape[1]
    n_tok_tiles = t // TOKEN_TILE
    cls = meta[0].astype(jnp.int32)
    rank = meta[1].astype(jnp.int32)
    own = cls.reshape(n_tok_tiles, TOKEN_TILE, 1) == jnp.arange(N_CLASSES, dtype=jnp.int32)
    counts = jnp.sum(own.astype(jnp.int32), axis=1)
    total = jnp.sum(counts, axis=0)
    total_pad = (total + EXPERT_TILE - 1) // EXPERT_TILE * EXPERT_TILE
    ends = jnp.cumsum(total_pad)
    base = (ends - total_pad)[None, :] + jnp.cumsum(counts, axis=0) - counts
    pos = jnp.sum(jnp.where(own, base[:, None, :], 0), axis=2).reshape(t) + rank

    tile_start = jnp.arange(n_tiles_e, dtype=jnp.int32) * EXPERT_TILE
    n_valid = ends[-1] // EXPERT_TILE
    tile_cls = jnp.sum((tile_start[:, None] >= ends[None, :]).astype(jnp.int32), axis=1)
    last_cls = jnp.take(tile_cls, n_valid - 1)
    valid = jnp.arange(n_tiles_e, dtype=jnp.int32) < n_valid
    tile_cls = jnp.where(valid, tile_cls, last_cls)
    pair_lo = jnp.asarray([p[0] for p in PAIR_ORDER], jnp.int32)
    pair_hi = jnp.asarray([p[1] for p in PAIR_ORDER], jnp.int32)
    grp = tile_cls // PAIRS_PER_GROUP
    e_lo = grp * EXPERTS_PER_GROUP + jnp.take(pair_lo, tile_cls % PAIRS_PER_GROUP)
    e_hi = grp * EXPERTS_PER_GROUP + jnp.take(pair_hi, tile_cls % PAIRS_PER_GROUP)
    pad_tile = (ends // EXPERT_TILE - 1).astype(jnp.int32)
    used = jnp.concatenate([(total > 0).astype(jnp.int32), n_valid[None].astype(jnp.int32)])
    return (pos * ROW_CHUNKS).astype(jnp.int32), pad_tile, used, e_lo, e_hi, valid.astype(jnp.int32)


DMA_UNROLL = 8


def _row_copies(n_rows, make_copy):
    def issue(ui, carry):
        for u in range(DMA_UNROLL):
            make_copy(ui * DMA_UNROLL + u).start(priority=u % 2)
        return carry
    lax.fori_loop(0, n_rows // DMA_UNROLL, issue, 0)


def _token_slab(ref, first_row):
    return ref.at[pl.ds(pl.multiple_of(first_row, ROW_CHUNKS), ROW_CHUNKS)]


def _dispatch_kernel(pos_ref, pad_tile_ref, used_ref, rows_ref, wg_ref, wu_ref, wd_ref,
                     dst_hbm, wg_out, wu_out, wd_out, zero_ref, sem, zero_sem):
    tm = rows_ref.shape[0] // ROW_CHUNKS
    tile_rows = EXPERT_TILE * ROW_CHUNKS

    @pl.when(pl.program_id(0) == 0)
    def _():
        zero_ref[...] = jnp.zeros_like(zero_ref)
        n_tiles = dst_hbm.shape[0] // tile_rows

        def clear(tile):
            first = pl.multiple_of(tile * tile_rows, tile_rows)
            return pltpu.make_async_copy(zero_ref, dst_hbm.at[pl.ds(first, tile_rows)], zero_sem)

        def start_tail(tile, carry):
            clear(tile).start()
            return carry

        def wait_tail(tile, carry):
            clear(tile).wait()
            return carry
        for c in range(N_CLASSES):
            @pl.when(used_ref[c] != 0)
            def _():
                clear(pad_tile_ref[c]).start()
        lax.fori_loop(used_ref[N_CLASSES], n_tiles, start_tail, 0)
        for c in range(N_CLASSES):
            @pl.when(used_ref[c] != 0)
            def _():
                clear(pad_tile_ref[c]).wait()
        lax.fori_loop(used_ref[N_CLASSES], n_tiles, wait_tail, 0)

    base = pl.program_id(0) * tm
    _row_copies(tm, lambda r: pltpu.make_async_copy(
        _token_slab(rows_ref, r * ROW_CHUNKS), _token_slab(dst_hbm, pos_ref[base + r]), sem))
    wg_out[...] = wg_ref[0].astype(BF16)
    wu_out[...] = wu_ref[0].astype(BF16)
    wd_out[...] = wd_ref[0].astype(BF16)
    pltpu.make_async_copy(rows_ref, dst_hbm.at[pl.ds(0, tm * ROW_CHUNKS)], sem).wait()


def _dispatch_rows(pos, pad_tile, used, rows, n_dst, w_gate, w_up, w_down, layer):
    t = rows.shape[0] // ROW_CHUNKS
    tm = TOKEN_TILE
    n_steps = t // tm
    assert N_EXPERTS % n_steps == 0
    eps = N_EXPERTS // n_steps
    d = D_MODEL
    w_in3 = lambda a, b: pl.BlockSpec((1, eps, a, b), lambda i, p, pt, us: (layer, i, 0, 0))
    w_out3 = lambda a, b: pl.BlockSpec((eps, a, b), lambda i, p, pt, us: (i, 0, 0))
    return pl.pallas_call(
        _dispatch_kernel,
        out_shape=(jax.ShapeDtypeStruct((n_dst * ROW_CHUNKS, LANES), rows.dtype),
                   jax.ShapeDtypeStruct((N_EXPERTS, d, D_EXPERT), BF16),
                   jax.ShapeDtypeStruct((N_EXPERTS, d, D_EXPERT), BF16),
                   jax.ShapeDtypeStruct((N_EXPERTS, D_EXPERT, d), BF16)),
        grid_spec=pltpu.PrefetchScalarGridSpec(
            num_scalar_prefetch=3, grid=(n_steps,),
            in_specs=[pl.BlockSpec((tm * ROW_CHUNKS, LANES), lambda i, p, pt, us: (i, 0)),
                      w_in3(d, D_EXPERT), w_in3(d, D_EXPERT), w_in3(D_EXPERT, d)],
            out_specs=(pl.BlockSpec(memory_space=pl.ANY),
                       w_out3(d, D_EXPERT), w_out3(d, D_EXPERT), w_out3(D_EXPERT, d)),
            scratch_shapes=[pltpu.VMEM((EXPERT_TILE * ROW_CHUNKS, LANES), rows.dtype),
                            pltpu.SemaphoreType.DMA, pltpu.SemaphoreType.DMA]),
        compiler_params=pltpu.CompilerParams(dimension_semantics=("arbitrary",), disable_bounds_checks=True,
                                             has_side_effects=True, vmem_limit_bytes=V7X_VMEM_LIMIT_BYTES),
        name="dispatch_rows",
    )(pos, pad_tile, used, rows, w_gate, w_up, w_down)


def _moe_kernel(elo_ref, ehi_ref, valid_ref, x_ref, wrt_ref, wgl_ref, wul_ref, wdl_ref, wgh_ref, wuh_ref,
                wdh_ref, y_ref):
    j = pl.program_id(0)

    @pl.when(valid_ref[j] == 0)
    def _():
        y_ref[...] = jnp.zeros_like(y_ref)

    @pl.when(valid_ref[j] != 0)
    def _():
        h2 = _load_rows(x_ref, EXPERT_TILE)
        s_lo = _sigmoid(jnp.sum(h2 * wrt_ref[pl.ds(elo_ref[j], 1), :], axis=1, keepdims=True))
        s_hi = _sigmoid(jnp.sum(h2 * wrt_ref[pl.ds(ehi_ref[j], 1), :], axis=1, keepdims=True))
        total = s_lo + s_hi
        hb = h2.astype(BF16)
        he_lo = _silu(_dot(hb, wgl_ref[0])) * _dot(hb, wul_ref[0])
        y = (s_lo / total) * _dot(he_lo.astype(BF16), wdl_ref[0])
        he_hi = _silu(_dot(hb, wgh_ref[0])) * _dot(hb, wuh_ref[0])
        _store_rows(y_ref, y + (s_hi / total) * _dot(he_hi.astype(BF16), wdh_ref[0]))


def _moe(rows_sorted, e_lo, e_hi, valid, w_router, wg, wu, wd):
    n_pad = rows_sorted.shape[0] // ROW_CHUNKS
    d = D_MODEL
    lo3 = lambda j, el, eh, va: (el[j], 0, 0)
    hi3 = lambda j, el, eh, va: (eh[j], 0, 0)
    tile = pl.BlockSpec((EXPERT_TILE * ROW_CHUNKS, LANES), lambda j, el, eh, va: (j, 0))
    return pl.pallas_call(
        _moe_kernel,
        out_shape=jax.ShapeDtypeStruct(rows_sorted.shape, F32),
        grid_spec=pltpu.PrefetchScalarGridSpec(
            num_scalar_prefetch=3, grid=(n_pad // EXPERT_TILE,),
            in_specs=[tile, _resident((N_EXPERTS, d)),
                      pl.BlockSpec((1, d, D_EXPERT), lo3), pl.BlockSpec((1, d, D_EXPERT), lo3),
                      pl.BlockSpec((1, D_EXPERT, d), lo3),
                      pl.BlockSpec((1, d, D_EXPERT), hi3), pl.BlockSpec((1, d, D_EXPERT), hi3),
                      pl.BlockSpec((1, D_EXPERT, d), hi3)],
            out_specs=tile),
        compiler_params=_params("arbitrary"),
        name="moe_experts",
    )(e_lo, e_hi, valid, rows_sorted, w_router.T, wg, wu, wd, wg, wu, wd)


def _collect_kernel(pos_ref, x_ref, ys_hbm, mod_ref, fg_ref, o_ref, buf0, buf1, sems, *, final):
    tm = x_ref.shape[0]
    i = pl.program_id(0)
    n = pl.num_programs(0)
    bufs = (buf0, buf1)

    def fetch(tile, slot):
        base = tile * tm
        _row_copies(tm, lambda r: pltpu.make_async_copy(
            _token_slab(ys_hbm, pos_ref[base + r]), _token_slab(bufs[slot], r * ROW_CHUNKS), sems.at[slot]))

    def finish(slot):
        pltpu.make_async_copy(ys_hbm.at[pl.ds(0, tm * ROW_CHUNKS)], bufs[slot], sems.at[slot]).wait()
        xo = x_ref[...] + mod_ref[0][5:6] * _load_rows(bufs[slot], tm)
        if final:
            xo = _rms(xo, fg_ref[...])
        o_ref[...] = xo

    @pl.when(i == 0)
    def _():
        fetch(0, 0)

    for slot in range(2):
        @pl.when(jnp.logical_and(i + 1 < n, (i + 1) % 2 == slot))
        def _():
            fetch(i + 1, slot)

    for slot in range(2):
        @pl.when(i % 2 == slot)
        def _():
            finish(slot)


def _collect_residual(pos, xn, y_sorted, mod_l, final_g, final, seq):
    t, d = xn.shape
    tm = COLLECT_TILE
    rows = pl.BlockSpec((tm, d), lambda i, p: (i, 0))
    return pl.pallas_call(
        functools.partial(_collect_kernel, final=final),
        out_shape=jax.ShapeDtypeStruct((t, d), F32),
        grid_spec=pltpu.PrefetchScalarGridSpec(
            num_scalar_prefetch=1, grid=(t // tm,),
            in_specs=[rows, pl.BlockSpec(memory_space=pl.ANY),
                      pl.BlockSpec((1, 6, d), lambda i, p: ((i * tm) // seq, 0, 0)),
                      pl.BlockSpec((1, d), lambda i, p: (0, 0))],
            out_specs=rows,
            scratch_shapes=[pltpu.VMEM((tm * ROW_CHUNKS, LANES), F32), pltpu.VMEM((tm * ROW_CHUNKS, LANES), F32),
                            pltpu.SemaphoreType.DMA((2,))]),
        compiler_params=pltpu.CompilerParams(dimension_semantics=("arbitrary",), disable_bounds_checks=True,
                                             vmem_limit_bytes=V7X_VMEM_LIMIT_BYTES),
        name="collect_residual",
    )(pos, xn, y_sorted, mod_l, final_g[None, :])


def kernel(x, c, positions, ada_w, ada_b, norm1_g, w_in, hg_lb_logits, hg_norm_g, q_norm_g, w_q_up,
           kv_norm_g, w_kv_up, w_br_a, w_br_b, w_out, norm2_g, w_router, router_bias, w_gate, w_up,
           w_down, final_g):
    batch, seq, d = x.shape
    depth = ada_w.shape[0]
    t = batch * seq
    n_tiles_e = (t + N_CLASSES * (EXPERT_TILE - 1)) // EXPERT_TILE
    mod = _modulation(c, ada_w, ada_b).reshape(depth, batch, 6, d)
    cos_t, sin_t = _rope_tables(positions)
    x2 = x.reshape(t, d)
    for l in range(depth):
        hq, hf, hi, hgate, gsig, q, k, v = _input_projection(
            x2, mod[l], norm1_g[l], cos_t, sin_t, w_in[l], q_norm_g[l], w_q_up[l],
            kv_norm_g[l], w_kv_up[l], seq)
        oa = _hgrn(hg_lb_logits, hg_norm_g[l], hq, hf, hi, hgate, l, batch, seq)
        ob = _attention(q, k, v, batch, seq)
        xn, rows, meta = _merge(x2, oa, ob, gsig, mod[l], w_br_a[l], w_br_b[l], w_out[l], norm2_g[l],
                                w_router, router_bias, seq)
        pos, pad_tile, used, e_lo, e_hi, valid = _dispatch_plan(meta, n_tiles_e)
        rows_sorted, wg, wu, wd = _dispatch_rows(pos, pad_tile, used, rows, n_tiles_e * EXPERT_TILE,
                                                 w_gate, w_up, w_down, l)
        y_sorted = _moe(rows_sorted, e_lo, e_hi, valid, w_router, wg, wu, wd)
        x2 = _collect_residual(pos, xn, y_sorted, mod[l], final_g, l == depth - 1, seq)
    return x2.reshape(batch, seq, d)
```

```python
import functools
import math

import jax
import jax.numpy as jnp
from jax import lax
from jax.experimental import pallas as pl
from jax.experimental.pallas import tpu as pltpu

F32 = jnp.float32
BF16 = jnp.bfloat16

D_MODEL = 1024
CHUNK = 64
EPS = 1e-6

HG_HEADS = 4
HG_DK = 128
HG_DV = 128
HG_WIDTH = HG_HEADS * HG_DV
HG_SUB = 8
HG_LEVELS = (8, 16, 32)

MLA_HEADS = 8
MLA_NOPE = 64
MLA_ROPE = 32
MLA_V = 64
MLA_Q_LORA = 384
MLA_KV_LORA = 256
MLA_DQK = MLA_NOPE + MLA_ROPE
MLA_WIDTH = MLA_HEADS * MLA_V
ROPE_BASE = 10000.0
HEAD_PAD = 128
LOG2_E = math.log2(math.e)

N_EXPERTS = 16
N_GROUPS = 4
EXPERTS_PER_GROUP = N_EXPERTS // N_GROUPS
D_EXPERT = 512

IN_SIZES = (HG_HEADS * HG_DK, HG_HEADS * HG_DK, HG_HEADS * HG_DV, HG_WIDTH,
            MLA_Q_LORA, MLA_KV_LORA, MLA_ROPE, 2 * D_MODEL)

V7X_VMEM_LIMIT_BYTES = 56 * 1024 * 1024


def _params(*sem):
    return pltpu.CompilerParams(dimension_semantics=sem, vmem_limit_bytes=V7X_VMEM_LIMIT_BYTES)


def _resident(shape):
    nd = len(shape)
    return pl.BlockSpec(shape, lambda *_: (0,) * nd, pipeline_mode=pl.Buffered(1))


def _sigmoid(x):
    return 1.0 / (1.0 + jnp.exp(-x))


def _silu(x):
    return x * _sigmoid(x)


def _dot(a, b):
    return jnp.dot(a, b, preferred_element_type=F32)


def _dot_nt(a, b):
    return lax.dot_general(a, b, (((1,), (1,)), ((), ())), preferred_element_type=F32)


def _dot_tn(a, b):
    return lax.dot_general(a, b, (((0,), (0,)), ((), ())), preferred_element_type=F32)


def _split_bf16(x):
    hi = x.astype(BF16)
    lo = (x - hi.astype(F32)).astype(BF16)
    return hi, lo


def _rms(x, g):
    return x * lax.rsqrt(jnp.mean(x * x, axis=-1, keepdims=True) + EPS) * g


def _mod_kernel(c_ref, w_ref, b_ref, o_ref):
    @pl.when(pl.program_id(1) == 0)
    def _():
        o_ref[0] = jnp.broadcast_to(b_ref[0], o_ref.shape[1:])
    ca = _silu(c_ref[...])
    o_ref[0] += _dot(ca.astype(BF16), w_ref[0].astype(BF16))


def _modulation(c, ada_w, ada_b):
    depth, d, n = ada_w.shape
    b = c.shape[0]
    tk = 256
    return pl.pallas_call(
        _mod_kernel,
        out_shape=jax.ShapeDtypeStruct((depth, b, n), F32),
        grid=(depth, d // tk),
        in_specs=[pl.BlockSpec((b, tk), lambda l, k: (0, k)),
                  pl.BlockSpec((1, tk, n), lambda l, k: (l, k, 0)),
                  pl.BlockSpec((1, 1, n), lambda l, k: (l, 0, 0))],
        out_specs=pl.BlockSpec((1, b, n), lambda l, k: (l, 0, 0)),
        compiler_params=_params("parallel", "arbitrary"),
        name="adaln_modulation",
    )(c, ada_w, ada_b.reshape(depth, 1, n))


def _rope_kernel(pos_ref, inv_ref, msk_ref, sgn_ref, cos_ref, sin_ref):
    ang = pos_ref[...].astype(F32) * inv_ref[...]
    cos_ref[...] = jnp.cos(ang) * msk_ref[...]
    sin_ref[...] = jnp.sin(ang) * sgn_ref[...]


def _rope_tables(positions):
    t = positions.size
    half = MLA_ROPE // 2
    inv = ROPE_BASE ** (-jnp.arange(half, dtype=F32) / half)
    z64, z32, one16 = jnp.zeros((MLA_NOPE,), F32), jnp.zeros((32,), F32), jnp.ones((half,), F32)
    inv_row = jnp.concatenate([z64, inv, inv, z32])[None, :]
    msk_row = jnp.concatenate([z64, one16, one16, z32])[None, :]
    sgn_row = jnp.concatenate([z64, -one16, one16, z32])[None, :]
    tr = min(2048, t)
    row = pl.BlockSpec((1, HEAD_PAD), lambda i: (0, 0))
    tab = pl.BlockSpec((tr, HEAD_PAD), lambda i: (i, 0))
    return pl.pallas_call(
        _rope_kernel,
        out_shape=(jax.ShapeDtypeStruct((t, HEAD_PAD), F32),) * 2,
        grid=(t // tr,),
        in_specs=[pl.BlockSpec((tr, 1), lambda i: (i, 0)), row, row, row],
        out_specs=(tab, tab),
        compiler_params=_params("parallel"),
        name="rope_tables",
    )(positions.reshape(t, 1), inv_row, msk_row, sgn_row)


def _proj_kernel(x_ref, mod_ref, n1_ref, cos_ref, sin_ref, wh_ref, wg_ref, wc_ref,
                 qn_ref, wqa_ref, wqs_ref, kn_ref, wk_ref, wv_ref, one_ref,
                 hq_ref, hf_ref, hi_ref, hgate_ref, gsig_ref, q_ref, k_ref, v_ref):
    x = x_ref[...]
    mod = mod_ref[0]
    h = _rms(x, n1_ref[...]) * (1.0 + mod[1:2]) + mod[0:1]
    hb = h.astype(BF16)

    ph = _dot(hb, wh_ref[...])
    w = HG_WIDTH
    hq_ref[...] = ph[:, 0:w].astype(BF16)
    hf_ref[...] = ph[:, w:2 * w]
    hi_ref[...] = ph[:, 2 * w:3 * w].astype(BF16)
    hgate_ref[...] = ph[:, 3 * w:4 * w].astype(BF16)

    gsig_ref[...] = _sigmoid(_dot(hb, wg_ref[...])).astype(BF16)

    pc = _dot(hb, wc_ref[...])
    cq = pc[:, 0:MLA_Q_LORA]
    ckv = pc[:, MLA_Q_LORA:MLA_Q_LORA + MLA_KV_LORA]
    kra = pc[:, MLA_Q_LORA + MLA_KV_LORA:MLA_Q_LORA + MLA_KV_LORA + HEAD_PAD]
    krb = pc[:, MLA_Q_LORA + MLA_KV_LORA + HEAD_PAD:]

    cos_t = cos_ref[...]
    sin_t = sin_ref[...]
    lane = lax.broadcasted_iota(jnp.int32, cos_t.shape, 1)
    scale = MLA_DQK ** -0.5 * LOG2_E
    cq_tab = jnp.tile(scale * (cos_t + jnp.where(lane < MLA_NOPE, 1.0, 0.0)), (1, MLA_HEADS))
    sq_tab = jnp.tile(scale * sin_t, (1, MLA_HEADS))

    cqn = _rms(cq, qn_ref[...]).astype(BF16)
    q = _dot(cqn, wqa_ref[...]) * cq_tab + _dot(cqn, wqs_ref[...]) * sq_tab
    q_ref[...] = q.astype(BF16)

    ckvn = _rms(ckv, kn_ref[...]).astype(BF16)
    kpe = kra * cos_t + krb * sin_t
    k_ref[...] = (_dot(ckvn, wk_ref[...]) + jnp.tile(kpe, (1, MLA_HEADS))).astype(BF16)
    v_ref[...] = (_dot(ckvn, wv_ref[...]) + one_ref[...]).astype(BF16)


def _pad_heads(w, lo, hi, at):
    k, nh, _ = w.shape
    out = jnp.zeros((k, nh, HEAD_PAD), w.dtype)
    out = out.at[:, :, at:at + (hi - lo)].set(w[:, :, lo:hi])
    return out


def _input_projection(x2, mod_l, n1, cos_t, sin_t, w_in, q_norm_g, w_q_up, kv_norm_g, w_kv_up, seq):
    t, d = x2.shape
    tm = 256
    splits = [0]
    for s in IN_SIZES:
        splits.append(splits[-1] + s)
    kr = w_in[:, splits[6]:splits[7]]
    half = MLA_ROPE // 2
    z64 = jnp.zeros((d, MLA_NOPE), F32)
    z32 = jnp.zeros((d, HEAD_PAD - MLA_NOPE - MLA_ROPE), F32)
    kr_a = jnp.concatenate([z64, kr, z32], axis=1)
    kr_b = jnp.concatenate([z64, kr[:, half:], kr[:, :half], z32], axis=1)
    w_h = w_in[:, splits[0]:splits[4]].astype(BF16)
    w_g = w_in[:, splits[7]:splits[8]].astype(BF16)
    w_c = jnp.concatenate([w_in[:, splits[4]:splits[6]], kr_a, kr_b], axis=1).astype(BF16)

    wq = w_q_up.reshape(MLA_Q_LORA, MLA_HEADS, MLA_DQK)
    wq_all = _pad_heads(wq, 0, MLA_DQK, 0).reshape(MLA_Q_LORA, -1).astype(BF16)
    wq_swap = (_pad_heads(wq, MLA_NOPE + half, MLA_DQK, MLA_NOPE)
               + _pad_heads(wq, MLA_NOPE, MLA_NOPE + half, MLA_NOPE + half))
    wq_swap = wq_swap.reshape(MLA_Q_LORA, -1).astype(BF16)
    wkv = w_kv_up.reshape(MLA_KV_LORA, MLA_HEADS, MLA_NOPE + MLA_V)
    wk_all = _pad_heads(wkv, 0, MLA_NOPE, 0).reshape(MLA_KV_LORA, -1).astype(BF16)
    wv_all = _pad_heads(wkv, MLA_NOPE, MLA_NOPE + MLA_V, 0).reshape(MLA_KV_LORA, -1).astype(BF16)
    ones_row = jnp.tile(jnp.zeros((HEAD_PAD,), F32).at[MLA_V].set(1.0), MLA_HEADS)[None, :]

    hp = MLA_HEADS * HEAD_PAD
    rows = lambda n: pl.BlockSpec((tm, n), lambda i: (i, 0))
    outs = [(HG_WIDTH, BF16), (HG_WIDTH, F32), (HG_WIDTH, BF16), (HG_WIDTH, BF16),
            (2 * d, BF16), (hp, BF16), (hp, BF16), (hp, BF16)]
    return pl.pallas_call(
        _proj_kernel,
        out_shape=tuple(jax.ShapeDtypeStruct((t, n), dt) for n, dt in outs),
        grid=(t // tm,),
        in_specs=[rows(d),
                  pl.BlockSpec((1, 6, d), lambda i: ((i * tm) // seq, 0, 0)),
                  _resident((1, d)),
                  rows(HEAD_PAD), rows(HEAD_PAD),
                  _resident(w_h.shape), _resident(w_g.shape), _resident(w_c.shape),
                  _resident((1, MLA_Q_LORA)), _resident(wq_all.shape), _resident(wq_swap.shape),
                  _resident((1, MLA_KV_LORA)), _resident(wk_all.shape), _resident(wv_all.shape),
                  _resident((1, hp))],
        out_specs=tuple(rows(n) for n, _ in outs),
        compiler_params=_params("parallel"),
        name="norm_input_projection",
    )(x2, mod_l, n1[None, :], cos_t, sin_t, w_h, w_g, w_c,
      q_norm_g[None, :], wq_all, wq_swap, kv_norm_g[None, :], wk_all, wv_all, ones_row)


def _hgrn_kernel(lbl_ref, gn_ref, q_ref, f_ref, v_ref, gate_ref, o_ref, st_ref, *, layer, n_chunks):
    @pl.when(pl.program_id(1) == 0)
    def _():
        st_ref[...] = jnp.zeros_like(st_ref)

    lg = lbl_ref[...]
    ex = jnp.exp(lg - jnp.max(lg, axis=0, keepdims=True))
    soft = ex / jnp.sum(ex, axis=0, keepdims=True)
    lb_all = jnp.zeros_like(soft[0:1])
    for i in range(1, layer + 1):
        lb_all = lb_all + soft[i:i + 1]

    r_i = lax.broadcasted_iota(jnp.int32, (CHUNK, CHUNK), 0)
    c_i = lax.broadcasted_iota(jnp.int32, (CHUNK, CHUNK), 1)
    tri = jnp.where(r_i >= c_i, 1.0, 0.0).astype(BF16)
    level_masks = []
    for hs in HG_LEVELS:
        same = (r_i // (2 * hs)) == (c_i // (2 * hs))
        level_masks.append(jnp.logical_and(same, jnp.logical_and(r_i % (2 * hs) >= hs, c_i % (2 * hs) < hs)))
    sub_row = lax.broadcasted_iota(jnp.int32, (HG_SUB, 1), 0)
    n_sub = CHUNK // HG_SUB

    def chunk_body(ci, carry):
        r0 = pl.multiple_of(ci * CHUNK, CHUNK)
        rows = pl.ds(r0, CHUNK)
        for h in range(HG_HEADS):
            sl = slice(h * HG_DK, (h + 1) * HG_DK)
            lb = lb_all[:, sl]
            q = q_ref[rows, sl].astype(F32)
            vb = v_ref[rows, sl]
            v = vb.astype(F32)
            f = lb + (1.0 - lb) * _sigmoid(f_ref[rows, sl])
            g = jnp.log2(f)
            k = 1.0 - f
            qf = _silu(q) * (HG_DK ** -0.5)
            g_hi, g_lo = _split_bf16(g)
            b = _dot(tri, g_hi) + _dot(tri, g_lo)

            att = jnp.zeros((CHUNK, CHUNK), F32)
            for hs, mask in zip(HG_LEVELS, level_masks):
                ref = jnp.concatenate(
                    [jnp.broadcast_to(b[j + hs - 1:j + hs], (2 * hs, HG_DK)) for j in range(0, CHUNK, 2 * hs)],
                    axis=0)
                qt = (qf * jnp.exp2(b - ref)).astype(BF16)
                kt = (k * jnp.exp2(ref - b)).astype(BF16)
                att = att + jnp.where(mask, _dot_nt(qt, kt), 0.0)
            o = _dot(att.astype(BF16), vb)

            blocks = []
            for i in range(n_sub):
                rs = slice(i * HG_SUB, (i + 1) * HG_SUB)
                b_i, q_i, k_i, v_i = b[rs], qf[rs], k[rs], v[rs]
                acc = jnp.zeros((HG_SUB, HG_DV), F32)
                for s in range(HG_SUB):
                    a = jnp.sum(jnp.exp2(b_i - b_i[s:s + 1]) * (q_i * k_i[s:s + 1]), axis=1, keepdims=True)
                    acc = acc + jnp.where(sub_row >= s, a, 0.0) * v_i[s:s + 1]
                blocks.append(acc)
            o = o + jnp.concatenate(blocks, axis=0)

            st = st_ref[h]
            o = o + _dot_nt((qf * jnp.exp2(b)).astype(BF16), st.astype(BF16))
            b_last = b[CHUNK - 1:CHUNK]
            kd = (k * jnp.exp2(b_last - b)).astype(BF16)
            st_ref[h] = st * jnp.exp2(b_last) + _dot_tn(vb, kd)

            gt = gate_ref[rows, sl].astype(F32)
            o_ref[rows, sl] = (_rms(o, gn_ref[:, sl]) * _silu(gt)).astype(BF16)
        return carry

    lax.fori_loop(0, n_chunks, chunk_body, 0, unroll=2)


def _hgrn(hg_lb_logits, hg_norm_g, hq, hf, hi, hgate, layer, batch, seq):
    t = hq.shape[0]
    lc = min(512, seq)
    nb = seq // lc
    rows = pl.BlockSpec((lc, HG_WIDTH), lambda b, j: (b * nb + j, 0))
    kern = functools.partial(_hgrn_kernel, layer=layer, n_chunks=lc // CHUNK)
    return pl.pallas_call(
        kern,
        out_shape=jax.ShapeDtypeStruct((t, HG_WIDTH), BF16),
        grid=(batch, nb),
        in_specs=[pl.BlockSpec(hg_lb_logits.shape, lambda b, j: (0, 0)),
                  pl.BlockSpec((1, HG_WIDTH), lambda b, j: (0, 0)),
                  rows, rows, rows, rows],
        out_specs=rows,
        scratch_shapes=[pltpu.VMEM((HG_HEADS, HG_DV, HG_DK), F32)],
        compiler_params=_params("parallel", "arbitrary"),
        name="hgrn2_chunkwise",
    )(hg_lb_logits, hg_norm_g[None, :], hq, hf, hi, hgate)


ATTN_BLOCK = 1024
STEP_FULL, STEP_TOP_DIAG, STEP_BOTTOM_DIAG = range(3)


def _attn_kernel(qi_ref, kj_ref, kind_ref, q_ref, k_ref, v_ref, o_ref, m_ref, acc_ref, *, tb):
    del qi_ref, kj_ref
    p_id = pl.program_id(1)
    kind = kind_ref[p_id]
    half = tb // 2

    def reset():
        m_ref[...] = jnp.full_like(m_ref, -jnp.inf)
        acc_ref[...] = jnp.zeros_like(acc_ref)

    def step(r0, nr, nk, diagonal):
        rows = slice(r0, r0 + nr)
        if diagonal:
            qc = (lax.broadcasted_iota(jnp.int32, (nr, nk), 0) + r0) // CHUNK
            kc = lax.broadcasted_iota(jnp.int32, (nr, nk), 1) // CHUNK
            visible = kc <= qc
        for h in range(MLA_HEADS):
            sl = slice(h * HEAD_PAD, (h + 1) * HEAD_PAD)
            s = _dot_nt(q_ref[rows, sl], k_ref[0:nk, sl])
            if diagonal:
                s = jnp.where(visible, s, -jnp.inf)
            tiles = [s[:, t * HEAD_PAD:(t + 1) * HEAD_PAD] for t in range(nk // HEAD_PAD)]
            m_tile = tiles[0]
            for tl in tiles[1:]:
                m_tile = jnp.maximum(m_tile, tl)
            m_prev = m_ref[h, rows]
            m_new = jnp.maximum(m_prev, jnp.max(m_tile, axis=1, keepdims=True))
            alpha = jnp.exp2(m_prev - m_new)
            p = jnp.concatenate([jnp.exp2((tl - m_new).astype(BF16)) for tl in tiles], axis=1)
            acc_ref[h, rows] = acc_ref[h, rows] * alpha + _dot(p, v_ref[0:nk, sl])
            m_ref[h, rows] = m_new

    @pl.when(p_id == 0)
    def _():
        reset()

    @pl.when(kind == STEP_FULL)
    def _():
        step(0, tb, tb, False)

    @pl.when(kind == STEP_TOP_DIAG)
    def _():
        step(0, half, half, True)

    @pl.when(kind == STEP_BOTTOM_DIAG)
    def _():
        step(half, half, tb, True)
        outs = []
        for h in range(MLA_HEADS):
            acc = acc_ref[h]
            outs.append(acc[:, 0:MLA_V] / acc[:, MLA_V:MLA_V + 1])
        o_ref[...] = jnp.concatenate(outs, axis=1).astype(BF16)
        reset()


def _attention(q, k, v, batch, seq):
    t, hp = q.shape
    tb = min(ATTN_BLOCK, seq)
    nb = seq // tb
    steps = []
    for i in range(nb):
        steps += [(i, j, STEP_FULL) for j in range(i)]
        steps += [(i, i, STEP_TOP_DIAG), (i, i, STEP_BOTTOM_DIAG)]
    q_of, k_of, kinds = (jnp.asarray([st[c] for st in steps], jnp.int32) for c in range(3))
    qspec = pl.BlockSpec((tb, hp), lambda b, p, qi, kj, kd: (b * nb + qi[p], 0))
    kspec = pl.BlockSpec((tb, hp), lambda b, p, qi, kj, kd: (b * nb + kj[p], 0))
    return pl.pallas_call(
        functools.partial(_attn_kernel, tb=tb),
        out_shape=jax.ShapeDtypeStruct((t, MLA_WIDTH), BF16),
        grid_spec=pltpu.PrefetchScalarGridSpec(
            num_scalar_prefetch=3,
            grid=(batch, len(steps)),
            in_specs=[qspec, kspec, kspec],
            out_specs=pl.BlockSpec((tb, MLA_WIDTH), lambda b, p, qi, kj, kd: (b * nb + qi[p], 0)),
            scratch_shapes=[pltpu.VMEM((MLA_HEADS, tb, HEAD_PAD), F32),
                            pltpu.VMEM((MLA_HEADS, tb, HEAD_PAD), F32)]),
        compiler_params=_params("parallel", "arbitrary"),
        name="mla_flash_attention",
    )(q_of, k_of, kinds, q, k, v)


PAIR_ORDER = ((0, 1), (0, 2), (0, 3), (1, 3), (1, 2), (2, 3))
PAIRS_PER_GROUP = len(PAIR_ORDER)
N_CLASSES = N_GROUPS * PAIRS_PER_GROUP
LANES = 128
ROW_CHUNKS = D_MODEL // LANES
TOKEN_TILE = 1024
COLLECT_TILE = 512
EXPERT_TILE = 256


N_CLASS_ROWS = 32


def _route(scores, bias):
    biased = scores + bias
    col = [biased[e:e + 1, :] for e in range(N_EXPERTS)]
    gscore = []
    for g in range(N_GROUPS):
        a, b, c, d = col[g * EXPERTS_PER_GROUP:(g + 1) * EXPERTS_PER_GROUP]
        gscore.append(jnp.maximum(jnp.maximum(jnp.maximum(a + b, a + c), jnp.maximum(a + d, b + c)),
                                  jnp.maximum(b + d, c + d)))
    sel = []
    for g in range(N_GROUPS):
        ok = None
        for o in range(N_GROUPS):
            if o == g:
                continue
            cond = (gscore[g] > gscore[o]) if o < g else (gscore[g] >= gscore[o])
            ok = cond if ok is None else jnp.logical_and(ok, cond)
        grp = col[g * EXPERTS_PER_GROUP:(g + 1) * EXPERTS_PER_GROUP]
        for e in range(EXPERTS_PER_GROUP):
            beaten = jnp.zeros_like(grp[e])
            for o in range(EXPERTS_PER_GROUP):
                if o == e:
                    continue
                ahead = (grp[o] >= grp[e]) if o < e else (grp[o] > grp[e])
                beaten = beaten + jnp.where(ahead, 1.0, 0.0)
            sel.append(jnp.logical_and(ok, beaten < 1.5))
    lo = jnp.full_like(col[0], float(N_EXPERTS))
    hi = jnp.full_like(col[0], -1.0)
    for e in range(N_EXPERTS):
        lo = jnp.where(sel[e], jnp.minimum(lo, float(e)), lo)
        hi = jnp.where(sel[e], jnp.maximum(hi, float(e)), hi)
    return lo, hi


def _store_rows(ref, x):
    n = x.shape[0]
    for c in range(ROW_CHUNKS):
        ref[pl.ds(c, n, stride=ROW_CHUNKS), :] = x[:, c * LANES:(c + 1) * LANES]


def _load_rows(ref, n):
    return jnp.concatenate([ref[pl.ds(c, n, stride=ROW_CHUNKS), :] for c in range(ROW_CHUNKS)], axis=1)


def _merge_kernel(x_ref, oa_ref, ob_ref, g_ref, mod_ref, wa_ref, wb_ref, wo_ref, n2_ref,
                  wrt_ref, rb_ref, xn_ref, row_ref, meta_ref):
    d = D_MODEL
    tm = x_ref.shape[0]
    mod = mod_ref[0]
    ya = _dot(oa_ref[...], wa_ref[...])
    yb = _dot(ob_ref[...], wb_ref[...])
    g = g_ref[...].astype(F32)
    merged = g[:, 0:d] * ya + g[:, d:2 * d] * yb
    xn = x_ref[...] + mod[2:3] * _dot(merged.astype(BF16), wo_ref[...])
    xn_ref[...] = xn
    h2 = _rms(xn, n2_ref[...]) * (1.0 + mod[4:5]) + mod[3:4]
    _store_rows(row_ref, h2)

    h_hi, h_lo = _split_bf16(h2)
    w_hi, w_lo = _split_bf16(wrt_ref[...])
    scores = _sigmoid(_dot_nt(w_hi, h_hi) + _dot_nt(w_hi, h_lo) + _dot_nt(w_lo, h_hi))
    lo, hi = _route(scores, rb_ref[...])

    grp = jnp.floor(lo * (1.0 / EXPERTS_PER_GROUP))
    a = lo - grp * EXPERTS_PER_GROUP
    b = hi - grp * EXPERTS_PER_GROUP
    pair = jnp.zeros_like(a)
    for p, (p_lo, p_hi) in enumerate(PAIR_ORDER):
        pair = pair + jnp.where(jnp.logical_and(a == p_lo, b == p_hi), float(p), 0.0)
    cls = grp * PAIRS_PER_GROUP + pair
    c_row = lax.broadcasted_iota(jnp.int32, (N_CLASS_ROWS, tm), 0).astype(F32)
    onehot = jnp.where(c_row == cls, 1.0, 0.0)
    r_i = lax.broadcasted_iota(jnp.int32, (tm, tm), 0)
    c_i = lax.broadcasted_iota(jnp.int32, (tm, tm), 1)
    earlier = jnp.where(r_i < c_i, 1.0, 0.0).astype(BF16)
    rank = jnp.sum(onehot * _dot(onehot.astype(BF16), earlier), axis=0, keepdims=True)
    m_row = lax.broadcasted_iota(jnp.int32, (8, tm), 0)
    meta_ref[...] = jnp.where(m_row == 0, cls, jnp.where(m_row == 1, rank, 0.0))


def _merge(x2, oa, ob, gsig, mod_l, w_br_a, w_br_b, w_out, n2, w_router, router_bias, seq):
    t, d = x2.shape
    tm = TOKEN_TILE
    rows = lambda n: pl.BlockSpec((tm, n), lambda i: (i, 0))
    return pl.pallas_call(
        _merge_kernel,
        out_shape=(jax.ShapeDtypeStruct((t, d), F32), jax.ShapeDtypeStruct((t * ROW_CHUNKS, LANES), F32),
                   jax.ShapeDtypeStruct((8, t), F32)),
        grid=(t // tm,),
        in_specs=[rows(d), rows(HG_WIDTH), rows(MLA_WIDTH), rows(2 * d),
                  pl.BlockSpec((1, 6, d), lambda i: ((i * tm) // seq, 0, 0)),
                  _resident(w_br_a.shape), _resident(w_br_b.shape), _resident(w_out.shape),
                  _resident((1, d)), _resident((N_EXPERTS, d)), _resident((N_EXPERTS, 1))],
        out_specs=(rows(d), pl.BlockSpec((tm * ROW_CHUNKS, LANES), lambda i: (i, 0)),
                   pl.BlockSpec((8, tm), lambda i: (0, i))),
        compiler_params=_params("parallel"),
        name="merge_outproj_router",
    )(x2, oa, ob, gsig, mod_l, w_br_a.astype(BF16), w_br_b.astype(BF16), w_out.astype(BF16),
      n2[None, :], w_router.T, router_bias[:, None])


def _dispatch_plan(meta, n_tiles_e):
    t = meta.shape[1]
    n_tok_tiles = t // TOKEN_TILE
    cls = meta[0].astype(jnp.int32)
    rank = meta[1].astype(jnp.int32)
    own = cls.reshape(n_tok_tiles, TOKEN_TILE, 1) == jnp.arange(N_CLASSES, dtype=jnp.int32)
    counts = jnp.sum(own.astype(jnp.int32), axis=1)
    total = jnp.sum(counts, axis=0)
    total_pad = (total + EXPERT_TILE - 1) // EXPERT_TILE * EXPERT_TILE
    ends = jnp.cumsum(total_pad)
    base = (ends - total_pad)[None, :] + jnp.cumsum(counts, axis=0) - counts
    pos = jnp.sum(jnp.where(own, base[:, None, :], 0), axis=2).reshape(t) + rank

    tile_start = jnp.arange(n_tiles_e, dtype=jnp.int32) * EXPERT_TILE
    n_valid = ends[-1] // EXPERT_TILE
    tile_cls = jnp.sum((tile_start[:, None] >= ends[None, :]).astype(jnp.int32), axis=1)
    last_cls = jnp.take(tile_cls, n_valid - 1)
    valid = jnp.arange(n_tiles_e, dtype=jnp.int32) < n_valid
    tile_cls = jnp.where(valid, tile_cls, last_cls)
    pair_lo = jnp.asarray([p[0] for p in PAIR_ORDER], jnp.int32)
    pair_hi = jnp.asarray([p[1] for p in PAIR_ORDER], jnp.int32)
    grp = tile_cls // PAIRS_PER_GROUP
    e_lo = grp * EXPERTS_PER_GROUP + jnp.take(pair_lo, tile_cls % PAIRS_PER_GROUP)
    e_hi = grp * EXPERTS_PER_GROUP + jnp.take(pair_hi, tile_cls % PAIRS_PER_GROUP)
    pad_tile = (ends // EXPERT_TILE - 1).astype(jnp.int32)
    used = jnp.concatenate([(total > 0).astype(jnp.int32), n_valid[None].astype(jnp.int32)])
    return (pos * ROW_CHUNKS).astype(jnp.int32), pad_tile, used, e_lo, e_hi, valid.astype(jnp.int32)


DMA_UNROLL = 8


def _row_copies(n_rows, make_copy):
    def issue(ui, carry):
        for u in range(DMA_UNROLL):
            make_copy(ui * DMA_UNROLL + u).start(priority=u % 2)
        return carry
    lax.fori_loop(0, n_rows // DMA_UNROLL, issue, 0)


def _token_slab(ref, first_row):
    return ref.at[pl.ds(pl.multiple_of(first_row, ROW_CHUNKS), ROW_CHUNKS)]


def _dispatch_kernel(pos_ref, pad_tile_ref, used_ref, rows_ref, wg_ref, wu_ref, wd_ref,
                     dst_hbm, wg_out, wu_out, wd_out, zero_ref, sem, zero_sem):
    tm = rows_ref.shape[0] // ROW_CHUNKS
    tile_rows = EXPERT_TILE * ROW_CHUNKS

    @pl.when(pl.program_id(0) == 0)
    def _():
        zero_ref[...] = jnp.zeros_like(zero_ref)
        n_tiles = dst_hbm.shape[0] // tile_rows

        def clear(tile):
            first = pl.multiple_of(tile * tile_rows, tile_rows)
            return pltpu.make_async_copy(zero_ref, dst_hbm.at[pl.ds(first, tile_rows)], zero_sem)

        def start_tail(tile, carry):
            clear(tile).start()
            return carry

        def wait_tail(tile, carry):
            clear(tile).wait()
            return carry
        for c in range(N_CLASSES):
            @pl.when(used_ref[c] != 0)
            def _():
                clear(pad_tile_ref[c]).start()
        lax.fori_loop(used_ref[N_CLASSES], n_tiles, start_tail, 0)
        for c in range(N_CLASSES):
            @pl.when(used_ref[c] != 0)
            def _():
                clear(pad_tile_ref[c]).wait()
        lax.fori_loop(used_ref[N_CLASSES], n_tiles, wait_tail, 0)

    base = pl.program_id(0) * tm
    _row_copies(tm, lambda r: pltpu.make_async_copy(
        _token_slab(rows_ref, r * ROW_CHUNKS), _token_slab(dst_hbm, pos_ref[base + r]), sem))
    wg_out[...] = wg_ref[0].astype(BF16)
    wu_out[...] = wu_ref[0].astype(BF16)
    wd_out[...] = wd_ref[0].astype(BF16)
    pltpu.make_async_copy(rows_ref, dst_hbm.at[pl.ds(0, tm * ROW_CHUNKS)], sem).wait()


def _dispatch_rows(pos, pad_tile, used, rows, n_dst, w_gate, w_up, w_down, layer):
    t = rows.shape[0] // ROW_CHUNKS
    tm = TOKEN_TILE
    n_steps = t // tm
    assert N_EXPERTS % n_steps == 0
    eps = N_EXPERTS // n_steps
    d = D_MODEL
    w_in3 = lambda a, b: pl.BlockSpec((1, eps, a, b), lambda i, p, pt, us: (layer, i, 0, 0))
    w_out3 = lambda a, b: pl.BlockSpec((eps, a, b), lambda i, p, pt, us: (i, 0, 0))
    return pl.pallas_call(
        _dispatch_kernel,
        out_shape=(jax.ShapeDtypeStruct((n_dst * ROW_CHUNKS, LANES), rows.dtype),
                   jax.ShapeDtypeStruct((N_EXPERTS, d, D_EXPERT), BF16),
                   jax.ShapeDtypeStruct((N_EXPERTS, d, D_EXPERT), BF16),
                   jax.ShapeDtypeStruct((N_EXPERTS, D_EXPERT, d), BF16)),
        grid_spec=pltpu.PrefetchScalarGridSpec(
            num_scalar_prefetch=3, grid=(n_steps,),
            in_specs=[pl.BlockSpec((tm * ROW_CHUNKS, LANES), lambda i, p, pt, us: (i, 0)),
                      w_in3(d, D_EXPERT), w_in3(d, D_EXPERT), w_in3(D_EXPERT, d)],
            out_specs=(pl.BlockSpec(memory_space=pl.ANY),
                       w_out3(d, D_EXPERT), w_out3(d, D_EXPERT), w_out3(D_EXPERT, d)),
            scratch_shapes=[pltpu.VMEM((EXPERT_TILE * ROW_CHUNKS, LANES), rows.dtype),
                            pltpu.SemaphoreType.DMA, pltpu.SemaphoreType.DMA]),
        compiler_params=pltpu.CompilerParams(dimension_semantics=("arbitrary",), disable_bounds_checks=True,
                                             has_side_effects=True, vmem_limit_bytes=V7X_VMEM_LIMIT_BYTES),
        name="dispatch_rows",
    )(pos, pad_tile, used, rows, w_gate, w_up, w_down)


def _moe_kernel(elo_ref, ehi_ref, valid_ref, x_ref, wrt_ref, wgl_ref, wul_ref, wdl_ref, wgh_ref, wuh_ref,
                wdh_ref, y_ref):
    j = pl.program_id(0)

    @pl.when(valid_ref[j] == 0)
    def _():
        y_ref[...] = jnp.zeros_like(y_ref)

    @pl.when(valid_ref[j] != 0)
    def _():
        h2 = _load_rows(x_ref, EXPERT_TILE)
        s_lo = _sigmoid(jnp.sum(h2 * wrt_ref[pl.ds(elo_ref[j], 1), :], axis=1, keepdims=True))
        s_hi = _sigmoid(jnp.sum(h2 * wrt_ref[pl.ds(ehi_ref[j], 1), :], axis=1, keepdims=True))
        total = s_lo + s_hi
        hb = h2.astype(BF16)
        he_lo = _silu(_dot(hb, wgl_ref[0])) * _dot(hb, wul_ref[0])
        y = (s_lo / total) * _dot(he_lo.astype(BF16), wdl_ref[0])
        he_hi = _silu(_dot(hb, wgh_ref[0])) * _dot(hb, wuh_ref[0])
        _store_rows(y_ref, y + (s_hi / total) * _dot(he_hi.astype(BF16), wdh_ref[0]))


def _moe(rows_sorted, e_lo, e_hi, valid, w_router, wg, wu, wd):
    n_pad = rows_sorted.shape[0] // ROW_CHUNKS
    d = D_MODEL
    lo3 = lambda j, el, eh, va: (el[j], 0, 0)
    hi3 = lambda j, el, eh, va: (eh[j], 0, 0)
    tile = pl.BlockSpec((EXPERT_TILE * ROW_CHUNKS, LANES), lambda j, el, eh, va: (j, 0))
    return pl.pallas_call(
        _moe_kernel,
        out_shape=jax.ShapeDtypeStruct(rows_sorted.shape, F32),
        grid_spec=pltpu.PrefetchScalarGridSpec(
            num_scalar_prefetch=3, grid=(n_pad // EXPERT_TILE,),
            in_specs=[tile, _resident((N_EXPERTS, d)),
                      pl.BlockSpec((1, d, D_EXPERT), lo3), pl.BlockSpec((1, d, D_EXPERT), lo3),
                      pl.BlockSpec((1, D_EXPERT, d), lo3),
                      pl.BlockSpec((1, d, D_EXPERT), hi3), pl.BlockSpec((1, d, D_EXPERT), hi3),
                      pl.BlockSpec((1, D_EXPERT, d), hi3)],
            out_specs=tile),
        compiler_params=_params("arbitrary"),
        name="moe_experts",
    )(e_lo, e_hi, valid, rows_sorted, w_router.T, wg, wu, wd, wg, wu, wd)


def _collect_kernel(pos_ref, x_ref, ys_hbm, mod_ref, fg_ref, o_ref, buf0, buf1, sems, *, final):
    tm = x_ref.shape[0]
    i = pl.program_id(0)
    n = pl.num_programs(0)
    bufs = (buf0, buf1)

    def fetch(tile, slot):
        base = tile * tm
        _row_copies(tm, lambda r: pltpu.make_async_copy(
            _token_slab(ys_hbm, pos_ref[base + r]), _token_slab(bufs[slot], r * ROW_CHUNKS), sems.at[slot]))

    def finish(slot):
        pltpu.make_async_copy(ys_hbm.at[pl.ds(0, tm * ROW_CHUNKS)], bufs[slot], sems.at[slot]).wait()
        xo = x_ref[...] + mod_ref[0][5:6] * _load_rows(bufs[slot], tm)
        if final:
            xo = _rms(xo, fg_ref[...])
        o_ref[...] = xo

    @pl.when(i == 0)
    def _():
        fetch(0, 0)

    for slot in range(2):
        @pl.when(jnp.logical_and(i + 1 < n, (i + 1) % 2 == slot))
        def _():
            fetch(i + 1, slot)

    for slot in range(2):
        @pl.when(i % 2 == slot)
        def _():
            finish(slot)


def _collect_residual(pos, xn, y_sorted, mod_l, final_g, final, seq):
    t, d = xn.shape
    tm = COLLECT_TILE
    rows = pl.BlockSpec((tm, d), lambda i, p: (i, 0))
    return pl.pallas_call(
        functools.partial(_collect_kernel, final=final),
        out_shape=jax.ShapeDtypeStruct((t, d), F32),
        grid_spec=pltpu.PrefetchScalarGridSpec(
            num_scalar_prefetch=1, grid=(t // tm,),
            in_specs=[rows, pl.BlockSpec(memory_space=pl.ANY),
                      pl.BlockSpec((1, 6, d), lambda i, p: ((i * tm) // seq, 0, 0)),
                      pl.BlockSpec((1, d), lambda i, p: (0, 0))],
            out_specs=rows,
            scratch_shapes=[pltpu.VMEM((tm * ROW_CHUNKS, LANES), F32), pltpu.VMEM((tm * ROW_CHUNKS, LANES), F32),
                            pltpu.SemaphoreType.DMA((2,))]),
        compiler_params=pltpu.CompilerParams(dimension_semantics=("arbitrary",), disable_bounds_checks=True,
                                             vmem_limit_bytes=V7X_VMEM_LIMIT_BYTES),
        name="collect_residual",
    )(pos, xn, y_sorted, mod_l, final_g[None, :])


def kernel(x, c, positions, ada_w, ada_b, norm1_g, w_in, hg_lb_logits, hg_norm_g, q_norm_g, w_q_up,
           kv_norm_g, w_kv_up, w_br_a, w_br_b, w_out, norm2_g, w_router, router_bias, w_gate, w_up,
           w_down, final_g):
    batch, seq, d = x.shape
    depth = ada_w.shape[0]
    t = batch * seq
    n_tiles_e = (t + N_CLASSES * (EXPERT_TILE - 1)) // EXPERT_TILE
    mod = _modulation(c, ada_w, ada_b).reshape(depth, batch, 6, d)
    cos_t, sin_t = _rope_tables(positions)
    x2 = x.reshape(t, d)
    for l in range(depth):
        hq, hf, hi, hgate, gsig, q, k, v = _input_projection(
            x2, mod[l], norm1_g[l], cos_t, sin_t, w_in[l], q_norm_g[l], w_q_up[l],
            kv_norm_g[l], w_kv_up[l], seq)
        oa = _hgrn(hg_lb_logits, hg_norm_g[l], hq, hf, hi, hgate, l, batch, seq)
        ob = _attention(q, k, v, batch, seq)
        xn, rows, meta = _merge(x2, oa, ob, gsig, mod[l], w_br_a[l], w_br_b[l], w_out[l], norm2_g[l],
                                w_router, router_bias, seq)
        pos, pad_tile, used, e_lo, e_hi, valid = _dispatch_plan(meta, n_tiles_e)
        rows_sorted, wg, wu, wd = _dispatch_rows(pos, pad_tile, used, rows, n_tiles_e * EXPERT_TILE,
                                                 w_gate, w_up, w_down, l)
        y_sorted = _moe(rows_sorted, e_lo, e_hi, valid, w_router, wg, wu, wd)
        x2 = _collect_residual(pos, xn, y_sorted, mod[l], final_g, l == depth - 1, seq)
    return x2.reshape(batch, seq, d)
```

```python
import functools
import math

import jax
import jax.numpy as jnp
from jax import lax
from jax.experimental import pallas as pl
from jax.experimental.pallas import tpu as pltpu

F32 = jnp.float32
BF16 = jnp.bfloat16

D_MODEL = 1024
CHUNK = 64
EPS = 1e-6

HG_HEADS = 4
HG_DK = 128
HG_DV = 128
HG_WIDTH = HG_HEADS * HG_DV
HG_SUB = 8
HG_LEVELS = (8, 16, 32)

MLA_HEADS = 8
MLA_NOPE = 64
MLA_ROPE = 32
MLA_V = 64
MLA_Q_LORA = 384
MLA_KV_LORA = 256
MLA_DQK = MLA_NOPE + MLA_ROPE
MLA_WIDTH = MLA_HEADS * MLA_V
ROPE_BASE = 10000.0
HEAD_PAD = 128
LOG2_E = math.log2(math.e)

N_EXPERTS = 16
N_GROUPS = 4
EXPERTS_PER_GROUP = N_EXPERTS // N_GROUPS
D_EXPERT = 512

IN_SIZES = (HG_HEADS * HG_DK, HG_HEADS * HG_DK, HG_HEADS * HG_DV, HG_WIDTH,
            MLA_Q_LORA, MLA_KV_LORA, MLA_ROPE, 2 * D_MODEL)

V7X_VMEM_LIMIT_BYTES = 56 * 1024 * 1024


def _params(*sem):
    return pltpu.CompilerParams(dimension_semantics=sem, vmem_limit_bytes=V7X_VMEM_LIMIT_BYTES)


def _resident(shape):
    nd = len(shape)
    return pl.BlockSpec(shape, lambda *_: (0,) * nd, pipeline_mode=pl.Buffered(1))


def _sigmoid(x):
    return 1.0 / (1.0 + jnp.exp(-x))


def _silu(x):
    return x * _sigmoid(x)


def _dot(a, b):
    return jnp.dot(a, b, preferred_element_type=F32)


def _dot_nt(a, b):
    return lax.dot_general(a, b, (((1,), (1,)), ((), ())), preferred_element_type=F32)


def _dot_tn(a, b):
    return lax.dot_general(a, b, (((0,), (0,)), ((), ())), preferred_element_type=F32)


def _split_bf16(x):
    hi = x.astype(BF16)
    lo = (x - hi.astype(F32)).astype(BF16)
    return hi, lo


def _rms(x, g):
    return x * lax.rsqrt(jnp.mean(x * x, axis=-1, keepdims=True) + EPS) * g


def _mod_kernel(c_ref, w_ref, b_ref, o_ref):
    @pl.when(pl.program_id(1) == 0)
    def _():
        o_ref[0] = jnp.broadcast_to(b_ref[0], o_ref.shape[1:])
    ca = _silu(c_ref[...])
    o_ref[0] += _dot(ca.astype(BF16), w_ref[0].astype(BF16))


def _modulation(c, ada_w, ada_b):
    depth, d, n = ada_w.shape
    b = c.shape[0]
    tk = 256
    return pl.pallas_call(
        _mod_kernel,
        out_shape=jax.ShapeDtypeStruct((depth, b, n), F32),
        grid=(depth, d // tk),
        in_specs=[pl.BlockSpec((b, tk), lambda l, k: (0, k)),
                  pl.BlockSpec((1, tk, n), lambda l, k: (l, k, 0)),
                  pl.BlockSpec((1, 1, n), lambda l, k: (l, 0, 0))],
        out_specs=pl.BlockSpec((1, b, n), lambda l, k: (l, 0, 0)),
        compiler_params=_params("parallel", "arbitrary"),
        name="adaln_modulation",
    )(c, ada_w, ada_b.reshape(depth, 1, n))


def _rope_kernel(pos_ref, inv_ref, msk_ref, sgn_ref, cos_ref, sin_ref):
    ang = pos_ref[...].astype(F32) * inv_ref[...]
    cos_ref[...] = jnp.cos(ang) * msk_ref[...]
    sin_ref[...] = jnp.sin(ang) * sgn_ref[...]


def _rope_tables(positions):
    t = positions.size
    half = MLA_ROPE // 2
    inv = ROPE_BASE ** (-jnp.arange(half, dtype=F32) / half)
    z64, z32, one16 = jnp.zeros((MLA_NOPE,), F32), jnp.zeros((32,), F32), jnp.ones((half,), F32)
    inv_row = jnp.concatenate([z64, inv, inv, z32])[None, :]
    msk_row = jnp.concatenate([z64, one16, one16, z32])[None, :]
    sgn_row = jnp.concatenate([z64, -one16, one16, z32])[None, :]
    tr = min(2048, t)
    row = pl.BlockSpec((1, HEAD_PAD), lambda i: (0, 0))
    tab = pl.BlockSpec((tr, HEAD_PAD), lambda i: (i, 0))
    return pl.pallas_call(
        _rope_kernel,
        out_shape=(jax.ShapeDtypeStruct((t, HEAD_PAD), F32),) * 2,
        grid=(t // tr,),
        in_specs=[pl.BlockSpec((tr, 1), lambda i: (i, 0)), row, row, row],
        out_specs=(tab, tab),
        compiler_params=_params("parallel"),
        name="rope_tables",
    )(positions.reshape(t, 1), inv_row, msk_row, sgn_row)


def _proj_kernel(x_ref, mod_ref, n1_ref, cos_ref, sin_ref, wh_ref, wg_ref, wc_ref,
                 qn_ref, wqa_ref, wqs_ref, kn_ref, wk_ref, wv_ref, one_ref,
                 hq_ref, hf_ref, hi_ref, hgate_ref, gsig_ref, q_ref, k_ref, v_ref):
    x = x_ref[...]
    mod = mod_ref[0]
    h = _rms(x, n1_ref[...]) * (1.0 + mod[1:2]) + mod[0:1]
    hb = h.astype(BF16)

    ph = _dot(hb, wh_ref[...])
    w = HG_WIDTH
    hq_ref[...] = ph[:, 0:w].astype(BF16)
    hf_ref[...] = ph[:, w:2 * w]
    hi_ref[...] = ph[:, 2 * w:3 * w].astype(BF16)
    hgate_ref[...] = ph[:, 3 * w:4 * w].astype(BF16)

    gsig_ref[...] = _sigmoid(_dot(hb, wg_ref[...])).astype(BF16)

    pc = _dot(hb, wc_ref[...])
    cq = pc[:, 0:MLA_Q_LORA]
    ckv = pc[:, MLA_Q_LORA:MLA_Q_LORA + MLA_KV_LORA]
    kra = pc[:, MLA_Q_LORA + MLA_KV_LORA:MLA_Q_LORA + MLA_KV_LORA + HEAD_PAD]
    krb = pc[:, MLA_Q_LORA + MLA_KV_LORA + HEAD_PAD:]

    cos_t = cos_ref[...]
    sin_t = sin_ref[...]
    lane = lax.broadcasted_iota(jnp.int32, cos_t.shape, 1)
    scale = MLA_DQK ** -0.5 * LOG2_E
    cq_tab = jnp.tile(scale * (cos_t + jnp.where(lane < MLA_NOPE, 1.0, 0.0)), (1, MLA_HEADS))
    sq_tab = jnp.tile(scale * sin_t, (1, MLA_HEADS))

    cqn = _rms(cq, qn_ref[...]).astype(BF16)
    q = _dot(cqn, wqa_ref[...]) * cq_tab + _dot(cqn, wqs_ref[...]) * sq_tab
    q_ref[...] = q.astype(BF16)

    ckvn = _rms(ckv, kn_ref[...]).astype(BF16)
    kpe = kra * cos_t + krb * sin_t
    k_ref[...] = (_dot(ckvn, wk_ref[...]) + jnp.tile(kpe, (1, MLA_HEADS))).astype(BF16)
    v_ref[...] = (_dot(ckvn, wv_ref[...]) + one_ref[...]).astype(BF16)


def _pad_heads(w, lo, hi, at):
    k, nh, _ = w.shape
    out = jnp.zeros((k, nh, HEAD_PAD), w.dtype)
    out = out.at[:, :, at:at + (hi - lo)].set(w[:, :, lo:hi])
    return out


def _input_projection(x2, mod_l, n1, cos_t, sin_t, w_in, q_norm_g, w_q_up, kv_norm_g, w_kv_up, seq):
    t, d = x2.shape
    tm = 256
    splits = [0]
    for s in IN_SIZES:
        splits.append(splits[-1] + s)
    kr = w_in[:, splits[6]:splits[7]]
    half = MLA_ROPE // 2
    z64 = jnp.zeros((d, MLA_NOPE), F32)
    z32 = jnp.zeros((d, HEAD_PAD - MLA_NOPE - MLA_ROPE), F32)
    kr_a = jnp.concatenate([z64, kr, z32], axis=1)
    kr_b = jnp.concatenate([z64, kr[:, half:], kr[:, :half], z32], axis=1)
    w_h = w_in[:, splits[0]:splits[4]].astype(BF16)
    w_g = w_in[:, splits[7]:splits[8]].astype(BF16)
    w_c = jnp.concatenate([w_in[:, splits[4]:splits[6]], kr_a, kr_b], axis=1).astype(BF16)

    wq = w_q_up.reshape(MLA_Q_LORA, MLA_HEADS, MLA_DQK)
    wq_all = _pad_heads(wq, 0, MLA_DQK, 0).reshape(MLA_Q_LORA, -1).astype(BF16)
    wq_swap = (_pad_heads(wq, MLA_NOPE + half, MLA_DQK, MLA_NOPE)
               + _pad_heads(wq, MLA_NOPE, MLA_NOPE + half, MLA_NOPE + half))
    wq_swap = wq_swap.reshape(MLA_Q_LORA, -1).astype(BF16)
    wkv = w_kv_up.reshape(MLA_KV_LORA, MLA_HEADS, MLA_NOPE + MLA_V)
    wk_all = _pad_heads(wkv, 0, MLA_NOPE, 0).reshape(MLA_KV_LORA, -1).astype(BF16)
    wv_all = _pad_heads(wkv, MLA_NOPE, MLA_NOPE + MLA_V, 0).reshape(MLA_KV_LORA, -1).astype(BF16)
    ones_row = jnp.tile(jnp.zeros((HEAD_PAD,), F32).at[MLA_V].set(1.0), MLA_HEADS)[None, :]

    hp = MLA_HEADS * HEAD_PAD
    rows = lambda n: pl.BlockSpec((tm, n), lambda i: (i, 0))
    outs = [(HG_WIDTH, BF16), (HG_WIDTH, F32), (HG_WIDTH, BF16), (HG_WIDTH, BF16),
            (2 * d, BF16), (hp, BF16), (hp, BF16), (hp, BF16)]
    return pl.pallas_call(
        _proj_kernel,
        out_shape=tuple(jax.ShapeDtypeStruct((t, n), dt) for n, dt in outs),
        grid=(t // tm,),
        in_specs=[rows(d),
                  pl.BlockSpec((1, 6, d), lambda i: ((i * tm) // seq, 0, 0)),
                  _resident((1, d)),
                  rows(HEAD_PAD), rows(HEAD_PAD),
                  _resident(w_h.shape), _resident(w_g.shape), _resident(w_c.shape),
                  _resident((1, MLA_Q_LORA)), _resident(wq_all.shape), _resident(wq_swap.shape),
                  _resident((1, MLA_KV_LORA)), _resident(wk_all.shape), _resident(wv_all.shape),
                  _resident((1, hp))],
        out_specs=tuple(rows(n) for n, _ in outs),
        compiler_params=_params("parallel"),
        name="norm_input_projection",
    )(x2, mod_l, n1[None, :], cos_t, sin_t, w_h, w_g, w_c,
      q_norm_g[None, :], wq_all, wq_swap, kv_norm_g[None, :], wk_all, wv_all, ones_row)


def _hgrn_kernel(lbl_ref, gn_ref, q_ref, f_ref, v_ref, gate_ref, o_ref, st_ref, *, layer, n_chunks):
    @pl.when(pl.program_id(1) == 0)
    def _():
        st_ref[...] = jnp.zeros_like(st_ref)

    lg = lbl_ref[...]
    ex = jnp.exp(lg - jnp.max(lg, axis=0, keepdims=True))
    soft = ex / jnp.sum(ex, axis=0, keepdims=True)
    lb_all = jnp.zeros_like(soft[0:1])
    for i in range(1, layer + 1):
        lb_all = lb_all + soft[i:i + 1]

    r_i = lax.broadcasted_iota(jnp.int32, (CHUNK, CHUNK), 0)
    c_i = lax.broadcasted_iota(jnp.int32, (CHUNK, CHUNK), 1)
    tri = jnp.where(r_i >= c_i, 1.0, 0.0).astype(BF16)
    level_masks = []
    for hs in HG_LEVELS:
        same = (r_i // (2 * hs)) == (c_i // (2 * hs))
        level_masks.append(jnp.logical_and(same, jnp.logical_and(r_i % (2 * hs) >= hs, c_i % (2 * hs) < hs)))
    sub_row = lax.broadcasted_iota(jnp.int32, (HG_SUB, 1), 0)
    n_sub = CHUNK // HG_SUB

    def chunk_body(ci, carry):
        r0 = pl.multiple_of(ci * CHUNK, CHUNK)
        rows = pl.ds(r0, CHUNK)
        for h in range(HG_HEADS):
            sl = slice(h * HG_DK, (h + 1) * HG_DK)
            lb = lb_all[:, sl]
            q = q_ref[rows, sl].astype(F32)
            vb = v_ref[rows, sl]
            v = vb.astype(F32)
            f = lb + (1.0 - lb) * _sigmoid(f_ref[rows, sl])
            g = jnp.log2(f)
            k = 1.0 - f
            qf = _silu(q) * (HG_DK ** -0.5)
            g_hi, g_lo = _split_bf16(g)
            b = _dot(tri, g_hi) + _dot(tri, g_lo)

            att = jnp.zeros((CHUNK, CHUNK), F32)
            for hs, mask in zip(HG_LEVELS, level_masks):
                ref = jnp.concatenate(
                    [jnp.broadcast_to(b[j + hs - 1:j + hs], (2 * hs, HG_DK)) for j in range(0, CHUNK, 2 * hs)],
                    axis=0)
                qt = (qf * jnp.exp2(b - ref)).astype(BF16)
                kt = (k * jnp.exp2(ref - b)).astype(BF16)
                att = att + jnp.where(mask, _dot_nt(qt, kt), 0.0)
            o = _dot(att.astype(BF16), vb)

            blocks = []
            for i in range(n_sub):
                rs = slice(i * HG_SUB, (i + 1) * HG_SUB)
                b_i, q_i, k_i, v_i = b[rs], qf[rs], k[rs], v[rs]
                acc = jnp.zeros((HG_SUB, HG_DV), F32)
                for s in range(HG_SUB):
                    a = jnp.sum(jnp.exp2(b_i - b_i[s:s + 1]) * (q_i * k_i[s:s + 1]), axis=1, keepdims=True)
                    acc = acc + jnp.where(sub_row >= s, a, 0.0) * v_i[s:s + 1]
                blocks.append(acc)
            o = o + jnp.concatenate(blocks, axis=0)

            st = st_ref[h]
            o = o + _dot_nt((qf * jnp.exp2(b)).astype(BF16), st.astype(BF16))
            b_last = b[CHUNK - 1:CHUNK]
            kd = (k * jnp.exp2(b_last - b)).astype(BF16)
            st_ref[h] = st * jnp.exp2(b_last) + _dot_tn(vb, kd)

            gt = gate_ref[rows, sl].astype(F32)
            o_ref[rows, sl] = (_rms(o, gn_ref[:, sl]) * _silu(gt)).astype(BF16)
        return carry

    lax.fori_loop(0, n_chunks, chunk_body, 0, unroll=True)


def _hgrn(hg_lb_logits, hg_norm_g, hq, hf, hi, hgate, layer, batch, seq):
    t = hq.shape[0]
    lc = min(512, seq)
    nb = seq // lc
    rows = pl.BlockSpec((lc, HG_WIDTH), lambda b, j: (b * nb + j, 0))
    kern = functools.partial(_hgrn_kernel, layer=layer, n_chunks=lc // CHUNK)
    return pl.pallas_call(
        kern,
        out_shape=jax.ShapeDtypeStruct((t, HG_WIDTH), BF16),
        grid=(batch, nb),
        in_specs=[pl.BlockSpec(hg_lb_logits.shape, lambda b, j: (0, 0)),
                  pl.BlockSpec((1, HG_WIDTH), lambda b, j: (0, 0)),
                  rows, rows, rows, rows],
        out_specs=rows,
        scratch_shapes=[pltpu.VMEM((HG_HEADS, HG_DV, HG_DK), F32)],
        compiler_params=_params("parallel", "arbitrary"),
        name="hgrn2_chunkwise",
    )(hg_lb_logits, hg_norm_g[None, :], hq, hf, hi, hgate)


ATTN_BLOCK = 1024
STEP_FULL, STEP_TOP_DIAG, STEP_BOTTOM_DIAG = range(3)


def _attn_kernel(qi_ref, kj_ref, kind_ref, q_ref, k_ref, v_ref, o_ref, m_ref, acc_ref, *, tb):
    del qi_ref, kj_ref
    p_id = pl.program_id(1)
    kind = kind_ref[p_id]
    half = tb // 2

    def reset():
        m_ref[...] = jnp.full_like(m_ref, -jnp.inf)
        acc_ref[...] = jnp.zeros_like(acc_ref)

    def step(r0, nr, nk, diagonal):
        rows = slice(r0, r0 + nr)
        if diagonal:
            qc = (lax.broadcasted_iota(jnp.int32, (nr, nk), 0) + r0) // CHUNK
            kc = lax.broadcasted_iota(jnp.int32, (nr, nk), 1) // CHUNK
            visible = kc <= qc
        for h in range(MLA_HEADS):
            sl = slice(h * HEAD_PAD, (h + 1) * HEAD_PAD)
            s = _dot_nt(q_ref[rows, sl], k_ref[0:nk, sl])
            if diagonal:
                s = jnp.where(visible, s, -jnp.inf)
            tiles = [s[:, t * HEAD_PAD:(t + 1) * HEAD_PAD] for t in range(nk // HEAD_PAD)]
            m_tile = tiles[0]
            for tl in tiles[1:]:
                m_tile = jnp.maximum(m_tile, tl)
            m_prev = m_ref[h, rows]
            m_new = jnp.maximum(m_prev, jnp.max(m_tile, axis=1, keepdims=True))
            alpha = jnp.exp2(m_prev - m_new)
            p = jnp.concatenate([jnp.exp2((tl - m_new).astype(BF16)) for tl in tiles], axis=1)
            acc_ref[h, rows] = acc_ref[h, rows] * alpha + _dot(p, v_ref[0:nk, sl])
            m_ref[h, rows] = m_new

    @pl.when(p_id == 0)
    def _():
        reset()

    @pl.when(kind == STEP_FULL)
    def _():
        step(0, tb, tb, False)

    @pl.when(kind == STEP_TOP_DIAG)
    def _():
        step(0, half, half, True)

    @pl.when(kind == STEP_BOTTOM_DIAG)
    def _():
        step(half, half, tb, True)
        outs = []
        for h in range(MLA_HEADS):
            acc = acc_ref[h]
            outs.append(acc[:, 0:MLA_V] / acc[:, MLA_V:MLA_V + 1])
        o_ref[...] = jnp.concatenate(outs, axis=1).astype(BF16)
        reset()


def _attention(q, k, v, batch, seq):
    t, hp = q.shape
    tb = min(ATTN_BLOCK, seq)
    nb = seq // tb
    steps = []
    for i in range(nb):
        steps += [(i, j, STEP_FULL) for j in range(i)]
        steps += [(i, i, STEP_TOP_DIAG), (i, i, STEP_BOTTOM_DIAG)]
    q_of, k_of, kinds = (jnp.asarray([st[c] for st in steps], jnp.int32) for c in range(3))
    qspec = pl.BlockSpec((tb, hp), lambda b, p, qi, kj, kd: (b * nb + qi[p], 0))
    kspec = pl.BlockSpec((tb, hp), lambda b, p, qi, kj, kd: (b * nb + kj[p], 0))
    return pl.pallas_call(
        functools.partial(_attn_kernel, tb=tb),
        out_shape=jax.ShapeDtypeStruct((t, MLA_WIDTH), BF16),
        grid_spec=pltpu.PrefetchScalarGridSpec(
            num_scalar_prefetch=3,
            grid=(batch, len(steps)),
            in_specs=[qspec, kspec, kspec],
            out_specs=pl.BlockSpec((tb, MLA_WIDTH), lambda b, p, qi, kj, kd: (b * nb + qi[p], 0)),
            scratch_shapes=[pltpu.VMEM((MLA_HEADS, tb, HEAD_PAD), F32),
                            pltpu.VMEM((MLA_HEADS, tb, HEAD_PAD), F32)]),
        compiler_params=_params("parallel", "arbitrary"),
        name="mla_flash_attention",
    )(q_of, k_of, kinds, q, k, v)


PAIR_ORDER = ((0, 1), (0, 2), (0, 3), (1, 3), (1, 2), (2, 3))
PAIRS_PER_GROUP = len(PAIR_ORDER)
N_CLASSES = N_GROUPS * PAIRS_PER_GROUP
LANES = 128
ROW_CHUNKS = D_MODEL // LANES
TOKEN_TILE = 1024
COLLECT_TILE = 512
EXPERT_TILE = 256


N_CLASS_ROWS = 32


def _route(scores, bias):
    biased = scores + bias
    col = [biased[e:e + 1, :] for e in range(N_EXPERTS)]
    gscore = []
    for g in range(N_GROUPS):
        a, b, c, d = col[g * EXPERTS_PER_GROUP:(g + 1) * EXPERTS_PER_GROUP]
        gscore.append(jnp.maximum(jnp.maximum(jnp.maximum(a + b, a + c), jnp.maximum(a + d, b + c)),
                                  jnp.maximum(b + d, c + d)))
    sel = []
    for g in range(N_GROUPS):
        ok = None
        for o in range(N_GROUPS):
            if o == g:
                continue
            cond = (gscore[g] > gscore[o]) if o < g else (gscore[g] >= gscore[o])
            ok = cond if ok is None else jnp.logical_and(ok, cond)
        grp = col[g * EXPERTS_PER_GROUP:(g + 1) * EXPERTS_PER_GROUP]
        for e in range(EXPERTS_PER_GROUP):
            beaten = jnp.zeros_like(grp[e])
            for o in range(EXPERTS_PER_GROUP):
                if o == e:
                    continue
                ahead = (grp[o] >= grp[e]) if o < e else (grp[o] > grp[e])
                beaten = beaten + jnp.where(ahead, 1.0, 0.0)
            sel.append(jnp.logical_and(ok, beaten < 1.5))
    lo = jnp.full_like(col[0], float(N_EXPERTS))
    hi = jnp.full_like(col[0], -1.0)
    for e in range(N_EXPERTS):
        lo = jnp.where(sel[e], jnp.minimum(lo, float(e)), lo)
        hi = jnp.where(sel[e], jnp.maximum(hi, float(e)), hi)
    return lo, hi


def _store_rows(ref, x):
    n = x.shape[0]
    for c in range(ROW_CHUNKS):
        ref[pl.ds(c, n, stride=ROW_CHUNKS), :] = x[:, c * LANES:(c + 1) * LANES]


def _load_rows(ref, n):
    return jnp.concatenate([ref[pl.ds(c, n, stride=ROW_CHUNKS), :] for c in range(ROW_CHUNKS)], axis=1)


def _merge_kernel(x_ref, oa_ref, ob_ref, g_ref, mod_ref, wa_ref, wb_ref, wo_ref, n2_ref,
                  wrt_ref, rb_ref, xn_ref, row_ref, meta_ref):
    d = D_MODEL
    tm = x_ref.shape[0]
    mod = mod_ref[0]
    ya = _dot(oa_ref[...], wa_ref[...])
    yb = _dot(ob_ref[...], wb_ref[...])
    g = g_ref[...].astype(F32)
    merged = g[:, 0:d] * ya + g[:, d:2 * d] * yb
    xn = x_ref[...] + mod[2:3] * _dot(merged.astype(BF16), wo_ref[...])
    xn_ref[...] = xn
    h2 = _rms(xn, n2_ref[...]) * (1.0 + mod[4:5]) + mod[3:4]
    _store_rows(row_ref, h2)

    h_hi, h_lo = _split_bf16(h2)
    w_hi, w_lo = _split_bf16(wrt_ref[...])
    scores = _sigmoid(_dot_nt(w_hi, h_hi) + _dot_nt(w_hi, h_lo) + _dot_nt(w_lo, h_hi))
    lo, hi = _route(scores, rb_ref[...])

    grp = jnp.floor(lo * (1.0 / EXPERTS_PER_GROUP))
    a = lo - grp * EXPERTS_PER_GROUP
    b = hi - grp * EXPERTS_PER_GROUP
    pair = jnp.zeros_like(a)
    for p, (p_lo, p_hi) in enumerate(PAIR_ORDER):
        pair = pair + jnp.where(jnp.logical_and(a == p_lo, b == p_hi), float(p), 0.0)
    cls = grp * PAIRS_PER_GROUP + pair
    c_row = lax.broadcasted_iota(jnp.int32, (N_CLASS_ROWS, tm), 0).astype(F32)
    onehot = jnp.where(c_row == cls, 1.0, 0.0)
    r_i = lax.broadcasted_iota(jnp.int32, (tm, tm), 0)
    c_i = lax.broadcasted_iota(jnp.int32, (tm, tm), 1)
    earlier = jnp.where(r_i < c_i, 1.0, 0.0).astype(BF16)
    rank = jnp.sum(onehot * _dot(onehot.astype(BF16), earlier), axis=0, keepdims=True)
    m_row = lax.broadcasted_iota(jnp.int32, (8, tm), 0)
    meta_ref[...] = jnp.where(m_row == 0, cls, jnp.where(m_row == 1, rank, 0.0))


def _merge(x2, oa, ob, gsig, mod_l, w_br_a, w_br_b, w_out, n2, w_router, router_bias, seq):
    t, d = x2.shape
    tm = TOKEN_TILE
    rows = lambda n: pl.BlockSpec((tm, n), lambda i: (i, 0))
    return pl.pallas_call(
        _merge_kernel,
        out_shape=(jax.ShapeDtypeStruct((t, d), F32), jax.ShapeDtypeStruct((t * ROW_CHUNKS, LANES), F32),
                   jax.ShapeDtypeStruct((8, t), F32)),
        grid=(t // tm,),
        in_specs=[rows(d), rows(HG_WIDTH), rows(MLA_WIDTH), rows(2 * d),
                  pl.BlockSpec((1, 6, d), lambda i: ((i * tm) // seq, 0, 0)),
                  _resident(w_br_a.shape), _resident(w_br_b.shape), _resident(w_out.shape),
                  _resident((1, d)), _resident((N_EXPERTS, d)), _resident((N_EXPERTS, 1))],
        out_specs=(rows(d), pl.BlockSpec((tm * ROW_CHUNKS, LANES), lambda i: (i, 0)),
                   pl.BlockSpec((8, tm), lambda i: (0, i))),
        compiler_params=_params("parallel"),
        name="merge_outproj_router",
    )(x2, oa, ob, gsig, mod_l, w_br_a.astype(BF16), w_br_b.astype(BF16), w_out.astype(BF16),
      n2[None, :], w_router.T, router_bias[:, None])


def _dispatch_plan(meta, n_tiles_e):
    t = meta.shape[1]
    n_tok_tiles = t // TOKEN_TILE
    cls = meta[0].astype(jnp.int32)
    rank = meta[1].astype(jnp.int32)
    own = cls.reshape(n_tok_tiles, TOKEN_TILE, 1) == jnp.arange(N_CLASSES, dtype=jnp.int32)
    counts = jnp.sum(own.astype(jnp.int32), axis=1)
    total = jnp.sum(counts, axis=0)
    total_pad = (total + EXPERT_TILE - 1) // EXPERT_TILE * EXPERT_TILE
    ends = jnp.cumsum(total_pad)
    base = (ends - total_pad)[None, :] + jnp.cumsum(counts, axis=0) - counts
    pos = jnp.sum(jnp.where(own, base[:, None, :], 0), axis=2).reshape(t) + rank

    tile_start = jnp.arange(n_tiles_e, dtype=jnp.int32) * EXPERT_TILE
    n_valid = ends[-1] // EXPERT_TILE
    tile_cls = jnp.sum((tile_start[:, None] >= ends[None, :]).astype(jnp.int32), axis=1)
    last_cls = jnp.take(tile_cls, n_valid - 1)
    valid = jnp.arange(n_tiles_e, dtype=jnp.int32) < n_valid
    tile_cls = jnp.where(valid, tile_cls, last_cls)
    pair_lo = jnp.asarray([p[0] for p in PAIR_ORDER], jnp.int32)
    pair_hi = jnp.asarray([p[1] for p in PAIR_ORDER], jnp.int32)
    grp = tile_cls // PAIRS_PER_GROUP
    e_lo = grp * EXPERTS_PER_GROUP + jnp.take(pair_lo, tile_cls % PAIRS_PER_GROUP)
    e_hi = grp * EXPERTS_PER_GROUP + jnp.take(pair_hi, tile_cls % PAIRS_PER_GROUP)
    pad_tile = (ends // EXPERT_TILE - 1).astype(jnp.int32)
    used = jnp.concatenate([(total > 0).astype(jnp.int32), n_valid[None].astype(jnp.int32)])
    return (pos * ROW_CHUNKS).astype(jnp.int32), pad_tile, used, e_lo, e_hi, valid.astype(jnp.int32)


DMA_UNROLL = 8


def _row_copies(n_rows, make_copy):
    def issue(ui, carry):
        for u in range(DMA_UNROLL):
            make_copy(ui * DMA_UNROLL + u).start(priority=u % 2)
        return carry
    lax.fori_loop(0, n_rows // DMA_UNROLL, issue, 0)


def _token_slab(ref, first_row):
    return ref.at[pl.ds(pl.multiple_of(first_row, ROW_CHUNKS), ROW_CHUNKS)]


def _dispatch_kernel(pos_ref, pad_tile_ref, used_ref, rows_ref, wg_ref, wu_ref, wd_ref,
                     dst_hbm, wg_out, wu_out, wd_out, zero_ref, sem, zero_sem):
    tm = rows_ref.shape[0] // ROW_CHUNKS
    tile_rows = EXPERT_TILE * ROW_CHUNKS

    @pl.when(pl.program_id(0) == 0)
    def _():
        zero_ref[...] = jnp.zeros_like(zero_ref)
        n_tiles = dst_hbm.shape[0] // tile_rows

        def clear(tile):
            first = pl.multiple_of(tile * tile_rows, tile_rows)
            return pltpu.make_async_copy(zero_ref, dst_hbm.at[pl.ds(first, tile_rows)], zero_sem)

        def start_tail(tile, carry):
            clear(tile).start()
            return carry

        def wait_tail(tile, carry):
            clear(tile).wait()
            return carry
        for c in range(N_CLASSES):
            @pl.when(used_ref[c] != 0)
            def _():
                clear(pad_tile_ref[c]).start()
        lax.fori_loop(used_ref[N_CLASSES], n_tiles, start_tail, 0)
        for c in range(N_CLASSES):
            @pl.when(used_ref[c] != 0)
            def _():
                clear(pad_tile_ref[c]).wait()
        lax.fori_loop(used_ref[N_CLASSES], n_tiles, wait_tail, 0)

    base = pl.program_id(0) * tm
    _row_copies(tm, lambda r: pltpu.make_async_copy(
        _token_slab(rows_ref, r * ROW_CHUNKS), _token_slab(dst_hbm, pos_ref[base + r]), sem))
    wg_out[...] = wg_ref[0].astype(BF16)
    wu_out[...] = wu_ref[0].astype(BF16)
    wd_out[...] = wd_ref[0].astype(BF16)
    pltpu.make_async_copy(rows_ref, dst_hbm.at[pl.ds(0, tm * ROW_CHUNKS)], sem).wait()


def _dispatch_rows(pos, pad_tile, used, rows, n_dst, w_gate, w_up, w_down, layer):
    t = rows.shape[0] // ROW_CHUNKS
    tm = TOKEN_TILE
    n_steps = t // tm
    assert N_EXPERTS % n_steps == 0
    eps = N_EXPERTS // n_steps
    d = D_MODEL
    w_in3 = lambda a, b: pl.BlockSpec((1, eps, a, b), lambda i, p, pt, us: (layer, i, 0, 0))
    w_out3 = lambda a, b: pl.BlockSpec((eps, a, b), lambda i, p, pt, us: (i, 0, 0))
    return pl.pallas_call(
        _dispatch_kernel,
        out_shape=(jax.ShapeDtypeStruct((n_dst * ROW_CHUNKS, LANES), rows.dtype),
                   jax.ShapeDtypeStruct((N_EXPERTS, d, D_EXPERT), BF16),
                   jax.ShapeDtypeStruct((N_EXPERTS, d, D_EXPERT), BF16),
                   jax.ShapeDtypeStruct((N_EXPERTS, D_EXPERT, d), BF16)),
        grid_spec=pltpu.PrefetchScalarGridSpec(
            num_scalar_prefetch=3, grid=(n_steps,),
            in_specs=[pl.BlockSpec((tm * ROW_CHUNKS, LANES), lambda i, p, pt, us: (i, 0)),
                      w_in3(d, D_EXPERT), w_in3(d, D_EXPERT), w_in3(D_EXPERT, d)],
            out_specs=(pl.BlockSpec(memory_space=pl.ANY),
                       w_out3(d, D_EXPERT), w_out3(d, D_EXPERT), w_out3(D_EXPERT, d)),
            scratch_shapes=[pltpu.VMEM((EXPERT_TILE * ROW_CHUNKS, LANES), rows.dtype),
                            pltpu.SemaphoreType.DMA, pltpu.SemaphoreType.DMA]),
        compiler_params=pltpu.CompilerParams(dimension_semantics=("arbitrary",), disable_bounds_checks=True,
                                             has_side_effects=True, vmem_limit_bytes=V7X_VMEM_LIMIT_BYTES),
        name="dispatch_rows",
    )(pos, pad_tile, used, rows, w_gate, w_up, w_down)


def _moe_kernel(elo_ref, ehi_ref, valid_ref, x_ref, wrt_ref, wgl_ref, wul_ref, wdl_ref, wgh_ref, wuh_ref,
                wdh_ref, y_ref):
    j = pl.program_id(0)

    @pl.when(valid_ref[j] == 0)
    def _():
        y_ref[...] = jnp.zeros_like(y_ref)

    @pl.when(valid_ref[j] != 0)
    def _():
        h2 = _load_rows(x_ref, EXPERT_TILE)
        s_lo = _sigmoid(jnp.sum(h2 * wrt_ref[pl.ds(elo_ref[j], 1), :], axis=1, keepdims=True))
        s_hi = _sigmoid(jnp.sum(h2 * wrt_ref[pl.ds(ehi_ref[j], 1), :], axis=1, keepdims=True))
        total = s_lo + s_hi
        hb = h2.astype(BF16)
        he_lo = _silu(_dot(hb, wgl_ref[0])) * _dot(hb, wul_ref[0])
        y = (s_lo / total) * _dot(he_lo.astype(BF16), wdl_ref[0])
        he_hi = _silu(_dot(hb, wgh_ref[0])) * _dot(hb, wuh_ref[0])
        _store_rows(y_ref, y + (s_hi / total) * _dot(he_hi.astype(BF16), wdh_ref[0]))


def _moe(rows_sorted, e_lo, e_hi, valid, w_router, wg, wu, wd):
    n_pad = rows_sorted.shape[0] // ROW_CHUNKS
    d = D_MODEL
    lo3 = lambda j, el, eh, va: (el[j], 0, 0)
    hi3 = lambda j, el, eh, va: (eh[j], 0, 0)
    tile = pl.BlockSpec((EXPERT_TILE * ROW_CHUNKS, LANES), lambda j, el, eh, va: (j, 0))
    return pl.pallas_call(
        _moe_kernel,
        out_shape=jax.ShapeDtypeStruct(rows_sorted.shape, F32),
        grid_spec=pltpu.PrefetchScalarGridSpec(
            num_scalar_prefetch=3, grid=(n_pad // EXPERT_TILE,),
            in_specs=[tile, _resident((N_EXPERTS, d)),
                      pl.BlockSpec((1, d, D_EXPERT), lo3), pl.BlockSpec((1, d, D_EXPERT), lo3),
                      pl.BlockSpec((1, D_EXPERT, d), lo3),
                      pl.BlockSpec((1, d, D_EXPERT), hi3), pl.BlockSpec((1, d, D_EXPERT), hi3),
                      pl.BlockSpec((1, D_EXPERT, d), hi3)],
            out_specs=tile),
        compiler_params=_params("arbitrary"),
        name="moe_experts",
    )(e_lo, e_hi, valid, rows_sorted, w_router.T, wg, wu, wd, wg, wu, wd)


def _collect_kernel(pos_ref, x_ref, ys_hbm, mod_ref, fg_ref, o_ref, buf0, buf1, sems, *, final):
    tm = x_ref.shape[0]
    i = pl.program_id(0)
    n = pl.num_programs(0)
    bufs = (buf0, buf1)

    def fetch(tile, slot):
        base = tile * tm
        _row_copies(tm, lambda r: pltpu.make_async_copy(
            _token_slab(ys_hbm, pos_ref[base + r]), _token_slab(bufs[slot], r * ROW_CHUNKS), sems.at[slot]))

    def finish(slot):
        pltpu.make_async_copy(ys_hbm.at[pl.ds(0, tm * ROW_CHUNKS)], bufs[slot], sems.at[slot]).wait()
        xo = x_ref[...] + mod_ref[0][5:6] * _load_rows(bufs[slot], tm)
        if final:
            xo = _rms(xo, fg_ref[...])
        o_ref[...] = xo

    @pl.when(i == 0)
    def _():
        fetch(0, 0)

    for slot in range(2):
        @pl.when(jnp.logical_and(i + 1 < n, (i + 1) % 2 == slot))
        def _():
            fetch(i + 1, slot)

    for slot in range(2):
        @pl.when(i % 2 == slot)
        def _():
            finish(slot)


def _collect_residual(pos, xn, y_sorted, mod_l, final_g, final, seq):
    t, d = xn.shape
    tm = COLLECT_TILE
    rows = pl.BlockSpec((tm, d), lambda i, p: (i, 0))
    return pl.pallas_call(
        functools.partial(_collect_kernel, final=final),
        out_shape=jax.ShapeDtypeStruct((t, d), F32),
        grid_spec=pltpu.PrefetchScalarGridSpec(
            num_scalar_prefetch=1, grid=(t // tm,),
            in_specs=[rows, pl.BlockSpec(memory_space=pl.ANY),
                      pl.BlockSpec((1, 6, d), lambda i, p: ((i * tm) // seq, 0, 0)),
                      pl.BlockSpec((1, d), lambda i, p: (0, 0))],
            out_specs=rows,
            scratch_shapes=[pltpu.VMEM((tm * ROW_CHUNKS, LANES), F32), pltpu.VMEM((tm * ROW_CHUNKS, LANES), F32),
                            pltpu.SemaphoreType.DMA((2,))]),
        compiler_params=pltpu.CompilerParams(dimension_semantics=("arbitrary",), disable_bounds_checks=True,
                                             vmem_limit_bytes=V7X_VMEM_LIMIT_BYTES),
        name="collect_residual",
    )(pos, xn, y_sorted, mod_l, final_g[None, :])


def kernel(x, c, positions, ada_w, ada_b, norm1_g, w_in, hg_lb_logits, hg_norm_g, q_norm_g, w_q_up,
           kv_norm_g, w_kv_up, w_br_a, w_br_b, w_out, norm2_g, w_router, router_bias, w_gate, w_up,
           w_down, final_g):
    batch, seq, d = x.shape
    depth = ada_w.shape[0]
    t = batch * seq
    n_tiles_e = (t + N_CLASSES * (EXPERT_TILE - 1)) // EXPERT_TILE
    mod = _modulation(c, ada_w, ada_b).reshape(depth, batch, 6, d)
    cos_t, sin_t = _rope_tables(positions)
    x2 = x.reshape(t, d)
    for l in range(depth):
        hq, hf, hi, hgate, gsig, q, k, v = _input_projection(
            x2, mod[l], norm1_g[l], cos_t, sin_t, w_in[l], q_norm_g[l], w_q_up[l],
            kv_norm_g[l], w_kv_up[l], seq)
        oa = _hgrn(hg_lb_logits, hg_norm_g[l], hq, hf, hi, hgate, l, batch, seq)
        ob = _attention(q, k, v, batch, seq)
        xn, rows, meta = _merge(x2, oa, ob, gsig, mod[l], w_br_a[l], w_br_b[l], w_out[l], norm2_g[l],
                                w_router, router_bias, seq)
        pos, pad_tile, used, e_lo, e_hi, valid = _dispatch_plan(meta, n_tiles_e)
        rows_sorted, wg, wu, wd = _dispatch_rows(pos, pad_tile, used, rows, n_tiles_e * EXPERT_TILE,
                                                 w_gate, w_up, w_down, l)
        y_sorted = _moe(rows_sorted, e_lo, e_hi, valid, w_router, wg, wu, wd)
        x2 = _collect_residual(pos, xn, y_sorted, mod[l], final_g, l == depth - 1, seq)
    return x2.reshape(batch, seq, d)
```

```python
import functools
import math

import jax
import jax.numpy as jnp
from jax import lax
from jax.experimental import pallas as pl
from jax.experimental.pallas import tpu as pltpu

F32 = jnp.float32
BF16 = jnp.bfloat16

D_MODEL = 1024
CHUNK = 64
EPS = 1e-6

HG_HEADS = 4
HG_DK = 128
HG_DV = 128
HG_WIDTH = HG_HEADS * HG_DV
HG_BATCH = 2
HG_SUB = 8
HG_LEVELS = (8, 16, 32)

MLA_HEADS = 8
MLA_NOPE = 64
MLA_ROPE = 32
MLA_V = 64
MLA_Q_LORA = 384
MLA_KV_LORA = 256
MLA_DQK = MLA_NOPE + MLA_ROPE
MLA_WIDTH = MLA_HEADS * MLA_V
ROPE_BASE = 10000.0
HEAD_PAD = 128
LOG2_E = math.log2(math.e)

N_EXPERTS = 16
N_GROUPS = 4
EXPERTS_PER_GROUP = N_EXPERTS // N_GROUPS
D_EXPERT = 512

IN_SIZES = (HG_HEADS * HG_DK, HG_HEADS * HG_DK, HG_HEADS * HG_DV, HG_WIDTH,
            MLA_Q_LORA, MLA_KV_LORA, MLA_ROPE, 2 * D_MODEL)

V7X_VMEM_LIMIT_BYTES = 56 * 1024 * 1024


def _params(*sem):
    return pltpu.CompilerParams(dimension_semantics=sem, vmem_limit_bytes=V7X_VMEM_LIMIT_BYTES)


def _resident(shape):
    nd = len(shape)
    return pl.BlockSpec(shape, lambda *_: (0,) * nd, pipeline_mode=pl.Buffered(1))


def _sigmoid(x):
    return 1.0 / (1.0 + jnp.exp(-x))


def _silu(x):
    return x * _sigmoid(x)


def _dot(a, b):
    return jnp.dot(a, b, preferred_element_type=F32)


def _dot_nt(a, b):
    return lax.dot_general(a, b, (((1,), (1,)), ((), ())), preferred_element_type=F32)


def _dot_tn(a, b):
    return lax.dot_general(a, b, (((0,), (0,)), ((), ())), preferred_element_type=F32)


def _split_bf16(x):
    hi = x.astype(BF16)
    lo = (x - hi.astype(F32)).astype(BF16)
    return hi, lo


def _rms(x, g):
    return x * lax.rsqrt(jnp.mean(x * x, axis=-1, keepdims=True) + EPS) * g


def _mod_kernel(c_ref, w_ref, b_ref, o_ref):
    @pl.when(pl.program_id(1) == 0)
    def _():
        o_ref[0] = jnp.broadcast_to(b_ref[0], o_ref.shape[1:])
    ca = _silu(c_ref[...])
    o_ref[0] += _dot(ca.astype(BF16), w_ref[0].astype(BF16))


def _modulation(c, ada_w, ada_b):
    depth, d, n = ada_w.shape
    b = c.shape[0]
    tk = 256
    return pl.pallas_call(
        _mod_kernel,
        out_shape=jax.ShapeDtypeStruct((depth, b, n), F32),
        grid=(depth, d // tk),
        in_specs=[pl.BlockSpec((b, tk), lambda l, k: (0, k)),
                  pl.BlockSpec((1, tk, n), lambda l, k: (l, k, 0)),
                  pl.BlockSpec((1, 1, n), lambda l, k: (l, 0, 0))],
        out_specs=pl.BlockSpec((1, b, n), lambda l, k: (l, 0, 0)),
        compiler_params=_params("parallel", "arbitrary"),
        name="adaln_modulation",
    )(c, ada_w, ada_b.reshape(depth, 1, n))


def _rope_kernel(pos_ref, inv_ref, msk_ref, sgn_ref, cos_ref, sin_ref):
    ang = pos_ref[...].astype(F32) * inv_ref[...]
    cos_ref[...] = jnp.cos(ang) * msk_ref[...]
    sin_ref[...] = jnp.sin(ang) * sgn_ref[...]


def _rope_tables(positions):
    t = positions.size
    half = MLA_ROPE // 2
    inv = ROPE_BASE ** (-jnp.arange(half, dtype=F32) / half)
    z64, z32, one16 = jnp.zeros((MLA_NOPE,), F32), jnp.zeros((32,), F32), jnp.ones((half,), F32)
    inv_row = jnp.concatenate([z64, inv, inv, z32])[None, :]
    msk_row = jnp.concatenate([z64, one16, one16, z32])[None, :]
    sgn_row = jnp.concatenate([z64, -one16, one16, z32])[None, :]
    tr = min(2048, t)
    row = pl.BlockSpec((1, HEAD_PAD), lambda i: (0, 0))
    tab = pl.BlockSpec((tr, HEAD_PAD), lambda i: (i, 0))
    return pl.pallas_call(
        _rope_kernel,
        out_shape=(jax.ShapeDtypeStruct((t, HEAD_PAD), F32),) * 2,
        grid=(t // tr,),
        in_specs=[pl.BlockSpec((tr, 1), lambda i: (i, 0)), row, row, row],
        out_specs=(tab, tab),
        compiler_params=_params("parallel"),
        name="rope_tables",
    )(positions.reshape(t, 1), inv_row, msk_row, sgn_row)


def _proj_kernel(x_ref, mod_ref, n1_ref, cos_ref, sin_ref, wh_ref, wg_ref, wc_ref,
                 qn_ref, wqa_ref, wqs_ref, kn_ref, wk_ref, wv_ref, one_ref,
                 hq_ref, hf_ref, hi_ref, hgate_ref, gsig_ref, q_ref, k_ref, v_ref):
    x = x_ref[...]
    mod = mod_ref[0]
    h = _rms(x, n1_ref[...]) * (1.0 + mod[1:2]) + mod[0:1]
    hb = h.astype(BF16)

    ph = _dot(hb, wh_ref[...])
    w = HG_WIDTH
    hq_ref[...] = ph[:, 0:w].astype(BF16)
    hf_ref[...] = ph[:, w:2 * w]
    hi_ref[...] = ph[:, 2 * w:3 * w].astype(BF16)
    hgate_ref[...] = ph[:, 3 * w:4 * w].astype(BF16)

    gsig_ref[...] = _sigmoid(_dot(hb, wg_ref[...])).astype(BF16)

    pc = _dot(hb, wc_ref[...])
    cq = pc[:, 0:MLA_Q_LORA]
    ckv = pc[:, MLA_Q_LORA:MLA_Q_LORA + MLA_KV_LORA]
    kra = pc[:, MLA_Q_LORA + MLA_KV_LORA:MLA_Q_LORA + MLA_KV_LORA + HEAD_PAD]
    krb = pc[:, MLA_Q_LORA + MLA_KV_LORA + HEAD_PAD:]

    cos_t = cos_ref[...]
    sin_t = sin_ref[...]
    lane = lax.broadcasted_iota(jnp.int32, cos_t.shape, 1)
    scale = MLA_DQK ** -0.5 * LOG2_E
    cq_tab = jnp.tile(scale * (cos_t + jnp.where(lane < MLA_NOPE, 1.0, 0.0)), (1, MLA_HEADS))
    sq_tab = jnp.tile(scale * sin_t, (1, MLA_HEADS))

    cqn = _rms(cq, qn_ref[...]).astype(BF16)
    q = _dot(cqn, wqa_ref[...]) * cq_tab + _dot(cqn, wqs_ref[...]) * sq_tab
    q_ref[...] = q.astype(BF16)

    ckvn = _rms(ckv, kn_ref[...]).astype(BF16)
    kpe = kra * cos_t + krb * sin_t
    k_ref[...] = (_dot(ckvn, wk_ref[...]) + jnp.tile(kpe, (1, MLA_HEADS))).astype(BF16)
    v_ref[...] = (_dot(ckvn, wv_ref[...]) + one_ref[...]).astype(BF16)


def _pad_heads(w, lo, hi, at):
    k, nh, _ = w.shape
    out = jnp.zeros((k, nh, HEAD_PAD), w.dtype)
    out = out.at[:, :, at:at + (hi - lo)].set(w[:, :, lo:hi])
    return out


def _input_projection(x2, mod_l, n1, cos_t, sin_t, w_in, q_norm_g, w_q_up, kv_norm_g, w_kv_up, seq):
    t, d = x2.shape
    tm = 256
    splits = [0]
    for s in IN_SIZES:
        splits.append(splits[-1] + s)
    kr = w_in[:, splits[6]:splits[7]]
    half = MLA_ROPE // 2
    z64 = jnp.zeros((d, MLA_NOPE), F32)
    z32 = jnp.zeros((d, HEAD_PAD - MLA_NOPE - MLA_ROPE), F32)
    kr_a = jnp.concatenate([z64, kr, z32], axis=1)
    kr_b = jnp.concatenate([z64, kr[:, half:], kr[:, :half], z32], axis=1)
    w_h = w_in[:, splits[0]:splits[4]].astype(BF16)
    w_g = w_in[:, splits[7]:splits[8]].astype(BF16)
    w_c = jnp.concatenate([w_in[:, splits[4]:splits[6]], kr_a, kr_b], axis=1).astype(BF16)

    wq = w_q_up.reshape(MLA_Q_LORA, MLA_HEADS, MLA_DQK)
    wq_all = _pad_heads(wq, 0, MLA_DQK, 0).reshape(MLA_Q_LORA, -1).astype(BF16)
    wq_swap = (_pad_heads(wq, MLA_NOPE + half, MLA_DQK, MLA_NOPE)
               + _pad_heads(wq, MLA_NOPE, MLA_NOPE + half, MLA_NOPE + half))
    wq_swap = wq_swap.reshape(MLA_Q_LORA, -1).astype(BF16)
    wkv = w_kv_up.reshape(MLA_KV_LORA, MLA_HEADS, MLA_NOPE + MLA_V)
    wk_all = _pad_heads(wkv, 0, MLA_NOPE, 0).reshape(MLA_KV_LORA, -1).astype(BF16)
    wv_all = _pad_heads(wkv, MLA_NOPE, MLA_NOPE + MLA_V, 0).reshape(MLA_KV_LORA, -1).astype(BF16)
    ones_row = jnp.tile(jnp.zeros((HEAD_PAD,), F32).at[MLA_V].set(1.0), MLA_HEADS)[None, :]

    hp = MLA_HEADS * HEAD_PAD
    rows = lambda n: pl.BlockSpec((tm, n), lambda i: (i, 0))
    outs = [(HG_WIDTH, BF16), (HG_WIDTH, F32), (HG_WIDTH, BF16), (HG_WIDTH, BF16),
            (2 * d, BF16), (hp, BF16), (hp, BF16), (hp, BF16)]
    return pl.pallas_call(
        _proj_kernel,
        out_shape=tuple(jax.ShapeDtypeStruct((t, n), dt) for n, dt in outs),
        grid=(t // tm,),
        in_specs=[rows(d),
                  pl.BlockSpec((1, 6, d), lambda i: ((i * tm) // seq, 0, 0)),
                  _resident((1, d)),
                  rows(HEAD_PAD), rows(HEAD_PAD),
                  _resident(w_h.shape), _resident(w_g.shape), _resident(w_c.shape),
                  _resident((1, MLA_Q_LORA)), _resident(wq_all.shape), _resident(wq_swap.shape),
                  _resident((1, MLA_KV_LORA)), _resident(wk_all.shape), _resident(wv_all.shape),
                  _resident((1, hp))],
        out_specs=tuple(rows(n) for n, _ in outs),
        compiler_params=_params("parallel"),
        name="norm_input_projection",
    )(x2, mod_l, n1[None, :], cos_t, sin_t, w_h, w_g, w_c,
      q_norm_g[None, :], wq_all, wq_swap, kv_norm_g[None, :], wk_all, wv_all, ones_row)


def _hgrn_kernel(lbl_ref, gn_ref, q_ref, f_ref, v_ref, gate_ref, o_ref, st_ref, *, layer, n_chunks):
    @pl.when(pl.program_id(1) == 0)
    def _():
        st_ref[...] = jnp.zeros_like(st_ref)

    lg = lbl_ref[...]
    ex = jnp.exp(lg - jnp.max(lg, axis=0, keepdims=True))
    soft = ex / jnp.sum(ex, axis=0, keepdims=True)
    lb_all = jnp.zeros_like(soft[0:1])
    for i in range(1, layer + 1):
        lb_all = lb_all + soft[i:i + 1]

    r_i = lax.broadcasted_iota(jnp.int32, (CHUNK, CHUNK), 0)
    c_i = lax.broadcasted_iota(jnp.int32, (CHUNK, CHUNK), 1)
    tri = jnp.where(r_i >= c_i, 1.0, 0.0).astype(BF16)
    level_masks = []
    for hs in HG_LEVELS:
        same = (r_i // (2 * hs)) == (c_i // (2 * hs))
        level_masks.append(jnp.logical_and(same, jnp.logical_and(r_i % (2 * hs) >= hs, c_i % (2 * hs) < hs)))
    sub_row = lax.broadcasted_iota(jnp.int32, (HG_SUB, 1), 0)
    n_sub = CHUNK // HG_SUB

    def chunk_body(ci, carry):
        r0 = pl.multiple_of(ci * CHUNK, CHUNK)
        rows = pl.ds(r0, CHUNK)
        for bi, h in [(bi, h) for bi in range(HG_BATCH) for h in range(HG_HEADS)]:
            sl = slice(h * HG_DK, (h + 1) * HG_DK)
            lb = lb_all[:, sl]
            q = q_ref[bi, rows, sl].astype(F32)
            vb = v_ref[bi, rows, sl]
            v = vb.astype(F32)
            f = lb + (1.0 - lb) * _sigmoid(f_ref[bi, rows, sl])
            g = jnp.log2(f)
            k = 1.0 - f
            qf = _silu(q) * (HG_DK ** -0.5)
            g_hi, g_lo = _split_bf16(g)
            b = _dot(tri, g_hi) + _dot(tri, g_lo)

            att = jnp.zeros((CHUNK, CHUNK), F32)
            for hs, mask in zip(HG_LEVELS, level_masks):
                ref = jnp.concatenate(
                    [jnp.broadcast_to(b[j + hs - 1:j + hs], (2 * hs, HG_DK)) for j in range(0, CHUNK, 2 * hs)],
                    axis=0)
                qt = (qf * jnp.exp2(b - ref)).astype(BF16)
                kt = (k * jnp.exp2(ref - b)).astype(BF16)
                att = att + jnp.where(mask, _dot_nt(qt, kt), 0.0)
            o = _dot(att.astype(BF16), vb)

            blocks = []
            for i in range(n_sub):
                rs = slice(i * HG_SUB, (i + 1) * HG_SUB)
                b_i, q_i, k_i, v_i = b[rs], qf[rs], k[rs], v[rs]
                acc = jnp.zeros((HG_SUB, HG_DV), F32)
                for s in range(HG_SUB):
                    a = jnp.sum(jnp.exp2(b_i - b_i[s:s + 1]) * (q_i * k_i[s:s + 1]), axis=1, keepdims=True)
                    acc = acc + jnp.where(sub_row >= s, a, 0.0) * v_i[s:s + 1]
                blocks.append(acc)
            o = o + jnp.concatenate(blocks, axis=0)

            st = st_ref[bi * HG_HEADS + h]
            o = o + _dot_nt((qf * jnp.exp2(b)).astype(BF16), st.astype(BF16))
            b_last = b[CHUNK - 1:CHUNK]
            kd = (k * jnp.exp2(b_last - b)).astype(BF16)
            st_ref[bi * HG_HEADS + h] = st * jnp.exp2(b_last) + _dot_tn(vb, kd)

            gt = gate_ref[bi, rows, sl].astype(F32)
            o_ref[bi, rows, sl] = (_rms(o, gn_ref[:, sl]) * _silu(gt)).astype(BF16)
        return carry

    lax.fori_loop(0, n_chunks, chunk_body, 0, unroll=True)


def _hgrn(hg_lb_logits, hg_norm_g, hq, hf, hi, hgate, layer, batch, seq):
    t = hq.shape[0]
    lc = min(512, seq)
    nb = seq // lc
    assert batch % HG_BATCH == 0
    rows = pl.BlockSpec((HG_BATCH, lc, HG_WIDTH), lambda b, j: (b, j, 0))
    kern = functools.partial(_hgrn_kernel, layer=layer, n_chunks=lc // CHUNK)
    per_batch = lambda a: a.reshape(batch, seq, HG_WIDTH)
    return pl.pallas_call(
        kern,
        out_shape=jax.ShapeDtypeStruct((batch, seq, HG_WIDTH), BF16),
        grid=(batch // HG_BATCH, nb),
        in_specs=[pl.BlockSpec(hg_lb_logits.shape, lambda b, j: (0, 0)),
                  pl.BlockSpec((1, HG_WIDTH), lambda b, j: (0, 0)),
                  rows, rows, rows, rows],
        out_specs=rows,
        scratch_shapes=[pltpu.VMEM((HG_BATCH * HG_HEADS, HG_DV, HG_DK), F32)],
        compiler_params=_params("parallel", "arbitrary"),
        name="hgrn2_chunkwise",
    )(hg_lb_logits, hg_norm_g[None, :], per_batch(hq), per_batch(hf), per_batch(hi),
      per_batch(hgate)).reshape(t, HG_WIDTH)


ATTN_BLOCK = 1024
STEP_FULL, STEP_TOP_DIAG, STEP_BOTTOM_DIAG = range(3)


def _attn_kernel(qi_ref, kj_ref, kind_ref, q_ref, k_ref, v_ref, o_ref, m_ref, acc_ref, *, tb):
    del qi_ref, kj_ref
    p_id = pl.program_id(1)
    kind = kind_ref[p_id]
    half = tb // 2

    def reset():
        m_ref[...] = jnp.full_like(m_ref, -jnp.inf)
        acc_ref[...] = jnp.zeros_like(acc_ref)

    def step(r0, nr, nk, diagonal):
        rows = slice(r0, r0 + nr)
        if diagonal:
            qc = (lax.broadcasted_iota(jnp.int32, (nr, nk), 0) + r0) // CHUNK
            kc = lax.broadcasted_iota(jnp.int32, (nr, nk), 1) // CHUNK
            visible = kc <= qc
        for h in range(MLA_HEADS):
            sl = slice(h * HEAD_PAD, (h + 1) * HEAD_PAD)
            s = _dot_nt(q_ref[rows, sl], k_ref[0:nk, sl])
            if diagonal:
                s = jnp.where(visible, s, -jnp.inf)
            tiles = [s[:, t * HEAD_PAD:(t + 1) * HEAD_PAD] for t in range(nk // HEAD_PAD)]
            m_tile = tiles[0]
            for tl in tiles[1:]:
                m_tile = jnp.maximum(m_tile, tl)
            m_prev = m_ref[h, rows]
            m_new = jnp.maximum(m_prev, jnp.max(m_tile, axis=1, keepdims=True))
            alpha = jnp.exp2(m_prev - m_new)
            p = jnp.concatenate([jnp.exp2((tl - m_new).astype(BF16)) for tl in tiles], axis=1)
            acc_ref[h, rows] = acc_ref[h, rows] * alpha + _dot(p, v_ref[0:nk, sl])
            m_ref[h, rows] = m_new

    @pl.when(p_id == 0)
    def _():
        reset()

    @pl.when(kind == STEP_FULL)
    def _():
        step(0, tb, tb, False)

    @pl.when(kind == STEP_TOP_DIAG)
    def _():
        step(0, half, half, True)

    @pl.when(kind == STEP_BOTTOM_DIAG)
    def _():
        step(half, half, tb, True)
        outs = []
        for h in range(MLA_HEADS):
            acc = acc_ref[h]
            outs.append(acc[:, 0:MLA_V] / acc[:, MLA_V:MLA_V + 1])
        o_ref[...] = jnp.concatenate(outs, axis=1).astype(BF16)
        reset()


def _attention(q, k, v, batch, seq):
    t, hp = q.shape
    tb = min(ATTN_BLOCK, seq)
    nb = seq // tb
    steps = []
    for i in range(nb):
        steps += [(i, j, STEP_FULL) for j in range(i)]
        steps += [(i, i, STEP_TOP_DIAG), (i, i, STEP_BOTTOM_DIAG)]
    q_of, k_of, kinds = (jnp.asarray([st[c] for st in steps], jnp.int32) for c in range(3))
    qspec = pl.BlockSpec((tb, hp), lambda b, p, qi, kj, kd: (b * nb + qi[p], 0))
    kspec = pl.BlockSpec((tb, hp), lambda b, p, qi, kj, kd: (b * nb + kj[p], 0))
    return pl.pallas_call(
        functools.partial(_attn_kernel, tb=tb),
        out_shape=jax.ShapeDtypeStruct((t, MLA_WIDTH), BF16),
        grid_spec=pltpu.PrefetchScalarGridSpec(
            num_scalar_prefetch=3,
            grid=(batch, len(steps)),
            in_specs=[qspec, kspec, kspec],
            out_specs=pl.BlockSpec((tb, MLA_WIDTH), lambda b, p, qi, kj, kd: (b * nb + qi[p], 0)),
            scratch_shapes=[pltpu.VMEM((MLA_HEADS, tb, HEAD_PAD), F32),
                            pltpu.VMEM((MLA_HEADS, tb, HEAD_PAD), F32)]),
        compiler_params=_params("parallel", "arbitrary"),
        name="mla_flash_attention",
    )(q_of, k_of, kinds, q, k, v)


PAIR_ORDER = ((0, 1), (0, 2), (0, 3), (1, 3), (1, 2), (2, 3))
PAIRS_PER_GROUP = len(PAIR_ORDER)
N_CLASSES = N_GROUPS * PAIRS_PER_GROUP
LANES = 128
ROW_CHUNKS = D_MODEL // LANES
TOKEN_TILE = 1024
COLLECT_TILE = 512
EXPERT_TILE = 256


N_CLASS_ROWS = 32


def _route(scores, bias):
    biased = scores + bias
    col = [biased[e:e + 1, :] for e in range(N_EXPERTS)]
    gscore = []
    for g in range(N_GROUPS):
        a, b, c, d = col[g * EXPERTS_PER_GROUP:(g + 1) * EXPERTS_PER_GROUP]
        gscore.append(jnp.maximum(jnp.maximum(jnp.maximum(a + b, a + c), jnp.maximum(a + d, b + c)),
                                  jnp.maximum(b + d, c + d)))
    sel = []
    for g in range(N_GROUPS):
        ok = None
        for o in range(N_GROUPS):
            if o == g:
                continue
            cond = (gscore[g] > gscore[o]) if o < g else (gscore[g] >= gscore[o])
            ok = cond if ok is None else jnp.logical_and(ok, cond)
        grp = col[g * EXPERTS_PER_GROUP:(g + 1) * EXPERTS_PER_GROUP]
        for e in range(EXPERTS_PER_GROUP):
            beaten = jnp.zeros_like(grp[e])
            for o in range(EXPERTS_PER_GROUP):
                if o == e:
                    continue
                ahead = (grp[o] >= grp[e]) if o < e else (grp[o] > grp[e])
                beaten = beaten + jnp.where(ahead, 1.0, 0.0)
            sel.append(jnp.logical_and(ok, beaten < 1.5))
    lo = jnp.full_like(col[0], float(N_EXPERTS))
    hi = jnp.full_like(col[0], -1.0)
    for e in range(N_EXPERTS):
        lo = jnp.where(sel[e], jnp.minimum(lo, float(e)), lo)
        hi = jnp.where(sel[e], jnp.maximum(hi, float(e)), hi)
    return lo, hi


def _store_rows(ref, x):
    n = x.shape[0]
    for c in range(ROW_CHUNKS):
        ref[pl.ds(c, n, stride=ROW_CHUNKS), :] = x[:, c * LANES:(c + 1) * LANES]


def _load_rows(ref, n):
    return jnp.concatenate([ref[pl.ds(c, n, stride=ROW_CHUNKS), :] for c in range(ROW_CHUNKS)], axis=1)


def _merge_kernel(x_ref, oa_ref, ob_ref, g_ref, mod_ref, wa_ref, wb_ref, wo_ref, n2_ref,
                  wrt_ref, rb_ref, xn_ref, row_ref, meta_ref):
    d = D_MODEL
    tm = x_ref.shape[0]
    mod = mod_ref[0]
    ya = _dot(oa_ref[...], wa_ref[...])
    yb = _dot(ob_ref[...], wb_ref[...])
    g = g_ref[...].astype(F32)
    merged = g[:, 0:d] * ya + g[:, d:2 * d] * yb
    xn = x_ref[...] + mod[2:3] * _dot(merged.astype(BF16), wo_ref[...])
    xn_ref[...] = xn
    h2 = _rms(xn, n2_ref[...]) * (1.0 + mod[4:5]) + mod[3:4]
    _store_rows(row_ref, h2)

    h_hi, h_lo = _split_bf16(h2)
    w_hi, w_lo = _split_bf16(wrt_ref[...])
    scores = _sigmoid(_dot_nt(w_hi, h_hi) + _dot_nt(w_hi, h_lo) + _dot_nt(w_lo, h_hi))
    lo, hi = _route(scores, rb_ref[...])

    grp = jnp.floor(lo * (1.0 / EXPERTS_PER_GROUP))
    a = lo - grp * EXPERTS_PER_GROUP
    b = hi - grp * EXPERTS_PER_GROUP
    pair = jnp.zeros_like(a)
    for p, (p_lo, p_hi) in enumerate(PAIR_ORDER):
        pair = pair + jnp.where(jnp.logical_and(a == p_lo, b == p_hi), float(p), 0.0)
    cls = grp * PAIRS_PER_GROUP + pair
    c_row = lax.broadcasted_iota(jnp.int32, (N_CLASS_ROWS, tm), 0).astype(F32)
    onehot = jnp.where(c_row == cls, 1.0, 0.0)
    r_i = lax.broadcasted_iota(jnp.int32, (tm, tm), 0)
    c_i = lax.broadcasted_iota(jnp.int32, (tm, tm), 1)
    earlier = jnp.where(r_i < c_i, 1.0, 0.0).astype(BF16)
    rank = jnp.sum(onehot * _dot(onehot.astype(BF16), earlier), axis=0, keepdims=True)
    m_row = lax.broadcasted_iota(jnp.int32, (8, tm), 0)
    meta_ref[...] = jnp.where(m_row == 0, cls, jnp.where(m_row == 1, rank, 0.0))


def _merge(x2, oa, ob, gsig, mod_l, w_br_a, w_br_b, w_out, n2, w_router, router_bias, seq):
    t, d = x2.shape
    tm = TOKEN_TILE
    rows = lambda n: pl.BlockSpec((tm, n), lambda i: (i, 0))
    return pl.pallas_call(
        _merge_kernel,
        out_shape=(jax.ShapeDtypeStruct((t, d), F32), jax.ShapeDtypeStruct((t * ROW_CHUNKS, LANES), F32),
                   jax.ShapeDtypeStruct((8, t), F32)),
        grid=(t // tm,),
        in_specs=[rows(d), rows(HG_WIDTH), rows(MLA_WIDTH), rows(2 * d),
                  pl.BlockSpec((1, 6, d), lambda i: ((i * tm) // seq, 0, 0)),
                  _resident(w_br_a.shape), _resident(w_br_b.shape), _resident(w_out.shape),
                  _resident((1, d)), _resident((N_EXPERTS, d)), _resident((N_EXPERTS, 1))],
        out_specs=(rows(d), pl.BlockSpec((tm * ROW_CHUNKS, LANES), lambda i: (i, 0)),
                   pl.BlockSpec((8, tm), lambda i: (0, i))),
        compiler_params=_params("parallel"),
        name="merge_outproj_router",
    )(x2, oa, ob, gsig, mod_l, w_br_a.astype(BF16), w_br_b.astype(BF16), w_out.astype(BF16),
      n2[None, :], w_router.T, router_bias[:, None])


def _dispatch_plan(meta, n_tiles_e):
    t = meta.shape[1]
    n_tok_tiles = t // TOKEN_TILE
    cls = meta[0].astype(jnp.int32)
    rank = meta[1].astype(jnp.int32)
    own = cls.reshape(n_tok_tiles, TOKEN_TILE, 1) == jnp.arange(N_CLASSES, dtype=jnp.int32)
    counts = jnp.sum(own.astype(jnp.int32), axis=1)
    total = jnp.sum(counts, axis=0)
    total_pad = (total + EXPERT_TILE - 1) // EXPERT_TILE * EXPERT_TILE
    ends = jnp.cumsum(total_pad)
    base = (ends - total_pad)[None, :] + jnp.cumsum(counts, axis=0) - counts
    pos = jnp.sum(jnp.where(own, base[:, None, :], 0), axis=2).reshape(t) + rank

    tile_start = jnp.arange(n_tiles_e, dtype=jnp.int32) * EXPERT_TILE
    n_valid = ends[-1] // EXPERT_TILE
    tile_cls = jnp.sum((tile_start[:, None] >= ends[None, :]).astype(jnp.int32), axis=1)
    last_cls = jnp.take(tile_cls, n_valid - 1)
    valid = jnp.arange(n_tiles_e, dtype=jnp.int32) < n_valid
    tile_cls = jnp.where(valid, tile_cls, last_cls)
    pair_lo = jnp.asarray([p[0] for p in PAIR_ORDER], jnp.int32)
    pair_hi = jnp.asarray([p[1] for p in PAIR_ORDER], jnp.int32)
    grp = tile_cls // PAIRS_PER_GROUP
    e_lo = grp * EXPERTS_PER_GROUP + jnp.take(pair_lo, tile_cls % PAIRS_PER_GROUP)
    e_hi = grp * EXPERTS_PER_GROUP + jnp.take(pair_hi, tile_cls % PAIRS_PER_GROUP)
    pad_tile = (ends // EXPERT_TILE - 1).astype(jnp.int32)
    used = jnp.concatenate([(total > 0).astype(jnp.int32), n_valid[None].astype(jnp.int32)])
    return (pos * ROW_CHUNKS).astype(jnp.int32), pad_tile, used, e_lo, e_hi, valid.astype(jnp.int32)


DMA_UNROLL = 8


def _row_copies(n_rows, make_copy):
    def issue(ui, carry):
        for u in range(DMA_UNROLL):
            make_copy(ui * DMA_UNROLL + u).start(priority=u % 2)
        return carry
    lax.fori_loop(0, n_rows // DMA_UNROLL, issue, 0)


def _token_slab(ref, first_row):
    return ref.at[pl.ds(pl.multiple_of(first_row, ROW_CHUNKS), ROW_CHUNKS)]


def _dispatch_kernel(pos_ref, pad_tile_ref, used_ref, rows_ref, wg_ref, wu_ref, wd_ref,
                     dst_hbm, wg_out, wu_out, wd_out, zero_ref, sem, zero_sem):
    tm = rows_ref.shape[0] // ROW_CHUNKS
    tile_rows = EXPERT_TILE * ROW_CHUNKS

    @pl.when(pl.program_id(0) == 0)
    def _():
        zero_ref[...] = jnp.zeros_like(zero_ref)
        n_tiles = dst_hbm.shape[0] // tile_rows

        def clear(tile):
            first = pl.multiple_of(tile * tile_rows, tile_rows)
            return pltpu.make_async_copy(zero_ref, dst_hbm.at[pl.ds(first, tile_rows)], zero_sem)

        def start_tail(tile, carry):
            clear(tile).start()
            return carry

        def wait_tail(tile, carry):
            clear(tile).wait()
            return carry
        for c in range(N_CLASSES):
            @pl.when(used_ref[c] != 0)
            def _():
                clear(pad_tile_ref[c]).start()
        lax.fori_loop(used_ref[N_CLASSES], n_tiles, start_tail, 0)
        for c in range(N_CLASSES):
            @pl.when(used_ref[c] != 0)
            def _():
                clear(pad_tile_ref[c]).wait()
        lax.fori_loop(used_ref[N_CLASSES], n_tiles, wait_tail, 0)

    base = pl.program_id(0) * tm
    _row_copies(tm, lambda r: pltpu.make_async_copy(
        _token_slab(rows_ref, r * ROW_CHUNKS), _token_slab(dst_hbm, pos_ref[base + r]), sem))
    wg_out[...] = wg_ref[0].astype(BF16)
    wu_out[...] = wu_ref[0].astype(BF16)
    wd_out[...] = wd_ref[0].astype(BF16)
    pltpu.make_async_copy(rows_ref, dst_hbm.at[pl.ds(0, tm * ROW_CHUNKS)], sem).wait()


def _dispatch_rows(pos, pad_tile, used, rows, n_dst, w_gate, w_up, w_down, layer):
    t = rows.shape[0] // ROW_CHUNKS
    tm = TOKEN_TILE
    n_steps = t // tm
    assert N_EXPERTS % n_steps == 0
    eps = N_EXPERTS // n_steps
    d = D_MODEL
    w_in3 = lambda a, b: pl.BlockSpec((1, eps, a, b), lambda i, p, pt, us: (layer, i, 0, 0))
    w_out3 = lambda a, b: pl.BlockSpec((eps, a, b), lambda i, p, pt, us: (i, 0, 0))
    return pl.pallas_call(
        _dispatch_kernel,
        out_shape=(jax.ShapeDtypeStruct((n_dst * ROW_CHUNKS, LANES), rows.dtype),
                   jax.ShapeDtypeStruct((N_EXPERTS, d, D_EXPERT), BF16),
                   jax.ShapeDtypeStruct((N_EXPERTS, d, D_EXPERT), BF16),
                   jax.ShapeDtypeStruct((N_EXPERTS, D_EXPERT, d), BF16)),
        grid_spec=pltpu.PrefetchScalarGridSpec(
            num_scalar_prefetch=3, grid=(n_steps,),
            in_specs=[pl.BlockSpec((tm * ROW_CHUNKS, LANES), lambda i, p, pt, us: (i, 0)),
                      w_in3(d, D_EXPERT), w_in3(d, D_EXPERT), w_in3(D_EXPERT, d)],
            out_specs=(pl.BlockSpec(memory_space=pl.ANY),
                       w_out3(d, D_EXPERT), w_out3(d, D_EXPERT), w_out3(D_EXPERT, d)),
            scratch_shapes=[pltpu.VMEM((EXPERT_TILE * ROW_CHUNKS, LANES), rows.dtype),
                            pltpu.SemaphoreType.DMA, pltpu.SemaphoreType.DMA]),
        compiler_params=pltpu.CompilerParams(dimension_semantics=("arbitrary",), disable_bounds_checks=True,
                                             has_side_effects=True, vmem_limit_bytes=V7X_VMEM_LIMIT_BYTES),
        name="dispatch_rows",
    )(pos, pad_tile, used, rows, w_gate, w_up, w_down)


def _moe_kernel(elo_ref, ehi_ref, valid_ref, x_ref, wrt_ref, wgl_ref, wul_ref, wdl_ref, wgh_ref, wuh_ref,
                wdh_ref, y_ref):
    j = pl.program_id(0)

    @pl.when(valid_ref[j] == 0)
    def _():
        y_ref[...] = jnp.zeros_like(y_ref)

    @pl.when(valid_ref[j] != 0)
    def _():
        h2 = _load_rows(x_ref, EXPERT_TILE)
        s_lo = _sigmoid(jnp.sum(h2 * wrt_ref[pl.ds(elo_ref[j], 1), :], axis=1, keepdims=True))
        s_hi = _sigmoid(jnp.sum(h2 * wrt_ref[pl.ds(ehi_ref[j], 1), :], axis=1, keepdims=True))
        total = s_lo + s_hi
        hb = h2.astype(BF16)
        he_lo = _silu(_dot(hb, wgl_ref[0])) * _dot(hb, wul_ref[0])
        y = (s_lo / total) * _dot(he_lo.astype(BF16), wdl_ref[0])
        he_hi = _silu(_dot(hb, wgh_ref[0])) * _dot(hb, wuh_ref[0])
        _store_rows(y_ref, y + (s_hi / total) * _dot(he_hi.astype(BF16), wdh_ref[0]))


def _moe(rows_sorted, e_lo, e_hi, valid, w_router, wg, wu, wd):
    n_pad = rows_sorted.shape[0] // ROW_CHUNKS
    d = D_MODEL
    lo3 = lambda j, el, eh, va: (el[j], 0, 0)
    hi3 = lambda j, el, eh, va: (eh[j], 0, 0)
    tile = pl.BlockSpec((EXPERT_TILE * ROW_CHUNKS, LANES), lambda j, el, eh, va: (j, 0))
    return pl.pallas_call(
        _moe_kernel,
        out_shape=jax.ShapeDtypeStruct(rows_sorted.shape, F32),
        grid_spec=pltpu.PrefetchScalarGridSpec(
            num_scalar_prefetch=3, grid=(n_pad // EXPERT_TILE,),
            in_specs=[tile, _resident((N_EXPERTS, d)),
                      pl.BlockSpec((1, d, D_EXPERT), lo3), pl.BlockSpec((1, d, D_EXPERT), lo3),
                      pl.BlockSpec((1, D_EXPERT, d), lo3),
                      pl.BlockSpec((1, d, D_EXPERT), hi3), pl.BlockSpec((1, d, D_EXPERT), hi3),
                      pl.BlockSpec((1, D_EXPERT, d), hi3)],
            out_specs=tile),
        compiler_params=_params("arbitrary"),
        name="moe_experts",
    )(e_lo, e_hi, valid, rows_sorted, w_router.T, wg, wu, wd, wg, wu, wd)


def _collect_kernel(pos_ref, x_ref, ys_hbm, mod_ref, fg_ref, o_ref, buf0, buf1, sems, *, final):
    tm = x_ref.shape[0]
    i = pl.program_id(0)
    n = pl.num_programs(0)
    bufs = (buf0, buf1)

    def fetch(tile, slot):
        base = tile * tm
        _row_copies(tm, lambda r: pltpu.make_async_copy(
            _token_slab(ys_hbm, pos_ref[base + r]), _token_slab(bufs[slot], r * ROW_CHUNKS), sems.at[slot]))

    def finish(slot):
        pltpu.make_async_copy(ys_hbm.at[pl.ds(0, tm * ROW_CHUNKS)], bufs[slot], sems.at[slot]).wait()
        xo = x_ref[...] + mod_ref[0][5:6] * _load_rows(bufs[slot], tm)
        if final:
            xo = _rms(xo, fg_ref[...])
        o_ref[...] = xo

    @pl.when(i == 0)
    def _():
        fetch(0, 0)

    for slot in range(2):
        @pl.when(jnp.logical_and(i + 1 < n, (i + 1) % 2 == slot))
        def _():
            fetch(i + 1, slot)

    for slot in range(2):
        @pl.when(i % 2 == slot)
        def _():
            finish(slot)


def _collect_residual(pos, xn, y_sorted, mod_l, final_g, final, seq):
    t, d = xn.shape
    tm = COLLECT_TILE
    rows = pl.BlockSpec((tm, d), lambda i, p: (i, 0))
    return pl.pallas_call(
        functools.partial(_collect_kernel, final=final),
        out_shape=jax.ShapeDtypeStruct((t, d), F32),
        grid_spec=pltpu.PrefetchScalarGridSpec(
            num_scalar_prefetch=1, grid=(t // tm,),
            in_specs=[rows, pl.BlockSpec(memory_space=pl.ANY),
                      pl.BlockSpec((1, 6, d), lambda i, p: ((i * tm) // seq, 0, 0)),
                      pl.BlockSpec((1, d), lambda i, p: (0, 0))],
            out_specs=rows,
            scratch_shapes=[pltpu.VMEM((tm * ROW_CHUNKS, LANES), F32), pltpu.VMEM((tm * ROW_CHUNKS, LANES), F32),
                            pltpu.SemaphoreType.DMA((2,))]),
        compiler_params=pltpu.CompilerParams(dimension_semantics=("arbitrary",), disable_bounds_checks=True,
                                             vmem_limit_bytes=V7X_VMEM_LIMIT_BYTES),
        name="collect_residual",
    )(pos, xn, y_sorted, mod_l, final_g[None, :])


def kernel(x, c, positions, ada_w, ada_b, norm1_g, w_in, hg_lb_logits, hg_norm_g, q_norm_g, w_q_up,
           kv_norm_g, w_kv_up, w_br_a, w_br_b, w_out, norm2_g, w_router, router_bias, w_gate, w_up,
           w_down, final_g):
    batch, seq, d = x.shape
    depth = ada_w.shape[0]
    t = batch * seq
    n_tiles_e = (t + N_CLASSES * (EXPERT_TILE - 1)) // EXPERT_TILE
    mod = _modulation(c, ada_w, ada_b).reshape(depth, batch, 6, d)
    cos_t, sin_t = _rope_tables(positions)
    x2 = x.reshape(t, d)
    for l in range(depth):
        hq, hf, hi, hgate, gsig, q, k, v = _input_projection(
            x2, mod[l], norm1_g[l], cos_t, sin_t, w_in[l], q_norm_g[l], w_q_up[l],
            kv_norm_g[l], w_kv_up[l], seq)
        oa = _hgrn(hg_lb_logits, hg_norm_g[l], hq, hf, hi, hgate, l, batch, seq)
        ob = _attention(q, k, v, batch, seq)
        xn, rows, meta = _merge(x2, oa, ob, gsig, mod[l], w_br_a[l], w_br_b[l], w_out[l], norm2_g[l],
                                w_router, router_bias, seq)
        pos, pad_tile, used, e_lo, e_hi, valid = _dispatch_plan(meta, n_tiles_e)
        rows_sorted, wg, wu, wd = _dispatch_rows(pos, pad_tile, used, rows, n_tiles_e * EXPERT_TILE,
                                                 w_gate, w_up, w_down, l)
        y_sorted = _moe(rows_sorted, e_lo, e_hi, valid, w_router, wg, wu, wd)
        x2 = _collect_residual(pos, xn, y_sorted, mod[l], final_g, l == depth - 1, seq)
    return x2.reshape(batch, seq, d)
```
